```python
import math
import jax, jax.numpy as jnp
from jax import lax
import numpy as np

D_MODEL = 1024
BATCH = 4
SEQ = 8192
DEPTH = 1

CHUNK = 64
Q_BLOCK = 128
POOL_WINDOWS = (2, 4, 8, 16)
POOL_GROUPS = len(POOL_WINDOWS)
POOL_WIDTH = D_MODEL // 2
POOL_GROUP_DIM = POOL_WIDTH // POOL_GROUPS
POOL_OUT_GROUP_DIM = D_MODEL // POOL_GROUPS
N_HEADS = 8
QK_NOPE_DIM = D_MODEL // 16
QK_ROPE_DIM = D_MODEL // 32
V_HEAD_DIM = D_MODEL // 16
Q_LORA_RANK = 3 * D_MODEL // 8
KV_LORA_RANK = D_MODEL // 4
ROPE_THETA = 10000.0
N_BRANCHES = 2
IN_SPLITS = (POOL_WIDTH,
             POOL_WIDTH + Q_LORA_RANK,
             POOL_WIDTH + Q_LORA_RANK + KV_LORA_RANK,
             POOL_WIDTH + Q_LORA_RANK + KV_LORA_RANK + QK_ROPE_DIM)
IN_COLS = IN_SPLITS[-1] + N_BRANCHES * D_MODEL
N_GROUPS = 4
EXPERTS_PER_GROUP = 8
N_EXPERTS = N_GROUPS * EXPERTS_PER_GROUP
TOP_K = 2
D_EXPERT = D_MODEL // 2
MOE_BLOCK = 128
NORM_EPS = 1e-5
DEEPNORM_ALPHA = (2.0 * DEPTH) ** 0.25
DEEPNORM_BETA = (8.0 * DEPTH) ** -0.25

kernel_name = "chunk_causal_pool_mla_hiermoe_deepnorm"


def layer_norm(x, g, b):
    xf = x.astype(jnp.float32)
    mu = jnp.mean(xf, axis=-1, keepdims=True)
    var = jnp.mean(jnp.square(xf - mu), axis=-1, keepdims=True)
    return ((xf - mu) * lax.rsqrt(var + NORM_EPS) * g.astype(jnp.float32) + b.astype(jnp.float32)).astype(x.dtype)


def rms_norm(x, g):
    xf = x.astype(jnp.float32)
    ms = jnp.mean(jnp.square(xf), axis=-1, keepdims=True)
    return (xf * lax.rsqrt(ms + NORM_EPS) * g.astype(jnp.float32)).astype(x.dtype)


def rope_tables(seq):
    inv = 1.0 / (ROPE_THETA ** (jnp.arange(0, QK_ROPE_DIM, 2, dtype=jnp.float32) / QK_ROPE_DIM))
    ang = jnp.arange(seq, dtype=jnp.float32)[:, None] * inv[None, :]
    return jnp.cos(ang), jnp.sin(ang)


def apply_rope(x, cos, sin):
    xf = x.astype(jnp.float32)
    x1, x2 = jnp.split(xf, 2, axis=-1)
    return jnp.concatenate([x1 * cos - x2 * sin, x1 * sin + x2 * cos], axis=-1).astype(x.dtype)


def multiscale_pool(u):
    B, S, _ = u.shape
    uf = u.astype(jnp.float32).reshape(B, S, POOL_GROUPS, POOL_GROUP_DIM)
    cs = jnp.pad(jnp.cumsum(uf, axis=1), ((0, 0), (1, 0), (0, 0), (0, 0)))
    t = jnp.arange(S)
    win = jnp.array(POOL_WINDOWS, dtype=jnp.int32)
    lo = jnp.maximum(t[:, None] + 1 - win[None, :], 0)
    gidx = jnp.arange(POOL_GROUPS)[None, :]
    win_sum = cs[:, 1:] - cs[:, lo, gidx, :]
    count = jnp.minimum(t[:, None] + 1, win[None, :]).astype(jnp.float32)
    return win_sum / count[None, :, :, None] - uf


def mla_branch(c_q, c_kv, k_pe_raw, q_norm_g, w_uq, kv_norm_g, w_ukv, w_mla_o, cos, sin):
    B, S, _ = c_q.shape
    q = (rms_norm(c_q, q_norm_g) @ w_uq).reshape(B, S, N_HEADS, QK_NOPE_DIM + QK_ROPE_DIM)
    q_nope, q_pe = q[..., :QK_NOPE_DIM], q[..., QK_NOPE_DIM:]
    q_pe = apply_rope(q_pe, cos[None, :, None, :], sin[None, :, None, :])
    kv = (rms_norm(c_kv, kv_norm_g) @ w_ukv).reshape(B, S, N_HEADS, QK_NOPE_DIM + V_HEAD_DIM)
    k_nope, v = kv[..., :QK_NOPE_DIM], kv[..., QK_NOPE_DIM:]
    k_pe = apply_rope(k_pe_raw, cos[None], sin[None])
    scale = (QK_NOPE_DIM + QK_ROPE_DIM) ** -0.5
    key_chunk = jnp.arange(S) // CHUNK

    def query_block(i):
        qs = i * Q_BLOCK
        qn = lax.dynamic_slice_in_dim(q_nope, qs, Q_BLOCK, axis=1)
        qp = lax.dynamic_slice_in_dim(q_pe, qs, Q_BLOCK, axis=1)
        s = (jnp.einsum('bqhd,bkhd->bhqk', qn, k_nope)
             + jnp.einsum('bqhd,bkd->bhqk', qp, k_pe)).astype(jnp.float32) * scale
        q_chunk = (qs + jnp.arange(Q_BLOCK)) // CHUNK
        mask = key_chunk[None, :] <= q_chunk[:, None]
        p = jax.nn.softmax(jnp.where(mask[None, None], s, -jnp.inf), axis=-1).astype(v.dtype)
        return jnp.einsum('bhqk,bkhd->bqhd', p, v)

    o = lax.map(query_block, jnp.arange(S // Q_BLOCK))
    o = jnp.transpose(o, (1, 0, 2, 3, 4)).reshape(B, S, N_HEADS * V_HEAD_DIM)
    return o @ w_mla_o


def hier_moe(h, w_rg, b_rg, w_re, b_re, w_gate, w_up, w_down):
    B, S, D = h.shape
    N = B * S
    t = h.reshape(N, D)
    g_prob = jax.nn.softmax((t @ w_rg).astype(jnp.float32) + b_rg.astype(jnp.float32), axis=-1)
    g_idx = jnp.argmax(g_prob, axis=-1)
    g_w = jnp.max(g_prob, axis=-1)
    e_logits = ((t @ w_re).astype(jnp.float32) + b_re.astype(jnp.float32)).reshape(N, N_GROUPS, EXPERTS_PER_GROUP)
    e_logits = jnp.take_along_axis(e_logits, g_idx[:, None, None], axis=1)[:, 0]
    top_p, top_i = lax.top_k(jax.nn.softmax(e_logits, axis=-1), TOP_K)
    gate = g_w[:, None] * top_p / jnp.sum(top_p, axis=-1, keepdims=True)
    expert = g_idx[:, None] * EXPERTS_PER_GROUP + top_i

    A = N * TOP_K
    n_blocks = -(-(A + N_EXPERTS * (MOE_BLOCK - 1)) // MOE_BLOCK)
    R = n_blocks * MOE_BLOCK
    flat_e = expert.reshape(A)
    flat_tok = jnp.repeat(jnp.arange(N), TOP_K)
    flat_w = gate.reshape(A)
    order = jnp.argsort(flat_e)
    sorted_e = flat_e[order]
    counts = jnp.bincount(flat_e, length=N_EXPERTS)
    padded = ((counts + MOE_BLOCK - 1) // MOE_BLOCK) * MOE_BLOCK
    pad_end = jnp.cumsum(padded)
    pad_start = pad_end - padded
    start = jnp.cumsum(counts) - counts
    dest = pad_start[sorted_e] + (jnp.arange(A) - start[sorted_e])
    row_tok = jnp.zeros((R,), jnp.int32).at[dest].set(flat_tok[order].astype(jnp.int32))
    row_w = jnp.zeros((R,), jnp.float32).at[dest].set(flat_w[order])
    block_e = jnp.minimum(jnp.searchsorted(pad_end, jnp.arange(n_blocks) * MOE_BLOCK, side='right'), N_EXPERTS - 1)

    def expert_block(args):
        tok, e = args
        xb = t[tok]
        hid = jax.nn.silu(xb @ w_gate[e]) * (xb @ w_up[e])
        return hid @ w_down[e]

    y = lax.map(expert_block, (row_tok.reshape(n_blocks, MOE_BLOCK), block_e)).reshape(R, D)
    out = jnp.zeros((N, D), jnp.float32).at[row_tok].add(y.astype(jnp.float32) * row_w[:, None])
    return out.astype(h.dtype).reshape(B, S, D)


def setup_inputs(seed: int = 0) -> dict:
    key = jax.random.key(seed)
    ks = jax.random.split(key, 24)
    L = DEPTH
    nrm = lambda k, shape, fan_in, s=1.0: jax.random.normal(k, shape, jnp.float32) * (s * fan_in ** -0.5)
    gain = lambda k, shape: 1.0 + 0.05 * jax.random.normal(k, shape, jnp.float32)
    small = lambda k, shape, s: s * jax.random.normal(k, shape, jnp.float32)
    return {
        "x": jax.random.normal(ks[0], (BATCH, SEQ, D_MODEL), jnp.float32),
        "w_in": nrm(ks[1], (L, D_MODEL, IN_COLS), D_MODEL),
        "pool_mix_w": nrm(ks[2], (L, POOL_GROUPS, POOL_GROUP_DIM, POOL_OUT_GROUP_DIM), POOL_GROUP_DIM),
        "pool_scale": gain(ks[3], (L, D_MODEL)),
        "q_norm_g": gain(ks[4], (L, Q_LORA_RANK)),
        "w_uq": nrm(ks[5], (L, Q_LORA_RANK, N_HEADS * (QK_NOPE_DIM + QK_ROPE_DIM)), Q_LORA_RANK),
        "kv_norm_g": gain(ks[6], (L, KV_LORA_RANK)),
        "w_ukv": nrm(ks[7], (L, KV_LORA_RANK, N_HEADS * (QK_NOPE_DIM + V_HEAD_DIM)), KV_LORA_RANK),
        "w_mla_o": nrm(ks[8], (L, N_HEADS * V_HEAD_DIM, D_MODEL), N_HEADS * V_HEAD_DIM),
        "w_out": nrm(ks[9], (L, D_MODEL, D_MODEL), D_MODEL, DEEPNORM_BETA),
        "ln1_g": gain(ks[10], (L, D_MODEL)),
        "ln1_b": small(ks[11], (L, D_MODEL), 0.02),
        "w_router_group": nrm(ks[12], (L, D_MODEL, N_GROUPS), D_MODEL),
        "b_router_group": small(ks[13], (L, N_GROUPS), 0.01),
        "w_router_expert": nrm(ks[14], (L, D_MODEL, N_EXPERTS), D_MODEL),
        "b_router_expert": small(ks[15], (L, N_EXPERTS), 0.01),
        "w_gate": nrm(ks[16], (L, N_EXPERTS, D_MODEL, D_EXPERT), D_MODEL),
        "w_up": nrm(ks[17], (L, N_EXPERTS, D_MODEL, D_EXPERT), D_MODEL),
        "w_down": nrm(ks[18], (L, N_EXPERTS, D_EXPERT, D_MODEL), D_EXPERT, DEEPNORM_BETA),
        "ln2_g": gain(ks[19], (L, D_MODEL)),
        "ln2_b": small(ks[20], (L, D_MODEL), 0.02),
    }


def reference(x, w_in, pool_mix_w, pool_scale, q_norm_g, w_uq, kv_norm_g, w_ukv, w_mla_o, w_out,
              ln1_g, ln1_b, w_router_group, b_router_group, w_router_expert, b_router_expert,
              w_gate, w_up, w_down, ln2_g, ln2_b):
    B, S, D = x.shape
    cos, sin = rope_tables(S)
    for l in range(DEPTH):
        proj = x @ w_in[l]
        u_pool, c_q, c_kv, k_pe_raw, gate_logits = jnp.split(proj, IN_SPLITS, axis=-1)
        pooled = multiscale_pool(u_pool).astype(x.dtype)
        y_pool = jnp.einsum('bsgc,gcd->bsgd', pooled, pool_mix_w[l]).reshape(B, S, D) * pool_scale[l]
        y_mla = mla_branch(c_q, c_kv, k_pe_raw, q_norm_g[l], w_uq[l], kv_norm_g[l], w_ukv[l],
                           w_mla_o[l], cos, sin)
        g = jax.nn.sigmoid(gate_logits.astype(jnp.float32)).reshape(B, S, N_BRANCHES, D)
        merged = (g[:, :, 0] * y_pool.astype(jnp.float32) + g[:, :, 1] * y_mla.astype(jnp.float32)).astype(x.dtype)
        x = layer_norm(DEEPNORM_ALPHA * x + merged @ w_out[l], ln1_g[l], ln1_b[l])
        moe = hier_moe(x, w_router_group[l], b_router_group[l], w_router_expert[l], b_router_expert[l],
                       w_gate[l], w_up[l], w_down[l])
        x = layer_norm(DEEPNORM_ALPHA * x + moe, ln2_g[l], ln2_b[l])
    return x
```

```python
import functools
import math

import jax
import jax.numpy as jnp
from jax import lax
from jax.experimental import pallas as pl
from jax.experimental.pallas import tpu as pltpu

D_MODEL = 1024
CHUNK = 64
POOL_WINDOWS = (2, 4, 8, 16)
POOL_GROUPS = len(POOL_WINDOWS)
POOL_WIDTH = D_MODEL // 2
POOL_GROUP_DIM = POOL_WIDTH // POOL_GROUPS
POOL_OUT_GROUP_DIM = D_MODEL // POOL_GROUPS
POOL_HALO = 16
N_HEADS = 8
QK_NOPE_DIM = D_MODEL // 16
QK_ROPE_DIM = D_MODEL // 32
HALF_ROPE = QK_ROPE_DIM // 2
V_HEAD_DIM = D_MODEL // 16
Q_LORA_RANK = 3 * D_MODEL // 8
KV_LORA_RANK = D_MODEL // 4
ROPE_THETA = 10000.0
N_GROUPS = 4
EXPERTS_PER_GROUP = 8
N_EXPERTS = N_GROUPS * EXPERTS_PER_GROUP
TOP_K = 2
D_EXPERT = D_MODEL // 2
NORM_EPS = 1e-5
DEPTH = 1
DEEPNORM_ALPHA = (2.0 * DEPTH) ** 0.25

LANES = 128
SUBLANES = 8
HEAD_PAD = LANES
ROW_TILES = D_MODEL // LANES
GROUP_LANE0 = 64
NEG_BIG = -1e30
VMEM_LIMIT = 56 * 1024 * 1024

TILE_LATENT = 512
TILE_Q = 512
TILE_K = 256
TILE_MIX = 256
TILE_EXPERT = 256
TILE_COMBINE = 256

F32 = jnp.float32
BF16 = jnp.bfloat16


def _dot(a, b):
    return jnp.dot(a, b, preferred_element_type=F32)


def _dot_nt(a, b):
    return lax.dot_general(a, b, (((1,), (1,)), ((), ())), preferred_element_type=F32)


def _dot_tn(a, b):
    return lax.dot_general(a, b, (((0,), (0,)), ((), ())), preferred_element_type=F32)


def _rms(v, g):
    ms = jnp.mean(jnp.square(v), axis=-1, keepdims=True)
    return v * lax.rsqrt(ms + NORM_EPS) * g


def _layer_norm(v, g, b):
    mu = jnp.mean(v, axis=-1, keepdims=True)
    c = v - mu
    var = jnp.mean(jnp.square(c), axis=-1, keepdims=True)
    return c * lax.rsqrt(var + NORM_EPS) * g + b


def _latent_kernel(x_ref, wlat_ref, gq_ref, gkv_ref, wuqT_ref, wuk_ref, wuvT_ref,
                   cosT_ref, sinT_ref, ra_ref, rm_ref, rp_ref,
                   qT_ref, k_ref, vT_ref, *, q_scale):
    xb = x_ref[...].astype(BF16)
    lat = _dot(xb, wlat_ref[...])
    c_q = lat[:, :Q_LORA_RANK]
    c_kv = lat[:, Q_LORA_RANK:Q_LORA_RANK + KV_LORA_RANK]
    kpe = lat[:, Q_LORA_RANK + KV_LORA_RANK:]
    qn = _rms(c_q, gq_ref[...]).astype(BF16)
    kvn = _rms(c_kv, gkv_ref[...]).astype(BF16)

    qT = _dot_nt(wuqT_ref[...], qn) * q_scale
    cosT = cosT_ref[...]
    sinT = sinT_ref[...]
    for h in range(N_HEADS):
        r0 = h * HEAD_PAD
        x1 = qT[r0 + QK_NOPE_DIM:r0 + QK_NOPE_DIM + HALF_ROPE]
        x2 = qT[r0 + QK_NOPE_DIM + HALF_ROPE:r0 + QK_NOPE_DIM + QK_ROPE_DIM]
        qT_ref[0, r0:r0 + QK_NOPE_DIM, :] = qT[r0:r0 + QK_NOPE_DIM].astype(BF16)
        qT_ref[0, r0 + QK_NOPE_DIM:r0 + QK_NOPE_DIM + HALF_ROPE, :] = (x1 * cosT - x2 * sinT).astype(BF16)
        qT_ref[0, r0 + QK_NOPE_DIM + HALF_ROPE:r0 + QK_NOPE_DIM + QK_ROPE_DIM, :] = (
            x1 * sinT + x2 * cosT).astype(BF16)
        qT_ref[0, r0 + QK_NOPE_DIM + QK_ROPE_DIM:r0 + HEAD_PAD, :] = jnp.zeros(
            (HEAD_PAD - QK_NOPE_DIM - QK_ROPE_DIM, qT.shape[1]), BF16)

    kpe_rot = (kpe * ra_ref[...] + pltpu.roll(kpe, LANES - HALF_ROPE, 1) * rm_ref[...]
               + pltpu.roll(kpe, HALF_ROPE, 1) * rp_ref[...])
    k = _dot(kvn, wuk_ref[...])
    for h in range(N_HEADS):
        k_ref[0, :, h * HEAD_PAD:(h + 1) * HEAD_PAD] = (k[:, h * HEAD_PAD:(h + 1) * HEAD_PAD] + kpe_rot).astype(BF16)
    vT_ref[0] = _dot_nt(wuvT_ref[...], kvn).astype(BF16)


def _attn_kernel(qT_ref, k_ref, vT_ref, oT_ref):
    qi = pl.program_id(2)
    qT = qT_ref[0]
    tq = qT.shape[1]

    def block(j, carry, masked):
        m, l, acc = carry
        k0 = pl.multiple_of(j * TILE_K, TILE_K)
        kb = k_ref[0, pl.ds(k0, TILE_K), :]
        s = _dot(kb, qT)
        if masked:
            key_chunk = (k0 + lax.broadcasted_iota(jnp.int32, s.shape, 0)) // CHUNK
            q_chunk = (qi * tq + lax.broadcasted_iota(jnp.int32, s.shape, 1)) // CHUNK
            s = jnp.where(key_chunk <= q_chunk, s, NEG_BIG)
        m_new = jnp.maximum(m, jnp.max(s, axis=0, keepdims=True))
        alpha = jnp.exp2(m - m_new)
        p = jnp.exp2(s - m_new)
        l = alpha * l + jnp.sum(p, axis=0, keepdims=True)
        vb = vT_ref[0, :, pl.ds(k0, TILE_K)]
        acc = alpha * acc + _dot(vb, p.astype(BF16))
        return m_new, l, acc

    carry = (jnp.full((1, tq), NEG_BIG, F32), jnp.zeros((1, tq), F32), jnp.zeros((V_HEAD_DIM, tq), F32))
    blocks_per_q = tq // TILE_K
    carry = lax.fori_loop(0, qi * blocks_per_q, lambda j, c: block(j, c, False), carry)
    for d in range(blocks_per_q):
        carry = block(qi * blocks_per_q + d, carry, True)
    _, l, acc = carry
    oT_ref[0] = (acc / l).astype(BF16)


def _mix_kernel(x_ref, oT_ref, wpool_ref, wgate_ref, mixw_ref, pscale_ref, wo_ref, wout_ref,
                g1_ref, b1_ref, wr_ref, br_ref, tri_ref,
                htm_ref, route_ref, cnt_ref, ext_ref, base_ref):
    b = pl.program_id(0)
    si = pl.program_id(1)
    tm = x_ref.shape[0]

    @pl.when(jnp.logical_and(b == 0, si == 0))
    def _():
        base_ref[...] = jnp.zeros_like(base_ref)

    @pl.when(si == 0)
    def _():
        ext_ref[0:POOL_HALO, :] = jnp.zeros((POOL_HALO, POOL_WIDTH), F32)

    x = x_ref[...]
    xb = x.astype(BF16)
    u = _dot(xb, wpool_ref[...])
    ext_ref[POOL_HALO:POOL_HALO + tm, :] = u

    pos1 = si * tm + lax.broadcasted_iota(jnp.int32, (tm, POOL_GROUP_DIM), 0) + 1
    y_parts = []
    for g, win in enumerate(POOL_WINDOWS):
        c0 = g * POOL_GROUP_DIM
        ws = u[:, c0:c0 + POOL_GROUP_DIM]
        for kk in range(1, win):
            ws = ws + ext_ref[POOL_HALO - kk:POOL_HALO - kk + tm, c0:c0 + POOL_GROUP_DIM]
        count = jnp.minimum(pos1, win).astype(F32)
        pooled = ws / count - u[:, c0:c0 + POOL_GROUP_DIM]
        y_parts.append(_dot(pooled.astype(BF16), mixw_ref[g]))
    ext_ref[0:POOL_HALO, :] = ext_ref[tm:tm + POOL_HALO, :]
    y_pool = jnp.concatenate(y_parts, axis=-1) * pscale_ref[...]

    glog = _dot(xb, wgate_ref[...])
    y_mla = _dot_tn(oT_ref[0], wo_ref[...])
    merged = (jax.nn.sigmoid(glog[:, :D_MODEL]) * y_pool + jax.nn.sigmoid(glog[:, D_MODEL:]) * y_mla)
    r = DEEPNORM_ALPHA * x + _dot(merged.astype(BF16), wout_ref[...])
    h = _layer_norm(r, g1_ref[...], b1_ref[...])
    for j in range(ROW_TILES):
        htm_ref[pl.ds(j, tm, stride=ROW_TILES), :] = h[:, j * LANES:(j + 1) * LANES]

    logits = _dot(h.astype(BF16), wr_ref[...]) + br_ref[...]
    lane_i = lax.broadcasted_iota(jnp.int32, logits.shape, 1)
    lane = lane_i.astype(F32)
    big = float(LANES)
    is_group = jnp.logical_and(lane_i >= GROUP_LANE0, lane_i < GROUP_LANE0 + N_GROUPS)
    gl = jnp.where(is_group, logits, NEG_BIG)
    gmax = jnp.max(gl, axis=-1, keepdims=True)
    g_w = 1.0 / jnp.sum(jnp.exp(gl - gmax), axis=-1, keepdims=True)
    g_idx = jnp.min(jnp.where(gl == gmax, lane - GROUP_LANE0, big), axis=-1, keepdims=True)
    lane_group = (lane_i // EXPERTS_PER_GROUP).astype(F32)
    in_group = jnp.logical_and(lane_i < N_EXPERTS, lane_group == g_idx)
    el = jnp.where(in_group, logits, NEG_BIG)
    e1max = jnp.max(el, axis=-1, keepdims=True)
    i1 = jnp.min(jnp.where(el == e1max, lane, big), axis=-1, keepdims=True)
    el2 = jnp.where(lane == i1, NEG_BIG, el)
    e2max = jnp.max(el2, axis=-1, keepdims=True)
    i2 = jnp.min(jnp.where(el2 == e2max, lane, big), axis=-1, keepdims=True)
    ratio = jnp.exp(e2max - e1max)
    gate1 = g_w / (1.0 + ratio)
    gate2 = g_w * ratio / (1.0 + ratio)

    hit1 = lane == i1
    hit2 = lane == i2
    onehot = jnp.where(jnp.logical_or(hit1, hit2), 1.0, 0.0)
    before = _dot(tri_ref[...], onehot.astype(BF16)) + base_ref[0:1, :]
    rank1 = jnp.sum(jnp.where(hit1, before, 0.0), axis=-1, keepdims=True)
    rank2 = jnp.sum(jnp.where(hit2, before, 0.0), axis=-1, keepdims=True)
    new_base = base_ref[0:1, :] + jnp.sum(onehot, axis=0, keepdims=True)
    base_ref[...] = jnp.broadcast_to(new_base, base_ref.shape)
    cnt_ref[...] = jnp.broadcast_to(new_base, cnt_ref.shape)

    route = jnp.where(lane_i == 0, i1, 0.0)
    route = jnp.where(lane_i == 1, i2, route)
    route = jnp.where(lane_i == 2, gate1, route)
    route = jnp.where(lane_i == 3, gate2, route)
    route = jnp.where(lane_i == 4, rank1, route)
    route = jnp.where(lane_i == 5, rank2, route)
    route_ref[...] = route


def _gather_rows(idx_ref, src_hbm, dst_ref, slot, sem, n_rows, unroll=8):
    def body(c, _):
        for u in range(unroll):
            r = c * unroll + u
            src0 = pl.multiple_of(idx_ref[0, 0, r] * ROW_TILES, ROW_TILES)
            pltpu.make_async_copy(src_hbm.at[pl.ds(src0, ROW_TILES), :],
                                  dst_ref.at[slot, pl.ds(r * ROW_TILES, ROW_TILES), :],
                                  sem.at[slot]).start()
        return 0
    lax.fori_loop(0, n_rows // unroll, body, 0)


def _wait_rows(dst_ref, slot, sem):
    pltpu.make_async_copy(dst_ref.at[slot], dst_ref.at[slot], sem.at[slot]).wait()


def _expert_kernel(be_ref, nused_ref, tok_ref, tokn_ref, htm_hbm, wg_ref, wu_ref, wd_ref,
                   y_ref, xbuf, wgb, wub, wdb, sem):
    i = pl.program_id(0)
    n_used = nused_ref[0]
    slot = i % 2
    tb = TILE_EXPERT

    @pl.when(jnp.logical_and(i == 0, n_used > 0))
    def _():
        _gather_rows(tok_ref, htm_hbm, xbuf, 0, sem, tb)

    @pl.when(i + 1 < n_used)
    def _():
        _gather_rows(tokn_ref, htm_hbm, xbuf, 1 - slot, sem, tb)

    changed = jnp.logical_or(i == 0, be_ref[i] != be_ref[jnp.maximum(i - 1, 0)])

    @pl.when(jnp.logical_and(changed, i < n_used))
    def _():
        wgb[...] = wg_ref[0].astype(BF16)
        wub[...] = wu_ref[0].astype(BF16)
        wdb[...] = wd_ref[0].astype(BF16)

    @pl.when(i < n_used)
    def _():
        _wait_rows(xbuf, slot, sem)
        xs = jnp.concatenate(
            [xbuf[slot, pl.ds(j, tb, stride=ROW_TILES), :] for j in range(ROW_TILES)], axis=-1).astype(BF16)
        gate = _dot(xs, wgb[...])
        up = _dot(xs, wub[...])
        hid = (jax.nn.silu(gate) * up).astype(BF16)
        y = _dot(hid, wdb[...])
        for j in range(ROW_TILES):
            y_ref[pl.ds(j, tb, stride=ROW_TILES), :] = y[:, j * LANES:(j + 1) * LANES]

    @pl.when(i >= n_used)
    def _():
        y_ref[...] = jnp.zeros_like(y_ref)


def _combine_kernel(dst_ref, dstn_ref, htm_ref, route_ref, ytm_hbm, g2_ref, b2_ref, o_ref, gbuf, sem):
    i = pl.program_id(0)
    n = pl.num_programs(0)
    slot = i % 2
    tf = TILE_COMBINE

    @pl.when(i == 0)
    def _():
        _gather_rows(dst_ref, ytm_hbm, gbuf, 0, sem, TOP_K * tf)

    @pl.when(i + 1 < n)
    def _():
        _gather_rows(dstn_ref, ytm_hbm, gbuf, 1 - slot, sem, TOP_K * tf)

    _wait_rows(gbuf, slot, sem)
    route = route_ref[...]
    gate1 = route[:, 2:3]
    gate2 = route[:, 3:4]
    parts = []
    for j in range(ROW_TILES):
        hj = htm_ref[pl.ds(j, tf, stride=ROW_TILES), :]
        y1 = gbuf[slot, pl.ds(j, tf, stride=ROW_TILES), :]
        y2 = gbuf[slot, pl.ds(tf * ROW_TILES + j, tf, stride=ROW_TILES), :]
        parts.append(DEEPNORM_ALPHA * hj + (gate1 * y1 + gate2 * y2))
    z = jnp.concatenate(parts, axis=-1)
    o_ref[...] = _layer_norm(z, g2_ref[...], b2_ref[...])


def _rope_tables(seq):
    inv = 1.0 / (ROPE_THETA ** (jnp.arange(0, QK_ROPE_DIM, 2, dtype=F32) / QK_ROPE_DIM))
    ang = jnp.arange(seq, dtype=F32)[:, None] * inv[None, :]
    cos, sin = jnp.cos(ang), jnp.sin(ang)
    zeros = jnp.zeros((seq, QK_NOPE_DIM), F32)
    pad = jnp.zeros((seq, HEAD_PAD - QK_NOPE_DIM - QK_ROPE_DIM), F32)
    z16 = jnp.zeros((seq, HALF_ROPE), F32)
    rot_a = jnp.concatenate([zeros, cos, cos, pad], axis=1)
    rot_m = jnp.concatenate([zeros, -sin, z16, pad], axis=1)
    rot_p = jnp.concatenate([zeros, z16, sin, pad], axis=1)
    return cos.T, sin.T, rot_a, rot_m, rot_p


def _full(shape):
    return pl.BlockSpec(shape, lambda *_: (0,) * len(shape))


def _params(sem):
    return pltpu.CompilerParams(dimension_semantics=sem, vmem_limit_bytes=VMEM_LIMIT)


def kernel(x, w_in, pool_mix_w, pool_scale, q_norm_g, w_uq, kv_norm_g, w_ukv, w_mla_o, w_out, ln1_g, ln1_b,
           w_router_group, b_router_group, w_router_expert, b_router_expert, w_gate, w_up, w_down, ln2_g, ln2_b):
    B, S, D = x.shape
    assert D == D_MODEL and w_in.shape[0] == DEPTH == 1
    assert S % TILE_LATENT == 0 and S % TILE_Q == 0 and S % TILE_MIX == 0 and TILE_Q % TILE_K == 0
    N = B * S
    assert N % TILE_COMBINE == 0
    H = N_HEADS

    w = w_in[0]
    o1 = POOL_WIDTH
    o2 = o1 + Q_LORA_RANK
    o3 = o2 + KV_LORA_RANK
    o4 = o3 + QK_ROPE_DIM
    w_pool = w[:, :o1].astype(BF16)
    kpe_cols = jnp.pad(w[:, o3:o4], ((0, 0), (QK_NOPE_DIM, HEAD_PAD - QK_NOPE_DIM - QK_ROPE_DIM)))
    w_lat = jnp.concatenate([w[:, o1:o3], kpe_cols], axis=1).astype(BF16)
    w_gates = w[:, o4:].astype(BF16)
    qd = QK_NOPE_DIM + QK_ROPE_DIM
    wuq = jnp.pad(w_uq[0].reshape(Q_LORA_RANK, H, qd), ((0, 0), (0, 0), (0, HEAD_PAD - qd)))
    wuqT = wuq.reshape(Q_LORA_RANK, H * HEAD_PAD).T.astype(BF16)
    wukv = w_ukv[0].reshape(KV_LORA_RANK, H, QK_NOPE_DIM + V_HEAD_DIM)
    wuk = jnp.pad(wukv[:, :, :QK_NOPE_DIM], ((0, 0), (0, 0), (0, HEAD_PAD - QK_NOPE_DIM)))
    wuk = wuk.reshape(KV_LORA_RANK, H * HEAD_PAD).astype(BF16)
    wuvT = wukv[:, :, QK_NOPE_DIM:].reshape(KV_LORA_RANK, H * V_HEAD_DIM).T.astype(BF16)
    w_r = jnp.zeros((D, LANES), F32)
    w_r = w_r.at[:, :N_EXPERTS].set(w_router_expert[0]).at[:, GROUP_LANE0:GROUP_LANE0 + N_GROUPS].set(
        w_router_group[0]).astype(BF16)
    b_r = jnp.zeros((1, LANES), F32)
    b_r = b_r.at[0, :N_EXPERTS].set(b_router_expert[0]).at[0, GROUP_LANE0:GROUP_LANE0 + N_GROUPS].set(
        b_router_group[0])
    cosT, sinT, rot_a, rot_m, rot_p = _rope_tables(S)
    q_scale = (QK_NOPE_DIM + QK_ROPE_DIM) ** -0.5 * math.log2(math.e)
    x2 = x.reshape(N, D)

    ta = TILE_LATENT
    nsa = S // ta
    qT, k, vT = pl.pallas_call(
        functools.partial(_latent_kernel, q_scale=q_scale),
        grid=(B, nsa),
        in_specs=[
            pl.BlockSpec((ta, D), lambda b, s: (b * nsa + s, 0)),
            _full(w_lat.shape), _full((1, Q_LORA_RANK)), _full((1, KV_LORA_RANK)),
            _full(wuqT.shape), _full(wuk.shape), _full(wuvT.shape),
            pl.BlockSpec((HALF_ROPE, ta), lambda b, s: (0, s)),
            pl.BlockSpec((HALF_ROPE, ta), lambda b, s: (0, s)),
            pl.BlockSpec((ta, LANES), lambda b, s: (s, 0)),
            pl.BlockSpec((ta, LANES), lambda b, s: (s, 0)),
            pl.BlockSpec((ta, LANES), lambda b, s: (s, 0)),
        ],
        out_specs=[
            pl.BlockSpec((1, H * HEAD_PAD, ta), lambda b, s: (b, 0, s)),
            pl.BlockSpec((1, ta, H * HEAD_PAD), lambda b, s: (b, s, 0)),
            pl.BlockSpec((1, H * V_HEAD_DIM, ta), lambda b, s: (b, 0, s)),
        ],
        out_shape=[
            jax.ShapeDtypeStruct((B, H * HEAD_PAD, S), BF16),
            jax.ShapeDtypeStruct((B, S, H * HEAD_PAD), BF16),
            jax.ShapeDtypeStruct((B, H * V_HEAD_DIM, S), BF16),
        ],
        compiler_params=_params(("parallel", "parallel")),
    )(x2, w_lat, q_norm_g[0][None], kv_norm_g[0][None], wuqT, wuk, wuvT, cosT, sinT, rot_a, rot_m, rot_p)

    tq = TILE_Q
    oT = pl.pallas_call(
        _attn_kernel,
        grid=(B, H, S // tq),
        in_specs=[
            pl.BlockSpec((1, HEAD_PAD, tq), lambda b, h, q: (b, h, q)),
            pl.BlockSpec((1, S, HEAD_PAD), lambda b, h, q: (b, 0, h)),
            pl.BlockSpec((1, V_HEAD_DIM, S), lambda b, h, q: (b, h, 0)),
        ],
        out_specs=pl.BlockSpec((1, V_HEAD_DIM, tq), lambda b, h, q: (b, h, q)),
        out_shape=jax.ShapeDtypeStruct((B, H * V_HEAD_DIM, S), BF16),
        compiler_params=_params(("parallel", "parallel", "arbitrary")),
    )(qT, k, vT)

    tm = TILE_MIX
    nsm = S // tm
    tri = (jnp.arange(tm)[:, None] > jnp.arange(tm)[None, :]).astype(BF16)
    htm, route, counts = pl.pallas_call(
        _mix_kernel,
        grid=(B, nsm),
        in_specs=[
            pl.BlockSpec((tm, D), lambda b, s: (b * nsm + s, 0)),
            pl.BlockSpec((1, H * V_HEAD_DIM, tm), lambda b, s: (b, 0, s)),
            _full(w_pool.shape), _full(w_gates.shape), _full(pool_mix_w.shape[1:]), _full((1, D)),
            _full(w_mla_o.shape[1:]), _full(w_out.shape[1:]), _full((1, D)), _full((1, D)),
            _full(w_r.shape), _full(b_r.shape), _full(tri.shape),
        ],
        out_specs=[
            pl.BlockSpec((tm * ROW_TILES, LANES), lambda b, s: (b * nsm + s, 0)),
            pl.BlockSpec((tm, LANES), lambda b, s: (b * nsm + s, 0)),
            _full((SUBLANES, LANES)),
        ],
        out_shape=[
            jax.ShapeDtypeStruct((N * ROW_TILES, LANES), F32),
            jax.ShapeDtypeStruct((N, LANES), F32),
            jax.ShapeDtypeStruct((SUBLANES, LANES), F32),
        ],
        scratch_shapes=[pltpu.VMEM((tm + POOL_HALO, POOL_WIDTH), F32), pltpu.VMEM((SUBLANES, LANES), F32)],
        compiler_params=_params(("arbitrary", "arbitrary")),
    )(x2, oT, w_pool, w_gates, pool_mix_w[0].astype(BF16), pool_scale[0][None], w_mla_o[0].astype(BF16),
      w_out[0].astype(BF16), ln1_g[0][None], ln1_b[0][None], w_r, b_r, tri)

    tb = TILE_EXPERT
    A = N * TOP_K
    n_blocks = -(-(A + N_EXPERTS * (tb - 1)) // tb)
    R = n_blocks * tb
    e_idx = route[:, 0:2].astype(jnp.int32)
    rank = route[:, 4:6].astype(jnp.int32)
    cnt = counts[0, :N_EXPERTS].astype(jnp.int32)
    padded = ((cnt + tb - 1) // tb) * tb
    pad_end = jnp.cumsum(padded)
    pad_start = pad_end - padded
    dest = pad_start[e_idx] + rank
    tok = jnp.broadcast_to(jnp.arange(N, dtype=jnp.int32)[:, None], (N, TOP_K))
    row_tok = jnp.zeros((R,), jnp.int32).at[dest.reshape(A)].set(tok.reshape(A))
    block_e = jnp.minimum(jnp.searchsorted(pad_end, jnp.arange(n_blocks) * tb, side='right'),
                          N_EXPERTS - 1).astype(jnp.int32)
    n_used = (pad_end[-1] // tb).astype(jnp.int32)[None]
    row_tok3 = row_tok.reshape(n_blocks, 1, tb)

    ytm = pl.pallas_call(
        _expert_kernel,
        grid_spec=pltpu.PrefetchScalarGridSpec(
            num_scalar_prefetch=2,
            grid=(n_blocks,),
            in_specs=[
                pl.BlockSpec((1, 1, tb), lambda i, be, nu: (i, 0, 0), memory_space=pltpu.SMEM),
                pl.BlockSpec((1, 1, tb), lambda i, be, nu: (jnp.minimum(i + 1, n_blocks - 1), 0, 0),
                             memory_space=pltpu.SMEM),
                pl.BlockSpec(memory_space=pl.ANY),
                pl.BlockSpec((1, D, D_EXPERT), lambda i, be, nu: (be[i], 0, 0)),
                pl.BlockSpec((1, D, D_EXPERT), lambda i, be, nu: (be[i], 0, 0)),
                pl.BlockSpec((1, D_EXPERT, D), lambda i, be, nu: (be[i], 0, 0)),
            ],
            out_specs=pl.BlockSpec((tb * ROW_TILES, LANES), lambda i, be, nu: (i, 0)),
            scratch_shapes=[
                pltpu.VMEM((2, tb * ROW_TILES, LANES), F32),
                pltpu.VMEM((D, D_EXPERT), BF16), pltpu.VMEM((D, D_EXPERT), BF16), pltpu.VMEM((D_EXPERT, D), BF16),
                pltpu.SemaphoreType.DMA((2,)),
            ],
        ),
        out_shape=jax.ShapeDtypeStruct((R * ROW_TILES, LANES), F32),
        compiler_params=_params(("arbitrary",)),
    )(block_e, n_used, row_tok3, row_tok3, htm, w_gate[0], w_up[0], w_down[0])

    tf = TILE_COMBINE
    nf = N // tf
    dest3 = dest.reshape(nf, tf, TOP_K).transpose(0, 2, 1).reshape(nf, 1, TOP_K * tf)
    out = pl.pallas_call(
        _combine_kernel,
        grid=(nf,),
        in_specs=[
            pl.BlockSpec((1, 1, TOP_K * tf), lambda i: (i, 0, 0), memory_space=pltpu.SMEM),
            pl.BlockSpec((1, 1, TOP_K * tf), lambda i: (jnp.minimum(i + 1, nf - 1), 0, 0), memory_space=pltpu.SMEM),
            pl.BlockSpec((tf * ROW_TILES, LANES), lambda i: (i, 0)),
            pl.BlockSpec((tf, LANES), lambda i: (i, 0)),
            pl.BlockSpec(memory_space=pl.ANY),
            _full((1, D)), _full((1, D)),
        ],
        out_specs=pl.BlockSpec((tf, D), lambda i: (i, 0)),
        out_shape=jax.ShapeDtypeStruct((N, D), F32),
        scratch_shapes=[pltpu.VMEM((2, TOP_K * tf * ROW_TILES, LANES), F32), pltpu.SemaphoreType.DMA((2,))],
        compiler_params=_params(("arbitrary",)),
    )(dest3, dest3, htm, route, ytm, ln2_g[0][None], ln2_b[0][None])
    return out.reshape(B, S, D)
```

```python
import functools
import math

import jax
import jax.numpy as jnp
from jax import lax
from jax.experimental import pallas as pl
from jax.experimental.pallas import tpu as pltpu

D_MODEL = 1024
CHUNK = 64
POOL_WINDOWS = (2, 4, 8, 16)
POOL_GROUPS = len(POOL_WINDOWS)
POOL_WIDTH = D_MODEL // 2
POOL_GROUP_DIM = POOL_WIDTH // POOL_GROUPS
POOL_OUT_GROUP_DIM = D_MODEL // POOL_GROUPS
POOL_HALO = 16
N_HEADS = 8
QK_NOPE_DIM = D_MODEL // 16
QK_ROPE_DIM = D_MODEL // 32
HALF_ROPE = QK_ROPE_DIM // 2
V_HEAD_DIM = D_MODEL // 16
V_ROWS = V_HEAD_DIM + 16
Q_LORA_RANK = 3 * D_MODEL // 8
KV_LORA_RANK = D_MODEL // 4
ROPE_THETA = 10000.0
N_GROUPS = 4
EXPERTS_PER_GROUP = 8
N_EXPERTS = N_GROUPS * EXPERTS_PER_GROUP
TOP_K = 2
D_EXPERT = D_MODEL // 2
NORM_EPS = 1e-5
DEPTH = 1
DEEPNORM_ALPHA = (2.0 * DEPTH) ** 0.25

LANES = 128
SUBLANES = 8
HEAD_PAD = LANES
ROW_TILES = D_MODEL // LANES
GROUP_LANE0 = 64
NEG_BIG = -1e30
VMEM_LIMIT = 56 * 1024 * 1024

TILE_LATENT = 512
TILE_Q = 512
TILE_K = 256
TILE_MIX = 256
TILE_EXPERT = 256
TILE_COMBINE = 256

F32 = jnp.float32
BF16 = jnp.bfloat16


def _dot(a, b):
    return jnp.dot(a, b, preferred_element_type=F32)


def _dot_nt(a, b):
    return lax.dot_general(a, b, (((1,), (1,)), ((), ())), preferred_element_type=F32)


def _dot_tn(a, b):
    return lax.dot_general(a, b, (((0,), (0,)), ((), ())), preferred_element_type=F32)


def _rms(v, g):
    ms = jnp.mean(jnp.square(v), axis=-1, keepdims=True)
    return v * lax.rsqrt(ms + NORM_EPS) * g


def _layer_norm(v, g, b):
    mu = jnp.mean(v, axis=-1, keepdims=True)
    c = v - mu
    var = jnp.mean(jnp.square(c), axis=-1, keepdims=True)
    return c * lax.rsqrt(var + NORM_EPS) * g + b


def _latent_kernel(x_ref, wlat_ref, gq_ref, gkv_ref, wuqT_ref, wuk_ref, wuvT_ref,
                   cosT_ref, sinT_ref, ra_ref, rm_ref, rp_ref,
                   qT_ref, k_ref, vT_ref, *, q_scale):
    xb = x_ref[...].astype(BF16)
    lat = _dot(xb, wlat_ref[...])
    c_q = lat[:, :Q_LORA_RANK]
    c_kv = lat[:, Q_LORA_RANK:Q_LORA_RANK + KV_LORA_RANK]
    kpe = lat[:, Q_LORA_RANK + KV_LORA_RANK:]
    qn = _rms(c_q, gq_ref[...]).astype(BF16)
    kvn = _rms(c_kv, gkv_ref[...]).astype(BF16)

    qT = _dot_nt(wuqT_ref[...], qn) * q_scale
    cosT = cosT_ref[...]
    sinT = sinT_ref[...]
    for h in range(N_HEADS):
        r0 = h * HEAD_PAD
        x1 = qT[r0 + QK_NOPE_DIM:r0 + QK_NOPE_DIM + HALF_ROPE]
        x2 = qT[r0 + QK_NOPE_DIM + HALF_ROPE:r0 + QK_NOPE_DIM + QK_ROPE_DIM]
        qT_ref[0, r0:r0 + QK_NOPE_DIM, :] = qT[r0:r0 + QK_NOPE_DIM].astype(BF16)
        qT_ref[0, r0 + QK_NOPE_DIM:r0 + QK_NOPE_DIM + HALF_ROPE, :] = (x1 * cosT - x2 * sinT).astype(BF16)
        qT_ref[0, r0 + QK_NOPE_DIM + HALF_ROPE:r0 + QK_NOPE_DIM + QK_ROPE_DIM, :] = (
            x1 * sinT + x2 * cosT).astype(BF16)
        qT_ref[0, r0 + QK_NOPE_DIM + QK_ROPE_DIM:r0 + HEAD_PAD, :] = jnp.zeros(
            (HEAD_PAD - QK_NOPE_DIM - QK_ROPE_DIM, qT.shape[1]), BF16)

    kpe_rot = (kpe * ra_ref[...] + pltpu.roll(kpe, LANES - HALF_ROPE, 1) * rm_ref[...]
               + pltpu.roll(kpe, HALF_ROPE, 1) * rp_ref[...])
    k = _dot(kvn, wuk_ref[...])
    for h in range(N_HEADS):
        k_ref[0, :, h * HEAD_PAD:(h + 1) * HEAD_PAD] = (k[:, h * HEAD_PAD:(h + 1) * HEAD_PAD] + kpe_rot).astype(BF16)
    vT = _dot_nt(wuvT_ref[...], kvn)
    row = lax.broadcasted_iota(jnp.int32, vT.shape, 0) % V_ROWS
    vT_ref[0] = jnp.where(row == V_HEAD_DIM, 1.0, vT).astype(BF16)


def _attn_kernel(qT_ref, k_ref, vT_ref, bias_ref, oT_ref, s0, s1, p0, p1):
    qi = pl.program_id(2)
    qT = qT_ref[0]
    tq = qT.shape[1]

    def scores(j):
        k0 = pl.multiple_of(j * TILE_K, TILE_K)
        return _dot(k_ref[0, pl.ds(k0, TILE_K), :], qT)

    def pv(j, p_ref):
        k0 = pl.multiple_of(j * TILE_K, TILE_K)
        return _dot(vT_ref[0, :, pl.ds(k0, TILE_K)], p_ref[...])

    def col_max(s):
        while s.shape[0] > SUBLANES:
            half = s.shape[0] // 2
            s = jnp.maximum(s[:half], s[half:])
        return jnp.max(s, axis=0, keepdims=True)

    def step(j, s_cur, s_nxt, p_prev, p_cur, carry, bias=None):
        m, acc, alpha_prev = carry
        acc = alpha_prev * acc + pv(jnp.maximum(j - 1, 0), p_prev)
        if s_nxt is not None:
            s_nxt[...] = scores(j + 1)
        s = s_cur[...]
        if bias is not None:
            s = s + bias
        m_new = jnp.maximum(m, col_max(s))
        alpha = jnp.exp2(m - m_new)
        p_cur[...] = jnp.exp2(s - m_new).astype(BF16)
        return m_new, acc, alpha

    p1[...] = jnp.zeros_like(p1)
    s0[...] = scores(0)
    carry = (jnp.full((1, tq), NEG_BIG, F32), jnp.zeros((V_ROWS, tq), F32), jnp.ones((1, tq), F32))

    def pair(t, c):
        c = step(2 * t, s0, s1, p1, p0, c)
        return step(2 * t + 1, s1, s0, p0, p1, c)

    carry = lax.fori_loop(0, qi, pair, carry)
    jd = 2 * qi
    carry = step(jd, s0, s1, p1, p0, carry, bias=bias_ref[0])
    _, acc, alpha = step(jd + 1, s1, None, p0, p1, carry, bias=bias_ref[1])
    acc = alpha * acc + pv(jd + 1, p1)
    oT_ref[0] = (acc[:V_HEAD_DIM] / acc[V_HEAD_DIM:V_HEAD_DIM + 1]).astype(BF16)


def _mix_kernel(x_ref, oT_ref, wpool_ref, wgate_ref, mixw_ref, pscale_ref, wo_ref, wout_ref,
                g1_ref, b1_ref, wr_ref, br_ref, tri_ref,
                htm_ref, route_ref, cnt_ref, ext_ref, base_ref):
    b = pl.program_id(0)
    si = pl.program_id(1)
    tm = x_ref.shape[0]

    @pl.when(jnp.logical_and(b == 0, si == 0))
    def _():
        base_ref[...] = jnp.zeros_like(base_ref)

    @pl.when(si == 0)
    def _():
        ext_ref[0:POOL_HALO, :] = jnp.zeros((POOL_HALO, POOL_WIDTH), F32)

    x = x_ref[...]
    xb = x.astype(BF16)
    u = _dot(xb, wpool_ref[...])
    ext_ref[POOL_HALO:POOL_HALO + tm, :] = u

    pos1 = si * tm + lax.broadcasted_iota(jnp.int32, (tm, POOL_GROUP_DIM), 0) + 1
    y_parts = []
    for g, win in enumerate(POOL_WINDOWS):
        c0 = g * POOL_GROUP_DIM
        ws = u[:, c0:c0 + POOL_GROUP_DIM]
        for kk in range(1, win):
            ws = ws + ext_ref[POOL_HALO - kk:POOL_HALO - kk + tm, c0:c0 + POOL_GROUP_DIM]
        count = jnp.minimum(pos1, win).astype(F32)
        pooled = ws / count - u[:, c0:c0 + POOL_GROUP_DIM]
        y_parts.append(_dot(pooled.astype(BF16), mixw_ref[g]))
    ext_ref[0:POOL_HALO, :] = ext_ref[tm:tm + POOL_HALO, :]
    y_pool = jnp.concatenate(y_parts, axis=-1) * pscale_ref[...]

    glog = _dot(xb, wgate_ref[...])
    y_mla = _dot_tn(oT_ref[0], wo_ref[...])
    merged = (jax.nn.sigmoid(glog[:, :D_MODEL]) * y_pool + jax.nn.sigmoid(glog[:, D_MODEL:]) * y_mla)
    r = DEEPNORM_ALPHA * x + _dot(merged.astype(BF16), wout_ref[...])
    h = _layer_norm(r, g1_ref[...], b1_ref[...])
    for j in range(ROW_TILES):
        htm_ref[pl.ds(j, tm, stride=ROW_TILES), :] = h[:, j * LANES:(j + 1) * LANES]

    logits = _dot(h.astype(BF16), wr_ref[...]) + br_ref[...]
    lane_i = lax.broadcasted_iota(jnp.int32, logits.shape, 1)
    lane = lane_i.astype(F32)
    big = float(LANES)
    is_group = jnp.logical_and(lane_i >= GROUP_LANE0, lane_i < GROUP_LANE0 + N_GROUPS)
    gl = jnp.where(is_group, logits, NEG_BIG)
    gmax = jnp.max(gl, axis=-1, keepdims=True)
    g_w = 1.0 / jnp.sum(jnp.exp(gl - gmax), axis=-1, keepdims=True)
    g_idx = jnp.min(jnp.where(gl == gmax, lane - GROUP_LANE0, big), axis=-1, keepdims=True)
    lane_group = (lane_i // EXPERTS_PER_GROUP).astype(F32)
    in_group = jnp.logical_and(lane_i < N_EXPERTS, lane_group == g_idx)
    el = jnp.where(in_group, logits, NEG_BIG)
    e1max = jnp.max(el, axis=-1, keepdims=True)
    i1 = jnp.min(jnp.where(el == e1max, lane, big), axis=-1, keepdims=True)
    el2 = jnp.where(lane == i1, NEG_BIG, el)
    e2max = jnp.max(el2, axis=-1, keepdims=True)
    i2 = jnp.min(jnp.where(el2 == e2max, lane, big), axis=-1, keepdims=True)
    ratio = jnp.exp(e2max - e1max)
    gate1 = g_w / (1.0 + ratio)
    gate2 = g_w * ratio / (1.0 + ratio)

    hit1 = lane == i1
    hit2 = lane == i2
    onehot = jnp.where(jnp.logical_or(hit1, hit2), 1.0, 0.0)
    before = _dot(tri_ref[...], onehot.astype(BF16)) + base_ref[0:1, :]
    rank1 = jnp.sum(jnp.where(hit1, before, 0.0), axis=-1, keepdims=True)
    rank2 = jnp.sum(jnp.where(hit2, before, 0.0), axis=-1, keepdims=True)
    new_base = base_ref[0:1, :] + jnp.sum(onehot, axis=0, keepdims=True)
    base_ref[...] = jnp.broadcast_to(new_base, base_ref.shape)
    cnt_ref[...] = jnp.broadcast_to(new_base, cnt_ref.shape)

    route = jnp.where(lane_i == 0, i1, 0.0)
    route = jnp.where(lane_i == 1, i2, route)
    route = jnp.where(lane_i == 2, gate1, route)
    route = jnp.where(lane_i == 3, gate2, route)
    route = jnp.where(lane_i == 4, rank1, route)
    route = jnp.where(lane_i == 5, rank2, route)
    route_ref[...] = route


def _gather_rows(idx_ref, src_hbm, dst_ref, slot, sem, n_rows, unroll=8):
    def body(c, _):
        for u in range(unroll):
            r = c * unroll + u
            src0 = pl.multiple_of(idx_ref[0, 0, r] * ROW_TILES, ROW_TILES)
            pltpu.make_async_copy(src_hbm.at[pl.ds(src0, ROW_TILES), :],
                                  dst_ref.at[slot, pl.ds(r * ROW_TILES, ROW_TILES), :],
                                  sem.at[slot]).start()
        return 0
    lax.fori_loop(0, n_rows // unroll, body, 0)


def _wait_rows(dst_ref, slot, sem):
    pltpu.make_async_copy(dst_ref.at[slot], dst_ref.at[slot], sem.at[slot]).wait()


def _expert_kernel(be_ref, nused_ref, tok_ref, tokn_ref, htm_hbm, wg_ref, wu_ref, wd_ref,
                   y_ref, xbuf, wgb, wub, wdb, sem):
    i = pl.program_id(0)
    n_used = nused_ref[0]
    slot = i % 2
    tb = TILE_EXPERT

    @pl.when(jnp.logical_and(i == 0, n_used > 0))
    def _():
        _gather_rows(tok_ref, htm_hbm, xbuf, 0, sem, tb)

    @pl.when(i + 1 < n_used)
    def _():
        _gather_rows(tokn_ref, htm_hbm, xbuf, 1 - slot, sem, tb)

    changed = jnp.logical_or(i == 0, be_ref[i] != be_ref[jnp.maximum(i - 1, 0)])

    @pl.when(jnp.logical_and(changed, i < n_used))
    def _():
        wgb[...] = wg_ref[0].astype(BF16)
        wub[...] = wu_ref[0].astype(BF16)
        wdb[...] = wd_ref[0].astype(BF16)

    @pl.when(i < n_used)
    def _():
        _wait_rows(xbuf, slot, sem)
        xs = jnp.concatenate(
            [xbuf[slot, pl.ds(j, tb, stride=ROW_TILES), :] for j in range(ROW_TILES)], axis=-1).astype(BF16)
        gate = _dot(xs, wgb[...])
        up = _dot(xs, wub[...])
        hid = (jax.nn.silu(gate) * up).astype(BF16)
        y = _dot(hid, wdb[...])
        for j in range(ROW_TILES):
            y_ref[pl.ds(j, tb, stride=ROW_TILES), :] = y[:, j * LANES:(j + 1) * LANES]

    @pl.when(i >= n_used)
    def _():
        y_ref[...] = jnp.zeros_like(y_ref)


def _combine_kernel(dst_ref, dstn_ref, htm_ref, route_ref, ytm_hbm, g2_ref, b2_ref, o_ref, gbuf, sem):
    i = pl.program_id(0)
    n = pl.num_programs(0)
    slot = i % 2
    tf = TILE_COMBINE

    @pl.when(i == 0)
    def _():
        _gather_rows(dst_ref, ytm_hbm, gbuf, 0, sem, TOP_K * tf)

    @pl.when(i + 1 < n)
    def _():
        _gather_rows(dstn_ref, ytm_hbm, gbuf, 1 - slot, sem, TOP_K * tf)

    _wait_rows(gbuf, slot, sem)
    route = route_ref[...]
    gate1 = route[:, 2:3]
    gate2 = route[:, 3:4]
    parts = []
    for j in range(ROW_TILES):
        hj = htm_ref[pl.ds(j, tf, stride=ROW_TILES), :]
        y1 = gbuf[slot, pl.ds(j, tf, stride=ROW_TILES), :]
        y2 = gbuf[slot, pl.ds(tf * ROW_TILES + j, tf, stride=ROW_TILES), :]
        parts.append(DEEPNORM_ALPHA * hj + (gate1 * y1 + gate2 * y2))
    z = jnp.concatenate(parts, axis=-1)
    o_ref[...] = _layer_norm(z, g2_ref[...], b2_ref[...])


def _rope_tables(seq):
    inv = 1.0 / (ROPE_THETA ** (jnp.arange(0, QK_ROPE_DIM, 2, dtype=F32) / QK_ROPE_DIM))
    ang = jnp.arange(seq, dtype=F32)[:, None] * inv[None, :]
    cos, sin = jnp.cos(ang), jnp.sin(ang)
    zeros = jnp.zeros((seq, QK_NOPE_DIM), F32)
    pad = jnp.zeros((seq, HEAD_PAD - QK_NOPE_DIM - QK_ROPE_DIM), F32)
    z16 = jnp.zeros((seq, HALF_ROPE), F32)
    rot_a = jnp.concatenate([zeros, cos, cos, pad], axis=1)
    rot_m = jnp.concatenate([zeros, -sin, z16, pad], axis=1)
    rot_p = jnp.concatenate([zeros, z16, sin, pad], axis=1)
    return cos.T, sin.T, rot_a, rot_m, rot_p


def _full(shape):
    return pl.BlockSpec(shape, lambda *_: (0,) * len(shape))


def _params(sem):
    return pltpu.CompilerParams(dimension_semantics=sem, vmem_limit_bytes=VMEM_LIMIT)


def kernel(x, w_in, pool_mix_w, pool_scale, q_norm_g, w_uq, kv_norm_g, w_ukv, w_mla_o, w_out, ln1_g, ln1_b,
           w_router_group, b_router_group, w_router_expert, b_router_expert, w_gate, w_up, w_down, ln2_g, ln2_b):
    B, S, D = x.shape
    assert D == D_MODEL and w_in.shape[0] == DEPTH == 1
    assert S % TILE_LATENT == 0 and S % TILE_Q == 0 and S % TILE_MIX == 0 and TILE_Q % TILE_K == 0
    N = B * S
    assert N % TILE_COMBINE == 0
    H = N_HEADS

    w = w_in[0]
    o1 = POOL_WIDTH
    o2 = o1 + Q_LORA_RANK
    o3 = o2 + KV_LORA_RANK
    o4 = o3 + QK_ROPE_DIM
    w_pool = w[:, :o1].astype(BF16)
    kpe_cols = jnp.pad(w[:, o3:o4], ((0, 0), (QK_NOPE_DIM, HEAD_PAD - QK_NOPE_DIM - QK_ROPE_DIM)))
    w_lat = jnp.concatenate([w[:, o1:o3], kpe_cols], axis=1).astype(BF16)
    w_gates = w[:, o4:].astype(BF16)
    qd = QK_NOPE_DIM + QK_ROPE_DIM
    wuq = jnp.pad(w_uq[0].reshape(Q_LORA_RANK, H, qd), ((0, 0), (0, 0), (0, HEAD_PAD - qd)))
    wuqT = wuq.reshape(Q_LORA_RANK, H * HEAD_PAD).T.astype(BF16)
    wukv = w_ukv[0].reshape(KV_LORA_RANK, H, QK_NOPE_DIM + V_HEAD_DIM)
    wuk = jnp.pad(wukv[:, :, :QK_NOPE_DIM], ((0, 0), (0, 0), (0, HEAD_PAD - QK_NOPE_DIM)))
    wuk = wuk.reshape(KV_LORA_RANK, H * HEAD_PAD).astype(BF16)
    wuv = jnp.pad(wukv[:, :, QK_NOPE_DIM:], ((0, 0), (0, 0), (0, V_ROWS - V_HEAD_DIM)))
    wuvT = wuv.reshape(KV_LORA_RANK, H * V_ROWS).T.astype(BF16)
    w_r = jnp.zeros((D, LANES), F32)
    w_r = w_r.at[:, :N_EXPERTS].set(w_router_expert[0]).at[:, GROUP_LANE0:GROUP_LANE0 + N_GROUPS].set(
        w_router_group[0]).astype(BF16)
    b_r = jnp.zeros((1, LANES), F32)
    b_r = b_r.at[0, :N_EXPERTS].set(b_router_expert[0]).at[0, GROUP_LANE0:GROUP_LANE0 + N_GROUPS].set(
        b_router_group[0])
    cosT, sinT, rot_a, rot_m, rot_p = _rope_tables(S)
    q_scale = (QK_NOPE_DIM + QK_ROPE_DIM) ** -0.5 * math.log2(math.e)
    x2 = x.reshape(N, D)

    ta = TILE_LATENT
    nsa = S // ta
    qT, k, vT = pl.pallas_call(
        functools.partial(_latent_kernel, q_scale=q_scale),
        grid=(B, nsa),
        in_specs=[
            pl.BlockSpec((ta, D), lambda b, s: (b * nsa + s, 0)),
            _full(w_lat.shape), _full((1, Q_LORA_RANK)), _full((1, KV_LORA_RANK)),
            _full(wuqT.shape), _full(wuk.shape), _full(wuvT.shape),
            pl.BlockSpec((HALF_ROPE, ta), lambda b, s: (0, s)),
            pl.BlockSpec((HALF_ROPE, ta), lambda b, s: (0, s)),
            pl.BlockSpec((ta, LANES), lambda b, s: (s, 0)),
            pl.BlockSpec((ta, LANES), lambda b, s: (s, 0)),
            pl.BlockSpec((ta, LANES), lambda b, s: (s, 0)),
        ],
        out_specs=[
            pl.BlockSpec((1, H * HEAD_PAD, ta), lambda b, s: (b, 0, s)),
            pl.BlockSpec((1, ta, H * HEAD_PAD), lambda b, s: (b, s, 0)),
            pl.BlockSpec((1, H * V_ROWS, ta), lambda b, s: (b, 0, s)),
        ],
        out_shape=[
            jax.ShapeDtypeStruct((B, H * HEAD_PAD, S), BF16),
            jax.ShapeDtypeStruct((B, S, H * HEAD_PAD), BF16),
            jax.ShapeDtypeStruct((B, H * V_ROWS, S), BF16),
        ],
        compiler_params=_params(("parallel", "parallel")),
    )(x2, w_lat, q_norm_g[0][None], kv_norm_g[0][None], wuqT, wuk, wuvT, cosT, sinT, rot_a, rot_m, rot_p)

    tq = TILE_Q
    assert tq == 2 * TILE_K
    key_chunk = jnp.arange(tq)[:, None] // CHUNK
    q_chunk = jnp.arange(tq)[None, :] // CHUNK
    mask_bias = jnp.where(key_chunk <= q_chunk, 0.0, NEG_BIG).astype(F32).reshape(2, TILE_K, tq)
    oT = pl.pallas_call(
        _attn_kernel,
        grid=(B, H, S // tq),
        in_specs=[
            pl.BlockSpec((1, HEAD_PAD, tq), lambda b, h, q: (b, h, q)),
            pl.BlockSpec((1, S, HEAD_PAD), lambda b, h, q: (b, 0, h)),
            pl.BlockSpec((1, V_ROWS, S), lambda b, h, q: (b, h, 0)),
            _full(mask_bias.shape),
        ],
        out_specs=pl.BlockSpec((1, V_HEAD_DIM, tq), lambda b, h, q: (b, h, q)),
        out_shape=jax.ShapeDtypeStruct((B, H * V_HEAD_DIM, S), BF16),
        scratch_shapes=[pltpu.VMEM((TILE_K, tq), F32), pltpu.VMEM((TILE_K, tq), F32),
                        pltpu.VMEM((TILE_K, tq), BF16), pltpu.VMEM((TILE_K, tq), BF16)],
        compiler_params=_params(("parallel", "parallel", "arbitrary")),
    )(qT, k, vT, mask_bias)

    tm = TILE_MIX
    nsm = S // tm
    tri = (jnp.arange(tm)[:, None] > jnp.arange(tm)[None, :]).astype(BF16)
    htm, route, counts = pl.pallas_call(
        _mix_kernel,
        grid=(B, nsm),
        in_specs=[
            pl.BlockSpec((tm, D), lambda b, s: (b * nsm + s, 0)),
            pl.BlockSpec((1, H * V_HEAD_DIM, tm), lambda b, s: (b, 0, s)),
            _full(w_pool.shape), _full(w_gates.shape), _full(pool_mix_w.shape[1:]), _full((1, D)),
            _full(w_mla_o.shape[1:]), _full(w_out.shape[1:]), _full((1, D)), _full((1, D)),
            _full(w_r.shape), _full(b_r.shape), _full(tri.shape),
        ],
        out_specs=[
            pl.BlockSpec((tm * ROW_TILES, LANES), lambda b, s: (b * nsm + s, 0)),
            pl.BlockSpec((tm, LANES), lambda b, s: (b * nsm + s, 0)),
            _full((SUBLANES, LANES)),
        ],
        out_shape=[
            jax.ShapeDtypeStruct((N * ROW_TILES, LANES), F32),
            jax.ShapeDtypeStruct((N, LANES), F32),
            jax.ShapeDtypeStruct((SUBLANES, LANES), F32),
        ],
        scratch_shapes=[pltpu.VMEM((tm + POOL_HALO, POOL_WIDTH), F32), pltpu.VMEM((SUBLANES, LANES), F32)],
        compiler_params=_params(("arbitrary", "arbitrary")),
    )(x2, oT, w_pool, w_gates, pool_mix_w[0].astype(BF16), pool_scale[0][None], w_mla_o[0].astype(BF16),
      w_out[0].astype(BF16), ln1_g[0][None], ln1_b[0][None], w_r, b_r, tri)

    tb = TILE_EXPERT
    A = N * TOP_K
    n_blocks = -(-(A + N_EXPERTS * (tb - 1)) // tb)
    R = n_blocks * tb
    e_idx = route[:, 0:2].astype(jnp.int32)
    rank = route[:, 4:6].astype(jnp.int32)
    cnt = counts[0, :N_EXPERTS].astype(jnp.int32)
    padded = ((cnt + tb - 1) // tb) * tb
    pad_end = jnp.cumsum(padded)
    pad_start = pad_end - padded
    dest = pad_start[e_idx] + rank
    tok = jnp.broadcast_to(jnp.arange(N, dtype=jnp.int32)[:, None], (N, TOP_K))
    row_tok = jnp.zeros((R,), jnp.int32).at[dest.reshape(A)].set(tok.reshape(A))
    block_e = jnp.minimum(jnp.searchsorted(pad_end, jnp.arange(n_blocks) * tb, side='right'),
                          N_EXPERTS - 1).astype(jnp.int32)
    n_used = (pad_end[-1] // tb).astype(jnp.int32)[None]
    row_tok3 = row_tok.reshape(n_blocks, 1, tb)

    ytm = pl.pallas_call(
        _expert_kernel,
        grid_spec=pltpu.PrefetchScalarGridSpec(
            num_scalar_prefetch=2,
            grid=(n_blocks,),
            in_specs=[
                pl.BlockSpec((1, 1, tb), lambda i, be, nu: (i, 0, 0), memory_space=pltpu.SMEM),
                pl.BlockSpec((1, 1, tb), lambda i, be, nu: (jnp.minimum(i + 1, n_blocks - 1), 0, 0),
                             memory_space=pltpu.SMEM),
                pl.BlockSpec(memory_space=pl.ANY),
                pl.BlockSpec((1, D, D_EXPERT), lambda i, be, nu: (be[i], 0, 0)),
                pl.BlockSpec((1, D, D_EXPERT), lambda i, be, nu: (be[i], 0, 0)),
                pl.BlockSpec((1, D_EXPERT, D), lambda i, be, nu: (be[i], 0, 0)),
            ],
            out_specs=pl.BlockSpec((tb * ROW_TILES, LANES), lambda i, be, nu: (i, 0)),
            scratch_shapes=[
                pltpu.VMEM((2, tb * ROW_TILES, LANES), F32),
                pltpu.VMEM((D, D_EXPERT), BF16), pltpu.VMEM((D, D_EXPERT), BF16), pltpu.VMEM((D_EXPERT, D), BF16),
                pltpu.SemaphoreType.DMA((2,)),
            ],
        ),
        out_shape=jax.ShapeDtypeStruct((R * ROW_TILES, LANES), F32),
        compiler_params=_params(("arbitrary",)),
    )(block_e, n_used, row_tok3, row_tok3, htm, w_gate[0], w_up[0], w_down[0])

    tf = TILE_COMBINE
    nf = N // tf
    dest3 = dest.reshape(nf, tf, TOP_K).transpose(0, 2, 1).reshape(nf, 1, TOP_K * tf)
    out = pl.pallas_call(
        _combine_kernel,
        grid=(nf,),
        in_specs=[
            pl.BlockSpec((1, 1, TOP_K * tf), lambda i: (i, 0, 0), memory_space=pltpu.SMEM),
            pl.BlockSpec((1, 1, TOP_K * tf), lambda i: (jnp.minimum(i + 1, nf - 1), 0, 0), memory_space=pltpu.SMEM),
            pl.BlockSpec((tf * ROW_TILES, LANES), lambda i: (i, 0)),
            pl.BlockSpec((tf, LANES), lambda i: (i, 0)),
            pl.BlockSpec(memory_space=pl.ANY),
            _full((1, D)), _full((1, D)),
        ],
        out_specs=pl.BlockSpec((tf, D), lambda i: (i, 0)),
        out_shape=jax.ShapeDtypeStruct((N, D), F32),
        scratch_shapes=[pltpu.VMEM((2, TOP_K * tf * ROW_TILES, LANES), F32), pltpu.SemaphoreType.DMA((2,))],
        compiler_params=_params(("arbitrary",)),
    )(dest3, dest3, htm, route, ytm, ln2_g[0][None], ln2_b[0][None])
    return out.reshape(B, S, D)
```

```python
import functools
import math

import jax
import jax.numpy as jnp
from jax import lax
from jax.experimental import pallas as pl
from jax.experimental.pallas import tpu as pltpu

D_MODEL = 1024
CHUNK = 64
POOL_WINDOWS = (2, 4, 8, 16)
POOL_GROUPS = len(POOL_WINDOWS)
POOL_WIDTH = D_MODEL // 2
POOL_GROUP_DIM = POOL_WIDTH // POOL_GROUPS
POOL_OUT_GROUP_DIM = D_MODEL // POOL_GROUPS
POOL_HALO = 16
N_HEADS = 8
QK_NOPE_DIM = D_MODEL // 16
QK_ROPE_DIM = D_MODEL // 32
HALF_ROPE = QK_ROPE_DIM // 2
V_HEAD_DIM = D_MODEL // 16
V_ROWS = V_HEAD_DIM + 16
Q_LORA_RANK = 3 * D_MODEL // 8
KV_LORA_RANK = D_MODEL // 4
ROPE_THETA = 10000.0
N_GROUPS = 4
EXPERTS_PER_GROUP = 8
N_EXPERTS = N_GROUPS * EXPERTS_PER_GROUP
TOP_K = 2
D_EXPERT = D_MODEL // 2
NORM_EPS = 1e-5
DEPTH = 1
DEEPNORM_ALPHA = (2.0 * DEPTH) ** 0.25

LANES = 128
SUBLANES = 8
HEAD_PAD = LANES
ROW_TILES = D_MODEL // LANES
GROUP_LANE0 = 64
NEG_BIG = -1e30
VMEM_LIMIT = 56 * 1024 * 1024

TILE_LATENT = 512
TILE_Q = 512
TILE_K = 256
ATTN_HEADS_PER_STEP = 4
TILE_MIX = 256
TILE_EXPERT = 256
TILE_COMBINE = 256

F32 = jnp.float32
BF16 = jnp.bfloat16


def _dot(a, b):
    return jnp.dot(a, b, preferred_element_type=F32)


def _dot_nt(a, b):
    return lax.dot_general(a, b, (((1,), (1,)), ((), ())), preferred_element_type=F32)


def _dot_tn(a, b):
    return lax.dot_general(a, b, (((0,), (0,)), ((), ())), preferred_element_type=F32)


def _rms(v, g):
    ms = jnp.mean(jnp.square(v), axis=-1, keepdims=True)
    return v * lax.rsqrt(ms + NORM_EPS) * g


def _layer_norm(v, g, b):
    mu = jnp.mean(v, axis=-1, keepdims=True)
    c = v - mu
    var = jnp.mean(jnp.square(c), axis=-1, keepdims=True)
    return c * lax.rsqrt(var + NORM_EPS) * g + b


def _latent_kernel(x_ref, wlat_ref, gq_ref, gkv_ref, wuqT_ref, wuk_ref, wuvT_ref,
                   cosT_ref, sinT_ref, ra_ref, rm_ref, rp_ref,
                   qT_ref, k_ref, vT_ref, *, q_scale):
    xb = x_ref[...].astype(BF16)
    lat = _dot(xb, wlat_ref[...])
    c_q = lat[:, :Q_LORA_RANK]
    c_kv = lat[:, Q_LORA_RANK:Q_LORA_RANK + KV_LORA_RANK]
    kpe = lat[:, Q_LORA_RANK + KV_LORA_RANK:]
    qn = _rms(c_q, gq_ref[...]).astype(BF16)
    kvn = _rms(c_kv, gkv_ref[...]).astype(BF16)

    qT = _dot_nt(wuqT_ref[...], qn) * q_scale
    cosT = cosT_ref[...]
    sinT = sinT_ref[...]
    for h in range(N_HEADS):
        r0 = h * HEAD_PAD
        x1 = qT[r0 + QK_NOPE_DIM:r0 + QK_NOPE_DIM + HALF_ROPE]
        x2 = qT[r0 + QK_NOPE_DIM + HALF_ROPE:r0 + QK_NOPE_DIM + QK_ROPE_DIM]
        qT_ref[0, r0:r0 + QK_NOPE_DIM, :] = qT[r0:r0 + QK_NOPE_DIM].astype(BF16)
        qT_ref[0, r0 + QK_NOPE_DIM:r0 + QK_NOPE_DIM + HALF_ROPE, :] = (x1 * cosT - x2 * sinT).astype(BF16)
        qT_ref[0, r0 + QK_NOPE_DIM + HALF_ROPE:r0 + QK_NOPE_DIM + QK_ROPE_DIM, :] = (
            x1 * sinT + x2 * cosT).astype(BF16)
        qT_ref[0, r0 + QK_NOPE_DIM + QK_ROPE_DIM:r0 + HEAD_PAD, :] = jnp.zeros(
            (HEAD_PAD - QK_NOPE_DIM - QK_ROPE_DIM, qT.shape[1]), BF16)

    kpe_rot = (kpe * ra_ref[...] + pltpu.roll(kpe, LANES - HALF_ROPE, 1) * rm_ref[...]
               + pltpu.roll(kpe, HALF_ROPE, 1) * rp_ref[...])
    k = _dot(kvn, wuk_ref[...])
    for h in range(N_HEADS):
        k_ref[0, :, h * HEAD_PAD:(h + 1) * HEAD_PAD] = (k[:, h * HEAD_PAD:(h + 1) * HEAD_PAD] + kpe_rot).astype(BF16)
    vT = _dot_nt(wuvT_ref[...], kvn)
    row = lax.broadcasted_iota(jnp.int32, vT.shape, 0) % V_ROWS
    vT_ref[0] = jnp.where(row == V_HEAD_DIM, 1.0, vT).astype(BF16)


def _attn_kernel(qT_ref, k_ref, vT_ref, bias_ref, oT_ref, *bufs):
    qi = pl.program_id(2)
    tq = qT_ref.shape[2]
    heads = range(ATTN_HEADS_PER_STEP)
    s0, s1, p0, p1 = (bufs[i * ATTN_HEADS_PER_STEP:(i + 1) * ATTN_HEADS_PER_STEP] for i in range(4))

    def scores(h, j):
        k0 = pl.multiple_of(j * TILE_K, TILE_K)
        return _dot(k_ref[0, pl.ds(k0, TILE_K), h * HEAD_PAD:(h + 1) * HEAD_PAD],
                    qT_ref[0, h * HEAD_PAD:(h + 1) * HEAD_PAD, :])

    def pv(h, j, p_ref):
        k0 = pl.multiple_of(j * TILE_K, TILE_K)
        return _dot(vT_ref[0, h * V_ROWS:(h + 1) * V_ROWS, pl.ds(k0, TILE_K)], p_ref[...])

    def col_max(s):
        while s.shape[0] > SUBLANES:
            half = s.shape[0] // 2
            s = jnp.maximum(s[:half], s[half:])
        return jnp.max(s, axis=0, keepdims=True)

    def step(h, j, s_cur, s_nxt, p_prev, p_cur, carry, bias=None):
        m, acc, alpha_prev = carry
        if s_nxt is not None:
            s_nxt[h][...] = scores(h, j + 1)
        acc = alpha_prev * acc + pv(h, jnp.maximum(j - 1, 0), p_prev[h])
        s = s_cur[h][...]
        if bias is not None:
            s = s + bias
        m_new = jnp.maximum(m, col_max(s))
        p_cur[h][...] = jnp.exp2(s - m_new).astype(BF16)
        return m_new, acc, jnp.exp2(m - m_new)

    for h in heads:
        p1[h][...] = jnp.zeros_like(p1[h])
        s0[h][...] = scores(h, 0)
    carry = tuple((jnp.full((1, tq), NEG_BIG, F32), jnp.zeros((V_ROWS, tq), F32), jnp.ones((1, tq), F32))
                  for _ in heads)

    def pair(t, c):
        c = tuple(step(h, 2 * t, s0, s1, p1, p0, c[h]) for h in heads)
        return tuple(step(h, 2 * t + 1, s1, s0, p0, p1, c[h]) for h in heads)

    carry = lax.fori_loop(0, qi, pair, carry)
    jd = 2 * qi
    carry = tuple(step(h, jd, s0, s1, p1, p0, carry[h], bias=bias_ref[0]) for h in heads)
    carry = tuple(step(h, jd + 1, s1, None, p0, p1, carry[h], bias=bias_ref[1]) for h in heads)
    for h in heads:
        _, acc, alpha = carry[h]
        acc = alpha * acc + pv(h, jd + 1, p1[h])
        oT_ref[0, h * V_HEAD_DIM:(h + 1) * V_HEAD_DIM, :] = (
            acc[:V_HEAD_DIM] / acc[V_HEAD_DIM:V_HEAD_DIM + 1]).astype(BF16)


def _mix_kernel(x_ref, oT_ref, wpool_ref, wgate_ref, mixw_ref, pscale_ref, wo_ref, wout_ref,
                g1_ref, b1_ref, wr_ref, br_ref, tri_ref,
                htm_ref, route_ref, routeT_ref, cnt_ref, ext_ref, base_ref):
    b = pl.program_id(0)
    si = pl.program_id(1)
    tm = x_ref.shape[0]

    @pl.when(jnp.logical_and(b == 0, si == 0))
    def _():
        base_ref[...] = jnp.zeros_like(base_ref)

    @pl.when(si == 0)
    def _():
        ext_ref[0:POOL_HALO, :] = jnp.zeros((POOL_HALO, POOL_WIDTH), F32)

    x = x_ref[...]
    xb = x.astype(BF16)
    u = _dot(xb, wpool_ref[...])
    ext_ref[POOL_HALO:POOL_HALO + tm, :] = u

    pos1 = si * tm + lax.broadcasted_iota(jnp.int32, (tm, POOL_GROUP_DIM), 0) + 1
    y_parts = []
    for g, win in enumerate(POOL_WINDOWS):
        c0 = g * POOL_GROUP_DIM
        ws = u[:, c0:c0 + POOL_GROUP_DIM]
        for kk in range(1, win):
            ws = ws + ext_ref[POOL_HALO - kk:POOL_HALO - kk + tm, c0:c0 + POOL_GROUP_DIM]
        count = jnp.minimum(pos1, win).astype(F32)
        pooled = ws / count - u[:, c0:c0 + POOL_GROUP_DIM]
        y_parts.append(_dot(pooled.astype(BF16), mixw_ref[g]))
    ext_ref[0:POOL_HALO, :] = ext_ref[tm:tm + POOL_HALO, :]
    y_pool = jnp.concatenate(y_parts, axis=-1) * pscale_ref[...]

    glog = _dot(xb, wgate_ref[...])
    y_mla = _dot_tn(oT_ref[0], wo_ref[...])
    merged = (jax.nn.sigmoid(glog[:, :D_MODEL]) * y_pool + jax.nn.sigmoid(glog[:, D_MODEL:]) * y_mla)
    r = DEEPNORM_ALPHA * x + _dot(merged.astype(BF16), wout_ref[...])
    h = _layer_norm(r, g1_ref[...], b1_ref[...])
    for j in range(ROW_TILES):
        htm_ref[pl.ds(j, tm, stride=ROW_TILES), :] = h[:, j * LANES:(j + 1) * LANES]

    logits = _dot(h.astype(BF16), wr_ref[...]) + br_ref[...]
    lane_i = lax.broadcasted_iota(jnp.int32, logits.shape, 1)
    lane = lane_i.astype(F32)
    big = float(LANES)
    is_group = jnp.logical_and(lane_i >= GROUP_LANE0, lane_i < GROUP_LANE0 + N_GROUPS)
    gl = jnp.where(is_group, logits, NEG_BIG)
    gmax = jnp.max(gl, axis=-1, keepdims=True)
    g_w = 1.0 / jnp.sum(jnp.exp(gl - gmax), axis=-1, keepdims=True)
    g_idx = jnp.min(jnp.where(gl == gmax, lane - GROUP_LANE0, big), axis=-1, keepdims=True)
    lane_group = (lane_i // EXPERTS_PER_GROUP).astype(F32)
    in_group = jnp.logical_and(lane_i < N_EXPERTS, lane_group == g_idx)
    el = jnp.where(in_group, logits, NEG_BIG)
    e1max = jnp.max(el, axis=-1, keepdims=True)
    i1 = jnp.min(jnp.where(el == e1max, lane, big), axis=-1, keepdims=True)
    el2 = jnp.where(lane == i1, NEG_BIG, el)
    e2max = jnp.max(el2, axis=-1, keepdims=True)
    i2 = jnp.min(jnp.where(el2 == e2max, lane, big), axis=-1, keepdims=True)
    ratio = jnp.exp(e2max - e1max)
    gate1 = g_w / (1.0 + ratio)
    gate2 = g_w * ratio / (1.0 + ratio)

    hit1 = lane == i1
    hit2 = lane == i2
    onehot = jnp.where(jnp.logical_or(hit1, hit2), 1.0, 0.0)
    before = _dot(tri_ref[...], onehot.astype(BF16)) + base_ref[0:1, :]
    rank1 = jnp.sum(jnp.where(hit1, before, 0.0), axis=-1, keepdims=True)
    rank2 = jnp.sum(jnp.where(hit2, before, 0.0), axis=-1, keepdims=True)
    new_base = base_ref[0:1, :] + jnp.sum(onehot, axis=0, keepdims=True)
    base_ref[...] = jnp.broadcast_to(new_base, base_ref.shape)
    cnt_ref[...] = jnp.broadcast_to(new_base, cnt_ref.shape)

    route = jnp.where(lane_i == 0, i1, 0.0)
    route = jnp.where(lane_i == 1, i2, route)
    route = jnp.where(lane_i == 2, gate1, route)
    route = jnp.where(lane_i == 3, gate2, route)
    route = jnp.where(lane_i == 4, rank1, route)
    route = jnp.where(lane_i == 5, rank2, route)
    route_ref[...] = route
    routeT_ref[...] = route.T[:SUBLANES]


def _gather_row(idx_ref, r, src_hbm, dst_ref, slot, sem):
    src0 = pl.multiple_of(idx_ref[0, 0, r] * ROW_TILES, ROW_TILES)
    pltpu.make_async_copy(src_hbm.at[pl.ds(src0, ROW_TILES), :],
                          dst_ref.at[slot, pl.ds(r * ROW_TILES, ROW_TILES), :],
                          sem.at[slot]).start()


def _gather_rows_loop(idx_ref, src_hbm, dst_ref, slot, sem, n_rows, unroll=8):
    def body(c, _):
        for u in range(unroll):
            _gather_row(idx_ref, c * unroll + u, src_hbm, dst_ref, slot, sem)
        return 0
    lax.fori_loop(0, n_rows // unroll, body, 0)


def _gather_rows_inline(idx_ref, src_hbm, dst_ref, slot, sem, n_rows):
    for r in range(n_rows):
        _gather_row(idx_ref, r, src_hbm, dst_ref, slot, sem)


def _wait_rows(dst_ref, slot, sem):
    pltpu.make_async_copy(dst_ref.at[slot], dst_ref.at[slot], sem.at[slot]).wait()


def _expert_kernel(be_ref, nused_ref, tok_ref, tokn_ref, htm_hbm, wg_ref, wu_ref, wd_ref,
                   y_ref, xbuf, wgb, wub, wdb, sem):
    i = pl.program_id(0)
    n_used = nused_ref[0]
    slot = i % 2
    tb = TILE_EXPERT

    @pl.when(i == 0)
    def _():
        _gather_rows_loop(tok_ref, htm_hbm, xbuf, 0, sem, tb)

    changed = jnp.logical_or(i == 0, be_ref[i] != be_ref[jnp.maximum(i - 1, 0)])

    @pl.when(jnp.logical_and(changed, i < n_used))
    def _():
        wgb[...] = wg_ref[0].astype(BF16)
        wub[...] = wu_ref[0].astype(BF16)
        wdb[...] = wd_ref[0].astype(BF16)

    @pl.when(i == n_used)
    def _():
        _wait_rows(xbuf, slot, sem)

    @pl.when(i < n_used)
    def _():
        _wait_rows(xbuf, slot, sem)
        _gather_rows_inline(tokn_ref, htm_hbm, xbuf, 1 - slot, sem, tb)
        xs = jnp.concatenate(
            [xbuf[slot, pl.ds(j, tb, stride=ROW_TILES), :] for j in range(ROW_TILES)], axis=-1).astype(BF16)
        gate = _dot(xs, wgb[...])
        up = _dot(xs, wub[...])
        hid = (jax.nn.silu(gate) * up).astype(BF16)
        y = _dot(hid, wdb[...])
        for j in range(ROW_TILES):
            y_ref[pl.ds(j, tb, stride=ROW_TILES), :] = y[:, j * LANES:(j + 1) * LANES]

    @pl.when(i >= n_used)
    def _():
        y_ref[...] = jnp.zeros_like(y_ref)


def _combine_kernel(dst_ref, dstn_ref, htm_ref, route_ref, ytm_hbm, g2_ref, b2_ref, o_ref, gbuf, sem):
    i = pl.program_id(0)
    n = pl.num_programs(0)
    slot = i % 2
    tf = TILE_COMBINE

    @pl.when(i == 0)
    def _():
        _gather_rows_loop(dst_ref, ytm_hbm, gbuf, 0, sem, TOP_K * tf)

    _wait_rows(gbuf, slot, sem)
    _gather_rows_inline(dstn_ref, ytm_hbm, gbuf, 1 - slot, sem, TOP_K * tf)
    route = route_ref[...]
    gate1 = route[:, 2:3]
    gate2 = route[:, 3:4]
    parts = []
    for j in range(ROW_TILES):
        hj = htm_ref[pl.ds(j, tf, stride=ROW_TILES), :]
        y1 = gbuf[slot, pl.ds(j, tf, stride=ROW_TILES), :]
        y2 = gbuf[slot, pl.ds(tf * ROW_TILES + j, tf, stride=ROW_TILES), :]
        parts.append(DEEPNORM_ALPHA * hj + (gate1 * y1 + gate2 * y2))
    z = jnp.concatenate(parts, axis=-1)
    o_ref[...] = _layer_norm(z, g2_ref[...], b2_ref[...])

    @pl.when(i == n - 1)
    def _():
        _wait_rows(gbuf, 1 - slot, sem)


def _rope_tables(seq):
    inv = 1.0 / (ROPE_THETA ** (jnp.arange(0, QK_ROPE_DIM, 2, dtype=F32) / QK_ROPE_DIM))
    ang = jnp.arange(seq, dtype=F32)[:, None] * inv[None, :]
    cos, sin = jnp.cos(ang), jnp.sin(ang)
    zeros = jnp.zeros((seq, QK_NOPE_DIM), F32)
    pad = jnp.zeros((seq, HEAD_PAD - QK_NOPE_DIM - QK_ROPE_DIM), F32)
    z16 = jnp.zeros((seq, HALF_ROPE), F32)
    rot_a = jnp.concatenate([zeros, cos, cos, pad], axis=1)
    rot_m = jnp.concatenate([zeros, -sin, z16, pad], axis=1)
    rot_p = jnp.concatenate([zeros, z16, sin, pad], axis=1)
    return cos.T, sin.T, rot_a, rot_m, rot_p


def _full(shape):
    return pl.BlockSpec(shape, lambda *_: (0,) * len(shape))


def _params(sem):
    return pltpu.CompilerParams(dimension_semantics=sem, vmem_limit_bytes=VMEM_LIMIT)


def kernel(x, w_in, pool_mix_w, pool_scale, q_norm_g, w_uq, kv_norm_g, w_ukv, w_mla_o, w_out, ln1_g, ln1_b,
           w_router_group, b_router_group, w_router_expert, b_router_expert, w_gate, w_up, w_down, ln2_g, ln2_b):
    B, S, D = x.shape
    assert D == D_MODEL and w_in.shape[0] == DEPTH == 1
    assert S % TILE_LATENT == 0 and S % TILE_Q == 0 and S % TILE_MIX == 0 and TILE_Q % TILE_K == 0
    N = B * S
    assert N % TILE_COMBINE == 0
    H = N_HEADS

    w = w_in[0]
    o1 = POOL_WIDTH
    o2 = o1 + Q_LORA_RANK
    o3 = o2 + KV_LORA_RANK
    o4 = o3 + QK_ROPE_DIM
    w_pool = w[:, :o1].astype(BF16)
    kpe_cols = jnp.pad(w[:, o3:o4], ((0, 0), (QK_NOPE_DIM, HEAD_PAD - QK_NOPE_DIM - QK_ROPE_DIM)))
    w_lat = jnp.concatenate([w[:, o1:o3], kpe_cols], axis=1).astype(BF16)
    w_gates = w[:, o4:].astype(BF16)
    qd = QK_NOPE_DIM + QK_ROPE_DIM
    wuq = jnp.pad(w_uq[0].reshape(Q_LORA_RANK, H, qd), ((0, 0), (0, 0), (0, HEAD_PAD - qd)))
    wuqT = wuq.reshape(Q_LORA_RANK, H * HEAD_PAD).T.astype(BF16)
    wukv = w_ukv[0].reshape(KV_LORA_RANK, H, QK_NOPE_DIM + V_HEAD_DIM)
    wuk = jnp.pad(wukv[:, :, :QK_NOPE_DIM], ((0, 0), (0, 0), (0, HEAD_PAD - QK_NOPE_DIM)))
    wuk = wuk.reshape(KV_LORA_RANK, H * HEAD_PAD).astype(BF16)
    wuv = jnp.pad(wukv[:, :, QK_NOPE_DIM:], ((0, 0), (0, 0), (0, V_ROWS - V_HEAD_DIM)))
    wuvT = wuv.reshape(KV_LORA_RANK, H * V_ROWS).T.astype(BF16)
    w_r = jnp.zeros((D, LANES), F32)
    w_r = w_r.at[:, :N_EXPERTS].set(w_router_expert[0]).at[:, GROUP_LANE0:GROUP_LANE0 + N_GROUPS].set(
        w_router_group[0]).astype(BF16)
    b_r = jnp.zeros((1, LANES), F32)
    b_r = b_r.at[0, :N_EXPERTS].set(b_router_expert[0]).at[0, GROUP_LANE0:GROUP_LANE0 + N_GROUPS].set(
        b_router_group[0])
    cosT, sinT, rot_a, rot_m, rot_p = _rope_tables(S)
    q_scale = (QK_NOPE_DIM + QK_ROPE_DIM) ** -0.5 * math.log2(math.e)
    x2 = x.reshape(N, D)

    ta = TILE_LATENT
    nsa = S // ta
    qT, k, vT = pl.pallas_call(
        functools.partial(_latent_kernel, q_scale=q_scale),
        grid=(B, nsa),
        in_specs=[
            pl.BlockSpec((ta, D), lambda b, s: (b * nsa + s, 0)),
            _full(w_lat.shape), _full((1, Q_LORA_RANK)), _full((1, KV_LORA_RANK)),
            _full(wuqT.shape), _full(wuk.shape), _full(wuvT.shape),
            pl.BlockSpec((HALF_ROPE, ta), lambda b, s: (0, s)),
            pl.BlockSpec((HALF_ROPE, ta), lambda b, s: (0, s)),
            pl.BlockSpec((ta, LANES), lambda b, s: (s, 0)),
            pl.BlockSpec((ta, LANES), lambda b, s: (s, 0)),
            pl.BlockSpec((ta, LANES), lambda b, s: (s, 0)),
        ],
        out_specs=[
            pl.BlockSpec((1, H * HEAD_PAD, ta), lambda b, s: (b, 0, s)),
            pl.BlockSpec((1, ta, H * HEAD_PAD), lambda b, s: (b, s, 0)),
            pl.BlockSpec((1, H * V_ROWS, ta), lambda b, s: (b, 0, s)),
        ],
        out_shape=[
            jax.ShapeDtypeStruct((B, H * HEAD_PAD, S), BF16),
            jax.ShapeDtypeStruct((B, S, H * HEAD_PAD), BF16),
            jax.ShapeDtypeStruct((B, H * V_ROWS, S), BF16),
        ],
        compiler_params=_params(("parallel", "parallel")),
    )(x2, w_lat, q_norm_g[0][None], kv_norm_g[0][None], wuqT, wuk, wuvT, cosT, sinT, rot_a, rot_m, rot_p)

    tq = TILE_Q
    assert tq == 2 * TILE_K
    key_chunk = jnp.arange(tq)[:, None] // CHUNK
    q_chunk = jnp.arange(tq)[None, :] // CHUNK
    mask_bias = jnp.where(key_chunk <= q_chunk, 0.0, NEG_BIG).astype(F32).reshape(2, TILE_K, tq)
    hp = ATTN_HEADS_PER_STEP
    assert H % hp == 0
    oT = pl.pallas_call(
        _attn_kernel,
        grid=(B, H // hp, S // tq),
        in_specs=[
            pl.BlockSpec((1, hp * HEAD_PAD, tq), lambda b, h, q: (b, h, q)),
            pl.BlockSpec((1, S, hp * HEAD_PAD), lambda b, h, q: (b, 0, h)),
            pl.BlockSpec((1, hp * V_ROWS, S), lambda b, h, q: (b, h, 0)),
            _full(mask_bias.shape),
        ],
        out_specs=pl.BlockSpec((1, hp * V_HEAD_DIM, tq), lambda b, h, q: (b, h, q)),
        out_shape=jax.ShapeDtypeStruct((B, H * V_HEAD_DIM, S), BF16),
        scratch_shapes=([pltpu.VMEM((TILE_K, tq), F32)] * (2 * hp) + [pltpu.VMEM((TILE_K, tq), BF16)] * (2 * hp)),
        compiler_params=_params(("parallel", "parallel", "arbitrary")),
    )(qT, k, vT, mask_bias)

    tm = TILE_MIX
    nsm = S // tm
    tri = (jnp.arange(tm)[:, None] > jnp.arange(tm)[None, :]).astype(BF16)
    htm, route, routeT, counts = pl.pallas_call(
        _mix_kernel,
        grid=(B, nsm),
        in_specs=[
            pl.BlockSpec((tm, D), lambda b, s: (b * nsm + s, 0)),
            pl.BlockSpec((1, H * V_HEAD_DIM, tm), lambda b, s: (b, 0, s)),
            _full(w_pool.shape), _full(w_gates.shape), _full(pool_mix_w.shape[1:]), _full((1, D)),
            _full(w_mla_o.shape[1:]), _full(w_out.shape[1:]), _full((1, D)), _full((1, D)),
            _full(w_r.shape), _full(b_r.shape), _full(tri.shape),
        ],
        out_specs=[
            pl.BlockSpec((tm * ROW_TILES, LANES), lambda b, s: (b * nsm + s, 0)),
            pl.BlockSpec((tm, LANES), lambda b, s: (b * nsm + s, 0)),
            pl.BlockSpec((SUBLANES, tm), lambda b, s: (0, b * nsm + s)),
            _full((SUBLANES, LANES)),
        ],
        out_shape=[
            jax.ShapeDtypeStruct((N * ROW_TILES, LANES), F32),
            jax.ShapeDtypeStruct((N, LANES), F32),
            jax.ShapeDtypeStruct((SUBLANES, N), F32),
            jax.ShapeDtypeStruct((SUBLANES, LANES), F32),
        ],
        scratch_shapes=[pltpu.VMEM((tm + POOL_HALO, POOL_WIDTH), F32), pltpu.VMEM((SUBLANES, LANES), F32)],
        compiler_params=_params(("arbitrary", "arbitrary")),
    )(x2, oT, w_pool, w_gates, pool_mix_w[0].astype(BF16), pool_scale[0][None], w_mla_o[0].astype(BF16),
      w_out[0].astype(BF16), ln1_g[0][None], ln1_b[0][None], w_r, b_r, tri)

    tb = TILE_EXPERT
    A = N * TOP_K
    n_blocks = -(-(A + N_EXPERTS * (tb - 1)) // tb) + 1
    R = n_blocks * tb
    e_idx = routeT[0:2].astype(jnp.int32)
    rank = routeT[4:6].astype(jnp.int32)
    cnt = counts[0, :N_EXPERTS].astype(jnp.int32)
    padded = ((cnt + tb - 1) // tb) * tb
    pad_end = jnp.cumsum(padded)
    pad_start = pad_end - padded
    is_e = e_idx[:, None, :] == jnp.arange(N_EXPERTS, dtype=jnp.int32)[None, :, None]
    dest = jnp.sum(jnp.where(is_e, pad_start[None, :, None], 0), axis=1) + rank
    tok = jnp.broadcast_to(jnp.arange(N, dtype=jnp.int32)[None, :], (TOP_K, N))
    row_tok = jnp.zeros((R,), jnp.int32).at[dest.reshape(A)].set(tok.reshape(A))
    block_start = jnp.arange(n_blocks, dtype=jnp.int32) * tb
    block_e = jnp.minimum(jnp.sum(pad_end[None, :] <= block_start[:, None], axis=1), N_EXPERTS - 1).astype(jnp.int32)
    n_used = (pad_end[-1] // tb).astype(jnp.int32)[None]
    row_tok3 = row_tok.reshape(n_blocks, 1, tb)

    ytm = pl.pallas_call(
        _expert_kernel,
        grid_spec=pltpu.PrefetchScalarGridSpec(
            num_scalar_prefetch=2,
            grid=(n_blocks,),
            in_specs=[
                pl.BlockSpec((1, 1, tb), lambda i, be, nu: (i, 0, 0), memory_space=pltpu.SMEM),
                pl.BlockSpec((1, 1, tb), lambda i, be, nu: (jnp.minimum(i + 1, n_blocks - 1), 0, 0),
                             memory_space=pltpu.SMEM),
                pl.BlockSpec(memory_space=pl.ANY),
                pl.BlockSpec((1, D, D_EXPERT), lambda i, be, nu: (be[i], 0, 0)),
                pl.BlockSpec((1, D, D_EXPERT), lambda i, be, nu: (be[i], 0, 0)),
                pl.BlockSpec((1, D_EXPERT, D), lambda i, be, nu: (be[i], 0, 0)),
            ],
            out_specs=pl.BlockSpec((tb * ROW_TILES, LANES), lambda i, be, nu: (i, 0)),
            scratch_shapes=[
                pltpu.VMEM((2, tb * ROW_TILES, LANES), F32),
                pltpu.VMEM((D, D_EXPERT), BF16), pltpu.VMEM((D, D_EXPERT), BF16), pltpu.VMEM((D_EXPERT, D), BF16),
                pltpu.SemaphoreType.DMA((2,)),
            ],
        ),
        out_shape=jax.ShapeDtypeStruct((R * ROW_TILES, LANES), F32),
        compiler_params=_params(("arbitrary",)),
    )(block_e, n_used, row_tok3, row_tok3, htm, w_gate[0], w_up[0], w_down[0])

    tf = TILE_COMBINE
    nf = N // tf
    dest3 = dest.reshape(TOP_K, nf, tf).transpose(1, 0, 2).reshape(nf, 1, TOP_K * tf)
    out = pl.pallas_call(
        _combine_kernel,
        grid=(nf,),
        in_specs=[
            pl.BlockSpec((1, 1, TOP_K * tf), lambda i: (i, 0, 0), memory_space=pltpu.SMEM),
            pl.BlockSpec((1, 1, TOP_K * tf), lambda i: (jnp.minimum(i + 1, nf - 1), 0, 0), memory_space=pltpu.SMEM),
            pl.BlockSpec((tf * ROW_TILES, LANES), lambda i: (i, 0)),
            pl.BlockSpec((tf, LANES), lambda i: (i, 0)),
            pl.BlockSpec(memory_space=pl.ANY),
            _full((1, D)), _full((1, D)),
        ],
        out_specs=pl.BlockSpec((tf, D), lambda i: (i, 0)),
        out_shape=jax.ShapeDtypeStruct((N, D), F32),
        scratch_shapes=[pltpu.VMEM((2, TOP_K * tf * ROW_TILES, LANES), F32), pltpu.SemaphoreType.DMA((2,))],
        compiler_params=_params(("arbitrary",)),
    )(dest3, dest3, htm, route, ytm, ln2_g[0][None], ln2_b[0][None])
    return out.reshape(B, S, D)
```

```python
import functools
import math

import jax
import jax.numpy as jnp
from jax import lax
from jax.experimental import pallas as pl
from jax.experimental.pallas import tpu as pltpu

D_MODEL = 1024
CHUNK = 64
POOL_WINDOWS = (2, 4, 8, 16)
POOL_GROUPS = len(POOL_WINDOWS)
POOL_WIDTH = D_MODEL // 2
POOL_GROUP_DIM = POOL_WIDTH // POOL_GROUPS
POOL_OUT_GROUP_DIM = D_MODEL // POOL_GROUPS
POOL_HALO = 16
N_HEADS = 8
QK_NOPE_DIM = D_MODEL // 16
QK_ROPE_DIM = D_MODEL // 32
HALF_ROPE = QK_ROPE_DIM // 2
V_HEAD_DIM = D_MODEL // 16
V_ROWS = V_HEAD_DIM + 16
Q_LORA_RANK = 3 * D_MODEL // 8
KV_LORA_RANK = D_MODEL // 4
ROPE_THETA = 10000.0
N_GROUPS = 4
EXPERTS_PER_GROUP = 8
N_EXPERTS = N_GROUPS * EXPERTS_PER_GROUP
TOP_K = 2
D_EXPERT = D_MODEL // 2
NORM_EPS = 1e-5
DEPTH = 1
DEEPNORM_ALPHA = (2.0 * DEPTH) ** 0.25

LANES = 128
SUBLANES = 8
HEAD_PAD = LANES
ROW_TILES = D_MODEL // LANES
GROUP_LANE0 = 64
NEG_BIG = -1e30
VMEM_LIMIT = 56 * 1024 * 1024

TILE_LATENT = 512
TILE_Q = 512
TILE_K = 256
ATTN_HEADS_PER_STEP = 4
TILE_MIX = 512
GATHER_AHEAD = 2
GATHER_SLOTS = GATHER_AHEAD + 1
TILE_EXPERT = 256
TILE_COMBINE = 256

F32 = jnp.float32
BF16 = jnp.bfloat16


def _dot(a, b):
    return jnp.dot(a, b, preferred_element_type=F32)


def _dot_nt(a, b):
    return lax.dot_general(a, b, (((1,), (1,)), ((), ())), preferred_element_type=F32)


def _dot_tn(a, b):
    return lax.dot_general(a, b, (((0,), (0,)), ((), ())), preferred_element_type=F32)


def _rms(v, g):
    ms = jnp.mean(jnp.square(v), axis=-1, keepdims=True)
    return v * lax.rsqrt(ms + NORM_EPS) * g


def _layer_norm(v, g, b):
    mu = jnp.mean(v, axis=-1, keepdims=True)
    c = v - mu
    var = jnp.mean(jnp.square(c), axis=-1, keepdims=True)
    return c * lax.rsqrt(var + NORM_EPS) * g + b


def _latent_kernel(x_ref, wlat_ref, gq_ref, gkv_ref, wuqT_ref, wuk_ref, wuvT_ref,
                   cosT_ref, sinT_ref, ra_ref, rm_ref, rp_ref,
                   qT_ref, k_ref, vT_ref, *, q_scale):
    xb = x_ref[...].astype(BF16)
    lat = _dot(xb, wlat_ref[...])
    c_q = lat[:, :Q_LORA_RANK]
    c_kv = lat[:, Q_LORA_RANK:Q_LORA_RANK + KV_LORA_RANK]
    kpe = lat[:, Q_LORA_RANK + KV_LORA_RANK:]
    qn = _rms(c_q, gq_ref[...]).astype(BF16)
    kvn = _rms(c_kv, gkv_ref[...]).astype(BF16)

    qT = _dot_nt(wuqT_ref[...], qn) * q_scale
    cosT = cosT_ref[...]
    sinT = sinT_ref[...]
    for h in range(N_HEADS):
        r0 = h * HEAD_PAD
        x1 = qT[r0 + QK_NOPE_DIM:r0 + QK_NOPE_DIM + HALF_ROPE]
        x2 = qT[r0 + QK_NOPE_DIM + HALF_ROPE:r0 + QK_NOPE_DIM + QK_ROPE_DIM]
        qT_ref[0, r0:r0 + QK_NOPE_DIM, :] = qT[r0:r0 + QK_NOPE_DIM].astype(BF16)
        qT_ref[0, r0 + QK_NOPE_DIM:r0 + QK_NOPE_DIM + HALF_ROPE, :] = (x1 * cosT - x2 * sinT).astype(BF16)
        qT_ref[0, r0 + QK_NOPE_DIM + HALF_ROPE:r0 + QK_NOPE_DIM + QK_ROPE_DIM, :] = (
            x1 * sinT + x2 * cosT).astype(BF16)
        qT_ref[0, r0 + QK_NOPE_DIM + QK_ROPE_DIM:r0 + HEAD_PAD, :] = jnp.zeros(
            (HEAD_PAD - QK_NOPE_DIM - QK_ROPE_DIM, qT.shape[1]), BF16)

    kpe_rot = (kpe * ra_ref[...] + pltpu.roll(kpe, LANES - HALF_ROPE, 1) * rm_ref[...]
               + pltpu.roll(kpe, HALF_ROPE, 1) * rp_ref[...])
    k = _dot(kvn, wuk_ref[...])
    for h in range(N_HEADS):
        k_ref[0, :, h * HEAD_PAD:(h + 1) * HEAD_PAD] = (k[:, h * HEAD_PAD:(h + 1) * HEAD_PAD] + kpe_rot).astype(BF16)
    vT = _dot_nt(wuvT_ref[...], kvn)
    row = lax.broadcasted_iota(jnp.int32, vT.shape, 0) % V_ROWS
    vT_ref[0] = jnp.where(row == V_HEAD_DIM, 1.0, vT).astype(BF16)


def _attn_kernel(qT_ref, k_ref, vT_ref, bias_ref, oT_ref, *bufs):
    qi = pl.program_id(2)
    tq = qT_ref.shape[2]
    heads = range(ATTN_HEADS_PER_STEP)
    s0, s1, p0, p1 = (bufs[i * ATTN_HEADS_PER_STEP:(i + 1) * ATTN_HEADS_PER_STEP] for i in range(4))

    def scores(h, j):
        k0 = pl.multiple_of(j * TILE_K, TILE_K)
        return _dot(k_ref[0, pl.ds(k0, TILE_K), h * HEAD_PAD:(h + 1) * HEAD_PAD],
                    qT_ref[0, h * HEAD_PAD:(h + 1) * HEAD_PAD, :])

    def pv(h, j, p_ref):
        k0 = pl.multiple_of(j * TILE_K, TILE_K)
        return _dot(vT_ref[0, h * V_ROWS:(h + 1) * V_ROWS, pl.ds(k0, TILE_K)], p_ref[...])

    def col_max(s):
        while s.shape[0] > SUBLANES:
            half = s.shape[0] // 2
            s = jnp.maximum(s[:half], s[half:])
        return jnp.max(s, axis=0, keepdims=True)

    def step(h, j, s_cur, s_nxt, p_prev, p_cur, carry, bias=None):
        m, acc, alpha_prev = carry
        if s_nxt is not None:
            s_nxt[h][...] = scores(h, j + 1)
        acc = alpha_prev * acc + pv(h, jnp.maximum(j - 1, 0), p_prev[h])
        s = s_cur[h][...]
        if bias is not None:
            s = s + bias
        m_new = jnp.maximum(m, col_max(s))
        p_cur[h][...] = jnp.exp2(s - m_new).astype(BF16)
        return m_new, acc, jnp.exp2(m - m_new)

    for h in heads:
        p1[h][...] = jnp.zeros_like(p1[h])
        s0[h][...] = scores(h, 0)
    carry = tuple((jnp.full((1, tq), NEG_BIG, F32), jnp.zeros((V_ROWS, tq), F32), jnp.ones((1, tq), F32))
                  for _ in heads)

    def pair(t, c):
        c = tuple(step(h, 2 * t, s0, s1, p1, p0, c[h]) for h in heads)
        return tuple(step(h, 2 * t + 1, s1, s0, p0, p1, c[h]) for h in heads)

    carry = lax.fori_loop(0, qi, pair, carry)
    jd = 2 * qi
    carry = tuple(step(h, jd, s0, s1, p1, p0, carry[h], bias=bias_ref[0]) for h in heads)
    carry = tuple(step(h, jd + 1, s1, None, p0, p1, carry[h], bias=bias_ref[1]) for h in heads)
    for h in heads:
        _, acc, alpha = carry[h]
        acc = alpha * acc + pv(h, jd + 1, p1[h])
        oT_ref[0, h * V_HEAD_DIM:(h + 1) * V_HEAD_DIM, :] = (
            acc[:V_HEAD_DIM] / acc[V_HEAD_DIM:V_HEAD_DIM + 1]).astype(BF16)


def _mix_kernel(x_ref, oT_ref, wpool_ref, wgate_ref, mixw_ref, pscale_ref, wo_ref, wout_ref,
                g1_ref, b1_ref, wr_ref, br_ref, tri_ref,
                htm_ref, route_ref, routeT_ref, cnt_ref, ext_ref, base_ref):
    b = pl.program_id(0)
    si = pl.program_id(1)
    tm = x_ref.shape[0]

    @pl.when(jnp.logical_and(b == 0, si == 0))
    def _():
        base_ref[...] = jnp.zeros_like(base_ref)

    @pl.when(si == 0)
    def _():
        ext_ref[0:POOL_HALO, :] = jnp.zeros((POOL_HALO, POOL_WIDTH), F32)

    x = x_ref[...]
    xb = x.astype(BF16)
    u = _dot(xb, wpool_ref[...])
    ext_ref[POOL_HALO:POOL_HALO + tm, :] = u

    pos1 = si * tm + lax.broadcasted_iota(jnp.int32, (tm, POOL_GROUP_DIM), 0) + 1
    y_parts = []
    for g, win in enumerate(POOL_WINDOWS):
        c0 = g * POOL_GROUP_DIM
        ws = u[:, c0:c0 + POOL_GROUP_DIM]
        for kk in range(1, win):
            ws = ws + ext_ref[POOL_HALO - kk:POOL_HALO - kk + tm, c0:c0 + POOL_GROUP_DIM]
        count = jnp.minimum(pos1, win).astype(F32)
        pooled = ws / count - u[:, c0:c0 + POOL_GROUP_DIM]
        y_parts.append(_dot(pooled.astype(BF16), mixw_ref[g]))
    ext_ref[0:POOL_HALO, :] = ext_ref[tm:tm + POOL_HALO, :]
    y_pool = jnp.concatenate(y_parts, axis=-1) * pscale_ref[...]

    glog = _dot(xb, wgate_ref[...])
    y_mla = _dot_tn(oT_ref[0], wo_ref[...])
    merged = (jax.nn.sigmoid(glog[:, :D_MODEL]) * y_pool + jax.nn.sigmoid(glog[:, D_MODEL:]) * y_mla)
    r = DEEPNORM_ALPHA * x + _dot(merged.astype(BF16), wout_ref[...])
    h = _layer_norm(r, g1_ref[...], b1_ref[...])
    for j in range(ROW_TILES):
        htm_ref[pl.ds(j, tm, stride=ROW_TILES), :] = h[:, j * LANES:(j + 1) * LANES]

    logits = _dot(h.astype(BF16), wr_ref[...]) + br_ref[...]
    lane_i = lax.broadcasted_iota(jnp.int32, logits.shape, 1)
    lane = lane_i.astype(F32)
    big = float(LANES)
    is_group = jnp.logical_and(lane_i >= GROUP_LANE0, lane_i < GROUP_LANE0 + N_GROUPS)
    gl = jnp.where(is_group, logits, NEG_BIG)
    gmax = jnp.max(gl, axis=-1, keepdims=True)
    g_w = 1.0 / jnp.sum(jnp.exp(gl - gmax), axis=-1, keepdims=True)
    g_idx = jnp.min(jnp.where(gl == gmax, lane - GROUP_LANE0, big), axis=-1, keepdims=True)
    lane_group = (lane_i // EXPERTS_PER_GROUP).astype(F32)
    in_group = jnp.logical_and(lane_i < N_EXPERTS, lane_group == g_idx)
    el = jnp.where(in_group, logits, NEG_BIG)
    e1max = jnp.max(el, axis=-1, keepdims=True)
    i1 = jnp.min(jnp.where(el == e1max, lane, big), axis=-1, keepdims=True)
    el2 = jnp.where(lane == i1, NEG_BIG, el)
    e2max = jnp.max(el2, axis=-1, keepdims=True)
    i2 = jnp.min(jnp.where(el2 == e2max, lane, big), axis=-1, keepdims=True)
    ratio = jnp.exp(e2max - e1max)
    gate1 = g_w / (1.0 + ratio)
    gate2 = g_w * ratio / (1.0 + ratio)

    hit1 = lane == i1
    hit2 = lane == i2
    onehot = jnp.where(jnp.logical_or(hit1, hit2), 1.0, 0.0)
    before = _dot(tri_ref[...], onehot.astype(BF16)) + base_ref[0:1, :]
    rank1 = jnp.sum(jnp.where(hit1, before, 0.0), axis=-1, keepdims=True)
    rank2 = jnp.sum(jnp.where(hit2, before, 0.0), axis=-1, keepdims=True)
    new_base = base_ref[0:1, :] + jnp.sum(onehot, axis=0, keepdims=True)
    base_ref[...] = jnp.broadcast_to(new_base, base_ref.shape)
    cnt_ref[...] = jnp.broadcast_to(new_base, cnt_ref.shape)

    route = jnp.where(lane_i == 0, i1, 0.0)
    route = jnp.where(lane_i == 1, i2, route)
    route = jnp.where(lane_i == 2, gate1, route)
    route = jnp.where(lane_i == 3, gate2, route)
    route = jnp.where(lane_i == 4, rank1, route)
    route = jnp.where(lane_i == 5, rank2, route)
    route_ref[...] = route
    routeT_ref[...] = route.T[:SUBLANES]


def _gather_row(idx_ref, r, src_hbm, dst_ref, slot, sem, priority=0):
    src0 = pl.multiple_of(idx_ref[0, 0, r] * ROW_TILES, ROW_TILES)
    pltpu.make_async_copy(src_hbm.at[pl.ds(src0, ROW_TILES), :],
                          dst_ref.at[slot, pl.ds(r * ROW_TILES, ROW_TILES), :],
                          sem.at[slot]).start(priority=priority)


def _gather_rows_loop(idx_ref, src_hbm, dst_ref, slot, sem, n_rows, unroll=8):
    def body(c, _):
        for u in range(unroll):
            _gather_row(idx_ref, c * unroll + u, src_hbm, dst_ref, slot, sem)
        return 0
    lax.fori_loop(0, n_rows // unroll, body, 0)


def _gather_rows_inline(idx_ref, src_hbm, dst_ref, slot, sem, n_rows):
    for r in range(n_rows):
        _gather_row(idx_ref, r, src_hbm, dst_ref, slot, sem, priority=r % 2)


def _wait_rows(dst_ref, slot, sem):
    pltpu.make_async_copy(dst_ref.at[slot], dst_ref.at[slot], sem.at[slot]).wait()


def _expert_kernel(be_ref, nused_ref, tok_ref, tok1_ref, tok2_ref, htm_hbm, wg_ref, wu_ref, wd_ref,
                   y_ref, xbuf, wgb, wub, wdb, sem):
    i = pl.program_id(0)
    n_used = nused_ref[0]
    slot = lax.rem(i, GATHER_SLOTS)
    tb = TILE_EXPERT

    @pl.when(i == 0)
    def _():
        _gather_rows_loop(tok_ref, htm_hbm, xbuf, 0, sem, tb)
        _gather_rows_loop(tok1_ref, htm_hbm, xbuf, 1, sem, tb)

    changed = jnp.logical_or(i == 0, be_ref[i] != be_ref[jnp.maximum(i - 1, 0)])

    @pl.when(jnp.logical_and(changed, i < n_used))
    def _():
        wgb[...] = wg_ref[0].astype(BF16)
        wub[...] = wu_ref[0].astype(BF16)
        wdb[...] = wd_ref[0].astype(BF16)

    @pl.when(jnp.logical_and(i >= n_used, i < n_used + GATHER_AHEAD))
    def _():
        _wait_rows(xbuf, slot, sem)

    @pl.when(i < n_used)
    def _():
        _wait_rows(xbuf, slot, sem)
        _gather_rows_inline(tok2_ref, htm_hbm, xbuf, lax.rem(i + GATHER_AHEAD, GATHER_SLOTS), sem, tb)
        xs = jnp.concatenate(
            [xbuf[slot, pl.ds(j, tb, stride=ROW_TILES), :] for j in range(ROW_TILES)], axis=-1).astype(BF16)
        gate = _dot(xs, wgb[...])
        up = _dot(xs, wub[...])
        hid = (jax.nn.silu(gate) * up).astype(BF16)
        y = _dot(hid, wdb[...])
        for j in range(ROW_TILES):
            y_ref[pl.ds(j, tb, stride=ROW_TILES), :] = y[:, j * LANES:(j + 1) * LANES]

    @pl.when(i >= n_used)
    def _():
        y_ref[...] = jnp.zeros_like(y_ref)


def _combine_kernel(dst_ref, dst1_ref, dst2_ref, htm_ref, route_ref, ytm_hbm, g2_ref, b2_ref, o_ref, gbuf, sem):
    i = pl.program_id(0)
    n = pl.num_programs(0)
    slot = lax.rem(i, GATHER_SLOTS)
    tf = TILE_COMBINE

    @pl.when(i == 0)
    def _():
        _gather_rows_loop(dst_ref, ytm_hbm, gbuf, 0, sem, TOP_K * tf)
        _gather_rows_loop(dst1_ref, ytm_hbm, gbuf, 1, sem, TOP_K * tf)

    _wait_rows(gbuf, slot, sem)
    _gather_rows_inline(dst2_ref, ytm_hbm, gbuf, lax.rem(i + GATHER_AHEAD, GATHER_SLOTS), sem, TOP_K * tf)
    route = route_ref[...]
    gate1 = route[:, 2:3]
    gate2 = route[:, 3:4]
    parts = []
    for j in range(ROW_TILES):
        hj = htm_ref[pl.ds(j, tf, stride=ROW_TILES), :]
        y1 = gbuf[slot, pl.ds(j, tf, stride=ROW_TILES), :]
        y2 = gbuf[slot, pl.ds(tf * ROW_TILES + j, tf, stride=ROW_TILES), :]
        parts.append(DEEPNORM_ALPHA * hj + (gate1 * y1 + gate2 * y2))
    z = jnp.concatenate(parts, axis=-1)
    o_ref[...] = _layer_norm(z, g2_ref[...], b2_ref[...])

    @pl.when(i == n - 1)
    def _():
        for ahead in range(1, GATHER_AHEAD + 1):
            _wait_rows(gbuf, lax.rem(i + ahead, GATHER_SLOTS), sem)


def _rope_tables(seq):
    inv = 1.0 / (ROPE_THETA ** (jnp.arange(0, QK_ROPE_DIM, 2, dtype=F32) / QK_ROPE_DIM))
    ang = jnp.arange(seq, dtype=F32)[:, None] * inv[None, :]
    cos, sin = jnp.cos(ang), jnp.sin(ang)
    zeros = jnp.zeros((seq, QK_NOPE_DIM), F32)
    pad = jnp.zeros((seq, HEAD_PAD - QK_NOPE_DIM - QK_ROPE_DIM), F32)
    z16 = jnp.zeros((seq, HALF_ROPE), F32)
    rot_a = jnp.concatenate([zeros, cos, cos, pad], axis=1)
    rot_m = jnp.concatenate([zeros, -sin, z16, pad], axis=1)
    rot_p = jnp.concatenate([zeros, z16, sin, pad], axis=1)
    return cos.T, sin.T, rot_a, rot_m, rot_p


def _full(shape):
    return pl.BlockSpec(shape, lambda *_: (0,) * len(shape))


def _params(sem):
    return pltpu.CompilerParams(dimension_semantics=sem, vmem_limit_bytes=VMEM_LIMIT)


def kernel(x, w_in, pool_mix_w, pool_scale, q_norm_g, w_uq, kv_norm_g, w_ukv, w_mla_o, w_out, ln1_g, ln1_b,
           w_router_group, b_router_group, w_router_expert, b_router_expert, w_gate, w_up, w_down, ln2_g, ln2_b):
    B, S, D = x.shape
    assert D == D_MODEL and w_in.shape[0] == DEPTH == 1
    assert S % TILE_LATENT == 0 and S % TILE_Q == 0 and S % TILE_MIX == 0 and TILE_Q % TILE_K == 0
    N = B * S
    assert N % TILE_COMBINE == 0
    H = N_HEADS

    w = w_in[0]
    o1 = POOL_WIDTH
    o2 = o1 + Q_LORA_RANK
    o3 = o2 + KV_LORA_RANK
    o4 = o3 + QK_ROPE_DIM
    w_pool = w[:, :o1].astype(BF16)
    kpe_cols = jnp.pad(w[:, o3:o4], ((0, 0), (QK_NOPE_DIM, HEAD_PAD - QK_NOPE_DIM - QK_ROPE_DIM)))
    w_lat = jnp.concatenate([w[:, o1:o3], kpe_cols], axis=1).astype(BF16)
    w_gates = w[:, o4:].astype(BF16)
    qd = QK_NOPE_DIM + QK_ROPE_DIM
    wuq = jnp.pad(w_uq[0].reshape(Q_LORA_RANK, H, qd), ((0, 0), (0, 0), (0, HEAD_PAD - qd)))
    wuqT = wuq.reshape(Q_LORA_RANK, H * HEAD_PAD).T.astype(BF16)
    wukv = w_ukv[0].reshape(KV_LORA_RANK, H, QK_NOPE_DIM + V_HEAD_DIM)
    wuk = jnp.pad(wukv[:, :, :QK_NOPE_DIM], ((0, 0), (0, 0), (0, HEAD_PAD - QK_NOPE_DIM)))
    wuk = wuk.reshape(KV_LORA_RANK, H * HEAD_PAD).astype(BF16)
    wuv = jnp.pad(wukv[:, :, QK_NOPE_DIM:], ((0, 0), (0, 0), (0, V_ROWS - V_HEAD_DIM)))
    wuvT = wuv.reshape(KV_LORA_RANK, H * V_ROWS).T.astype(BF16)
    w_r = jnp.zeros((D, LANES), F32)
    w_r = w_r.at[:, :N_EXPERTS].set(w_router_expert[0]).at[:, GROUP_LANE0:GROUP_LANE0 + N_GROUPS].set(
        w_router_group[0]).astype(BF16)
    b_r = jnp.zeros((1, LANES), F32)
    b_r = b_r.at[0, :N_EXPERTS].set(b_router_expert[0]).at[0, GROUP_LANE0:GROUP_LANE0 + N_GROUPS].set(
        b_router_group[0])
    cosT, sinT, rot_a, rot_m, rot_p = _rope_tables(S)
    q_scale = (QK_NOPE_DIM + QK_ROPE_DIM) ** -0.5 * math.log2(math.e)
    x2 = x.reshape(N, D)

    ta = TILE_LATENT
    nsa = S // ta
    qT, k, vT = pl.pallas_call(
        functools.partial(_latent_kernel, q_scale=q_scale),
        grid=(B, nsa),
        in_specs=[
            pl.BlockSpec((ta, D), lambda b, s: (b * nsa + s, 0)),
            _full(w_lat.shape), _full((1, Q_LORA_RANK)), _full((1, KV_LORA_RANK)),
            _full(wuqT.shape), _full(wuk.shape), _full(wuvT.shape),
            pl.BlockSpec((HALF_ROPE, ta), lambda b, s: (0, s)),
            pl.BlockSpec((HALF_ROPE, ta), lambda b, s: (0, s)),
            pl.BlockSpec((ta, LANES), lambda b, s: (s, 0)),
            pl.BlockSpec((ta, LANES), lambda b, s: (s, 0)),
            pl.BlockSpec((ta, LANES), lambda b, s: (s, 0)),
        ],
        out_specs=[
            pl.BlockSpec((1, H * HEAD_PAD, ta), lambda b, s: (b, 0, s)),
            pl.BlockSpec((1, ta, H * HEAD_PAD), lambda b, s: (b, s, 0)),
            pl.BlockSpec((1, H * V_ROWS, ta), lambda b, s: (b, 0, s)),
        ],
        out_shape=[
            jax.ShapeDtypeStruct((B, H * HEAD_PAD, S), BF16),
            jax.ShapeDtypeStruct((B, S, H * HEAD_PAD), BF16),
            jax.ShapeDtypeStruct((B, H * V_ROWS, S), BF16),
        ],
        compiler_params=_params(("parallel", "parallel")),
    )(x2, w_lat, q_norm_g[0][None], kv_norm_g[0][None], wuqT, wuk, wuvT, cosT, sinT, rot_a, rot_m, rot_p)

    tq = TILE_Q
    assert tq == 2 * TILE_K
    key_chunk = jnp.arange(tq)[:, None] // CHUNK
    q_chunk = jnp.arange(tq)[None, :] // CHUNK
    mask_bias = jnp.where(key_chunk <= q_chunk, 0.0, NEG_BIG).astype(F32).reshape(2, TILE_K, tq)
    hp = ATTN_HEADS_PER_STEP
    assert H % hp == 0
    oT = pl.pallas_call(
        _attn_kernel,
        grid=(B, H // hp, S // tq),
        in_specs=[
            pl.BlockSpec((1, hp * HEAD_PAD, tq), lambda b, h, q: (b, h, q)),
            pl.BlockSpec((1, S, hp * HEAD_PAD), lambda b, h, q: (b, 0, h)),
            pl.BlockSpec((1, hp * V_ROWS, S), lambda b, h, q: (b, h, 0)),
            _full(mask_bias.shape),
        ],
        out_specs=pl.BlockSpec((1, hp * V_HEAD_DIM, tq), lambda b, h, q: (b, h, q)),
        out_shape=jax.ShapeDtypeStruct((B, H * V_HEAD_DIM, S), BF16),
        scratch_shapes=([pltpu.VMEM((TILE_K, tq), F32)] * (2 * hp) + [pltpu.VMEM((TILE_K, tq), BF16)] * (2 * hp)),
        compiler_params=_params(("parallel", "parallel", "arbitrary")),
    )(qT, k, vT, mask_bias)

    tm = TILE_MIX
    nsm = S // tm
    tri = (jnp.arange(tm)[:, None] > jnp.arange(tm)[None, :]).astype(BF16)
    htm, route, routeT, counts = pl.pallas_call(
        _mix_kernel,
        grid=(B, nsm),
        in_specs=[
            pl.BlockSpec((tm, D), lambda b, s: (b * nsm + s, 0)),
            pl.BlockSpec((1, H * V_HEAD_DIM, tm), lambda b, s: (b, 0, s)),
            _full(w_pool.shape), _full(w_gates.shape), _full(pool_mix_w.shape[1:]), _full((1, D)),
            _full(w_mla_o.shape[1:]), _full(w_out.shape[1:]), _full((1, D)), _full((1, D)),
            _full(w_r.shape), _full(b_r.shape), _full(tri.shape),
        ],
        out_specs=[
            pl.BlockSpec((tm * ROW_TILES, LANES), lambda b, s: (b * nsm + s, 0)),
            pl.BlockSpec((tm, LANES), lambda b, s: (b * nsm + s, 0)),
            pl.BlockSpec((SUBLANES, tm), lambda b, s: (0, b * nsm + s)),
            _full((SUBLANES, LANES)),
        ],
        out_shape=[
            jax.ShapeDtypeStruct((N * ROW_TILES, LANES), F32),
            jax.ShapeDtypeStruct((N, LANES), F32),
            jax.ShapeDtypeStruct((SUBLANES, N), F32),
            jax.ShapeDtypeStruct((SUBLANES, LANES), F32),
        ],
        scratch_shapes=[pltpu.VMEM((tm + POOL_HALO, POOL_WIDTH), F32), pltpu.VMEM((SUBLANES, LANES), F32)],
        compiler_params=_params(("arbitrary", "arbitrary")),
    )(x2, oT, w_pool, w_gates, pool_mix_w[0].astype(BF16), pool_scale[0][None], w_mla_o[0].astype(BF16),
      w_out[0].astype(BF16), ln1_g[0][None], ln1_b[0][None], w_r, b_r, tri)

    tb = TILE_EXPERT
    A = N * TOP_K
    n_blocks = -(-(A + N_EXPERTS * (tb - 1)) // tb) + GATHER_AHEAD
    R = n_blocks * tb
    e_idx = routeT[0:2].astype(jnp.int32)
    rank = routeT[4:6].astype(jnp.int32)
    cnt = counts[0, :N_EXPERTS].astype(jnp.int32)
    padded = ((cnt + tb - 1) // tb) * tb
    pad_end = jnp.cumsum(padded)
    pad_start = pad_end - padded
    is_e = e_idx[:, None, :] == jnp.arange(N_EXPERTS, dtype=jnp.int32)[None, :, None]
    dest = jnp.sum(jnp.where(is_e, pad_start[None, :, None], 0), axis=1) + rank
    tok = jnp.broadcast_to(jnp.arange(N, dtype=jnp.int32)[None, :], (TOP_K, N))
    row_tok = jnp.zeros((R,), jnp.int32).at[dest.reshape(A)].set(tok.reshape(A))
    block_start = jnp.arange(n_blocks, dtype=jnp.int32) * tb
    block_e = jnp.minimum(jnp.sum(pad_end[None, :] <= block_start[:, None], axis=1), N_EXPERTS - 1).astype(jnp.int32)
    n_used = (pad_end[-1] // tb).astype(jnp.int32)[None]
    row_tok3 = row_tok.reshape(n_blocks, 1, tb)

    ytm = pl.pallas_call(
        _expert_kernel,
        grid_spec=pltpu.PrefetchScalarGridSpec(
            num_scalar_prefetch=2,
            grid=(n_blocks,),
            in_specs=[
                pl.BlockSpec((1, 1, tb), lambda i, be, nu: (i, 0, 0), memory_space=pltpu.SMEM),
                pl.BlockSpec((1, 1, tb), lambda i, be, nu: (jnp.minimum(i + 1, n_blocks - 1), 0, 0),
                             memory_space=pltpu.SMEM),
                pl.BlockSpec((1, 1, tb), lambda i, be, nu: (jnp.minimum(i + GATHER_AHEAD, n_blocks - 1), 0, 0),
                             memory_space=pltpu.SMEM),
                pl.BlockSpec(memory_space=pl.ANY),
                pl.BlockSpec((1, D, D_EXPERT), lambda i, be, nu: (be[i], 0, 0)),
                pl.BlockSpec((1, D, D_EXPERT), lambda i, be, nu: (be[i], 0, 0)),
                pl.BlockSpec((1, D_EXPERT, D), lambda i, be, nu: (be[i], 0, 0)),
            ],
            out_specs=pl.BlockSpec((tb * ROW_TILES, LANES), lambda i, be, nu: (i, 0)),
            scratch_shapes=[
                pltpu.VMEM((GATHER_SLOTS, tb * ROW_TILES, LANES), F32),
                pltpu.VMEM((D, D_EXPERT), BF16), pltpu.VMEM((D, D_EXPERT), BF16), pltpu.VMEM((D_EXPERT, D), BF16),
                pltpu.SemaphoreType.DMA((GATHER_SLOTS,)),
            ],
        ),
        out_shape=jax.ShapeDtypeStruct((R * ROW_TILES, LANES), F32),
        compiler_params=_params(("arbitrary",)),
    )(block_e, n_used, row_tok3, row_tok3, row_tok3, htm, w_gate[0], w_up[0], w_down[0])

    tf = TILE_COMBINE
    nf = N // tf
    dest3 = dest.reshape(TOP_K, nf, tf).transpose(1, 0, 2).reshape(nf, 1, TOP_K * tf)
    out = pl.pallas_call(
        _combine_kernel,
        grid=(nf,),
        in_specs=[
            pl.BlockSpec((1, 1, TOP_K * tf), lambda i: (i, 0, 0), memory_space=pltpu.SMEM),
            pl.BlockSpec((1, 1, TOP_K * tf), lambda i: (jnp.minimum(i + 1, nf - 1), 0, 0), memory_space=pltpu.SMEM),
            pl.BlockSpec((1, 1, TOP_K * tf), lambda i: (jnp.minimum(i + GATHER_AHEAD, nf - 1), 0, 0),
                         memory_space=pltpu.SMEM),
            pl.BlockSpec((tf * ROW_TILES, LANES), lambda i: (i, 0)),
            pl.BlockSpec((tf, LANES), lambda i: (i, 0)),
            pl.BlockSpec(memory_space=pl.ANY),
            _full((1, D)), _full((1, D)),
        ],
        out_specs=pl.BlockSpec((tf, D), lambda i: (i, 0)),
        out_shape=jax.ShapeDtypeStruct((N, D), F32),
        scratch_shapes=[pltpu.VMEM((GATHER_SLOTS, TOP_K * tf * ROW_TILES, LANES), F32),
                        pltpu.SemaphoreType.DMA((GATHER_SLOTS,))],
        compiler_params=_params(("arbitrary",)),
    )(dest3, dest3, dest3, htm, route, ytm, ln2_g[0][None], ln2_b[0][None])
    return out.reshape(B, S, D)
```

```python
import functools
import math

import jax
import jax.numpy as jnp
from jax import lax
from jax.experimental import pallas as pl
from jax.experimental.pallas import tpu as pltpu

D_MODEL = 1024
CHUNK = 64
POOL_WINDOWS = (2, 4, 8, 16)
POOL_GROUPS = len(POOL_WINDOWS)
POOL_WIDTH = D_MODEL // 2
POOL_GROUP_DIM = POOL_WIDTH // POOL_GROUPS
POOL_OUT_GROUP_DIM = D_MODEL // POOL_GROUPS
POOL_HALO = 16
N_HEADS = 8
QK_NOPE_DIM = D_MODEL // 16
QK_ROPE_DIM = D_MODEL // 32
HALF_ROPE = QK_ROPE_DIM // 2
V_HEAD_DIM = D_MODEL // 16
V_ROWS = V_HEAD_DIM + 16
Q_LORA_RANK = 3 * D_MODEL // 8
KV_LORA_RANK = D_MODEL // 4
ROPE_THETA = 10000.0
N_GROUPS = 4
EXPERTS_PER_GROUP = 8
N_EXPERTS = N_GROUPS * EXPERTS_PER_GROUP
TOP_K = 2
D_EXPERT = D_MODEL // 2
NORM_EPS = 1e-5
DEPTH = 1
DEEPNORM_ALPHA = (2.0 * DEPTH) ** 0.25

LANES = 128
SUBLANES = 8
HEAD_PAD = LANES
ROW_TILES = D_MODEL // LANES
GROUP_LANE0 = 64
NEG_BIG = -1e30
VMEM_LIMIT = 56 * 1024 * 1024

TILE_LATENT = 512
TILE_Q = 512
TILE_K = 256
ATTN_HEADS_PER_STEP = 4
TILE_MIX = 512
GATHER_AHEAD = 2
GATHER_SLOTS = GATHER_AHEAD + 1
TILE_EXPERT = 256
TILE_COMBINE = 256

F32 = jnp.float32
BF16 = jnp.bfloat16


def _dot(a, b):
    return jnp.dot(a, b, preferred_element_type=F32)


def _dot_nt(a, b):
    return lax.dot_general(a, b, (((1,), (1,)), ((), ())), preferred_element_type=F32)


def _dot_tn(a, b):
    return lax.dot_general(a, b, (((0,), (0,)), ((), ())), preferred_element_type=F32)


def _rms(v, g):
    ms = jnp.mean(jnp.square(v), axis=-1, keepdims=True)
    return v * lax.rsqrt(ms + NORM_EPS) * g


def _layer_norm(v, g, b):
    mu = jnp.mean(v, axis=-1, keepdims=True)
    c = v - mu
    var = jnp.mean(jnp.square(c), axis=-1, keepdims=True)
    return c * lax.rsqrt(var + NORM_EPS) * g + b


def _latent_kernel(x_ref, wlat_ref, gq_ref, gkv_ref, wuqT_ref, wuk_ref, wuvT_ref,
                   cosT_ref, sinT_ref, ra_ref, rm_ref, rp_ref,
                   qT_ref, k_ref, vT_ref, *, q_scale):
    xb = x_ref[...].astype(BF16)
    lat = _dot(xb, wlat_ref[...])
    c_q = lat[:, :Q_LORA_RANK]
    c_kv = lat[:, Q_LORA_RANK:Q_LORA_RANK + KV_LORA_RANK]
    kpe = lat[:, Q_LORA_RANK + KV_LORA_RANK:]
    qn = _rms(c_q, gq_ref[...]).astype(BF16)
    kvn = _rms(c_kv, gkv_ref[...]).astype(BF16)

    qT = _dot_nt(wuqT_ref[...], qn) * q_scale
    cosT = cosT_ref[...]
    sinT = sinT_ref[...]
    for h in range(N_HEADS):
        r0 = h * HEAD_PAD
        x1 = qT[r0 + QK_NOPE_DIM:r0 + QK_NOPE_DIM + HALF_ROPE]
        x2 = qT[r0 + QK_NOPE_DIM + HALF_ROPE:r0 + QK_NOPE_DIM + QK_ROPE_DIM]
        qT_ref[0, r0:r0 + QK_NOPE_DIM, :] = qT[r0:r0 + QK_NOPE_DIM].astype(BF16)
        qT_ref[0, r0 + QK_NOPE_DIM:r0 + QK_NOPE_DIM + HALF_ROPE, :] = (x1 * cosT - x2 * sinT).astype(BF16)
        qT_ref[0, r0 + QK_NOPE_DIM + HALF_ROPE:r0 + QK_NOPE_DIM + QK_ROPE_DIM, :] = (
            x1 * sinT + x2 * cosT).astype(BF16)
        qT_ref[0, r0 + QK_NOPE_DIM + QK_ROPE_DIM:r0 + HEAD_PAD, :] = jnp.zeros(
            (HEAD_PAD - QK_NOPE_DIM - QK_ROPE_DIM, qT.shape[1]), BF16)

    kpe_rot = (kpe * ra_ref[...] + pltpu.roll(kpe, LANES - HALF_ROPE, 1) * rm_ref[...]
               + pltpu.roll(kpe, HALF_ROPE, 1) * rp_ref[...])
    k = _dot(kvn, wuk_ref[...])
    for h in range(N_HEADS):
        k_ref[0, :, h * HEAD_PAD:(h + 1) * HEAD_PAD] = (k[:, h * HEAD_PAD:(h + 1) * HEAD_PAD] + kpe_rot).astype(BF16)
    vT = _dot_nt(wuvT_ref[...], kvn)
    row = lax.broadcasted_iota(jnp.int32, vT.shape, 0) % V_ROWS
    vT_ref[0] = jnp.where(row == V_HEAD_DIM, 1.0, vT).astype(BF16)


def _attn_kernel(qT_ref, k_ref, vT_ref, bias_ref, oT_ref, *bufs):
    qi = pl.program_id(2)
    tq = qT_ref.shape[2]
    heads = range(ATTN_HEADS_PER_STEP)
    s0, s1, p0, p1 = (bufs[i * ATTN_HEADS_PER_STEP:(i + 1) * ATTN_HEADS_PER_STEP] for i in range(4))

    def scores(h, j):
        k0 = pl.multiple_of(j * TILE_K, TILE_K)
        return _dot(k_ref[0, pl.ds(k0, TILE_K), h * HEAD_PAD:(h + 1) * HEAD_PAD],
                    qT_ref[0, h * HEAD_PAD:(h + 1) * HEAD_PAD, :])

    def pv(h, j, p_ref):
        k0 = pl.multiple_of(j * TILE_K, TILE_K)
        return _dot(vT_ref[0, h * V_ROWS:(h + 1) * V_ROWS, pl.ds(k0, TILE_K)], p_ref[...])

    def col_max(s):
        while s.shape[0] > SUBLANES:
            half = s.shape[0] // 2
            s = jnp.maximum(s[:half], s[half:])
        return jnp.max(s, axis=0, keepdims=True)

    def step(h, j, s_cur, s_nxt, p_prev, p_cur, carry, bias=None):
        m, acc, alpha_prev = carry
        if s_nxt is not None:
            s_nxt[h][...] = scores(h, j + 1)
        acc = alpha_prev * acc + pv(h, jnp.maximum(j - 1, 0), p_prev[h])
        s = s_cur[h][...]
        if bias is not None:
            s = s + bias
        m_new = jnp.maximum(m, col_max(s))
        p_cur[h][...] = jnp.exp2(s - m_new).astype(BF16)
        return m_new, acc, jnp.exp2(m - m_new)

    for h in heads:
        p1[h][...] = jnp.zeros_like(p1[h])
        s0[h][...] = scores(h, 0)
    carry = tuple((jnp.full((1, tq), NEG_BIG, F32), jnp.zeros((V_ROWS, tq), F32), jnp.ones((1, tq), F32))
                  for _ in heads)

    def pair(t, c):
        c = tuple(step(h, 2 * t, s0, s1, p1, p0, c[h]) for h in heads)
        return tuple(step(h, 2 * t + 1, s1, s0, p0, p1, c[h]) for h in heads)

    carry = lax.fori_loop(0, qi, pair, carry)
    jd = 2 * qi
    carry = tuple(step(h, jd, s0, s1, p1, p0, carry[h], bias=bias_ref[0]) for h in heads)
    carry = tuple(step(h, jd + 1, s1, None, p0, p1, carry[h], bias=bias_ref[1]) for h in heads)
    for h in heads:
        _, acc, alpha = carry[h]
        acc = alpha * acc + pv(h, jd + 1, p1[h])
        oT_ref[0, h * V_HEAD_DIM:(h + 1) * V_HEAD_DIM, :] = (
            acc[:V_HEAD_DIM] / acc[V_HEAD_DIM:V_HEAD_DIM + 1]).astype(BF16)


def _mix_kernel(x_ref, oT_ref, wpool_ref, wgate_ref, mixw_ref, pscale_ref, wo_ref, wout_ref,
                g1_ref, b1_ref, wr_ref, br_ref, tri_ref,
                h_ref, route_ref, routeT_ref, cnt_ref, ext_ref, base_ref):
    b = pl.program_id(0)
    si = pl.program_id(1)
    tm = x_ref.shape[0]

    @pl.when(jnp.logical_and(b == 0, si == 0))
    def _():
        base_ref[...] = jnp.zeros_like(base_ref)

    @pl.when(si == 0)
    def _():
        ext_ref[0:POOL_HALO, :] = jnp.zeros((POOL_HALO, POOL_WIDTH), F32)

    x = x_ref[...]
    xb = x.astype(BF16)
    u = _dot(xb, wpool_ref[...])
    ext_ref[POOL_HALO:POOL_HALO + tm, :] = u

    pos1 = si * tm + lax.broadcasted_iota(jnp.int32, (tm, POOL_GROUP_DIM), 0) + 1
    y_parts = []
    for g, win in enumerate(POOL_WINDOWS):
        c0 = g * POOL_GROUP_DIM
        ws = u[:, c0:c0 + POOL_GROUP_DIM]
        for kk in range(1, win):
            ws = ws + ext_ref[POOL_HALO - kk:POOL_HALO - kk + tm, c0:c0 + POOL_GROUP_DIM]
        count = jnp.minimum(pos1, win).astype(F32)
        pooled = ws / count - u[:, c0:c0 + POOL_GROUP_DIM]
        y_parts.append(_dot(pooled.astype(BF16), mixw_ref[g]))
    ext_ref[0:POOL_HALO, :] = ext_ref[tm:tm + POOL_HALO, :]
    y_pool = jnp.concatenate(y_parts, axis=-1) * pscale_ref[...]

    glog = _dot(xb, wgate_ref[...])
    y_mla = _dot_tn(oT_ref[0], wo_ref[...])
    merged = (jax.nn.sigmoid(glog[:, :D_MODEL]) * y_pool + jax.nn.sigmoid(glog[:, D_MODEL:]) * y_mla)
    r = DEEPNORM_ALPHA * x + _dot(merged.astype(BF16), wout_ref[...])
    h = _layer_norm(r, g1_ref[...], b1_ref[...])
    for j in range(ROW_TILES):
        h_ref[pl.ds(j, tm, stride=ROW_TILES), :] = h[:, j * LANES:(j + 1) * LANES]

    logits = _dot(h.astype(BF16), wr_ref[...]) + br_ref[...]
    lane_i = lax.broadcasted_iota(jnp.int32, logits.shape, 1)
    lane = lane_i.astype(F32)
    big = float(LANES)
    is_group = jnp.logical_and(lane_i >= GROUP_LANE0, lane_i < GROUP_LANE0 + N_GROUPS)
    gl = jnp.where(is_group, logits, NEG_BIG)
    gmax = jnp.max(gl, axis=-1, keepdims=True)
    g_w = 1.0 / jnp.sum(jnp.exp(gl - gmax), axis=-1, keepdims=True)
    g_idx = jnp.min(jnp.where(gl == gmax, lane - GROUP_LANE0, big), axis=-1, keepdims=True)
    lane_group = (lane_i // EXPERTS_PER_GROUP).astype(F32)
    in_group = jnp.logical_and(lane_i < N_EXPERTS, lane_group == g_idx)
    el = jnp.where(in_group, logits, NEG_BIG)
    e1max = jnp.max(el, axis=-1, keepdims=True)
    i1 = jnp.min(jnp.where(el == e1max, lane, big), axis=-1, keepdims=True)
    el2 = jnp.where(lane == i1, NEG_BIG, el)
    e2max = jnp.max(el2, axis=-1, keepdims=True)
    i2 = jnp.min(jnp.where(el2 == e2max, lane, big), axis=-1, keepdims=True)
    ratio = jnp.exp(e2max - e1max)
    gate1 = g_w / (1.0 + ratio)
    gate2 = g_w * ratio / (1.0 + ratio)

    hit1 = lane == i1
    hit2 = lane == i2
    onehot = jnp.where(jnp.logical_or(hit1, hit2), 1.0, 0.0)
    before = _dot(tri_ref[...], onehot.astype(BF16)) + base_ref[0:1, :]
    rank1 = jnp.sum(jnp.where(hit1, before, 0.0), axis=-1, keepdims=True)
    rank2 = jnp.sum(jnp.where(hit2, before, 0.0), axis=-1, keepdims=True)
    new_base = base_ref[0:1, :] + jnp.sum(onehot, axis=0, keepdims=True)
    base_ref[...] = jnp.broadcast_to(new_base, base_ref.shape)
    cnt_ref[...] = jnp.broadcast_to(new_base, cnt_ref.shape)

    route = jnp.where(lane_i == 0, i1, 0.0)
    route = jnp.where(lane_i == 1, i2, route)
    route = jnp.where(lane_i == 2, gate1, route)
    route = jnp.where(lane_i == 3, gate2, route)
    route = jnp.where(lane_i == 4, rank1, route)
    route = jnp.where(lane_i == 5, rank2, route)
    route_ref[...] = route
    routeT_ref[...] = route.T[:SUBLANES]


def _gather_row(idx_ref, r, src_hbm, dst_ref, slot, sem, priority=0):
    src0 = pl.multiple_of(idx_ref[0, 0, r] * ROW_TILES, ROW_TILES)
    pltpu.make_async_copy(src_hbm.at[pl.ds(src0, ROW_TILES), :],
                          dst_ref.at[slot, :, r, :],
                          sem.at[slot]).start(priority=priority)


def _gather_rows_loop(idx_ref, src_hbm, dst_ref, slot, sem, n_rows, unroll=8):
    def body(c, _):
        for u in range(unroll):
            _gather_row(idx_ref, c * unroll + u, src_hbm, dst_ref, slot, sem)
        return 0
    lax.fori_loop(0, n_rows // unroll, body, 0)


def _gather_rows_inline(idx_ref, src_hbm, dst_ref, slot, sem, n_rows):
    for r in range(n_rows):
        _gather_row(idx_ref, r, src_hbm, dst_ref, slot, sem, priority=r % 2)


def _wait_rows(dst_ref, slot, sem):
    pltpu.make_async_copy(dst_ref.at[slot], dst_ref.at[slot], sem.at[slot]).wait()


def _expert_kernel(be_ref, nused_ref, tok_ref, tok1_ref, tok2_ref, h_hbm, wg_ref, wu_ref, wd_ref,
                   y_ref, xbuf, xs_ref, wgb, wub, wdb, sem):
    i = pl.program_id(0)
    n_used = nused_ref[0]
    slot = lax.rem(i, GATHER_SLOTS)
    tb = TILE_EXPERT

    @pl.when(i == 0)
    def _():
        _gather_rows_loop(tok_ref, h_hbm, xbuf, 0, sem, tb)
        _gather_rows_loop(tok1_ref, h_hbm, xbuf, 1, sem, tb)

    changed = jnp.logical_or(i == 0, be_ref[i] != be_ref[jnp.maximum(i - 1, 0)])

    @pl.when(jnp.logical_and(changed, i < n_used))
    def _():
        wgb[...] = wg_ref[0].astype(BF16)
        wub[...] = wu_ref[0].astype(BF16)
        wdb[...] = wd_ref[0].astype(BF16)

    @pl.when(jnp.logical_and(i >= n_used, i < n_used + GATHER_AHEAD))
    def _():
        _wait_rows(xbuf, slot, sem)

    @pl.when(i < n_used)
    def _():
        _wait_rows(xbuf, slot, sem)
        for j in range(ROW_TILES):
            xs_ref[:, j * LANES:(j + 1) * LANES] = xbuf[slot, j].astype(BF16)
        _gather_rows_inline(tok2_ref, h_hbm, xbuf, lax.rem(i + GATHER_AHEAD, GATHER_SLOTS), sem, tb)
        xs = xs_ref[...]
        gate = _dot(xs, wgb[...])
        up = _dot(xs, wub[...])
        hid = (jax.nn.silu(gate) * up).astype(BF16)
        y = _dot(hid, wdb[...])
        for j in range(ROW_TILES):
            y_ref[pl.ds(j, tb, stride=ROW_TILES), :] = y[:, j * LANES:(j + 1) * LANES]

    @pl.when(i >= n_used)
    def _():
        y_ref[...] = jnp.zeros_like(y_ref)


def _combine_kernel(dst_ref, dst1_ref, dst2_ref, h_ref, route_ref, y_hbm, g2_ref, b2_ref, o_ref, gbuf, sem):
    i = pl.program_id(0)
    n = pl.num_programs(0)
    slot = lax.rem(i, GATHER_SLOTS)
    tf = TILE_COMBINE

    @pl.when(i == 0)
    def _():
        _gather_rows_loop(dst_ref, y_hbm, gbuf, 0, sem, TOP_K * tf)
        _gather_rows_loop(dst1_ref, y_hbm, gbuf, 1, sem, TOP_K * tf)

    _wait_rows(gbuf, slot, sem)
    route = route_ref[...]
    gate1 = route[:, 2:3]
    gate2 = route[:, 3:4]
    parts = []
    for j in range(ROW_TILES):
        hj = h_ref[pl.ds(j, tf, stride=ROW_TILES), :]
        y1 = gbuf[slot, j, 0:tf, :]
        y2 = gbuf[slot, j, tf:TOP_K * tf, :]
        parts.append(DEEPNORM_ALPHA * hj + (gate1 * y1 + gate2 * y2))
    z = jnp.concatenate(parts, axis=-1)
    o_ref[...] = _layer_norm(z, g2_ref[...], b2_ref[...])
    _gather_rows_inline(dst2_ref, y_hbm, gbuf, lax.rem(i + GATHER_AHEAD, GATHER_SLOTS), sem, TOP_K * tf)

    @pl.when(i == n - 1)
    def _():
        for ahead in range(1, GATHER_AHEAD + 1):
            _wait_rows(gbuf, lax.rem(i + ahead, GATHER_SLOTS), sem)


def _rope_tables(seq):
    inv = 1.0 / (ROPE_THETA ** (jnp.arange(0, QK_ROPE_DIM, 2, dtype=F32) / QK_ROPE_DIM))
    ang = jnp.arange(seq, dtype=F32)[:, None] * inv[None, :]
    cos, sin = jnp.cos(ang), jnp.sin(ang)
    zeros = jnp.zeros((seq, QK_NOPE_DIM), F32)
    pad = jnp.zeros((seq, HEAD_PAD - QK_NOPE_DIM - QK_ROPE_DIM), F32)
    z16 = jnp.zeros((seq, HALF_ROPE), F32)
    rot_a = jnp.concatenate([zeros, cos, cos, pad], axis=1)
    rot_m = jnp.concatenate([zeros, -sin, z16, pad], axis=1)
    rot_p = jnp.concatenate([zeros, z16, sin, pad], axis=1)
    return cos.T, sin.T, rot_a, rot_m, rot_p


def _full(shape):
    return pl.BlockSpec(shape, lambda *_: (0,) * len(shape))


def _params(sem):
    return pltpu.CompilerParams(dimension_semantics=sem, vmem_limit_bytes=VMEM_LIMIT)


def kernel(x, w_in, pool_mix_w, pool_scale, q_norm_g, w_uq, kv_norm_g, w_ukv, w_mla_o, w_out, ln1_g, ln1_b,
           w_router_group, b_router_group, w_router_expert, b_router_expert, w_gate, w_up, w_down, ln2_g, ln2_b):
    B, S, D = x.shape
    assert D == D_MODEL and w_in.shape[0] == DEPTH == 1
    assert S % TILE_LATENT == 0 and S % TILE_Q == 0 and S % TILE_MIX == 0 and TILE_Q % TILE_K == 0
    N = B * S
    assert N % TILE_COMBINE == 0
    H = N_HEADS

    w = w_in[0]
    o1 = POOL_WIDTH
    o2 = o1 + Q_LORA_RANK
    o3 = o2 + KV_LORA_RANK
    o4 = o3 + QK_ROPE_DIM
    w_pool = w[:, :o1].astype(BF16)
    kpe_cols = jnp.pad(w[:, o3:o4], ((0, 0), (QK_NOPE_DIM, HEAD_PAD - QK_NOPE_DIM - QK_ROPE_DIM)))
    w_lat = jnp.concatenate([w[:, o1:o3], kpe_cols], axis=1).astype(BF16)
    w_gates = w[:, o4:].astype(BF16)
    qd = QK_NOPE_DIM + QK_ROPE_DIM
    wuq = jnp.pad(w_uq[0].reshape(Q_LORA_RANK, H, qd), ((0, 0), (0, 0), (0, HEAD_PAD - qd)))
    wuqT = wuq.reshape(Q_LORA_RANK, H * HEAD_PAD).T.astype(BF16)
    wukv = w_ukv[0].reshape(KV_LORA_RANK, H, QK_NOPE_DIM + V_HEAD_DIM)
    wuk = jnp.pad(wukv[:, :, :QK_NOPE_DIM], ((0, 0), (0, 0), (0, HEAD_PAD - QK_NOPE_DIM)))
    wuk = wuk.reshape(KV_LORA_RANK, H * HEAD_PAD).astype(BF16)
    wuv = jnp.pad(wukv[:, :, QK_NOPE_DIM:], ((0, 0), (0, 0), (0, V_ROWS - V_HEAD_DIM)))
    wuvT = wuv.reshape(KV_LORA_RANK, H * V_ROWS).T.astype(BF16)
    w_r = jnp.zeros((D, LANES), F32)
    w_r = w_r.at[:, :N_EXPERTS].set(w_router_expert[0]).at[:, GROUP_LANE0:GROUP_LANE0 + N_GROUPS].set(
        w_router_group[0]).astype(BF16)
    b_r = jnp.zeros((1, LANES), F32)
    b_r = b_r.at[0, :N_EXPERTS].set(b_router_expert[0]).at[0, GROUP_LANE0:GROUP_LANE0 + N_GROUPS].set(
        b_router_group[0])
    cosT, sinT, rot_a, rot_m, rot_p = _rope_tables(S)
    q_scale = (QK_NOPE_DIM + QK_ROPE_DIM) ** -0.5 * math.log2(math.e)
    x2 = x.reshape(N, D)

    ta = TILE_LATENT
    nsa = S // ta
    qT, k, vT = pl.pallas_call(
        functools.partial(_latent_kernel, q_scale=q_scale),
        grid=(B, nsa),
        in_specs=[
            pl.BlockSpec((ta, D), lambda b, s: (b * nsa + s, 0)),
            _full(w_lat.shape), _full((1, Q_LORA_RANK)), _full((1, KV_LORA_RANK)),
            _full(wuqT.shape), _full(wuk.shape), _full(wuvT.shape),
            pl.BlockSpec((HALF_ROPE, ta), lambda b, s: (0, s)),
            pl.BlockSpec((HALF_ROPE, ta), lambda b, s: (0, s)),
            pl.BlockSpec((ta, LANES), lambda b, s: (s, 0)),
            pl.BlockSpec((ta, LANES), lambda b, s: (s, 0)),
            pl.BlockSpec((ta, LANES), lambda b, s: (s, 0)),
        ],
        out_specs=[
            pl.BlockSpec((1, H * HEAD_PAD, ta), lambda b, s: (b, 0, s)),
            pl.BlockSpec((1, ta, H * HEAD_PAD), lambda b, s: (b, s, 0)),
            pl.BlockSpec((1, H * V_ROWS, ta), lambda b, s: (b, 0, s)),
        ],
        out_shape=[
            jax.ShapeDtypeStruct((B, H * HEAD_PAD, S), BF16),
            jax.ShapeDtypeStruct((B, S, H * HEAD_PAD), BF16),
            jax.ShapeDtypeStruct((B, H * V_ROWS, S), BF16),
        ],
        compiler_params=_params(("parallel", "parallel")),
    )(x2, w_lat, q_norm_g[0][None], kv_norm_g[0][None], wuqT, wuk, wuvT, cosT, sinT, rot_a, rot_m, rot_p)

    tq = TILE_Q
    assert tq == 2 * TILE_K
    key_chunk = jnp.arange(tq)[:, None] // CHUNK
    q_chunk = jnp.arange(tq)[None, :] // CHUNK
    mask_bias = jnp.where(key_chunk <= q_chunk, 0.0, NEG_BIG).astype(F32).reshape(2, TILE_K, tq)
    hp = ATTN_HEADS_PER_STEP
    assert H % hp == 0
    oT = pl.pallas_call(
        _attn_kernel,
        grid=(B, H // hp, S // tq),
        in_specs=[
            pl.BlockSpec((1, hp * HEAD_PAD, tq), lambda b, h, q: (b, h, q)),
            pl.BlockSpec((1, S, hp * HEAD_PAD), lambda b, h, q: (b, 0, h)),
            pl.BlockSpec((1, hp * V_ROWS, S), lambda b, h, q: (b, h, 0)),
            _full(mask_bias.shape),
        ],
        out_specs=pl.BlockSpec((1, hp * V_HEAD_DIM, tq), lambda b, h, q: (b, h, q)),
        out_shape=jax.ShapeDtypeStruct((B, H * V_HEAD_DIM, S), BF16),
        scratch_shapes=([pltpu.VMEM((TILE_K, tq), F32)] * (2 * hp) + [pltpu.VMEM((TILE_K, tq), BF16)] * (2 * hp)),
        compiler_params=_params(("parallel", "parallel", "arbitrary")),
    )(qT, k, vT, mask_bias)

    tm = TILE_MIX
    nsm = S // tm
    tri = (jnp.arange(tm)[:, None] > jnp.arange(tm)[None, :]).astype(BF16)
    h_tok, route, routeT, counts = pl.pallas_call(
        _mix_kernel,
        grid=(B, nsm),
        in_specs=[
            pl.BlockSpec((tm, D), lambda b, s: (b * nsm + s, 0)),
            pl.BlockSpec((1, H * V_HEAD_DIM, tm), lambda b, s: (b, 0, s)),
            _full(w_pool.shape), _full(w_gates.shape), _full(pool_mix_w.shape[1:]), _full((1, D)),
            _full(w_mla_o.shape[1:]), _full(w_out.shape[1:]), _full((1, D)), _full((1, D)),
            _full(w_r.shape), _full(b_r.shape), _full(tri.shape),
        ],
        out_specs=[
            pl.BlockSpec((tm * ROW_TILES, LANES), lambda b, s: (b * nsm + s, 0)),
            pl.BlockSpec((tm, LANES), lambda b, s: (b * nsm + s, 0)),
            pl.BlockSpec((SUBLANES, tm), lambda b, s: (0, b * nsm + s)),
            _full((SUBLANES, LANES)),
        ],
        out_shape=[
            jax.ShapeDtypeStruct((N * ROW_TILES, LANES), F32),
            jax.ShapeDtypeStruct((N, LANES), F32),
            jax.ShapeDtypeStruct((SUBLANES, N), F32),
            jax.ShapeDtypeStruct((SUBLANES, LANES), F32),
        ],
        scratch_shapes=[pltpu.VMEM((tm + POOL_HALO, POOL_WIDTH), F32), pltpu.VMEM((SUBLANES, LANES), F32)],
        compiler_params=_params(("arbitrary", "arbitrary")),
    )(x2, oT, w_pool, w_gates, pool_mix_w[0].astype(BF16), pool_scale[0][None], w_mla_o[0].astype(BF16),
      w_out[0].astype(BF16), ln1_g[0][None], ln1_b[0][None], w_r, b_r, tri)

    tb = TILE_EXPERT
    A = N * TOP_K
    n_blocks = -(-(A + N_EXPERTS * (tb - 1)) // tb) + GATHER_AHEAD
    R = n_blocks * tb
    e_idx = routeT[0:2].astype(jnp.int32)
    rank = routeT[4:6].astype(jnp.int32)
    cnt = counts[0, :N_EXPERTS].astype(jnp.int32)
    padded = ((cnt + tb - 1) // tb) * tb
    pad_end = jnp.cumsum(padded)
    pad_start = pad_end - padded
    is_e = e_idx[:, None, :] == jnp.arange(N_EXPERTS, dtype=jnp.int32)[None, :, None]
    dest = jnp.sum(jnp.where(is_e, pad_start[None, :, None], 0), axis=1) + rank
    tok = jnp.broadcast_to(jnp.arange(N, dtype=jnp.int32)[None, :], (TOP_K, N))
    row_tok = jnp.zeros((R,), jnp.int32).at[dest.reshape(A)].set(tok.reshape(A))
    block_start = jnp.arange(n_blocks, dtype=jnp.int32) * tb
    block_e = jnp.minimum(jnp.sum(pad_end[None, :] <= block_start[:, None], axis=1), N_EXPERTS - 1).astype(jnp.int32)
    n_used = (pad_end[-1] // tb).astype(jnp.int32)[None]
    row_tok3 = row_tok.reshape(n_blocks, 1, tb)

    y_rows = pl.pallas_call(
        _expert_kernel,
        grid_spec=pltpu.PrefetchScalarGridSpec(
            num_scalar_prefetch=2,
            grid=(n_blocks,),
            in_specs=[
                pl.BlockSpec((1, 1, tb), lambda i, be, nu: (i, 0, 0), memory_space=pltpu.SMEM),
                pl.BlockSpec((1, 1, tb), lambda i, be, nu: (jnp.minimum(i + 1, n_blocks - 1), 0, 0),
                             memory_space=pltpu.SMEM),
                pl.BlockSpec((1, 1, tb), lambda i, be, nu: (jnp.minimum(i + GATHER_AHEAD, n_blocks - 1), 0, 0),
                             memory_space=pltpu.SMEM),
                pl.BlockSpec(memory_space=pl.ANY),
                pl.BlockSpec((1, D, D_EXPERT), lambda i, be, nu: (be[i], 0, 0)),
                pl.BlockSpec((1, D, D_EXPERT), lambda i, be, nu: (be[i], 0, 0)),
                pl.BlockSpec((1, D_EXPERT, D), lambda i, be, nu: (be[i], 0, 0)),
            ],
            out_specs=pl.BlockSpec((tb * ROW_TILES, LANES), lambda i, be, nu: (i, 0)),
            scratch_shapes=[
                pltpu.VMEM((GATHER_SLOTS, ROW_TILES, tb, LANES), F32), pltpu.VMEM((tb, D), BF16),
                pltpu.VMEM((D, D_EXPERT), BF16), pltpu.VMEM((D, D_EXPERT), BF16), pltpu.VMEM((D_EXPERT, D), BF16),
                pltpu.SemaphoreType.DMA((GATHER_SLOTS,)),
            ],
        ),
        out_shape=jax.ShapeDtypeStruct((R * ROW_TILES, LANES), F32),
        compiler_params=_params(("arbitrary",)),
    )(block_e, n_used, row_tok3, row_tok3, row_tok3, h_tok, w_gate[0], w_up[0], w_down[0])

    tf = TILE_COMBINE
    nf = N // tf
    dest3 = dest.reshape(TOP_K, nf, tf).transpose(1, 0, 2).reshape(nf, 1, TOP_K * tf)
    out = pl.pallas_call(
        _combine_kernel,
        grid=(nf,),
        in_specs=[
            pl.BlockSpec((1, 1, TOP_K * tf), lambda i: (i, 0, 0), memory_space=pltpu.SMEM),
            pl.BlockSpec((1, 1, TOP_K * tf), lambda i: (jnp.minimum(i + 1, nf - 1), 0, 0), memory_space=pltpu.SMEM),
            pl.BlockSpec((1, 1, TOP_K * tf), lambda i: (jnp.minimum(i + GATHER_AHEAD, nf - 1), 0, 0),
                         memory_space=pltpu.SMEM),
            pl.BlockSpec((tf * ROW_TILES, LANES), lambda i: (i, 0)),
            pl.BlockSpec((tf, LANES), lambda i: (i, 0)),
            pl.BlockSpec(memory_space=pl.ANY),
            _full((1, D)), _full((1, D)),
        ],
        out_specs=pl.BlockSpec((tf, D), lambda i: (i, 0)),
        out_shape=jax.ShapeDtypeStruct((N, D), F32),
        scratch_shapes=[pltpu.VMEM((GATHER_SLOTS, ROW_TILES, TOP_K * tf, LANES), F32),
                        pltpu.SemaphoreType.DMA((GATHER_SLOTS,))],
        compiler_params=_params(("arbitrary",)),
    )(dest3, dest3, dest3, h_tok, route, y_rows, ln2_g[0][None], ln2_b[0][None])
    return out.reshape(B, S, D)
```

```python
import functools
import math

import jax
import jax.numpy as jnp
from jax import lax
from jax.experimental import pallas as pl
from jax.experimental.pallas import tpu as pltpu

D_MODEL = 1024
CHUNK = 64
POOL_WINDOWS = (2, 4, 8, 16)
POOL_GROUPS = len(POOL_WINDOWS)
POOL_WIDTH = D_MODEL // 2
POOL_GROUP_DIM = POOL_WIDTH // POOL_GROUPS
POOL_OUT_GROUP_DIM = D_MODEL // POOL_GROUPS
POOL_HALO = 16
N_HEADS = 8
QK_NOPE_DIM = D_MODEL // 16
QK_ROPE_DIM = D_MODEL // 32
HALF_ROPE = QK_ROPE_DIM // 2
V_HEAD_DIM = D_MODEL // 16
V_ROWS = V_HEAD_DIM + 16
Q_LORA_RANK = 3 * D_MODEL // 8
KV_LORA_RANK = D_MODEL // 4
ROPE_THETA = 10000.0
N_GROUPS = 4
EXPERTS_PER_GROUP = 8
N_EXPERTS = N_GROUPS * EXPERTS_PER_GROUP
TOP_K = 2
D_EXPERT = D_MODEL // 2
NORM_EPS = 1e-5
DEPTH = 1
DEEPNORM_ALPHA = (2.0 * DEPTH) ** 0.25

LANES = 128
SUBLANES = 8
HEAD_PAD = LANES
ROW_TILES = D_MODEL // LANES
GROUP_LANE0 = 64
NEG_BIG = -1e30
VMEM_LIMIT = 56 * 1024 * 1024

TILE_LATENT = 512
TILE_Q = 512
TILE_K = 256
ATTN_HEADS_PER_STEP = 4
TILE_MIX = 512
GATHER_AHEAD = 2
GATHER_SLOTS = GATHER_AHEAD + 1
TILE_EXPERT = 256
TILE_COMBINE = 256

F32 = jnp.float32
BF16 = jnp.bfloat16


def _dot(a, b):
    return jnp.dot(a, b, preferred_element_type=F32)


def _dot_nt(a, b):
    return lax.dot_general(a, b, (((1,), (1,)), ((), ())), preferred_element_type=F32)


def _dot_tn(a, b):
    return lax.dot_general(a, b, (((0,), (0,)), ((), ())), preferred_element_type=F32)


def _rms(v, g):
    ms = jnp.mean(jnp.square(v), axis=-1, keepdims=True)
    return v * lax.rsqrt(ms + NORM_EPS) * g


def _layer_norm(v, g, b):
    mu = jnp.mean(v, axis=-1, keepdims=True)
    c = v - mu
    var = jnp.mean(jnp.square(c), axis=-1, keepdims=True)
    return c * lax.rsqrt(var + NORM_EPS) * g + b


def _latent_kernel(x_ref, wlat_ref, gq_ref, gkv_ref, wuqT_ref, wuk_ref, wuvT_ref,
                   cosT_ref, sinT_ref, ra_ref, rm_ref, rp_ref,
                   qT_ref, k_ref, vT_ref, *, q_scale):
    xb = x_ref[...].astype(BF16)
    lat = _dot(xb, wlat_ref[...])
    c_q = lat[:, :Q_LORA_RANK]
    c_kv = lat[:, Q_LORA_RANK:Q_LORA_RANK + KV_LORA_RANK]
    kpe = lat[:, Q_LORA_RANK + KV_LORA_RANK:]
    qn = _rms(c_q, gq_ref[...]).astype(BF16)
    kvn = _rms(c_kv, gkv_ref[...]).astype(BF16)

    qT = _dot_nt(wuqT_ref[...], qn) * q_scale
    cosT = cosT_ref[...]
    sinT = sinT_ref[...]
    for h in range(N_HEADS):
        r0 = h * HEAD_PAD
        x1 = qT[r0 + QK_NOPE_DIM:r0 + QK_NOPE_DIM + HALF_ROPE]
        x2 = qT[r0 + QK_NOPE_DIM + HALF_ROPE:r0 + QK_NOPE_DIM + QK_ROPE_DIM]
        qT_ref[0, r0:r0 + QK_NOPE_DIM, :] = qT[r0:r0 + QK_NOPE_DIM].astype(BF16)
        qT_ref[0, r0 + QK_NOPE_DIM:r0 + QK_NOPE_DIM + HALF_ROPE, :] = (x1 * cosT - x2 * sinT).astype(BF16)
        qT_ref[0, r0 + QK_NOPE_DIM + HALF_ROPE:r0 + QK_NOPE_DIM + QK_ROPE_DIM, :] = (
            x1 * sinT + x2 * cosT).astype(BF16)
        qT_ref[0, r0 + QK_NOPE_DIM + QK_ROPE_DIM:r0 + HEAD_PAD, :] = jnp.zeros(
            (HEAD_PAD - QK_NOPE_DIM - QK_ROPE_DIM, qT.shape[1]), BF16)

    kpe_rot = (kpe * ra_ref[...] + pltpu.roll(kpe, LANES - HALF_ROPE, 1) * rm_ref[...]
               + pltpu.roll(kpe, HALF_ROPE, 1) * rp_ref[...])
    k = _dot(kvn, wuk_ref[...])
    for h in range(N_HEADS):
        k_ref[0, :, h * HEAD_PAD:(h + 1) * HEAD_PAD] = (k[:, h * HEAD_PAD:(h + 1) * HEAD_PAD] + kpe_rot).astype(BF16)
    vT = _dot_nt(wuvT_ref[...], kvn)
    row = lax.broadcasted_iota(jnp.int32, vT.shape, 0) % V_ROWS
    vT_ref[0] = jnp.where(row == V_HEAD_DIM, 1.0, vT).astype(BF16)


def _attn_kernel(qT_ref, k_ref, vT_ref, bias_ref, oT_ref, *bufs):
    qi = pl.program_id(2)
    tq = qT_ref.shape[2]
    heads = range(ATTN_HEADS_PER_STEP)
    s0, s1 = (bufs[i * ATTN_HEADS_PER_STEP:(i + 1) * ATTN_HEADS_PER_STEP] for i in range(2))

    def scores(h, j):
        k0 = pl.multiple_of(j * TILE_K, TILE_K)
        return _dot(k_ref[0, pl.ds(k0, TILE_K), h * HEAD_PAD:(h + 1) * HEAD_PAD],
                    qT_ref[0, h * HEAD_PAD:(h + 1) * HEAD_PAD, :])

    def col_max(s):
        while s.shape[0] > SUBLANES:
            half = s.shape[0] // 2
            s = jnp.maximum(s[:half], s[half:])
        return jnp.max(s, axis=0, keepdims=True)

    def step(h, j, s_cur, s_nxt, carry, bias=None):
        m, acc, cmax = carry
        cmax_nxt = None
        if s_nxt is not None:
            s_new = scores(h, j + 1)
            s_nxt[h][...] = s_new
            cmax_nxt = col_max(s_new)
        s = s_cur[h][...]
        if bias is not None:
            s = s + bias
            cmax = col_max(s)
        m_new = jnp.maximum(m, cmax)
        p = jnp.exp2(s - m_new).astype(BF16)
        k0 = pl.multiple_of(j * TILE_K, TILE_K)
        pv = _dot(vT_ref[0, h * V_ROWS:(h + 1) * V_ROWS, pl.ds(k0, TILE_K)], p)
        return m_new, jnp.exp2(m - m_new) * acc + pv, cmax_nxt

    carry = []
    for h in heads:
        s_first = scores(h, 0)
        s0[h][...] = s_first
        carry.append((jnp.full((1, tq), NEG_BIG, F32), jnp.zeros((V_ROWS, tq), F32), col_max(s_first)))
    carry = tuple(carry)

    def pair(t, c):
        c = tuple(step(h, 2 * t, s0, s1, c[h]) for h in heads)
        return tuple(step(h, 2 * t + 1, s1, s0, c[h]) for h in heads)

    carry = lax.fori_loop(0, qi, pair, carry)
    jd = 2 * qi
    carry = tuple(step(h, jd, s0, s1, carry[h], bias=bias_ref[0]) for h in heads)
    carry = tuple(step(h, jd + 1, s1, None, carry[h], bias=bias_ref[1]) for h in heads)
    for h in heads:
        acc = carry[h][1]
        oT_ref[0, h * V_HEAD_DIM:(h + 1) * V_HEAD_DIM, :] = (
            acc[:V_HEAD_DIM] / acc[V_HEAD_DIM:V_HEAD_DIM + 1]).astype(BF16)


def _mix_kernel(x_ref, oT_ref, wpool_ref, wgate_ref, mixw_ref, pscale_ref, wo_ref, wout_ref,
                g1_ref, b1_ref, wr_ref, br_ref, tri_ref,
                h_ref, route_ref, routeT_ref, cnt_ref, ext_ref, base_ref):
    b = pl.program_id(0)
    si = pl.program_id(1)
    tm = x_ref.shape[0]

    @pl.when(jnp.logical_and(b == 0, si == 0))
    def _():
        base_ref[...] = jnp.zeros_like(base_ref)

    @pl.when(si == 0)
    def _():
        ext_ref[0:POOL_HALO, :] = jnp.zeros((POOL_HALO, POOL_WIDTH), F32)

    x = x_ref[...]
    xb = x.astype(BF16)
    u = _dot(xb, wpool_ref[...])
    ext_ref[POOL_HALO:POOL_HALO + tm, :] = u

    pos1 = si * tm + lax.broadcasted_iota(jnp.int32, (tm, POOL_GROUP_DIM), 0) + 1
    y_parts = []
    for g, win in enumerate(POOL_WINDOWS):
        c0 = g * POOL_GROUP_DIM
        ws = u[:, c0:c0 + POOL_GROUP_DIM]
        for kk in range(1, win):
            ws = ws + ext_ref[POOL_HALO - kk:POOL_HALO - kk + tm, c0:c0 + POOL_GROUP_DIM]
        count = jnp.minimum(pos1, win).astype(F32)
        pooled = ws / count - u[:, c0:c0 + POOL_GROUP_DIM]
        y_parts.append(_dot(pooled.astype(BF16), mixw_ref[g]))
    ext_ref[0:POOL_HALO, :] = ext_ref[tm:tm + POOL_HALO, :]
    y_pool = jnp.concatenate(y_parts, axis=-1) * pscale_ref[...]

    glog = _dot(xb, wgate_ref[...])
    y_mla = _dot_tn(oT_ref[0], wo_ref[...])
    merged = (jax.nn.sigmoid(glog[:, :D_MODEL]) * y_pool + jax.nn.sigmoid(glog[:, D_MODEL:]) * y_mla)
    r = DEEPNORM_ALPHA * x + _dot(merged.astype(BF16), wout_ref[...])
    h = _layer_norm(r, g1_ref[...], b1_ref[...])
    for j in range(ROW_TILES):
        h_ref[pl.ds(j, tm, stride=ROW_TILES), :] = h[:, j * LANES:(j + 1) * LANES]

    logits = _dot(h.astype(BF16), wr_ref[...]) + br_ref[...]
    lane_i = lax.broadcasted_iota(jnp.int32, logits.shape, 1)
    lane = lane_i.astype(F32)
    big = float(LANES)
    is_group = jnp.logical_and(lane_i >= GROUP_LANE0, lane_i < GROUP_LANE0 + N_GROUPS)
    gl = jnp.where(is_group, logits, NEG_BIG)
    gmax = jnp.max(gl, axis=-1, keepdims=True)
    g_w = 1.0 / jnp.sum(jnp.exp(gl - gmax), axis=-1, keepdims=True)
    g_idx = jnp.min(jnp.where(gl == gmax, lane - GROUP_LANE0, big), axis=-1, keepdims=True)
    lane_group = (lane_i // EXPERTS_PER_GROUP).astype(F32)
    in_group = jnp.logical_and(lane_i < N_EXPERTS, lane_group == g_idx)
    el = jnp.where(in_group, logits, NEG_BIG)
    e1max = jnp.max(el, axis=-1, keepdims=True)
    i1 = jnp.min(jnp.where(el == e1max, lane, big), axis=-1, keepdims=True)
    el2 = jnp.where(lane == i1, NEG_BIG, el)
    e2max = jnp.max(el2, axis=-1, keepdims=True)
    i2 = jnp.min(jnp.where(el2 == e2max, lane, big), axis=-1, keepdims=True)
    ratio = jnp.exp(e2max - e1max)
    gate1 = g_w / (1.0 + ratio)
    gate2 = g_w * ratio / (1.0 + ratio)

    hit1 = lane == i1
    hit2 = lane == i2
    onehot = jnp.where(jnp.logical_or(hit1, hit2), 1.0, 0.0)
    before = _dot(tri_ref[...], onehot.astype(BF16)) + base_ref[0:1, :]
    rank1 = jnp.sum(jnp.where(hit1, before, 0.0), axis=-1, keepdims=True)
    rank2 = jnp.sum(jnp.where(hit2, before, 0.0), axis=-1, keepdims=True)
    new_base = base_ref[0:1, :] + jnp.sum(onehot, axis=0, keepdims=True)
    base_ref[...] = jnp.broadcast_to(new_base, base_ref.shape)
    cnt_ref[...] = jnp.broadcast_to(new_base, cnt_ref.shape)

    route = jnp.where(lane_i == 0, i1, 0.0)
    route = jnp.where(lane_i == 1, i2, route)
    route = jnp.where(lane_i == 2, gate1, route)
    route = jnp.where(lane_i == 3, gate2, route)
    route = jnp.where(lane_i == 4, rank1, route)
    route = jnp.where(lane_i == 5, rank2, route)
    route_ref[...] = route
    routeT_ref[...] = route.T[:SUBLANES]


def _gather_row(idx_ref, r, src_hbm, dst_ref, slot, sem, priority=0):
    src0 = pl.multiple_of(idx_ref[0, 0, r] * ROW_TILES, ROW_TILES)
    pltpu.make_async_copy(src_hbm.at[pl.ds(src0, ROW_TILES), :],
                          dst_ref.at[slot, :, r, :],
                          sem.at[slot]).start(priority=priority)


def _gather_rows_loop(idx_ref, src_hbm, dst_ref, slot, sem, n_rows, unroll=8):
    def body(c, _):
        for u in range(unroll):
            _gather_row(idx_ref, c * unroll + u, src_hbm, dst_ref, slot, sem)
        return 0
    lax.fori_loop(0, n_rows // unroll, body, 0)


def _gather_rows_inline(idx_ref, src_hbm, dst_ref, slot, sem, n_rows):
    for r in range(n_rows):
        _gather_row(idx_ref, r, src_hbm, dst_ref, slot, sem, priority=r % 2)


def _wait_rows(dst_ref, slot, sem):
    pltpu.make_async_copy(dst_ref.at[slot], dst_ref.at[slot], sem.at[slot]).wait()


def _expert_kernel(be_ref, nused_ref, tok_ref, tok1_ref, tok2_ref, h_hbm, wg_ref, wu_ref, wd_ref,
                   y_ref, xbuf, xs_ref, wgb, wub, wdb, sem):
    i = pl.program_id(0)
    n_used = nused_ref[0]
    slot = lax.rem(i, GATHER_SLOTS)
    tb = TILE_EXPERT

    @pl.when(i == 0)
    def _():
        _gather_rows_loop(tok_ref, h_hbm, xbuf, 0, sem, tb)
        _gather_rows_loop(tok1_ref, h_hbm, xbuf, 1, sem, tb)

    changed = jnp.logical_or(i == 0, be_ref[i] != be_ref[jnp.maximum(i - 1, 0)])

    @pl.when(jnp.logical_and(changed, i < n_used))
    def _():
        wgb[...] = wg_ref[0].astype(BF16)
        wub[...] = wu_ref[0].astype(BF16)
        wdb[...] = wd_ref[0].astype(BF16)

    @pl.when(jnp.logical_and(i >= n_used, i < n_used + GATHER_AHEAD))
    def _():
        _wait_rows(xbuf, slot, sem)

    @pl.when(i < n_used)
    def _():
        _wait_rows(xbuf, slot, sem)
        for j in range(ROW_TILES):
            xs_ref[:, j * LANES:(j + 1) * LANES] = xbuf[slot, j].astype(BF16)
        _gather_rows_inline(tok2_ref, h_hbm, xbuf, lax.rem(i + GATHER_AHEAD, GATHER_SLOTS), sem, tb)
        xs = xs_ref[...]
        gate = _dot(xs, wgb[...])
        up = _dot(xs, wub[...])
        hid = (jax.nn.silu(gate) * up).astype(BF16)
        y = _dot(hid, wdb[...])
        for j in range(ROW_TILES):
            y_ref[pl.ds(j, tb, stride=ROW_TILES), :] = y[:, j * LANES:(j + 1) * LANES]

    @pl.when(i >= n_used)
    def _():
        y_ref[...] = jnp.zeros_like(y_ref)


def _combine_kernel(dst_ref, dst1_ref, dst2_ref, h_ref, route_ref, y_hbm, g2_ref, b2_ref, o_ref, gbuf, sem):
    i = pl.program_id(0)
    n = pl.num_programs(0)
    slot = lax.rem(i, GATHER_SLOTS)
    tf = TILE_COMBINE

    @pl.when(i == 0)
    def _():
        _gather_rows_loop(dst_ref, y_hbm, gbuf, 0, sem, TOP_K * tf)
        _gather_rows_loop(dst1_ref, y_hbm, gbuf, 1, sem, TOP_K * tf)

    _wait_rows(gbuf, slot, sem)
    route = route_ref[...]
    gate1 = route[:, 2:3]
    gate2 = route[:, 3:4]
    parts = []
    for j in range(ROW_TILES):
        hj = h_ref[pl.ds(j, tf, stride=ROW_TILES), :]
        y1 = gbuf[slot, j, 0:tf, :]
        y2 = gbuf[slot, j, tf:TOP_K * tf, :]
        parts.append(DEEPNORM_ALPHA * hj + (gate1 * y1 + gate2 * y2))
    z = jnp.concatenate(parts, axis=-1)
    o_ref[...] = _layer_norm(z, g2_ref[...], b2_ref[...])
    _gather_rows_inline(dst2_ref, y_hbm, gbuf, lax.rem(i + GATHER_AHEAD, GATHER_SLOTS), sem, TOP_K * tf)

    @pl.when(i == n - 1)
    def _():
        for ahead in range(1, GATHER_AHEAD + 1):
            _wait_rows(gbuf, lax.rem(i + ahead, GATHER_SLOTS), sem)


def _rope_tables(seq):
    inv = 1.0 / (ROPE_THETA ** (jnp.arange(0, QK_ROPE_DIM, 2, dtype=F32) / QK_ROPE_DIM))
    ang = jnp.arange(seq, dtype=F32)[:, None] * inv[None, :]
    cos, sin = jnp.cos(ang), jnp.sin(ang)
    zeros = jnp.zeros((seq, QK_NOPE_DIM), F32)
    pad = jnp.zeros((seq, HEAD_PAD - QK_NOPE_DIM - QK_ROPE_DIM), F32)
    z16 = jnp.zeros((seq, HALF_ROPE), F32)
    rot_a = jnp.concatenate([zeros, cos, cos, pad], axis=1)
    rot_m = jnp.concatenate([zeros, -sin, z16, pad], axis=1)
    rot_p = jnp.concatenate([zeros, z16, sin, pad], axis=1)
    return cos.T, sin.T, rot_a, rot_m, rot_p


def _full(shape):
    return pl.BlockSpec(shape, lambda *_: (0,) * len(shape))


def _params(sem):
    return pltpu.CompilerParams(dimension_semantics=sem, vmem_limit_bytes=VMEM_LIMIT)


def kernel(x, w_in, pool_mix_w, pool_scale, q_norm_g, w_uq, kv_norm_g, w_ukv, w_mla_o, w_out, ln1_g, ln1_b,
           w_router_group, b_router_group, w_router_expert, b_router_expert, w_gate, w_up, w_down, ln2_g, ln2_b):
    B, S, D = x.shape
    assert D == D_MODEL and w_in.shape[0] == DEPTH == 1
    assert S % TILE_LATENT == 0 and S % TILE_Q == 0 and S % TILE_MIX == 0 and TILE_Q % TILE_K == 0
    N = B * S
    assert N % TILE_COMBINE == 0
    H = N_HEADS

    w = w_in[0]
    o1 = POOL_WIDTH
    o2 = o1 + Q_LORA_RANK
    o3 = o2 + KV_LORA_RANK
    o4 = o3 + QK_ROPE_DIM
    w_pool = w[:, :o1].astype(BF16)
    kpe_cols = jnp.pad(w[:, o3:o4], ((0, 0), (QK_NOPE_DIM, HEAD_PAD - QK_NOPE_DIM - QK_ROPE_DIM)))
    w_lat = jnp.concatenate([w[:, o1:o3], kpe_cols], axis=1).astype(BF16)
    w_gates = w[:, o4:].astype(BF16)
    qd = QK_NOPE_DIM + QK_ROPE_DIM
    wuq = jnp.pad(w_uq[0].reshape(Q_LORA_RANK, H, qd), ((0, 0), (0, 0), (0, HEAD_PAD - qd)))
    wuqT = wuq.reshape(Q_LORA_RANK, H * HEAD_PAD).T.astype(BF16)
    wukv = w_ukv[0].reshape(KV_LORA_RANK, H, QK_NOPE_DIM + V_HEAD_DIM)
    wuk = jnp.pad(wukv[:, :, :QK_NOPE_DIM], ((0, 0), (0, 0), (0, HEAD_PAD - QK_NOPE_DIM)))
    wuk = wuk.reshape(KV_LORA_RANK, H * HEAD_PAD).astype(BF16)
    wuv = jnp.pad(wukv[:, :, QK_NOPE_DIM:], ((0, 0), (0, 0), (0, V_ROWS - V_HEAD_DIM)))
    wuvT = wuv.reshape(KV_LORA_RANK, H * V_ROWS).T.astype(BF16)
    w_r = jnp.zeros((D, LANES), F32)
    w_r = w_r.at[:, :N_EXPERTS].set(w_router_expert[0]).at[:, GROUP_LANE0:GROUP_LANE0 + N_GROUPS].set(
        w_router_group[0]).astype(BF16)
    b_r = jnp.zeros((1, LANES), F32)
    b_r = b_r.at[0, :N_EXPERTS].set(b_router_expert[0]).at[0, GROUP_LANE0:GROUP_LANE0 + N_GROUPS].set(
        b_router_group[0])
    cosT, sinT, rot_a, rot_m, rot_p = _rope_tables(S)
    q_scale = (QK_NOPE_DIM + QK_ROPE_DIM) ** -0.5 * math.log2(math.e)
    x2 = x.reshape(N, D)

    ta = TILE_LATENT
    nsa = S // ta
    qT, k, vT = pl.pallas_call(
        functools.partial(_latent_kernel, q_scale=q_scale),
        grid=(B, nsa),
        in_specs=[
            pl.BlockSpec((ta, D), lambda b, s: (b * nsa + s, 0)),
            _full(w_lat.shape), _full((1, Q_LORA_RANK)), _full((1, KV_LORA_RANK)),
            _full(wuqT.shape), _full(wuk.shape), _full(wuvT.shape),
            pl.BlockSpec((HALF_ROPE, ta), lambda b, s: (0, s)),
            pl.BlockSpec((HALF_ROPE, ta), lambda b, s: (0, s)),
            pl.BlockSpec((ta, LANES), lambda b, s: (s, 0)),
            pl.BlockSpec((ta, LANES), lambda b, s: (s, 0)),
            pl.BlockSpec((ta, LANES), lambda b, s: (s, 0)),
        ],
        out_specs=[
            pl.BlockSpec((1, H * HEAD_PAD, ta), lambda b, s: (b, 0, s)),
            pl.BlockSpec((1, ta, H * HEAD_PAD), lambda b, s: (b, s, 0)),
            pl.BlockSpec((1, H * V_ROWS, ta), lambda b, s: (b, 0, s)),
        ],
        out_shape=[
            jax.ShapeDtypeStruct((B, H * HEAD_PAD, S), BF16),
            jax.ShapeDtypeStruct((B, S, H * HEAD_PAD), BF16),
            jax.ShapeDtypeStruct((B, H * V_ROWS, S), BF16),
        ],
        compiler_params=_params(("parallel", "parallel")),
    )(x2, w_lat, q_norm_g[0][None], kv_norm_g[0][None], wuqT, wuk, wuvT, cosT, sinT, rot_a, rot_m, rot_p)

    tq = TILE_Q
    assert tq == 2 * TILE_K
    key_chunk = jnp.arange(tq)[:, None] // CHUNK
    q_chunk = jnp.arange(tq)[None, :] // CHUNK
    mask_bias = jnp.where(key_chunk <= q_chunk, 0.0, NEG_BIG).astype(F32).reshape(2, TILE_K, tq)
    hp = ATTN_HEADS_PER_STEP
    assert H % hp == 0
    oT = pl.pallas_call(
        _attn_kernel,
        grid=(B, H // hp, S // tq),
        in_specs=[
            pl.BlockSpec((1, hp * HEAD_PAD, tq), lambda b, h, q: (b, h, q)),
            pl.BlockSpec((1, S, hp * HEAD_PAD), lambda b, h, q: (b, 0, h)),
            pl.BlockSpec((1, hp * V_ROWS, S), lambda b, h, q: (b, h, 0)),
            _full(mask_bias.shape),
        ],
        out_specs=pl.BlockSpec((1, hp * V_HEAD_DIM, tq), lambda b, h, q: (b, h, q)),
        out_shape=jax.ShapeDtypeStruct((B, H * V_HEAD_DIM, S), BF16),
        scratch_shapes=[pltpu.VMEM((TILE_K, tq), F32)] * (2 * hp),
        compiler_params=_params(("parallel", "parallel", "arbitrary")),
    )(qT, k, vT, mask_bias)

    tm = TILE_MIX
    nsm = S // tm
    tri = (jnp.arange(tm)[:, None] > jnp.arange(tm)[None, :]).astype(BF16)
    h_tok, route, routeT, counts = pl.pallas_call(
        _mix_kernel,
        grid=(B, nsm),
        in_specs=[
            pl.BlockSpec((tm, D), lambda b, s: (b * nsm + s, 0)),
            pl.BlockSpec((1, H * V_HEAD_DIM, tm), lambda b, s: (b, 0, s)),
            _full(w_pool.shape), _full(w_gates.shape), _full(pool_mix_w.shape[1:]), _full((1, D)),
            _full(w_mla_o.shape[1:]), _full(w_out.shape[1:]), _full((1, D)), _full((1, D)),
            _full(w_r.shape), _full(b_r.shape), _full(tri.shape),
        ],
        out_specs=[
            pl.BlockSpec((tm * ROW_TILES, LANES), lambda b, s: (b * nsm + s, 0)),
            pl.BlockSpec((tm, LANES), lambda b, s: (b * nsm + s, 0)),
            pl.BlockSpec((SUBLANES, tm), lambda b, s: (0, b * nsm + s)),
            _full((SUBLANES, LANES)),
        ],
        out_shape=[
            jax.ShapeDtypeStruct((N * ROW_TILES, LANES), F32),
            jax.ShapeDtypeStruct((N, LANES), F32),
            jax.ShapeDtypeStruct((SUBLANES, N), F32),
            jax.ShapeDtypeStruct((SUBLANES, LANES), F32),
        ],
        scratch_shapes=[pltpu.VMEM((tm + POOL_HALO, POOL_WIDTH), F32), pltpu.VMEM((SUBLANES, LANES), F32)],
        compiler_params=_params(("arbitrary", "arbitrary")),
    )(x2, oT, w_pool, w_gates, pool_mix_w[0].astype(BF16), pool_scale[0][None], w_mla_o[0].astype(BF16),
      w_out[0].astype(BF16), ln1_g[0][None], ln1_b[0][None], w_r, b_r, tri)

    tb = TILE_EXPERT
    A = N * TOP_K
    n_blocks = -(-(A + N_EXPERTS * (tb - 1)) // tb) + GATHER_AHEAD
    R = n_blocks * tb
    e_idx = routeT[0:2].astype(jnp.int32)
    rank = routeT[4:6].astype(jnp.int32)
    cnt = counts[0, :N_EXPERTS].astype(jnp.int32)
    padded = ((cnt + tb - 1) // tb) * tb
    pad_end = jnp.cumsum(padded)
    pad_start = pad_end - padded
    is_e = e_idx[:, None, :] == jnp.arange(N_EXPERTS, dtype=jnp.int32)[None, :, None]
    dest = jnp.sum(jnp.where(is_e, pad_start[None, :, None], 0), axis=1) + rank
    tok = jnp.broadcast_to(jnp.arange(N, dtype=jnp.int32)[None, :], (TOP_K, N))
    row_tok = jnp.zeros((R,), jnp.int32).at[dest.reshape(A)].set(tok.reshape(A))
    block_start = jnp.arange(n_blocks, dtype=jnp.int32) * tb
    block_e = jnp.minimum(jnp.sum(pad_end[None, :] <= block_start[:, None], axis=1), N_EXPERTS - 1).astype(jnp.int32)
    n_used = (pad_end[-1] // tb).astype(jnp.int32)[None]
    row_tok3 = row_tok.reshape(n_blocks, 1, tb)

    y_rows = pl.pallas_call(
        _expert_kernel,
        grid_spec=pltpu.PrefetchScalarGridSpec(
            num_scalar_prefetch=2,
            grid=(n_blocks,),
            in_specs=[
                pl.BlockSpec((1, 1, tb), lambda i, be, nu: (i, 0, 0), memory_space=pltpu.SMEM),
                pl.BlockSpec((1, 1, tb), lambda i, be, nu: (jnp.minimum(i + 1, n_blocks - 1), 0, 0),
                             memory_space=pltpu.SMEM),
                pl.BlockSpec((1, 1, tb), lambda i, be, nu: (jnp.minimum(i + GATHER_AHEAD, n_blocks - 1), 0, 0),
                             memory_space=pltpu.SMEM),
                pl.BlockSpec(memory_space=pl.ANY),
                pl.BlockSpec((1, D, D_EXPERT), lambda i, be, nu: (be[i], 0, 0)),
                pl.BlockSpec((1, D, D_EXPERT), lambda i, be, nu: (be[i], 0, 0)),
                pl.BlockSpec((1, D_EXPERT, D), lambda i, be, nu: (be[i], 0, 0)),
            ],
            out_specs=pl.BlockSpec((tb * ROW_TILES, LANES), lambda i, be, nu: (i, 0)),
            scratch_shapes=[
                pltpu.VMEM((GATHER_SLOTS, ROW_TILES, tb, LANES), F32), pltpu.VMEM((tb, D), BF16),
                pltpu.VMEM((D, D_EXPERT), BF16), pltpu.VMEM((D, D_EXPERT), BF16), pltpu.VMEM((D_EXPERT, D), BF16),
                pltpu.SemaphoreType.DMA((GATHER_SLOTS,)),
            ],
        ),
        out_shape=jax.ShapeDtypeStruct((R * ROW_TILES, LANES), F32),
        compiler_params=_params(("arbitrary",)),
    )(block_e, n_used, row_tok3, row_tok3, row_tok3, h_tok, w_gate[0], w_up[0], w_down[0])

    tf = TILE_COMBINE
    nf = N // tf
    dest3 = dest.reshape(TOP_K, nf, tf).transpose(1, 0, 2).reshape(nf, 1, TOP_K * tf)
    out = pl.pallas_call(
        _combine_kernel,
        grid=(nf,),
        in_specs=[
            pl.BlockSpec((1, 1, TOP_K * tf), lambda i: (i, 0, 0), memory_space=pltpu.SMEM),
            pl.BlockSpec((1, 1, TOP_K * tf), lambda i: (jnp.minimum(i + 1, nf - 1), 0, 0), memory_space=pltpu.SMEM),
            pl.BlockSpec((1, 1, TOP_K * tf), lambda i: (jnp.minimum(i + GATHER_AHEAD, nf - 1), 0, 0),
                         memory_space=pltpu.SMEM),
            pl.BlockSpec((tf * ROW_TILES, LANES), lambda i: (i, 0)),
            pl.BlockSpec((tf, LANES), lambda i: (i, 0)),
            pl.BlockSpec(memory_space=pl.ANY),
            _full((1, D)), _full((1, D)),
        ],
        out_specs=pl.BlockSpec((tf, D), lambda i: (i, 0)),
        out_shape=jax.ShapeDtypeStruct((N, D), F32),
        scratch_shapes=[pltpu.VMEM((GATHER_SLOTS, ROW_TILES, TOP_K * tf, LANES), F32),
                        pltpu.SemaphoreType.DMA((GATHER_SLOTS,))],
        compiler_params=_params(("arbitrary",)),
    )(dest3, dest3, dest3, h_tok, route, y_rows, ln2_g[0][None], ln2_b[0][None])
    return out.reshape(B, S, D)
```

```python
import functools
import math

import jax
import jax.numpy as jnp
from jax import lax
from jax.experimental import pallas as pl
from jax.experimental.pallas import tpu as pltpu
from jax.experimental.pallas import tpu_sc as plsc

D_MODEL = 1024
CHUNK = 64
POOL_WINDOWS = (2, 4, 8, 16)
POOL_GROUPS = len(POOL_WINDOWS)
POOL_WIDTH = D_MODEL // 2
POOL_GROUP_DIM = POOL_WIDTH // POOL_GROUPS
POOL_OUT_GROUP_DIM = D_MODEL // POOL_GROUPS
POOL_HALO = 16
N_HEADS = 8
QK_NOPE_DIM = D_MODEL // 16
QK_ROPE_DIM = D_MODEL // 32
HALF_ROPE = QK_ROPE_DIM // 2
V_HEAD_DIM = D_MODEL // 16
V_ROWS = V_HEAD_DIM + 16
Q_LORA_RANK = 3 * D_MODEL // 8
KV_LORA_RANK = D_MODEL // 4
ROPE_THETA = 10000.0
N_GROUPS = 4
EXPERTS_PER_GROUP = 8
N_EXPERTS = N_GROUPS * EXPERTS_PER_GROUP
TOP_K = 2
D_EXPERT = D_MODEL // 2
NORM_EPS = 1e-5
DEPTH = 1
DEEPNORM_ALPHA = (2.0 * DEPTH) ** 0.25

LANES = 128
SUBLANES = 8
HEAD_PAD = LANES
ROW_TILES = D_MODEL // LANES
GROUP_LANE0 = 64
NEG_BIG = -1e30
VMEM_LIMIT = 56 * 1024 * 1024

TILE_LATENT = 512
TILE_Q = 512
TILE_K = 256
ATTN_HEADS_PER_STEP = 4
TILE_MIX = 512
GATHER_AHEAD = 2
GATHER_SLOTS = GATHER_AHEAD + 1
TILE_EXPERT = 256
SC_SCATTER_WINDOW = 128
TILE_COMBINE = 256

F32 = jnp.float32
BF16 = jnp.bfloat16


def _dot(a, b):
    return jnp.dot(a, b, preferred_element_type=F32)


def _dot_nt(a, b):
    return lax.dot_general(a, b, (((1,), (1,)), ((), ())), preferred_element_type=F32)


def _dot_tn(a, b):
    return lax.dot_general(a, b, (((0,), (0,)), ((), ())), preferred_element_type=F32)


def _rms(v, g):
    ms = jnp.mean(jnp.square(v), axis=-1, keepdims=True)
    return v * lax.rsqrt(ms + NORM_EPS) * g


def _layer_norm(v, g, b):
    mu = jnp.mean(v, axis=-1, keepdims=True)
    c = v - mu
    var = jnp.mean(jnp.square(c), axis=-1, keepdims=True)
    return c * lax.rsqrt(var + NORM_EPS) * g + b


def _latent_kernel(x_ref, wlat_ref, gq_ref, gkv_ref, wuqT_ref, wuk_ref, wuvT_ref,
                   cosT_ref, sinT_ref, ra_ref, rm_ref, rp_ref,
                   qT_ref, k_ref, vT_ref, *, q_scale):
    xb = x_ref[...].astype(BF16)
    lat = _dot(xb, wlat_ref[...])
    c_q = lat[:, :Q_LORA_RANK]
    c_kv = lat[:, Q_LORA_RANK:Q_LORA_RANK + KV_LORA_RANK]
    kpe = lat[:, Q_LORA_RANK + KV_LORA_RANK:]
    qn = _rms(c_q, gq_ref[...]).astype(BF16)
    kvn = _rms(c_kv, gkv_ref[...]).astype(BF16)

    qT = _dot_nt(wuqT_ref[...], qn) * q_scale
    cosT = cosT_ref[...]
    sinT = sinT_ref[...]
    for h in range(N_HEADS):
        r0 = h * HEAD_PAD
        x1 = qT[r0 + QK_NOPE_DIM:r0 + QK_NOPE_DIM + HALF_ROPE]
        x2 = qT[r0 + QK_NOPE_DIM + HALF_ROPE:r0 + QK_NOPE_DIM + QK_ROPE_DIM]
        qT_ref[0, r0:r0 + QK_NOPE_DIM, :] = qT[r0:r0 + QK_NOPE_DIM].astype(BF16)
        qT_ref[0, r0 + QK_NOPE_DIM:r0 + QK_NOPE_DIM + HALF_ROPE, :] = (x1 * cosT - x2 * sinT).astype(BF16)
        qT_ref[0, r0 + QK_NOPE_DIM + HALF_ROPE:r0 + QK_NOPE_DIM + QK_ROPE_DIM, :] = (
            x1 * sinT + x2 * cosT).astype(BF16)
        qT_ref[0, r0 + QK_NOPE_DIM + QK_ROPE_DIM:r0 + HEAD_PAD, :] = jnp.zeros(
            (HEAD_PAD - QK_NOPE_DIM - QK_ROPE_DIM, qT.shape[1]), BF16)

    kpe_rot = (kpe * ra_ref[...] + pltpu.roll(kpe, LANES - HALF_ROPE, 1) * rm_ref[...]
               + pltpu.roll(kpe, HALF_ROPE, 1) * rp_ref[...])
    k = _dot(kvn, wuk_ref[...])
    for h in range(N_HEADS):
        k_ref[0, :, h * HEAD_PAD:(h + 1) * HEAD_PAD] = (k[:, h * HEAD_PAD:(h + 1) * HEAD_PAD] + kpe_rot).astype(BF16)
    vT = _dot_nt(wuvT_ref[...], kvn)
    row = lax.broadcasted_iota(jnp.int32, vT.shape, 0) % V_ROWS
    vT_ref[0] = jnp.where(row == V_HEAD_DIM, 1.0, vT).astype(BF16)


def _attn_kernel(qT_ref, k_ref, vT_ref, bias_ref, oT_ref, *bufs):
    qi = pl.program_id(2)
    tq = qT_ref.shape[2]
    heads = range(ATTN_HEADS_PER_STEP)
    s0, s1 = (bufs[i * ATTN_HEADS_PER_STEP:(i + 1) * ATTN_HEADS_PER_STEP] for i in range(2))

    def scores(h, j):
        k0 = pl.multiple_of(j * TILE_K, TILE_K)
        return _dot(k_ref[0, pl.ds(k0, TILE_K), h * HEAD_PAD:(h + 1) * HEAD_PAD],
                    qT_ref[0, h * HEAD_PAD:(h + 1) * HEAD_PAD, :])

    def col_max(s):
        while s.shape[0] > SUBLANES:
            half = s.shape[0] // 2
            s = jnp.maximum(s[:half], s[half:])
        return jnp.max(s, axis=0, keepdims=True)

    def step(h, j, s_cur, s_nxt, carry, bias=None):
        m, acc, cmax = carry
        cmax_nxt = None
        if s_nxt is not None:
            s_new = scores(h, j + 1)
            s_nxt[h][...] = s_new
            cmax_nxt = col_max(s_new)
        s = s_cur[h][...]
        if bias is not None:
            s = s + bias
            cmax = col_max(s)
        m_new = jnp.maximum(m, cmax)
        p = jnp.exp2(s - m_new).astype(BF16)
        k0 = pl.multiple_of(j * TILE_K, TILE_K)
        pv = _dot(vT_ref[0, h * V_ROWS:(h + 1) * V_ROWS, pl.ds(k0, TILE_K)], p)
        return m_new, jnp.exp2(m - m_new) * acc + pv, cmax_nxt

    carry = []
    for h in heads:
        s_first = scores(h, 0)
        s0[h][...] = s_first
        carry.append((jnp.full((1, tq), NEG_BIG, F32), jnp.zeros((V_ROWS, tq), F32), col_max(s_first)))
    carry = tuple(carry)

    def pair(t, c):
        c = tuple(step(h, 2 * t, s0, s1, c[h]) for h in heads)
        return tuple(step(h, 2 * t + 1, s1, s0, c[h]) for h in heads)

    carry = lax.fori_loop(0, qi, pair, carry)
    jd = 2 * qi
    carry = tuple(step(h, jd, s0, s1, carry[h], bias=bias_ref[0]) for h in heads)
    carry = tuple(step(h, jd + 1, s1, None, carry[h], bias=bias_ref[1]) for h in heads)
    for h in heads:
        acc = carry[h][1]
        oT_ref[0, h * V_HEAD_DIM:(h + 1) * V_HEAD_DIM, :] = (
            acc[:V_HEAD_DIM] / acc[V_HEAD_DIM:V_HEAD_DIM + 1]).astype(BF16)


def _mix_kernel(x_ref, oT_ref, wpool_ref, wgate_ref, mixw_ref, pscale_ref, wo_ref, wout_ref,
                g1_ref, b1_ref, wr_ref, br_ref, tri_ref,
                h_ref, route_ref, routeT_ref, cnt_ref, ext_ref, base_ref):
    b = pl.program_id(0)
    si = pl.program_id(1)
    tm = x_ref.shape[0]

    @pl.when(jnp.logical_and(b == 0, si == 0))
    def _():
        base_ref[...] = jnp.zeros_like(base_ref)

    @pl.when(si == 0)
    def _():
        ext_ref[0:POOL_HALO, :] = jnp.zeros((POOL_HALO, POOL_WIDTH), F32)

    x = x_ref[...]
    xb = x.astype(BF16)
    u = _dot(xb, wpool_ref[...])
    ext_ref[POOL_HALO:POOL_HALO + tm, :] = u

    pos1 = si * tm + lax.broadcasted_iota(jnp.int32, (tm, POOL_GROUP_DIM), 0) + 1
    y_parts = []
    for g, win in enumerate(POOL_WINDOWS):
        c0 = g * POOL_GROUP_DIM
        ws = u[:, c0:c0 + POOL_GROUP_DIM]
        for kk in range(1, win):
            ws = ws + ext_ref[POOL_HALO - kk:POOL_HALO - kk + tm, c0:c0 + POOL_GROUP_DIM]
        count = jnp.minimum(pos1, win).astype(F32)
        pooled = ws / count - u[:, c0:c0 + POOL_GROUP_DIM]
        y_parts.append(_dot(pooled.astype(BF16), mixw_ref[g]))
    ext_ref[0:POOL_HALO, :] = ext_ref[tm:tm + POOL_HALO, :]
    y_pool = jnp.concatenate(y_parts, axis=-1) * pscale_ref[...]

    glog = _dot(xb, wgate_ref[...])
    y_mla = _dot_tn(oT_ref[0], wo_ref[...])
    merged = (jax.nn.sigmoid(glog[:, :D_MODEL]) * y_pool + jax.nn.sigmoid(glog[:, D_MODEL:]) * y_mla)
    r = DEEPNORM_ALPHA * x + _dot(merged.astype(BF16), wout_ref[...])
    h = _layer_norm(r, g1_ref[...], b1_ref[...])
    for j in range(ROW_TILES):
        h_ref[pl.ds(j, tm, stride=ROW_TILES), :] = h[:, j * LANES:(j + 1) * LANES]

    logits = _dot(h.astype(BF16), wr_ref[...]) + br_ref[...]
    lane_i = lax.broadcasted_iota(jnp.int32, logits.shape, 1)
    lane = lane_i.astype(F32)
    big = float(LANES)
    is_group = jnp.logical_and(lane_i >= GROUP_LANE0, lane_i < GROUP_LANE0 + N_GROUPS)
    gl = jnp.where(is_group, logits, NEG_BIG)
    gmax = jnp.max(gl, axis=-1, keepdims=True)
    g_w = 1.0 / jnp.sum(jnp.exp(gl - gmax), axis=-1, keepdims=True)
    g_idx = jnp.min(jnp.where(gl == gmax, lane - GROUP_LANE0, big), axis=-1, keepdims=True)
    lane_group = (lane_i // EXPERTS_PER_GROUP).astype(F32)
    in_group = jnp.logical_and(lane_i < N_EXPERTS, lane_group == g_idx)
    el = jnp.where(in_group, logits, NEG_BIG)
    e1max = jnp.max(el, axis=-1, keepdims=True)
    i1 = jnp.min(jnp.where(el == e1max, lane, big), axis=-1, keepdims=True)
    el2 = jnp.where(lane == i1, NEG_BIG, el)
    e2max = jnp.max(el2, axis=-1, keepdims=True)
    i2 = jnp.min(jnp.where(el2 == e2max, lane, big), axis=-1, keepdims=True)
    ratio = jnp.exp(e2max - e1max)
    gate1 = g_w / (1.0 + ratio)
    gate2 = g_w * ratio / (1.0 + ratio)

    hit1 = lane == i1
    hit2 = lane == i2
    onehot = jnp.where(jnp.logical_or(hit1, hit2), 1.0, 0.0)
    before = _dot(tri_ref[...], onehot.astype(BF16)) + base_ref[0:1, :]
    rank1 = jnp.sum(jnp.where(hit1, before, 0.0), axis=-1, keepdims=True)
    rank2 = jnp.sum(jnp.where(hit2, before, 0.0), axis=-1, keepdims=True)
    new_base = base_ref[0:1, :] + jnp.sum(onehot, axis=0, keepdims=True)
    base_ref[...] = jnp.broadcast_to(new_base, base_ref.shape)
    cnt_ref[...] = jnp.broadcast_to(new_base, cnt_ref.shape)

    route = jnp.where(lane_i == 0, i1, 0.0)
    route = jnp.where(lane_i == 1, i2, route)
    route = jnp.where(lane_i == 2, gate1, route)
    route = jnp.where(lane_i == 3, gate2, route)
    route = jnp.where(lane_i == 4, rank1, route)
    route = jnp.where(lane_i == 5, rank2, route)
    route_ref[...] = route
    routeT_ref[...] = route.T[:SUBLANES]


def _gather_row(idx_ref, r, src_hbm, dst_ref, slot, sem, priority=0):
    src0 = pl.multiple_of(idx_ref[0, 0, r] * ROW_TILES, ROW_TILES)
    pltpu.make_async_copy(src_hbm.at[pl.ds(src0, ROW_TILES), :],
                          dst_ref.at[slot, :, r, :],
                          sem.at[slot]).start(priority=priority)


def _gather_rows_loop(idx_ref, src_hbm, dst_ref, slot, sem, n_rows, unroll=8):
    def body(c, _):
        for u in range(unroll):
            _gather_row(idx_ref, c * unroll + u, src_hbm, dst_ref, slot, sem)
        return 0
    lax.fori_loop(0, n_rows // unroll, body, 0)


def _gather_rows_inline(idx_ref, src_hbm, dst_ref, slot, sem, n_rows):
    for r in range(n_rows):
        _gather_row(idx_ref, r, src_hbm, dst_ref, slot, sem, priority=r % 2)


def _wait_rows(dst_ref, slot, sem):
    pltpu.make_async_copy(dst_ref.at[slot], dst_ref.at[slot], sem.at[slot]).wait()


def _expert_kernel(be_ref, nused_ref, tok_ref, tok1_ref, tok2_ref, h_hbm, wg_ref, wu_ref, wd_ref,
                   y_ref, xbuf, xs_ref, wgb, wub, wdb, sem):
    i = pl.program_id(0)
    n_used = nused_ref[0]
    slot = lax.rem(i, GATHER_SLOTS)
    tb = TILE_EXPERT

    @pl.when(i == 0)
    def _():
        _gather_rows_loop(tok_ref, h_hbm, xbuf, 0, sem, tb)
        _gather_rows_loop(tok1_ref, h_hbm, xbuf, 1, sem, tb)

    changed = jnp.logical_or(i == 0, be_ref[i] != be_ref[jnp.maximum(i - 1, 0)])

    @pl.when(jnp.logical_and(changed, i < n_used))
    def _():
        wgb[...] = wg_ref[0].astype(BF16)
        wub[...] = wu_ref[0].astype(BF16)
        wdb[...] = wd_ref[0].astype(BF16)

    @pl.when(jnp.logical_and(i >= n_used, i < n_used + GATHER_AHEAD))
    def _():
        _wait_rows(xbuf, slot, sem)

    @pl.when(i < n_used)
    def _():
        _wait_rows(xbuf, slot, sem)
        for j in range(ROW_TILES):
            xs_ref[:, j * LANES:(j + 1) * LANES] = xbuf[slot, j].astype(BF16)
        _gather_rows_inline(tok2_ref, h_hbm, xbuf, lax.rem(i + GATHER_AHEAD, GATHER_SLOTS), sem, tb)
        xs = xs_ref[...]
        gate = _dot(xs, wgb[...])
        up = _dot(xs, wub[...])
        hid = (jax.nn.silu(gate) * up).astype(BF16)
        y = _dot(hid, wdb[...])
        for j in range(ROW_TILES):
            y_ref[pl.ds(j, tb, stride=ROW_TILES), :] = y[:, j * LANES:(j + 1) * LANES]

    @pl.when(i >= n_used)
    def _():
        y_ref[...] = jnp.zeros_like(y_ref)


def _combine_kernel(dst_ref, dst1_ref, dst2_ref, h_ref, route_ref, y_hbm, g2_ref, b2_ref, o_ref, gbuf, sem):
    i = pl.program_id(0)
    n = pl.num_programs(0)
    slot = lax.rem(i, GATHER_SLOTS)
    tf = TILE_COMBINE

    @pl.when(i == 0)
    def _():
        _gather_rows_loop(dst_ref, y_hbm, gbuf, 0, sem, TOP_K * tf)
        _gather_rows_loop(dst1_ref, y_hbm, gbuf, 1, sem, TOP_K * tf)

    _wait_rows(gbuf, slot, sem)
    route = route_ref[...]
    gate1 = route[:, 2:3]
    gate2 = route[:, 3:4]
    parts = []
    for j in range(ROW_TILES):
        hj = h_ref[pl.ds(j, tf, stride=ROW_TILES), :]
        y1 = gbuf[slot, j, 0:tf, :]
        y2 = gbuf[slot, j, tf:TOP_K * tf, :]
        parts.append(DEEPNORM_ALPHA * hj + (gate1 * y1 + gate2 * y2))
    z = jnp.concatenate(parts, axis=-1)
    o_ref[...] = _layer_norm(z, g2_ref[...], b2_ref[...])
    _gather_rows_inline(dst2_ref, y_hbm, gbuf, lax.rem(i + GATHER_AHEAD, GATHER_SLOTS), sem, TOP_K * tf)

    @pl.when(i == n - 1)
    def _():
        for ahead in range(1, GATHER_AHEAD + 1):
            _wait_rows(gbuf, lax.rem(i + ahead, GATHER_SLOTS), sem)


def _rope_tables(seq):
    inv = 1.0 / (ROPE_THETA ** (jnp.arange(0, QK_ROPE_DIM, 2, dtype=F32) / QK_ROPE_DIM))
    ang = jnp.arange(seq, dtype=F32)[:, None] * inv[None, :]
    cos, sin = jnp.cos(ang), jnp.sin(ang)
    zeros = jnp.zeros((seq, QK_NOPE_DIM), F32)
    pad = jnp.zeros((seq, HEAD_PAD - QK_NOPE_DIM - QK_ROPE_DIM), F32)
    z16 = jnp.zeros((seq, HALF_ROPE), F32)
    rot_a = jnp.concatenate([zeros, cos, cos, pad], axis=1)
    rot_m = jnp.concatenate([zeros, -sin, z16, pad], axis=1)
    rot_p = jnp.concatenate([zeros, z16, sin, pad], axis=1)
    return cos.T, sin.T, rot_a, rot_m, rot_p


def _scatter_rows_sparsecore(x, indices, n_out):
    n, width = x.shape
    mesh = plsc.VectorSubcoreMesh(core_axis_name="core", subcore_axis_name="subcore")
    per_core = n // SC_SCATTER_WINDOW // mesh.num_cores
    assert per_core * SC_SCATTER_WINDOW * mesh.num_cores == n

    @functools.partial(pl.kernel, out_type=jax.ShapeDtypeStruct((n_out, width), x.dtype), mesh=mesh,
                       scratch_types=[])
    def scatter(x_hbm, i_hbm, o_hbm):
        first = lax.axis_index("core") * per_core

        def body(x_vmem, i_vmem):
            pltpu.sync_copy(x_vmem, o_hbm.at[i_vmem.at[0]])

        pltpu.emit_pipeline(
            body,
            grid=(per_core,),
            in_specs=[pl.BlockSpec((SC_SCATTER_WINDOW, width), index_map=lambda i: (first + i, 0)),
                      pl.BlockSpec((1, SC_SCATTER_WINDOW), index_map=lambda i: (0, first + i))],
            out_specs=[],
            core_axis_name="subcore",
            dimension_semantics=(pltpu.PARALLEL,),
        )(x_hbm, i_hbm)

    return scatter(x, indices.reshape(1, n))


def _full(shape):
    return pl.BlockSpec(shape, lambda *_: (0,) * len(shape))


def _params(sem):
    return pltpu.CompilerParams(dimension_semantics=sem, vmem_limit_bytes=VMEM_LIMIT)


def kernel(x, w_in, pool_mix_w, pool_scale, q_norm_g, w_uq, kv_norm_g, w_ukv, w_mla_o, w_out, ln1_g, ln1_b,
           w_router_group, b_router_group, w_router_expert, b_router_expert, w_gate, w_up, w_down, ln2_g, ln2_b):
    B, S, D = x.shape
    assert D == D_MODEL and w_in.shape[0] == DEPTH == 1
    assert S % TILE_LATENT == 0 and S % TILE_Q == 0 and S % TILE_MIX == 0 and TILE_Q % TILE_K == 0
    N = B * S
    assert N % TILE_COMBINE == 0
    H = N_HEADS

    w = w_in[0]
    o1 = POOL_WIDTH
    o2 = o1 + Q_LORA_RANK
    o3 = o2 + KV_LORA_RANK
    o4 = o3 + QK_ROPE_DIM
    w_pool = w[:, :o1].astype(BF16)
    kpe_cols = jnp.pad(w[:, o3:o4], ((0, 0), (QK_NOPE_DIM, HEAD_PAD - QK_NOPE_DIM - QK_ROPE_DIM)))
    w_lat = jnp.concatenate([w[:, o1:o3], kpe_cols], axis=1).astype(BF16)
    w_gates = w[:, o4:].astype(BF16)
    qd = QK_NOPE_DIM + QK_ROPE_DIM
    wuq = jnp.pad(w_uq[0].reshape(Q_LORA_RANK, H, qd), ((0, 0), (0, 0), (0, HEAD_PAD - qd)))
    wuqT = wuq.reshape(Q_LORA_RANK, H * HEAD_PAD).T.astype(BF16)
    wukv = w_ukv[0].reshape(KV_LORA_RANK, H, QK_NOPE_DIM + V_HEAD_DIM)
    wuk = jnp.pad(wukv[:, :, :QK_NOPE_DIM], ((0, 0), (0, 0), (0, HEAD_PAD - QK_NOPE_DIM)))
    wuk = wuk.reshape(KV_LORA_RANK, H * HEAD_PAD).astype(BF16)
    wuv = jnp.pad(wukv[:, :, QK_NOPE_DIM:], ((0, 0), (0, 0), (0, V_ROWS - V_HEAD_DIM)))
    wuvT = wuv.reshape(KV_LORA_RANK, H * V_ROWS).T.astype(BF16)
    w_r = jnp.zeros((D, LANES), F32)
    w_r = w_r.at[:, :N_EXPERTS].set(w_router_expert[0]).at[:, GROUP_LANE0:GROUP_LANE0 + N_GROUPS].set(
        w_router_group[0]).astype(BF16)
    b_r = jnp.zeros((1, LANES), F32)
    b_r = b_r.at[0, :N_EXPERTS].set(b_router_expert[0]).at[0, GROUP_LANE0:GROUP_LANE0 + N_GROUPS].set(
        b_router_group[0])
    cosT, sinT, rot_a, rot_m, rot_p = _rope_tables(S)
    q_scale = (QK_NOPE_DIM + QK_ROPE_DIM) ** -0.5 * math.log2(math.e)
    x2 = x.reshape(N, D)

    ta = TILE_LATENT
    nsa = S // ta
    qT, k, vT = pl.pallas_call(
        functools.partial(_latent_kernel, q_scale=q_scale),
        grid=(B, nsa),
        in_specs=[
            pl.BlockSpec((ta, D), lambda b, s: (b * nsa + s, 0)),
            _full(w_lat.shape), _full((1, Q_LORA_RANK)), _full((1, KV_LORA_RANK)),
            _full(wuqT.shape), _full(wuk.shape), _full(wuvT.shape),
            pl.BlockSpec((HALF_ROPE, ta), lambda b, s: (0, s)),
            pl.BlockSpec((HALF_ROPE, ta), lambda b, s: (0, s)),
            pl.BlockSpec((ta, LANES), lambda b, s: (s, 0)),
            pl.BlockSpec((ta, LANES), lambda b, s: (s, 0)),
            pl.BlockSpec((ta, LANES), lambda b, s: (s, 0)),
        ],
        out_specs=[
            pl.BlockSpec((1, H * HEAD_PAD, ta), lambda b, s: (b, 0, s)),
            pl.BlockSpec((1, ta, H * HEAD_PAD), lambda b, s: (b, s, 0)),
            pl.BlockSpec((1, H * V_ROWS, ta), lambda b, s: (b, 0, s)),
        ],
        out_shape=[
            jax.ShapeDtypeStruct((B, H * HEAD_PAD, S), BF16),
            jax.ShapeDtypeStruct((B, S, H * HEAD_PAD), BF16),
            jax.ShapeDtypeStruct((B, H * V_ROWS, S), BF16),
        ],
        compiler_params=_params(("parallel", "parallel")),
    )(x2, w_lat, q_norm_g[0][None], kv_norm_g[0][None], wuqT, wuk, wuvT, cosT, sinT, rot_a, rot_m, rot_p)

    tq = TILE_Q
    assert tq == 2 * TILE_K
    key_chunk = jnp.arange(tq)[:, None] // CHUNK
    q_chunk = jnp.arange(tq)[None, :] // CHUNK
    mask_bias = jnp.where(key_chunk <= q_chunk, 0.0, NEG_BIG).astype(F32).reshape(2, TILE_K, tq)
    hp = ATTN_HEADS_PER_STEP
    assert H % hp == 0
    oT = pl.pallas_call(
        _attn_kernel,
        grid=(B, H // hp, S // tq),
        in_specs=[
            pl.BlockSpec((1, hp * HEAD_PAD, tq), lambda b, h, q: (b, h, q)),
            pl.BlockSpec((1, S, hp * HEAD_PAD), lambda b, h, q: (b, 0, h)),
            pl.BlockSpec((1, hp * V_ROWS, S), lambda b, h, q: (b, h, 0)),
            _full(mask_bias.shape),
        ],
        out_specs=pl.BlockSpec((1, hp * V_HEAD_DIM, tq), lambda b, h, q: (b, h, q)),
        out_shape=jax.ShapeDtypeStruct((B, H * V_HEAD_DIM, S), BF16),
        scratch_shapes=[pltpu.VMEM((TILE_K, tq), F32)] * (2 * hp),
        compiler_params=_params(("parallel", "parallel", "arbitrary")),
    )(qT, k, vT, mask_bias)

    tm = TILE_MIX
    nsm = S // tm
    tri = (jnp.arange(tm)[:, None] > jnp.arange(tm)[None, :]).astype(BF16)
    h_tok, route, routeT, counts = pl.pallas_call(
        _mix_kernel,
        grid=(B, nsm),
        in_specs=[
            pl.BlockSpec((tm, D), lambda b, s: (b * nsm + s, 0)),
            pl.BlockSpec((1, H * V_HEAD_DIM, tm), lambda b, s: (b, 0, s)),
            _full(w_pool.shape), _full(w_gates.shape), _full(pool_mix_w.shape[1:]), _full((1, D)),
            _full(w_mla_o.shape[1:]), _full(w_out.shape[1:]), _full((1, D)), _full((1, D)),
            _full(w_r.shape), _full(b_r.shape), _full(tri.shape),
        ],
        out_specs=[
            pl.BlockSpec((tm * ROW_TILES, LANES), lambda b, s: (b * nsm + s, 0)),
            pl.BlockSpec((tm, LANES), lambda b, s: (b * nsm + s, 0)),
            pl.BlockSpec((SUBLANES, tm), lambda b, s: (0, b * nsm + s)),
            _full((SUBLANES, LANES)),
        ],
        out_shape=[
            jax.ShapeDtypeStruct((N * ROW_TILES, LANES), F32),
            jax.ShapeDtypeStruct((N, LANES), F32),
            jax.ShapeDtypeStruct((SUBLANES, N), F32),
            jax.ShapeDtypeStruct((SUBLANES, LANES), F32),
        ],
        scratch_shapes=[pltpu.VMEM((tm + POOL_HALO, POOL_WIDTH), F32), pltpu.VMEM((SUBLANES, LANES), F32)],
        compiler_params=_params(("arbitrary", "arbitrary")),
    )(x2, oT, w_pool, w_gates, pool_mix_w[0].astype(BF16), pool_scale[0][None], w_mla_o[0].astype(BF16),
      w_out[0].astype(BF16), ln1_g[0][None], ln1_b[0][None], w_r, b_r, tri)

    tb = TILE_EXPERT
    A = N * TOP_K
    n_blocks = -(-(A + N_EXPERTS * (tb - 1)) // tb) + GATHER_AHEAD
    R = n_blocks * tb
    e_idx = routeT[0:2].astype(jnp.int32)
    rank = routeT[4:6].astype(jnp.int32)
    cnt = counts[0, :N_EXPERTS].astype(jnp.int32)
    padded = ((cnt + tb - 1) // tb) * tb
    pad_end = jnp.cumsum(padded)
    pad_start = pad_end - padded
    is_e = e_idx[:, None, :] == jnp.arange(N_EXPERTS, dtype=jnp.int32)[None, :, None]
    dest = jnp.sum(jnp.where(is_e, pad_start[None, :, None], 0), axis=1) + rank
    block_start = jnp.arange(n_blocks, dtype=jnp.int32) * tb
    block_e = jnp.minimum(jnp.sum(pad_end[None, :] <= block_start[:, None], axis=1), N_EXPERTS - 1).astype(jnp.int32)
    n_used = (pad_end[-1] // tb).astype(jnp.int32)[None]
    tok = jnp.broadcast_to(jnp.arange(N, dtype=jnp.int32)[None, :, None], (TOP_K, N, LANES)).reshape(A, LANES)
    tok_rows = _scatter_rows_sparsecore(tok, dest.reshape(A), R)[:, 0].reshape(n_blocks, tb)
    row_in_expert = block_start[:, None] + jnp.arange(tb, dtype=jnp.int32)[None, :] - pad_start[block_e][:, None]
    row_valid = jnp.logical_and(row_in_expert < cnt[block_e][:, None],
                                jnp.arange(n_blocks, dtype=jnp.int32)[:, None] < n_used[0])
    row_tok3 = jnp.where(row_valid, tok_rows, 0).reshape(n_blocks, 1, tb)

    y_rows = pl.pallas_call(
        _expert_kernel,
        grid_spec=pltpu.PrefetchScalarGridSpec(
            num_scalar_prefetch=2,
            grid=(n_blocks,),
            in_specs=[
                pl.BlockSpec((1, 1, tb), lambda i, be, nu: (i, 0, 0), memory_space=pltpu.SMEM),
                pl.BlockSpec((1, 1, tb), lambda i, be, nu: (jnp.minimum(i + 1, n_blocks - 1), 0, 0),
                             memory_space=pltpu.SMEM),
                pl.BlockSpec((1, 1, tb), lambda i, be, nu: (jnp.minimum(i + GATHER_AHEAD, n_blocks - 1), 0, 0),
                             memory_space=pltpu.SMEM),
                pl.BlockSpec(memory_space=pl.ANY),
                pl.BlockSpec((1, D, D_EXPERT), lambda i, be, nu: (be[i], 0, 0)),
                pl.BlockSpec((1, D, D_EXPERT), lambda i, be, nu: (be[i], 0, 0)),
                pl.BlockSpec((1, D_EXPERT, D), lambda i, be, nu: (be[i], 0, 0)),
            ],
            out_specs=pl.BlockSpec((tb * ROW_TILES, LANES), lambda i, be, nu: (i, 0)),
            scratch_shapes=[
                pltpu.VMEM((GATHER_SLOTS, ROW_TILES, tb, LANES), F32), pltpu.VMEM((tb, D), BF16),
                pltpu.VMEM((D, D_EXPERT), BF16), pltpu.VMEM((D, D_EXPERT), BF16), pltpu.VMEM((D_EXPERT, D), BF16),
                pltpu.SemaphoreType.DMA((GATHER_SLOTS,)),
            ],
        ),
        out_shape=jax.ShapeDtypeStruct((R * ROW_TILES, LANES), F32),
        compiler_params=_params(("arbitrary",)),
    )(block_e, n_used, row_tok3, row_tok3, row_tok3, h_tok, w_gate[0], w_up[0], w_down[0])

    tf = TILE_COMBINE
    nf = N // tf
    dest3 = dest.reshape(TOP_K, nf, tf).transpose(1, 0, 2).reshape(nf, 1, TOP_K * tf)
    out = pl.pallas_call(
        _combine_kernel,
        grid=(nf,),
        in_specs=[
            pl.BlockSpec((1, 1, TOP_K * tf), lambda i: (i, 0, 0), memory_space=pltpu.SMEM),
            pl.BlockSpec((1, 1, TOP_K * tf), lambda i: (jnp.minimum(i + 1, nf - 1), 0, 0), memory_space=pltpu.SMEM),
            pl.BlockSpec((1, 1, TOP_K * tf), lambda i: (jnp.minimum(i + GATHER_AHEAD, nf - 1), 0, 0),
                         memory_space=pltpu.SMEM),
            pl.BlockSpec((tf * ROW_TILES, LANES), lambda i: (i, 0)),
            pl.BlockSpec((tf, LANES), lambda i: (i, 0)),
            pl.BlockSpec(memory_space=pl.ANY),
            _full((1, D)), _full((1, D)),
        ],
        out_specs=pl.BlockSpec((tf, D), lambda i: (i, 0)),
        out_shape=jax.ShapeDtypeStruct((N, D), F32),
        scratch_shapes=[pltpu.VMEM((GATHER_SLOTS, ROW_TILES, TOP_K * tf, LANES), F32),
                        pltpu.SemaphoreType.DMA((GATHER_SLOTS,))],
        compiler_params=_params(("arbitrary",)),
    )(dest3, dest3, dest3, h_tok, route, y_rows, ln2_g[0][None], ln2_b[0][None])
    return out.reshape(B, S, D)
```

```python
import functools
import math

import jax
import jax.numpy as jnp
import numpy as np
from jax import lax
from jax.experimental import pallas as pl
from jax.experimental.pallas import tpu as pltpu
from jax.experimental.pallas import tpu_sc as plsc

D_MODEL = 1024
CHUNK = 64
POOL_WINDOWS = (2, 4, 8, 16)
POOL_GROUPS = len(POOL_WINDOWS)
POOL_WIDTH = D_MODEL // 2
POOL_GROUP_DIM = POOL_WIDTH // POOL_GROUPS
POOL_OUT_GROUP_DIM = D_MODEL // POOL_GROUPS
POOL_HALO = 16
N_HEADS = 8
QK_NOPE_DIM = D_MODEL // 16
QK_ROPE_DIM = D_MODEL // 32
HALF_ROPE = QK_ROPE_DIM // 2
V_HEAD_DIM = D_MODEL // 16
V_ROWS = V_HEAD_DIM + 16
Q_LORA_RANK = 3 * D_MODEL // 8
KV_LORA_RANK = D_MODEL // 4
ROPE_THETA = 10000.0
N_GROUPS = 4
EXPERTS_PER_GROUP = 8
N_EXPERTS = N_GROUPS * EXPERTS_PER_GROUP
TOP_K = 2
D_EXPERT = D_MODEL // 2
NORM_EPS = 1e-5
DEPTH = 1
DEEPNORM_ALPHA = (2.0 * DEPTH) ** 0.25

LANES = 128
SUBLANES = 8
HEAD_PAD = LANES
ROW_TILES = D_MODEL // LANES
GROUP_LANE0 = 64
NEG_BIG = -1e30
VMEM_LIMIT = 56 * 1024 * 1024

TILE_LATENT = 512
TILE_Q = 512
TILE_K = 256
ATTN_HEADS_PER_STEP = 4
TILE_MIX = 512
MIX_ROW_GROUPS = 1
GATHER_AHEAD = 2
GATHER_SLOTS = GATHER_AHEAD + 1
TILE_EXPERT = 256
SC_SCATTER_WINDOW = 128
TILE_COMBINE = 256

F32 = jnp.float32
BF16 = jnp.bfloat16


def _dot(a, b):
    return jnp.dot(a, b, preferred_element_type=F32)


def _dot_nt(a, b):
    return lax.dot_general(a, b, (((1,), (1,)), ((), ())), preferred_element_type=F32)


def _dot_tn(a, b):
    return lax.dot_general(a, b, (((0,), (0,)), ((), ())), preferred_element_type=F32)


def _rms(v, g):
    ms = jnp.mean(jnp.square(v), axis=-1, keepdims=True)
    return v * lax.rsqrt(ms + NORM_EPS) * g


def _layer_norm(v, g, b):
    mu = jnp.mean(v, axis=-1, keepdims=True)
    c = v - mu
    var = jnp.mean(jnp.square(c), axis=-1, keepdims=True)
    return c * lax.rsqrt(var + NORM_EPS) * g + b


def _latent_kernel(x_ref, wlat_ref, gq_ref, gkv_ref, wuqT_ref, wuk_ref, wuvT_ref,
                   cosT_ref, sinT_ref, ra_ref, rm_ref, rp_ref,
                   qT_ref, k_ref, vT_ref, *, q_scale):
    xb = x_ref[...].astype(BF16)
    lat = _dot(xb, wlat_ref[...])
    c_q = lat[:, :Q_LORA_RANK]
    c_kv = lat[:, Q_LORA_RANK:Q_LORA_RANK + KV_LORA_RANK]
    kpe = lat[:, Q_LORA_RANK + KV_LORA_RANK:]
    qn = _rms(c_q, gq_ref[...]).astype(BF16)
    kvn = _rms(c_kv, gkv_ref[...]).astype(BF16)

    qT = _dot_nt(wuqT_ref[...], qn) * q_scale
    cosT = cosT_ref[...]
    sinT = sinT_ref[...]
    for h in range(N_HEADS):
        r0 = h * HEAD_PAD
        x1 = qT[r0 + QK_NOPE_DIM:r0 + QK_NOPE_DIM + HALF_ROPE]
        x2 = qT[r0 + QK_NOPE_DIM + HALF_ROPE:r0 + QK_NOPE_DIM + QK_ROPE_DIM]
        qT_ref[0, r0:r0 + QK_NOPE_DIM, :] = qT[r0:r0 + QK_NOPE_DIM].astype(BF16)
        qT_ref[0, r0 + QK_NOPE_DIM:r0 + QK_NOPE_DIM + HALF_ROPE, :] = (x1 * cosT - x2 * sinT).astype(BF16)
        qT_ref[0, r0 + QK_NOPE_DIM + HALF_ROPE:r0 + QK_NOPE_DIM + QK_ROPE_DIM, :] = (
            x1 * sinT + x2 * cosT).astype(BF16)
        qT_ref[0, r0 + QK_NOPE_DIM + QK_ROPE_DIM:r0 + HEAD_PAD, :] = jnp.zeros(
            (HEAD_PAD - QK_NOPE_DIM - QK_ROPE_DIM, qT.shape[1]), BF16)

    kpe_rot = (kpe * ra_ref[...] + pltpu.roll(kpe, LANES - HALF_ROPE, 1) * rm_ref[...]
               + pltpu.roll(kpe, HALF_ROPE, 1) * rp_ref[...])
    k = _dot(kvn, wuk_ref[...])
    for h in range(N_HEADS):
        k_ref[0, :, h * HEAD_PAD:(h + 1) * HEAD_PAD] = (k[:, h * HEAD_PAD:(h + 1) * HEAD_PAD] + kpe_rot).astype(BF16)
    vT = _dot_nt(wuvT_ref[...], kvn)
    row = lax.broadcasted_iota(jnp.int32, vT.shape, 0) % V_ROWS
    vT_ref[0] = jnp.where(row == V_HEAD_DIM, 1.0, vT).astype(BF16)


def _attn_kernel(qT_ref, k_ref, vT_ref, bias_ref, oT_ref, *bufs):
    qi = pl.program_id(2)
    tq = qT_ref.shape[2]
    heads = range(ATTN_HEADS_PER_STEP)
    s0, s1 = (bufs[i * ATTN_HEADS_PER_STEP:(i + 1) * ATTN_HEADS_PER_STEP] for i in range(2))

    def scores(h, j, c0=0):
        k0 = pl.multiple_of(j * TILE_K, TILE_K)
        return _dot(k_ref[0, pl.ds(k0, TILE_K), h * HEAD_PAD:(h + 1) * HEAD_PAD],
                    qT_ref[0, h * HEAD_PAD:(h + 1) * HEAD_PAD, c0:])

    def col_max(s):
        while s.shape[0] > SUBLANES:
            half = s.shape[0] // 2
            s = jnp.maximum(s[:half], s[half:])
        return jnp.max(s, axis=0, keepdims=True)

    def step(h, j, s_cur, s_nxt, carry, bias=None, c0=0, c0_nxt=0):
        m_all, acc_all, cmax = carry
        cmax_nxt = None
        if s_nxt is not None:
            s_new = scores(h, j + 1, c0_nxt)
            s_nxt[h][:, c0_nxt:] = s_new
            cmax_nxt = col_max(s_new)
        m, acc = m_all[:, c0:], acc_all[:, c0:]
        s = s_cur[h][:, c0:]
        if bias is not None:
            s = s + bias[:, c0:]
            cmax = col_max(s)
        m_new = jnp.maximum(m, cmax)
        p = jnp.exp2(s - m_new).astype(BF16)
        k0 = pl.multiple_of(j * TILE_K, TILE_K)
        pv = _dot(vT_ref[0, h * V_ROWS:(h + 1) * V_ROWS, pl.ds(k0, TILE_K)], p)
        acc = jnp.exp2(m - m_new) * acc + pv
        if c0:
            m_new = jnp.concatenate([m_all[:, :c0], m_new], axis=1)
            acc = jnp.concatenate([acc_all[:, :c0], acc], axis=1)
        return m_new, acc, cmax_nxt

    carry = []
    for h in heads:
        s_first = scores(h, 0)
        s0[h][...] = s_first
        carry.append((jnp.full((1, tq), NEG_BIG, F32), jnp.zeros((V_ROWS, tq), F32), col_max(s_first)))
    carry = tuple(carry)

    def pair(t, c):
        c = tuple(step(h, 2 * t, s0, s1, c[h]) for h in heads)
        return tuple(step(h, 2 * t + 1, s1, s0, c[h]) for h in heads)

    carry = lax.fori_loop(0, qi, pair, carry)
    jd = 2 * qi
    carry = tuple(step(h, jd, s0, s1, carry[h], bias=bias_ref.at[0], c0_nxt=TILE_K) for h in heads)
    carry = tuple(step(h, jd + 1, s1, None, carry[h], bias=bias_ref.at[1], c0=TILE_K) for h in heads)
    for h in heads:
        acc = carry[h][1]
        oT_ref[0, h * V_HEAD_DIM:(h + 1) * V_HEAD_DIM, :] = (
            acc[:V_HEAD_DIM] / acc[V_HEAD_DIM:V_HEAD_DIM + 1]).astype(BF16)


def _mix_kernel(x_ref, oT_ref, wpool_ref, wgate_ref, mixw_ref, pscale_ref, wo_ref, wout_ref,
                g1_ref, b1_ref, wr_ref, br_ref, tri_ref,
                h_ref, route_ref, routeT_ref, cnt_ref, ext_ref, base_ref):
    b = pl.program_id(0)
    si = pl.program_id(1)
    tm = x_ref.shape[0]

    @pl.when(jnp.logical_and(b == 0, si == 0))
    def _():
        base_ref[...] = jnp.zeros_like(base_ref)

    @pl.when(si == 0)
    def _():
        ext_ref[0:POOL_HALO, :] = jnp.zeros((POOL_HALO, POOL_WIDTH), F32)

    tg = tm // MIX_ROW_GROUPS
    base = base_ref[0:1, :]
    for c in range(MIX_ROW_GROUPS):
        r0 = c * tg
        x = x_ref[r0:r0 + tg, :]
        xb = x.astype(BF16)
        u = _dot(xb, wpool_ref[...])
        ext_ref[POOL_HALO + r0:POOL_HALO + r0 + tg, :] = u
        pos1 = si * tm + r0 + lax.broadcasted_iota(jnp.int32, (tg, POOL_GROUP_DIM), 0) + 1
        y_parts = []
        for g, win in enumerate(POOL_WINDOWS):
            c0 = g * POOL_GROUP_DIM
            ws = u[:, c0:c0 + POOL_GROUP_DIM]
            for kk in range(1, win):
                ws = ws + ext_ref[POOL_HALO + r0 - kk:POOL_HALO + r0 - kk + tg, c0:c0 + POOL_GROUP_DIM]
            count = jnp.minimum(pos1, win).astype(F32)
            pooled = ws / count - u[:, c0:c0 + POOL_GROUP_DIM]
            y_parts.append(_dot(pooled.astype(BF16), mixw_ref[g]))
        y_pool = jnp.concatenate(y_parts, axis=-1) * pscale_ref[...]

        glog = _dot(xb, wgate_ref[...])
        y_mla = _dot_tn(oT_ref[0, :, r0:r0 + tg], wo_ref[...])
        merged = (jax.nn.sigmoid(glog[:, :D_MODEL]) * y_pool + jax.nn.sigmoid(glog[:, D_MODEL:]) * y_mla)
        r = DEEPNORM_ALPHA * x + _dot(merged.astype(BF16), wout_ref[...])
        h = _layer_norm(r, g1_ref[...], b1_ref[...])
        for j in range(ROW_TILES):
            h_ref[pl.ds(r0 * ROW_TILES + j, tg, stride=ROW_TILES), :] = h[:, j * LANES:(j + 1) * LANES]

        logits = _dot(h.astype(BF16), wr_ref[...]) + br_ref[...]
        lane_i = lax.broadcasted_iota(jnp.int32, logits.shape, 1)
        lane = lane_i.astype(F32)
        big = float(LANES)
        is_group = jnp.logical_and(lane_i >= GROUP_LANE0, lane_i < GROUP_LANE0 + N_GROUPS)
        gl = jnp.where(is_group, logits, NEG_BIG)
        gmax = jnp.max(gl, axis=-1, keepdims=True)
        g_w = 1.0 / jnp.sum(jnp.exp(gl - gmax), axis=-1, keepdims=True)
        g_idx = jnp.min(jnp.where(gl == gmax, lane - GROUP_LANE0, big), axis=-1, keepdims=True)
        lane_group = (lane_i // EXPERTS_PER_GROUP).astype(F32)
        in_group = jnp.logical_and(lane_i < N_EXPERTS, lane_group == g_idx)
        el = jnp.where(in_group, logits, NEG_BIG)
        e1max = jnp.max(el, axis=-1, keepdims=True)
        i1 = jnp.min(jnp.where(el == e1max, lane, big), axis=-1, keepdims=True)
        el2 = jnp.where(lane == i1, NEG_BIG, el)
        e2max = jnp.max(el2, axis=-1, keepdims=True)
        i2 = jnp.min(jnp.where(el2 == e2max, lane, big), axis=-1, keepdims=True)
        ratio = jnp.exp(e2max - e1max)
        gate1 = g_w / (1.0 + ratio)
        gate2 = g_w * ratio / (1.0 + ratio)

        hit1 = lane == i1
        hit2 = lane == i2
        onehot = jnp.where(jnp.logical_or(hit1, hit2), 1.0, 0.0)
        before = _dot(tri_ref[...], onehot.astype(BF16)) + base
        rank1 = jnp.sum(jnp.where(hit1, before, 0.0), axis=-1, keepdims=True)
        rank2 = jnp.sum(jnp.where(hit2, before, 0.0), axis=-1, keepdims=True)
        base = base + jnp.sum(onehot, axis=0, keepdims=True)

        route = jnp.where(lane_i == 0, i1, 0.0)
        route = jnp.where(lane_i == 1, i2, route)
        route = jnp.where(lane_i == 2, gate1, route)
        route = jnp.where(lane_i == 3, gate2, route)
        route = jnp.where(lane_i == 4, rank1, route)
        route = jnp.where(lane_i == 5, rank2, route)
        route_ref[r0:r0 + tg, :] = route
        routeT_ref[:, r0:r0 + tg] = route.T[:SUBLANES]

    ext_ref[0:POOL_HALO, :] = ext_ref[tm:tm + POOL_HALO, :]
    base_ref[...] = jnp.broadcast_to(base, base_ref.shape)
    cnt_ref[...] = jnp.broadcast_to(base, cnt_ref.shape)


def _gather_row(idx_ref, r, src_hbm, dst_ref, slot, sem, priority=0):
    src0 = pl.multiple_of(idx_ref[0, 0, r] * ROW_TILES, ROW_TILES)
    pltpu.make_async_copy(src_hbm.at[pl.ds(src0, ROW_TILES), :],
                          dst_ref.at[slot, :, r, :],
                          sem.at[slot]).start(priority=priority)


def _gather_rows_loop(idx_ref, src_hbm, dst_ref, slot, sem, n_rows, unroll=8):
    def body(c, _):
        for u in range(unroll):
            _gather_row(idx_ref, c * unroll + u, src_hbm, dst_ref, slot, sem)
        return 0
    lax.fori_loop(0, n_rows // unroll, body, 0)


def _gather_rows_inline(idx_ref, src_hbm, dst_ref, slot, sem, n_rows):
    for r in range(n_rows):
        _gather_row(idx_ref, r, src_hbm, dst_ref, slot, sem, priority=r % 2)


def _wait_rows(dst_ref, slot, sem):
    pltpu.make_async_copy(dst_ref.at[slot], dst_ref.at[slot], sem.at[slot]).wait()


def _expert_kernel(be_ref, nused_ref, tok_ref, tok1_ref, tok2_ref, h_hbm, wg_ref, wu_ref, wd_ref,
                   y_ref, xbuf, xs_ref, wgb, wub, wdb, sem):
    i = pl.program_id(0)
    n_used = nused_ref[0]
    slot = lax.rem(i, GATHER_SLOTS)
    tb = TILE_EXPERT

    @pl.when(i == 0)
    def _():
        _gather_rows_loop(tok_ref, h_hbm, xbuf, 0, sem, tb)
        _gather_rows_loop(tok1_ref, h_hbm, xbuf, 1, sem, tb)

    changed = jnp.logical_or(i == 0, be_ref[i] != be_ref[jnp.maximum(i - 1, 0)])

    @pl.when(jnp.logical_and(changed, i < n_used))
    def _():
        wgb[...] = wg_ref[0].astype(BF16)
        wub[...] = wu_ref[0].astype(BF16)
        wdb[...] = wd_ref[0].astype(BF16)

    @pl.when(jnp.logical_and(i >= n_used, i < n_used + GATHER_AHEAD))
    def _():
        _wait_rows(xbuf, slot, sem)

    @pl.when(i < n_used)
    def _():
        _wait_rows(xbuf, slot, sem)
        for j in range(ROW_TILES):
            xs_ref[:, j * LANES:(j + 1) * LANES] = xbuf[slot, j].astype(BF16)
        _gather_rows_inline(tok2_ref, h_hbm, xbuf, lax.rem(i + GATHER_AHEAD, GATHER_SLOTS), sem, tb)
        xs = xs_ref[...]
        gate = _dot(xs, wgb[...])
        up = _dot(xs, wub[...])
        hid = (jax.nn.silu(gate) * up).astype(BF16)
        y = _dot(hid, wdb[...])
        for j in range(ROW_TILES):
            y_ref[pl.ds(j, tb, stride=ROW_TILES), :] = y[:, j * LANES:(j + 1) * LANES]

    @pl.when(i >= n_used)
    def _():
        y_ref[...] = jnp.zeros_like(y_ref)


def _combine_kernel(dst_ref, dst1_ref, dst2_ref, h_ref, route_ref, y_hbm, g2_ref, b2_ref, o_ref, gbuf, sem):
    i = pl.program_id(0)
    n = pl.num_programs(0)
    slot = lax.rem(i, GATHER_SLOTS)
    tf = TILE_COMBINE

    @pl.when(i == 0)
    def _():
        _gather_rows_loop(dst_ref, y_hbm, gbuf, 0, sem, TOP_K * tf)
        _gather_rows_loop(dst1_ref, y_hbm, gbuf, 1, sem, TOP_K * tf)

    _wait_rows(gbuf, slot, sem)
    route = route_ref[...]
    gate1 = route[:, 2:3]
    gate2 = route[:, 3:4]
    parts = []
    for j in range(ROW_TILES):
        hj = h_ref[pl.ds(j, tf, stride=ROW_TILES), :]
        y1 = gbuf[slot, j, 0:tf, :]
        y2 = gbuf[slot, j, tf:TOP_K * tf, :]
        parts.append(DEEPNORM_ALPHA * hj + (gate1 * y1 + gate2 * y2))
    z = jnp.concatenate(parts, axis=-1)
    o_ref[...] = _layer_norm(z, g2_ref[...], b2_ref[...])
    _gather_rows_inline(dst2_ref, y_hbm, gbuf, lax.rem(i + GATHER_AHEAD, GATHER_SLOTS), sem, TOP_K * tf)

    @pl.when(i == n - 1)
    def _():
        for ahead in range(1, GATHER_AHEAD + 1):
            _wait_rows(gbuf, lax.rem(i + ahead, GATHER_SLOTS), sem)


def _rope_tables(seq):
    f32 = np.float32
    inv = (f32(1.0) / (f32(ROPE_THETA) ** (np.arange(0, QK_ROPE_DIM, 2, dtype=f32) / f32(QK_ROPE_DIM)))).astype(f32)
    ang = (np.arange(seq, dtype=f32)[:, None] * inv[None, :]).astype(f32)
    cos, sin = np.cos(ang.astype(np.float64)).astype(f32), np.sin(ang.astype(np.float64)).astype(f32)
    zeros = np.zeros((seq, QK_NOPE_DIM), f32)
    pad = np.zeros((seq, HEAD_PAD - QK_NOPE_DIM - QK_ROPE_DIM), f32)
    z16 = np.zeros((seq, HALF_ROPE), f32)
    rot_a = np.concatenate([zeros, cos, cos, pad], axis=1)
    rot_m = np.concatenate([zeros, -sin, z16, pad], axis=1)
    rot_p = np.concatenate([zeros, z16, sin, pad], axis=1)
    return tuple(jnp.asarray(t) for t in (np.ascontiguousarray(cos.T), np.ascontiguousarray(sin.T),
                                          rot_a, rot_m, rot_p))


def _scatter_rows_sparsecore(x, indices, n_out):
    n, width = x.shape
    mesh = plsc.VectorSubcoreMesh(core_axis_name="core", subcore_axis_name="subcore")
    per_core = n // SC_SCATTER_WINDOW // mesh.num_cores
    assert per_core * SC_SCATTER_WINDOW * mesh.num_cores == n

    @functools.partial(pl.kernel, out_type=jax.ShapeDtypeStruct((n_out, width), x.dtype), mesh=mesh,
                       scratch_types=[])
    def scatter(x_hbm, i_hbm, o_hbm):
        first = lax.axis_index("core") * per_core

        def body(x_vmem, i_vmem):
            pltpu.sync_copy(x_vmem, o_hbm.at[i_vmem.at[0]])

        pltpu.emit_pipeline(
            body,
            grid=(per_core,),
            in_specs=[pl.BlockSpec((SC_SCATTER_WINDOW, width), index_map=lambda i: (first + i, 0)),
                      pl.BlockSpec((1, SC_SCATTER_WINDOW), index_map=lambda i: (0, first + i))],
            out_specs=[],
            core_axis_name="subcore",
            dimension_semantics=(pltpu.PARALLEL,),
        )(x_hbm, i_hbm)

    return scatter(x, indices.reshape(1, n))


def _full(shape):
    return pl.BlockSpec(shape, lambda *_: (0,) * len(shape))


def _params(sem):
    return pltpu.CompilerParams(dimension_semantics=sem, vmem_limit_bytes=VMEM_LIMIT)


def kernel(x, w_in, pool_mix_w, pool_scale, q_norm_g, w_uq, kv_norm_g, w_ukv, w_mla_o, w_out, ln1_g, ln1_b,
           w_router_group, b_router_group, w_router_expert, b_router_expert, w_gate, w_up, w_down, ln2_g, ln2_b):
    B, S, D = x.shape
    assert D == D_MODEL and w_in.shape[0] == DEPTH == 1
    assert S % TILE_LATENT == 0 and S % TILE_Q == 0 and S % TILE_MIX == 0 and TILE_Q % TILE_K == 0
    N = B * S
    assert N % TILE_COMBINE == 0
    H = N_HEADS

    w = w_in[0]
    o1 = POOL_WIDTH
    o2 = o1 + Q_LORA_RANK
    o3 = o2 + KV_LORA_RANK
    o4 = o3 + QK_ROPE_DIM
    w_pool = w[:, :o1].astype(BF16)
    kpe_cols = jnp.pad(w[:, o3:o4], ((0, 0), (QK_NOPE_DIM, HEAD_PAD - QK_NOPE_DIM - QK_ROPE_DIM)))
    w_lat = jnp.concatenate([w[:, o1:o3], kpe_cols], axis=1).astype(BF16)
    w_gates = w[:, o4:].astype(BF16)
    qd = QK_NOPE_DIM + QK_ROPE_DIM
    wuq = jnp.pad(w_uq[0].reshape(Q_LORA_RANK, H, qd), ((0, 0), (0, 0), (0, HEAD_PAD - qd)))
    wuqT = wuq.reshape(Q_LORA_RANK, H * HEAD_PAD).T.astype(BF16)
    wukv = w_ukv[0].reshape(KV_LORA_RANK, H, QK_NOPE_DIM + V_HEAD_DIM)
    wuk = jnp.pad(wukv[:, :, :QK_NOPE_DIM], ((0, 0), (0, 0), (0, HEAD_PAD - QK_NOPE_DIM)))
    wuk = wuk.reshape(KV_LORA_RANK, H * HEAD_PAD).astype(BF16)
    wuv = jnp.pad(wukv[:, :, QK_NOPE_DIM:], ((0, 0), (0, 0), (0, V_ROWS - V_HEAD_DIM)))
    wuvT = wuv.reshape(KV_LORA_RANK, H * V_ROWS).T.astype(BF16)
    w_r = jnp.zeros((D, LANES), F32)
    w_r = w_r.at[:, :N_EXPERTS].set(w_router_expert[0]).at[:, GROUP_LANE0:GROUP_LANE0 + N_GROUPS].set(
        w_router_group[0]).astype(BF16)
    b_r = jnp.zeros((1, LANES), F32)
    b_r = b_r.at[0, :N_EXPERTS].set(b_router_expert[0]).at[0, GROUP_LANE0:GROUP_LANE0 + N_GROUPS].set(
        b_router_group[0])
    cosT, sinT, rot_a, rot_m, rot_p = _rope_tables(S)
    q_scale = (QK_NOPE_DIM + QK_ROPE_DIM) ** -0.5 * math.log2(math.e)
    x2 = x.reshape(N, D)

    ta = TILE_LATENT
    nsa = S // ta
    qT, k, vT = pl.pallas_call(
        functools.partial(_latent_kernel, q_scale=q_scale),
        grid=(B, nsa),
        in_specs=[
            pl.BlockSpec((ta, D), lambda b, s: (b * nsa + s, 0)),
            _full(w_lat.shape), _full((1, Q_LORA_RANK)), _full((1, KV_LORA_RANK)),
            _full(wuqT.shape), _full(wuk.shape), _full(wuvT.shape),
            pl.BlockSpec((HALF_ROPE, ta), lambda b, s: (0, s)),
            pl.BlockSpec((HALF_ROPE, ta), lambda b, s: (0, s)),
            pl.BlockSpec((ta, LANES), lambda b, s: (s, 0)),
            pl.BlockSpec((ta, LANES), lambda b, s: (s, 0)),
            pl.BlockSpec((ta, LANES), lambda b, s: (s, 0)),
        ],
        out_specs=[
            pl.BlockSpec((1, H * HEAD_PAD, ta), lambda b, s: (b, 0, s)),
            pl.BlockSpec((1, ta, H * HEAD_PAD), lambda b, s: (b, s, 0)),
            pl.BlockSpec((1, H * V_ROWS, ta), lambda b, s: (b, 0, s)),
        ],
        out_shape=[
            jax.ShapeDtypeStruct((B, H * HEAD_PAD, S), BF16),
            jax.ShapeDtypeStruct((B, S, H * HEAD_PAD), BF16),
            jax.ShapeDtypeStruct((B, H * V_ROWS, S), BF16),
        ],
        compiler_params=_params(("parallel", "parallel")),
    )(x2, w_lat, q_norm_g[0][None], kv_norm_g[0][None], wuqT, wuk, wuvT, cosT, sinT, rot_a, rot_m, rot_p)

    tq = TILE_Q
    assert tq == 2 * TILE_K
    key_chunk = jnp.arange(tq)[:, None] // CHUNK
    q_chunk = jnp.arange(tq)[None, :] // CHUNK
    mask_bias = jnp.where(key_chunk <= q_chunk, 0.0, NEG_BIG).astype(F32).reshape(2, TILE_K, tq)
    hp = ATTN_HEADS_PER_STEP
    assert H % hp == 0
    oT = pl.pallas_call(
        _attn_kernel,
        grid=(B, H // hp, S // tq),
        in_specs=[
            pl.BlockSpec((1, hp * HEAD_PAD, tq), lambda b, h, q: (b, h, q)),
            pl.BlockSpec((1, S, hp * HEAD_PAD), lambda b, h, q: (b, 0, h)),
            pl.BlockSpec((1, hp * V_ROWS, S), lambda b, h, q: (b, h, 0)),
            _full(mask_bias.shape),
        ],
        out_specs=pl.BlockSpec((1, hp * V_HEAD_DIM, tq), lambda b, h, q: (b, h, q)),
        out_shape=jax.ShapeDtypeStruct((B, H * V_HEAD_DIM, S), BF16),
        scratch_shapes=[pltpu.VMEM((TILE_K, tq), F32)] * (2 * hp),
        compiler_params=_params(("parallel", "parallel", "arbitrary")),
    )(qT, k, vT, mask_bias)

    tm = TILE_MIX
    nsm = S // tm
    tg = tm // MIX_ROW_GROUPS
    tri = (jnp.arange(tg)[:, None] > jnp.arange(tg)[None, :]).astype(BF16)
    h_tok, route, routeT, counts = pl.pallas_call(
        _mix_kernel,
        grid=(B, nsm),
        in_specs=[
            pl.BlockSpec((tm, D), lambda b, s: (b * nsm + s, 0)),
            pl.BlockSpec((1, H * V_HEAD_DIM, tm), lambda b, s: (b, 0, s)),
            _full(w_pool.shape), _full(w_gates.shape), _full(pool_mix_w.shape[1:]), _full((1, D)),
            _full(w_mla_o.shape[1:]), _full(w_out.shape[1:]), _full((1, D)), _full((1, D)),
            _full(w_r.shape), _full(b_r.shape), _full(tri.shape),
        ],
        out_specs=[
            pl.BlockSpec((tm * ROW_TILES, LANES), lambda b, s: (b * nsm + s, 0)),
            pl.BlockSpec((tm, LANES), lambda b, s: (b * nsm + s, 0)),
            pl.BlockSpec((SUBLANES, tm), lambda b, s: (0, b * nsm + s)),
            _full((SUBLANES, LANES)),
        ],
        out_shape=[
            jax.ShapeDtypeStruct((N * ROW_TILES, LANES), F32),
            jax.ShapeDtypeStruct((N, LANES), F32),
            jax.ShapeDtypeStruct((SUBLANES, N), F32),
            jax.ShapeDtypeStruct((SUBLANES, LANES), F32),
        ],
        scratch_shapes=[pltpu.VMEM((tm + POOL_HALO, POOL_WIDTH), F32), pltpu.VMEM((SUBLANES, LANES), F32)],
        compiler_params=_params(("arbitrary", "arbitrary")),
    )(x2, oT, w_pool, w_gates, pool_mix_w[0].astype(BF16), pool_scale[0][None], w_mla_o[0].astype(BF16),
      w_out[0].astype(BF16), ln1_g[0][None], ln1_b[0][None], w_r, b_r, tri)

    tb = TILE_EXPERT
    A = N * TOP_K
    n_blocks = -(-(A + N_EXPERTS * (tb - 1)) // tb) + GATHER_AHEAD
    R = n_blocks * tb
    e_idx = routeT[0:2].astype(jnp.int32)
    rank = routeT[4:6].astype(jnp.int32)
    cnt = counts[0, :N_EXPERTS].astype(jnp.int32)
    padded = ((cnt + tb - 1) // tb) * tb
    pad_end = jnp.cumsum(padded)
    pad_start = pad_end - padded
    is_e = e_idx[:, None, :] == jnp.arange(N_EXPERTS, dtype=jnp.int32)[None, :, None]
    dest = jnp.sum(jnp.where(is_e, pad_start[None, :, None], 0), axis=1) + rank
    block_start = jnp.arange(n_blocks, dtype=jnp.int32) * tb
    block_e = jnp.minimum(jnp.sum(pad_end[None, :] <= block_start[:, None], axis=1), N_EXPERTS - 1).astype(jnp.int32)
    n_used = (pad_end[-1] // tb).astype(jnp.int32)[None]
    tok = jnp.broadcast_to(jnp.arange(N, dtype=jnp.int32)[None, :, None], (TOP_K, N, LANES)).reshape(A, LANES)
    tok_rows = _scatter_rows_sparsecore(tok, dest.reshape(A), R)[:, 0].reshape(n_blocks, tb)
    block_is_e = block_e[:, None] == jnp.arange(N_EXPERTS, dtype=jnp.int32)[None, :]
    block_pad_start = jnp.sum(jnp.where(block_is_e, pad_start[None, :], 0), axis=1)
    block_cnt = jnp.sum(jnp.where(block_is_e, cnt[None, :], 0), axis=1)
    row_in_expert = block_start[:, None] + jnp.arange(tb, dtype=jnp.int32)[None, :] - block_pad_start[:, None]
    row_valid = jnp.logical_and(row_in_expert < block_cnt[:, None],
                                jnp.arange(n_blocks, dtype=jnp.int32)[:, None] < n_used[0])
    row_tok3 = jnp.where(row_valid, tok_rows, 0).reshape(n_blocks, 1, tb)

    y_rows = pl.pallas_call(
        _expert_kernel,
        grid_spec=pltpu.PrefetchScalarGridSpec(
            num_scalar_prefetch=2,
            grid=(n_blocks,),
            in_specs=[
                pl.BlockSpec((1, 1, tb), lambda i, be, nu: (i, 0, 0), memory_space=pltpu.SMEM),
                pl.BlockSpec((1, 1, tb), lambda i, be, nu: (jnp.minimum(i + 1, n_blocks - 1), 0, 0),
                             memory_space=pltpu.SMEM),
                pl.BlockSpec((1, 1, tb), lambda i, be, nu: (jnp.minimum(i + GATHER_AHEAD, n_blocks - 1), 0, 0),
                             memory_space=pltpu.SMEM),
                pl.BlockSpec(memory_space=pl.ANY),
                pl.BlockSpec((1, D, D_EXPERT), lambda i, be, nu: (be[i], 0, 0)),
                pl.BlockSpec((1, D, D_EXPERT), lambda i, be, nu: (be[i], 0, 0)),
                pl.BlockSpec((1, D_EXPERT, D), lambda i, be, nu: (be[i], 0, 0)),
            ],
            out_specs=pl.BlockSpec((tb * ROW_TILES, LANES), lambda i, be, nu: (i, 0)),
            scratch_shapes=[
                pltpu.VMEM((GATHER_SLOTS, ROW_TILES, tb, LANES), F32), pltpu.VMEM((tb, D), BF16),
                pltpu.VMEM((D, D_EXPERT), BF16), pltpu.VMEM((D, D_EXPERT), BF16), pltpu.VMEM((D_EXPERT, D), BF16),
                pltpu.SemaphoreType.DMA((GATHER_SLOTS,)),
            ],
        ),
        out_shape=jax.ShapeDtypeStruct((R * ROW_TILES, LANES), F32),
        compiler_params=_params(("arbitrary",)),
    )(block_e, n_used, row_tok3, row_tok3, row_tok3, h_tok, w_gate[0], w_up[0], w_down[0])

    tf = TILE_COMBINE
    nf = N // tf
    dest3 = dest.reshape(TOP_K, nf, tf).transpose(1, 0, 2).reshape(nf, 1, TOP_K * tf)
    out = pl.pallas_call(
        _combine_kernel,
        grid=(nf,),
        in_specs=[
            pl.BlockSpec((1, 1, TOP_K * tf), lambda i: (i, 0, 0), memory_space=pltpu.SMEM),
            pl.BlockSpec((1, 1, TOP_K * tf), lambda i: (jnp.minimum(i + 1, nf - 1), 0, 0), memory_space=pltpu.SMEM),
            pl.BlockSpec((1, 1, TOP_K * tf), lambda i: (jnp.minimum(i + GATHER_AHEAD, nf - 1), 0, 0),
                         memory_space=pltpu.SMEM),
            pl.BlockSpec((tf * ROW_TILES, LANES), lambda i: (i, 0)),
            pl.BlockSpec((tf, LANES), lambda i: (i, 0)),
            pl.BlockSpec(memory_space=pl.ANY),
            _full((1, D)), _full((1, D)),
        ],
        out_specs=pl.BlockSpec((tf, D), lambda i: (i, 0)),
        out_shape=jax.ShapeDtypeStruct((N, D), F32),
        scratch_shapes=[pltpu.VMEM((GATHER_SLOTS, ROW_TILES, TOP_K * tf, LANES), F32),
                        pltpu.SemaphoreType.DMA((GATHER_SLOTS,))],
        compiler_params=_params(("arbitrary",)),
    )(dest3, dest3, dest3, h_tok, route, y_rows, ln2_g[0][None], ln2_b[0][None])
    return out.reshape(B, S, D)
```

```python
import functools
import math

import jax
import jax.numpy as jnp
import numpy as np
from jax import lax
from jax.experimental import pallas as pl
from jax.experimental.pallas import tpu as pltpu
from jax.experimental.pallas import tpu_sc as plsc

D_MODEL = 1024
CHUNK = 64
POOL_WINDOWS = (2, 4, 8, 16)
POOL_GROUPS = len(POOL_WINDOWS)
POOL_WIDTH = D_MODEL // 2
POOL_GROUP_DIM = POOL_WIDTH // POOL_GROUPS
POOL_OUT_GROUP_DIM = D_MODEL // POOL_GROUPS
POOL_HALO = 16
N_HEADS = 8
QK_NOPE_DIM = D_MODEL // 16
QK_ROPE_DIM = D_MODEL // 32
HALF_ROPE = QK_ROPE_DIM // 2
V_HEAD_DIM = D_MODEL // 16
V_ROWS = V_HEAD_DIM + 16
Q_LORA_RANK = 3 * D_MODEL // 8
KV_LORA_RANK = D_MODEL // 4
ROPE_THETA = 10000.0
N_GROUPS = 4
EXPERTS_PER_GROUP = 8
N_EXPERTS = N_GROUPS * EXPERTS_PER_GROUP
TOP_K = 2
D_EXPERT = D_MODEL // 2
NORM_EPS = 1e-5
DEPTH = 1
DEEPNORM_ALPHA = (2.0 * DEPTH) ** 0.25

LANES = 128
SUBLANES = 8
HEAD_PAD = LANES
ROW_TILES = D_MODEL // LANES
GROUP_LANE0 = 64
NEG_BIG = -1e30
VMEM_LIMIT = 56 * 1024 * 1024

TILE_LATENT = 512
TILE_Q = 512
TILE_K = 256
ATTN_HEADS_PER_STEP = 4
ATTN_PAIRS_PER_TRIP = 2
TILE_MIX = 512
MIX_ROW_GROUPS = 1
GATHER_AHEAD = 2
GATHER_SLOTS = GATHER_AHEAD + 1
TILE_EXPERT = 256
EXPERT_GATHER_SPREAD = 4
SC_SCATTER_WINDOW = 128
TILE_COMBINE = 256

F32 = jnp.float32
BF16 = jnp.bfloat16


def _dot(a, b):
    return jnp.dot(a, b, preferred_element_type=F32)


def _dot_nt(a, b):
    return lax.dot_general(a, b, (((1,), (1,)), ((), ())), preferred_element_type=F32)


def _dot_tn(a, b):
    return lax.dot_general(a, b, (((0,), (0,)), ((), ())), preferred_element_type=F32)


def _rms(v, g):
    ms = jnp.mean(jnp.square(v), axis=-1, keepdims=True)
    return v * lax.rsqrt(ms + NORM_EPS) * g


def _layer_norm(v, g, b):
    mu = jnp.mean(v, axis=-1, keepdims=True)
    c = v - mu
    var = jnp.mean(jnp.square(c), axis=-1, keepdims=True)
    return c * lax.rsqrt(var + NORM_EPS) * g + b


def _latent_kernel(x_ref, wlat_ref, gq_ref, gkv_ref, wuqT_ref, wuk_ref, wuvT_ref,
                   cosT_ref, sinT_ref, ra_ref, rm_ref, rp_ref,
                   qT_ref, k_ref, vT_ref, *, q_scale):
    xb = x_ref[...].astype(BF16)
    lat = _dot(xb, wlat_ref[...])
    c_q = lat[:, :Q_LORA_RANK]
    c_kv = lat[:, Q_LORA_RANK:Q_LORA_RANK + KV_LORA_RANK]
    kpe = lat[:, Q_LORA_RANK + KV_LORA_RANK:]
    qn = _rms(c_q, gq_ref[...]).astype(BF16)
    kvn = _rms(c_kv, gkv_ref[...]).astype(BF16)

    qT = _dot_nt(wuqT_ref[...], qn) * q_scale
    cosT = cosT_ref[...]
    sinT = sinT_ref[...]
    for h in range(N_HEADS):
        r0 = h * HEAD_PAD
        x1 = qT[r0 + QK_NOPE_DIM:r0 + QK_NOPE_DIM + HALF_ROPE]
        x2 = qT[r0 + QK_NOPE_DIM + HALF_ROPE:r0 + QK_NOPE_DIM + QK_ROPE_DIM]
        qT_ref[0, r0:r0 + QK_NOPE_DIM, :] = qT[r0:r0 + QK_NOPE_DIM].astype(BF16)
        qT_ref[0, r0 + QK_NOPE_DIM:r0 + QK_NOPE_DIM + HALF_ROPE, :] = (x1 * cosT - x2 * sinT).astype(BF16)
        qT_ref[0, r0 + QK_NOPE_DIM + HALF_ROPE:r0 + QK_NOPE_DIM + QK_ROPE_DIM, :] = (
            x1 * sinT + x2 * cosT).astype(BF16)
        qT_ref[0, r0 + QK_NOPE_DIM + QK_ROPE_DIM:r0 + HEAD_PAD, :] = jnp.zeros(
            (HEAD_PAD - QK_NOPE_DIM - QK_ROPE_DIM, qT.shape[1]), BF16)

    kpe_rot = (kpe * ra_ref[...] + pltpu.roll(kpe, LANES - HALF_ROPE, 1) * rm_ref[...]
               + pltpu.roll(kpe, HALF_ROPE, 1) * rp_ref[...])
    k = _dot(kvn, wuk_ref[...])
    for h in range(N_HEADS):
        k_ref[0, :, h * HEAD_PAD:(h + 1) * HEAD_PAD] = (k[:, h * HEAD_PAD:(h + 1) * HEAD_PAD] + kpe_rot).astype(BF16)
    vT = _dot_nt(wuvT_ref[...], kvn)
    row = lax.broadcasted_iota(jnp.int32, vT.shape, 0) % V_ROWS
    vT_ref[0] = jnp.where(row == V_HEAD_DIM, 1.0, vT).astype(BF16)


def _attn_kernel(qT_ref, k_ref, vT_ref, bias_ref, oT_ref, *bufs):
    qi = pl.program_id(2)
    tq = qT_ref.shape[2]
    heads = range(ATTN_HEADS_PER_STEP)
    s0, s1 = (bufs[i * ATTN_HEADS_PER_STEP:(i + 1) * ATTN_HEADS_PER_STEP] for i in range(2))

    def scores(h, j, c0=0):
        k0 = pl.multiple_of(j * TILE_K, TILE_K)
        return _dot(k_ref[0, pl.ds(k0, TILE_K), h * HEAD_PAD:(h + 1) * HEAD_PAD],
                    qT_ref[0, h * HEAD_PAD:(h + 1) * HEAD_PAD, c0:])

    def col_max(s):
        while s.shape[0] > SUBLANES:
            half = s.shape[0] // 2
            s = jnp.maximum(s[:half], s[half:])
        return jnp.max(s, axis=0, keepdims=True)

    def step(h, j, s_cur, s_nxt, carry, bias=None, c0=0, c0_nxt=0):
        m_all, acc_all, cmax = carry
        cmax_nxt = None
        if s_nxt is not None:
            s_new = scores(h, j + 1, c0_nxt)
            s_nxt[h][:, c0_nxt:] = s_new
            cmax_nxt = col_max(s_new)
        m, acc = m_all[:, c0:], acc_all[:, c0:]
        s = s_cur[h][:, c0:]
        if bias is not None:
            s = s + bias[:, c0:]
            cmax = col_max(s)
        m_new = jnp.maximum(m, cmax)
        p = jnp.exp2(s - m_new).astype(BF16)
        k0 = pl.multiple_of(j * TILE_K, TILE_K)
        pv = _dot(vT_ref[0, h * V_ROWS:(h + 1) * V_ROWS, pl.ds(k0, TILE_K)], p)
        acc = jnp.exp2(m - m_new) * acc + pv
        if c0:
            m_new = jnp.concatenate([m_all[:, :c0], m_new], axis=1)
            acc = jnp.concatenate([acc_all[:, :c0], acc], axis=1)
        return m_new, acc, cmax_nxt

    carry = []
    for h in heads:
        s_first = scores(h, 0)
        s0[h][...] = s_first
        carry.append((jnp.full((1, tq), NEG_BIG, F32), jnp.zeros((V_ROWS, tq), F32), col_max(s_first)))
    carry = tuple(carry)

    def pair(t, c):
        c = tuple(step(h, 2 * t, s0, s1, c[h]) for h in heads)
        return tuple(step(h, 2 * t + 1, s1, s0, c[h]) for h in heads)

    def trip(u, c):
        for i in range(ATTN_PAIRS_PER_TRIP):
            c = pair(ATTN_PAIRS_PER_TRIP * u + i, c)
        return c

    carry = lax.fori_loop(0, qi // ATTN_PAIRS_PER_TRIP, trip, carry)
    done = (qi // ATTN_PAIRS_PER_TRIP) * ATTN_PAIRS_PER_TRIP
    for i in range(ATTN_PAIRS_PER_TRIP - 1):
        carry = lax.cond(done + i < qi, functools.partial(pair, done + i), lambda c: c, carry)
    jd = 2 * qi
    carry = tuple(step(h, jd, s0, s1, carry[h], bias=bias_ref.at[0], c0_nxt=TILE_K) for h in heads)
    carry = tuple(step(h, jd + 1, s1, None, carry[h], bias=bias_ref.at[1], c0=TILE_K) for h in heads)
    for h in heads:
        acc = carry[h][1]
        oT_ref[0, h * V_HEAD_DIM:(h + 1) * V_HEAD_DIM, :] = (
            acc[:V_HEAD_DIM] / acc[V_HEAD_DIM:V_HEAD_DIM + 1]).astype(BF16)


def _mix_kernel(x_ref, oT_ref, wpool_ref, wgate_ref, mixw_ref, pscale_ref, wo_ref, wout_ref,
                g1_ref, b1_ref, wr_ref, br_ref, tri_ref,
                h_ref, route_ref, routeT_ref, cnt_ref, ext_ref, base_ref):
    b = pl.program_id(0)
    si = pl.program_id(1)
    tm = x_ref.shape[0]

    @pl.when(jnp.logical_and(b == 0, si == 0))
    def _():
        base_ref[...] = jnp.zeros_like(base_ref)

    @pl.when(si == 0)
    def _():
        ext_ref[0:POOL_HALO, :] = jnp.zeros((POOL_HALO, POOL_WIDTH), F32)

    tg = tm // MIX_ROW_GROUPS
    base = base_ref[0:1, :]
    for c in range(MIX_ROW_GROUPS):
        r0 = c * tg
        x = x_ref[r0:r0 + tg, :]
        xb = x.astype(BF16)
        u = _dot(xb, wpool_ref[...])
        ext_ref[POOL_HALO + r0:POOL_HALO + r0 + tg, :] = u
        pos1 = si * tm + r0 + lax.broadcasted_iota(jnp.int32, (tg, POOL_GROUP_DIM), 0) + 1
        y_parts = []
        for g, win in enumerate(POOL_WINDOWS):
            c0 = g * POOL_GROUP_DIM
            ws = u[:, c0:c0 + POOL_GROUP_DIM]
            for kk in range(1, win):
                ws = ws + ext_ref[POOL_HALO + r0 - kk:POOL_HALO + r0 - kk + tg, c0:c0 + POOL_GROUP_DIM]
            count = jnp.minimum(pos1, win).astype(F32)
            pooled = ws / count - u[:, c0:c0 + POOL_GROUP_DIM]
            y_parts.append(_dot(pooled.astype(BF16), mixw_ref[g]))
        y_pool = jnp.concatenate(y_parts, axis=-1) * pscale_ref[...]

        glog = _dot(xb, wgate_ref[...])
        y_mla = _dot_tn(oT_ref[0, :, r0:r0 + tg], wo_ref[...])
        merged = (jax.nn.sigmoid(glog[:, :D_MODEL]) * y_pool + jax.nn.sigmoid(glog[:, D_MODEL:]) * y_mla)
        r = DEEPNORM_ALPHA * x + _dot(merged.astype(BF16), wout_ref[...])
        h = _layer_norm(r, g1_ref[...], b1_ref[...])
        for j in range(ROW_TILES):
            h_ref[pl.ds(r0 * ROW_TILES + j, tg, stride=ROW_TILES), :] = h[:, j * LANES:(j + 1) * LANES]

        logits = _dot(h.astype(BF16), wr_ref[...]) + br_ref[...]
        lane_i = lax.broadcasted_iota(jnp.int32, logits.shape, 1)
        lane = lane_i.astype(F32)
        big = float(LANES)
        is_group = jnp.logical_and(lane_i >= GROUP_LANE0, lane_i < GROUP_LANE0 + N_GROUPS)
        gl = jnp.where(is_group, logits, NEG_BIG)
        gmax = jnp.max(gl, axis=-1, keepdims=True)
        g_w = 1.0 / jnp.sum(jnp.exp(gl - gmax), axis=-1, keepdims=True)
        g_idx = jnp.min(jnp.where(gl == gmax, lane - GROUP_LANE0, big), axis=-1, keepdims=True)
        lane_group = (lane_i // EXPERTS_PER_GROUP).astype(F32)
        in_group = jnp.logical_and(lane_i < N_EXPERTS, lane_group == g_idx)
        el = jnp.where(in_group, logits, NEG_BIG)
        e1max = jnp.max(el, axis=-1, keepdims=True)
        i1 = jnp.min(jnp.where(el == e1max, lane, big), axis=-1, keepdims=True)
        el2 = jnp.where(lane == i1, NEG_BIG, el)
        e2max = jnp.max(el2, axis=-1, keepdims=True)
        i2 = jnp.min(jnp.where(el2 == e2max, lane, big), axis=-1, keepdims=True)
        ratio = jnp.exp(e2max - e1max)
        gate1 = g_w / (1.0 + ratio)
        gate2 = g_w * ratio / (1.0 + ratio)

        hit1 = lane == i1
        hit2 = lane == i2
        onehot = jnp.where(jnp.logical_or(hit1, hit2), 1.0, 0.0)
        before = _dot(tri_ref[...], onehot.astype(BF16)) + base
        rank1 = jnp.sum(jnp.where(hit1, before, 0.0), axis=-1, keepdims=True)
        rank2 = jnp.sum(jnp.where(hit2, before, 0.0), axis=-1, keepdims=True)
        base = base + jnp.sum(onehot, axis=0, keepdims=True)

        route = jnp.where(lane_i == 0, i1, 0.0)
        route = jnp.where(lane_i == 1, i2, route)
        route = jnp.where(lane_i == 2, gate1, route)
        route = jnp.where(lane_i == 3, gate2, route)
        route = jnp.where(lane_i == 4, rank1, route)
        route = jnp.where(lane_i == 5, rank2, route)
        route_ref[r0:r0 + tg, :] = route
        routeT_ref[:, r0:r0 + tg] = route.T[:SUBLANES]

    ext_ref[0:POOL_HALO, :] = ext_ref[tm:tm + POOL_HALO, :]
    base_ref[...] = jnp.broadcast_to(base, base_ref.shape)
    cnt_ref[...] = jnp.broadcast_to(base, cnt_ref.shape)


def _gather_row(idx_ref, r, src_hbm, dst_ref, slot, sem, priority=0, offset=0):
    row = idx_ref[0, 0, r] + offset
    src0 = pl.multiple_of(row * ROW_TILES, ROW_TILES)
    pltpu.make_async_copy(src_hbm.at[pl.ds(src0, ROW_TILES), :],
                          dst_ref.at[slot, :, r, :],
                          sem.at[slot]).start(priority=priority)
    return row


def _gather_rows_loop(idx_ref, src_hbm, dst_ref, slot, sem, n_rows, unroll=8):
    def body(c, _):
        for u in range(unroll):
            _gather_row(idx_ref, c * unroll + u, src_hbm, dst_ref, slot, sem)
        return 0
    lax.fori_loop(0, n_rows // unroll, body, 0)


def _zero_after(row, links):
    z = row
    for _ in range(links):
        z = lax.shift_right_logical(z, 32 // links)
    return z


def _gather_rows_inline(idx_ref, src_hbm, dst_ref, slot, sem, n_rows, spread_links=0):
    offset = 0
    for r in range(n_rows):
        row = _gather_row(idx_ref, r, src_hbm, dst_ref, slot, sem, priority=r % 2, offset=offset)
        if spread_links:
            offset = _zero_after(row, spread_links)


def _wait_rows(dst_ref, slot, sem):
    pltpu.make_async_copy(dst_ref.at[slot], dst_ref.at[slot], sem.at[slot]).wait()


def _expert_kernel(be_ref, nused_ref, tok_ref, tok1_ref, tok2_ref, h_hbm, wg_ref, wu_ref, wd_ref,
                   y_ref, xbuf, xs_ref, wgb, wub, wdb, sem):
    i = pl.program_id(0)
    n_used = nused_ref[0]
    slot = lax.rem(i, GATHER_SLOTS)
    tb = TILE_EXPERT

    @pl.when(i == 0)
    def _():
        _gather_rows_loop(tok_ref, h_hbm, xbuf, 0, sem, tb)
        _gather_rows_loop(tok1_ref, h_hbm, xbuf, 1, sem, tb)

    changed = jnp.logical_or(i == 0, be_ref[i] != be_ref[jnp.maximum(i - 1, 0)])

    @pl.when(jnp.logical_and(changed, i < n_used))
    def _():
        wgb[...] = wg_ref[0].astype(BF16)
        wub[...] = wu_ref[0].astype(BF16)
        wdb[...] = wd_ref[0].astype(BF16)

    @pl.when(jnp.logical_and(i >= n_used, i < n_used + GATHER_AHEAD))
    def _():
        _wait_rows(xbuf, slot, sem)

    @pl.when(i < n_used)
    def _():
        _wait_rows(xbuf, slot, sem)
        for j in range(ROW_TILES):
            xs_ref[:, j * LANES:(j + 1) * LANES] = xbuf[slot, j].astype(BF16)
        _gather_rows_inline(tok2_ref, h_hbm, xbuf, lax.rem(i + GATHER_AHEAD, GATHER_SLOTS), sem, tb,
                            spread_links=EXPERT_GATHER_SPREAD)
        xs = xs_ref[...]
        gate = _dot(xs, wgb[...])
        up = _dot(xs, wub[...])
        hid = (jax.nn.silu(gate) * up).astype(BF16)
        y = _dot(hid, wdb[...])
        for j in range(ROW_TILES):
            y_ref[pl.ds(j, tb, stride=ROW_TILES), :] = y[:, j * LANES:(j + 1) * LANES]

    @pl.when(i >= n_used)
    def _():
        y_ref[...] = jnp.zeros_like(y_ref)


def _combine_kernel(dst_ref, dst1_ref, dst2_ref, h_ref, route_ref, y_hbm, g2_ref, b2_ref, o_ref, gbuf, sem):
    i = pl.program_id(0)
    n = pl.num_programs(0)
    slot = lax.rem(i, GATHER_SLOTS)
    tf = TILE_COMBINE

    @pl.when(i == 0)
    def _():
        _gather_rows_loop(dst_ref, y_hbm, gbuf, 0, sem, TOP_K * tf)
        _gather_rows_loop(dst1_ref, y_hbm, gbuf, 1, sem, TOP_K * tf)

    _wait_rows(gbuf, slot, sem)
    route = route_ref[...]
    gate1 = route[:, 2:3]
    gate2 = route[:, 3:4]
    parts = []
    for j in range(ROW_TILES):
        hj = h_ref[pl.ds(j, tf, stride=ROW_TILES), :]
        y1 = gbuf[slot, j, 0:tf, :]
        y2 = gbuf[slot, j, tf:TOP_K * tf, :]
        parts.append(DEEPNORM_ALPHA * hj + (gate1 * y1 + gate2 * y2))
    z = jnp.concatenate(parts, axis=-1)
    o_ref[...] = _layer_norm(z, g2_ref[...], b2_ref[...])
    _gather_rows_inline(dst2_ref, y_hbm, gbuf, lax.rem(i + GATHER_AHEAD, GATHER_SLOTS), sem, TOP_K * tf)

    @pl.when(i == n - 1)
    def _():
        for ahead in range(1, GATHER_AHEAD + 1):
            _wait_rows(gbuf, lax.rem(i + ahead, GATHER_SLOTS), sem)


def _rope_tables(seq):
    f32 = np.float32
    inv = (f32(1.0) / (f32(ROPE_THETA) ** (np.arange(0, QK_ROPE_DIM, 2, dtype=f32) / f32(QK_ROPE_DIM)))).astype(f32)
    ang = (np.arange(seq, dtype=f32)[:, None] * inv[None, :]).astype(f32)
    cos, sin = np.cos(ang.astype(np.float64)).astype(f32), np.sin(ang.astype(np.float64)).astype(f32)
    zeros = np.zeros((seq, QK_NOPE_DIM), f32)
    pad = np.zeros((seq, HEAD_PAD - QK_NOPE_DIM - QK_ROPE_DIM), f32)
    z16 = np.zeros((seq, HALF_ROPE), f32)
    rot_a = np.concatenate([zeros, cos, cos, pad], axis=1)
    rot_m = np.concatenate([zeros, -sin, z16, pad], axis=1)
    rot_p = np.concatenate([zeros, z16, sin, pad], axis=1)
    return tuple(jnp.asarray(t) for t in (np.ascontiguousarray(cos.T), np.ascontiguousarray(sin.T),
                                          rot_a, rot_m, rot_p))


def _scatter_rows_sparsecore(x, indices, n_out):
    n, width = x.shape
    mesh = plsc.VectorSubcoreMesh(core_axis_name="core", subcore_axis_name="subcore")
    per_core = n // SC_SCATTER_WINDOW // mesh.num_cores
    assert per_core * SC_SCATTER_WINDOW * mesh.num_cores == n

    @functools.partial(pl.kernel, out_type=jax.ShapeDtypeStruct((n_out, width), x.dtype), mesh=mesh,
                       scratch_types=[])
    def scatter(x_hbm, i_hbm, o_hbm):
        first = lax.axis_index("core") * per_core

        def body(x_vmem, i_vmem):
            pltpu.sync_copy(x_vmem, o_hbm.at[i_vmem.at[0]])

        pltpu.emit_pipeline(
            body,
            grid=(per_core,),
            in_specs=[pl.BlockSpec((SC_SCATTER_WINDOW, width), index_map=lambda i: (first + i, 0)),
                      pl.BlockSpec((1, SC_SCATTER_WINDOW), index_map=lambda i: (0, first + i))],
            out_specs=[],
            core_axis_name="subcore",
            dimension_semantics=(pltpu.PARALLEL,),
        )(x_hbm, i_hbm)

    return scatter(x, indices.reshape(1, n))


def _full(shape):
    return pl.BlockSpec(shape, lambda *_: (0,) * len(shape))


def _params(sem):
    return pltpu.CompilerParams(dimension_semantics=sem, vmem_limit_bytes=VMEM_LIMIT)


def kernel(x, w_in, pool_mix_w, pool_scale, q_norm_g, w_uq, kv_norm_g, w_ukv, w_mla_o, w_out, ln1_g, ln1_b,
           w_router_group, b_router_group, w_router_expert, b_router_expert, w_gate, w_up, w_down, ln2_g, ln2_b):
    B, S, D = x.shape
    assert D == D_MODEL and w_in.shape[0] == DEPTH == 1
    assert S % TILE_LATENT == 0 and S % TILE_Q == 0 and S % TILE_MIX == 0 and TILE_Q % TILE_K == 0
    N = B * S
    assert N % TILE_COMBINE == 0
    H = N_HEADS

    w = w_in[0]
    o1 = POOL_WIDTH
    o2 = o1 + Q_LORA_RANK
    o3 = o2 + KV_LORA_RANK
    o4 = o3 + QK_ROPE_DIM
    w_pool = w[:, :o1].astype(BF16)
    kpe_cols = jnp.pad(w[:, o3:o4], ((0, 0), (QK_NOPE_DIM, HEAD_PAD - QK_NOPE_DIM - QK_ROPE_DIM)))
    w_lat = jnp.concatenate([w[:, o1:o3], kpe_cols], axis=1).astype(BF16)
    w_gates = w[:, o4:].astype(BF16)
    qd = QK_NOPE_DIM + QK_ROPE_DIM
    wuq = jnp.pad(w_uq[0].reshape(Q_LORA_RANK, H, qd), ((0, 0), (0, 0), (0, HEAD_PAD - qd)))
    wuqT = wuq.reshape(Q_LORA_RANK, H * HEAD_PAD).T.astype(BF16)
    wukv = w_ukv[0].reshape(KV_LORA_RANK, H, QK_NOPE_DIM + V_HEAD_DIM)
    wuk = jnp.pad(wukv[:, :, :QK_NOPE_DIM], ((0, 0), (0, 0), (0, HEAD_PAD - QK_NOPE_DIM)))
    wuk = wuk.reshape(KV_LORA_RANK, H * HEAD_PAD).astype(BF16)
    wuv = jnp.pad(wukv[:, :, QK_NOPE_DIM:], ((0, 0), (0, 0), (0, V_ROWS - V_HEAD_DIM)))
    wuvT = wuv.reshape(KV_LORA_RANK, H * V_ROWS).T.astype(BF16)
    w_r = jnp.zeros((D, LANES), F32)
    w_r = w_r.at[:, :N_EXPERTS].set(w_router_expert[0]).at[:, GROUP_LANE0:GROUP_LANE0 + N_GROUPS].set(
        w_router_group[0]).astype(BF16)
    b_r = jnp.zeros((1, LANES), F32)
    b_r = b_r.at[0, :N_EXPERTS].set(b_router_expert[0]).at[0, GROUP_LANE0:GROUP_LANE0 + N_GROUPS].set(
        b_router_group[0])
    cosT, sinT, rot_a, rot_m, rot_p = _rope_tables(S)
    q_scale = (QK_NOPE_DIM + QK_ROPE_DIM) ** -0.5 * math.log2(math.e)
    x2 = x.reshape(N, D)

    ta = TILE_LATENT
    nsa = S // ta
    qT, k, vT = pl.pallas_call(
        functools.partial(_latent_kernel, q_scale=q_scale),
        grid=(B, nsa),
        in_specs=[
            pl.BlockSpec((ta, D), lambda b, s: (b * nsa + s, 0)),
            _full(w_lat.shape), _full((1, Q_LORA_RANK)), _full((1, KV_LORA_RANK)),
            _full(wuqT.shape), _full(wuk.shape), _full(wuvT.shape),
            pl.BlockSpec((HALF_ROPE, ta), lambda b, s: (0, s)),
            pl.BlockSpec((HALF_ROPE, ta), lambda b, s: (0, s)),
            pl.BlockSpec((ta, LANES), lambda b, s: (s, 0)),
            pl.BlockSpec((ta, LANES), lambda b, s: (s, 0)),
            pl.BlockSpec((ta, LANES), lambda b, s: (s, 0)),
        ],
        out_specs=[
            pl.BlockSpec((1, H * HEAD_PAD, ta), lambda b, s: (b, 0, s)),
            pl.BlockSpec((1, ta, H * HEAD_PAD), lambda b, s: (b, s, 0)),
            pl.BlockSpec((1, H * V_ROWS, ta), lambda b, s: (b, 0, s)),
        ],
        out_shape=[
            jax.ShapeDtypeStruct((B, H * HEAD_PAD, S), BF16),
            jax.ShapeDtypeStruct((B, S, H * HEAD_PAD), BF16),
            jax.ShapeDtypeStruct((B, H * V_ROWS, S), BF16),
        ],
        compiler_params=_params(("parallel", "parallel")),
    )(x2, w_lat, q_norm_g[0][None], kv_norm_g[0][None], wuqT, wuk, wuvT, cosT, sinT, rot_a, rot_m, rot_p)

    tq = TILE_Q
    assert tq == 2 * TILE_K
    key_chunk = jnp.arange(tq)[:, None] // CHUNK
    q_chunk = jnp.arange(tq)[None, :] // CHUNK
    mask_bias = jnp.where(key_chunk <= q_chunk, 0.0, NEG_BIG).astype(F32).reshape(2, TILE_K, tq)
    hp = ATTN_HEADS_PER_STEP
    assert H % hp == 0
    oT = pl.pallas_call(
        _attn_kernel,
        grid=(B, H // hp, S // tq),
        in_specs=[
            pl.BlockSpec((1, hp * HEAD_PAD, tq), lambda b, h, q: (b, h, q)),
            pl.BlockSpec((1, S, hp * HEAD_PAD), lambda b, h, q: (b, 0, h)),
            pl.BlockSpec((1, hp * V_ROWS, S), lambda b, h, q: (b, h, 0)),
            _full(mask_bias.shape),
        ],
        out_specs=pl.BlockSpec((1, hp * V_HEAD_DIM, tq), lambda b, h, q: (b, h, q)),
        out_shape=jax.ShapeDtypeStruct((B, H * V_HEAD_DIM, S), BF16),
        scratch_shapes=[pltpu.VMEM((TILE_K, tq), F32)] * (2 * hp),
        compiler_params=_params(("parallel", "parallel", "arbitrary")),
    )(qT, k, vT, mask_bias)

    tm = TILE_MIX
    nsm = S // tm
    tg = tm // MIX_ROW_GROUPS
    tri = (jnp.arange(tg)[:, None] > jnp.arange(tg)[None, :]).astype(BF16)
    h_tok, route, routeT, counts = pl.pallas_call(
        _mix_kernel,
        grid=(B, nsm),
        in_specs=[
            pl.BlockSpec((tm, D), lambda b, s: (b * nsm + s, 0)),
            pl.BlockSpec((1, H * V_HEAD_DIM, tm), lambda b, s: (b, 0, s)),
            _full(w_pool.shape), _full(w_gates.shape), _full(pool_mix_w.shape[1:]), _full((1, D)),
            _full(w_mla_o.shape[1:]), _full(w_out.shape[1:]), _full((1, D)), _full((1, D)),
            _full(w_r.shape), _full(b_r.shape), _full(tri.shape),
        ],
        out_specs=[
            pl.BlockSpec((tm * ROW_TILES, LANES), lambda b, s: (b * nsm + s, 0)),
            pl.BlockSpec((tm, LANES), lambda b, s: (b * nsm + s, 0)),
            pl.BlockSpec((SUBLANES, tm), lambda b, s: (0, b * nsm + s)),
            _full((SUBLANES, LANES)),
        ],
        out_shape=[
            jax.ShapeDtypeStruct((N * ROW_TILES, LANES), F32),
            jax.ShapeDtypeStruct((N, LANES), F32),
            jax.ShapeDtypeStruct((SUBLANES, N), F32),
            jax.ShapeDtypeStruct((SUBLANES, LANES), F32),
        ],
        scratch_shapes=[pltpu.VMEM((tm + POOL_HALO, POOL_WIDTH), F32), pltpu.VMEM((SUBLANES, LANES), F32)],
        compiler_params=_params(("arbitrary", "arbitrary")),
    )(x2, oT, w_pool, w_gates, pool_mix_w[0].astype(BF16), pool_scale[0][None], w_mla_o[0].astype(BF16),
      w_out[0].astype(BF16), ln1_g[0][None], ln1_b[0][None], w_r, b_r, tri)

    tb = TILE_EXPERT
    A = N * TOP_K
    n_blocks = -(-(A + N_EXPERTS * (tb - 1)) // tb) + GATHER_AHEAD
    R = n_blocks * tb
    e_idx = routeT[0:2].astype(jnp.int32)
    rank = routeT[4:6].astype(jnp.int32)
    cnt = counts[0, :N_EXPERTS].astype(jnp.int32)
    padded = ((cnt + tb - 1) // tb) * tb
    pad_end = jnp.cumsum(padded)
    pad_start = pad_end - padded
    is_e = e_idx[:, None, :] == jnp.arange(N_EXPERTS, dtype=jnp.int32)[None, :, None]
    dest = jnp.sum(jnp.where(is_e, pad_start[None, :, None], 0), axis=1) + rank
    block_start = jnp.arange(n_blocks, dtype=jnp.int32) * tb
    block_e = jnp.minimum(jnp.sum(pad_end[None, :] <= block_start[:, None], axis=1), N_EXPERTS - 1).astype(jnp.int32)
    n_used = (pad_end[-1] // tb).astype(jnp.int32)[None]
    tok = jnp.broadcast_to(jnp.arange(N, dtype=jnp.int32)[None, :, None], (TOP_K, N, LANES)).reshape(A, LANES)
    tok_rows = _scatter_rows_sparsecore(tok, dest.reshape(A), R)[:, 0].reshape(n_blocks, tb)
    block_is_e = block_e[:, None] == jnp.arange(N_EXPERTS, dtype=jnp.int32)[None, :]
    block_pad_start = jnp.sum(jnp.where(block_is_e, pad_start[None, :], 0), axis=1)
    block_cnt = jnp.sum(jnp.where(block_is_e, cnt[None, :], 0), axis=1)
    row_in_expert = block_start[:, None] + jnp.arange(tb, dtype=jnp.int32)[None, :] - block_pad_start[:, None]
    row_valid = jnp.logical_and(row_in_expert < block_cnt[:, None],
                                jnp.arange(n_blocks, dtype=jnp.int32)[:, None] < n_used[0])
    row_tok3 = jnp.where(row_valid, tok_rows, 0).reshape(n_blocks, 1, tb)

    y_rows = pl.pallas_call(
        _expert_kernel,
        grid_spec=pltpu.PrefetchScalarGridSpec(
            num_scalar_prefetch=2,
            grid=(n_blocks,),
            in_specs=[
                pl.BlockSpec((1, 1, tb), lambda i, be, nu: (i, 0, 0), memory_space=pltpu.SMEM),
                pl.BlockSpec((1, 1, tb), lambda i, be, nu: (jnp.minimum(i + 1, n_blocks - 1), 0, 0),
                             memory_space=pltpu.SMEM),
                pl.BlockSpec((1, 1, tb), lambda i, be, nu: (jnp.minimum(i + GATHER_AHEAD, n_blocks - 1), 0, 0),
                             memory_space=pltpu.SMEM),
                pl.BlockSpec(memory_space=pl.ANY),
                pl.BlockSpec((1, D, D_EXPERT), lambda i, be, nu: (be[i], 0, 0)),
                pl.BlockSpec((1, D, D_EXPERT), lambda i, be, nu: (be[i], 0, 0)),
                pl.BlockSpec((1, D_EXPERT, D), lambda i, be, nu: (be[i], 0, 0)),
            ],
            out_specs=pl.BlockSpec((tb * ROW_TILES, LANES), lambda i, be, nu: (i, 0)),
            scratch_shapes=[
                pltpu.VMEM((GATHER_SLOTS, ROW_TILES, tb, LANES), F32), pltpu.VMEM((tb, D), BF16),
                pltpu.VMEM((D, D_EXPERT), BF16), pltpu.VMEM((D, D_EXPERT), BF16), pltpu.VMEM((D_EXPERT, D), BF16),
                pltpu.SemaphoreType.DMA((GATHER_SLOTS,)),
            ],
        ),
        out_shape=jax.ShapeDtypeStruct((R * ROW_TILES, LANES), F32),
        compiler_params=_params(("arbitrary",)),
    )(block_e, n_used, row_tok3, row_tok3, row_tok3, h_tok, w_gate[0], w_up[0], w_down[0])

    tf = TILE_COMBINE
    nf = N // tf
    dest3 = dest.reshape(TOP_K, nf, tf).transpose(1, 0, 2).reshape(nf, 1, TOP_K * tf)
    out = pl.pallas_call(
        _combine_kernel,
        grid=(nf,),
        in_specs=[
            pl.BlockSpec((1, 1, TOP_K * tf), lambda i: (i, 0, 0), memory_space=pltpu.SMEM),
            pl.BlockSpec((1, 1, TOP_K * tf), lambda i: (jnp.minimum(i + 1, nf - 1), 0, 0), memory_space=pltpu.SMEM),
            pl.BlockSpec((1, 1, TOP_K * tf), lambda i: (jnp.minimum(i + GATHER_AHEAD, nf - 1), 0, 0),
                         memory_space=pltpu.SMEM),
            pl.BlockSpec((tf * ROW_TILES, LANES), lambda i: (i, 0)),
            pl.BlockSpec((tf, LANES), lambda i: (i, 0)),
            pl.BlockSpec(memory_space=pl.ANY),
            _full((1, D)), _full((1, D)),
        ],
        out_specs=pl.BlockSpec((tf, D), lambda i: (i, 0)),
        out_shape=jax.ShapeDtypeStruct((N, D), F32),
        scratch_shapes=[pltpu.VMEM((GATHER_SLOTS, ROW_TILES, TOP_K * tf, LANES), F32),
                        pltpu.SemaphoreType.DMA((GATHER_SLOTS,))],
        compiler_params=_params(("arbitrary",)),
    )(dest3, dest3, dest3, h_tok, route, y_rows, ln2_g[0][None], ln2_b[0][None])
    return out.reshape(B, S, D)
```

```python
import functools
import math

import jax
import jax.numpy as jnp
import numpy as np
from jax import lax
from jax.experimental import pallas as pl
from jax.experimental.pallas import tpu as pltpu
from jax.experimental.pallas import tpu_sc as plsc

D_MODEL = 1024
CHUNK = 64
POOL_WINDOWS = (2, 4, 8, 16)
POOL_GROUPS = len(POOL_WINDOWS)
POOL_WIDTH = D_MODEL // 2
POOL_GROUP_DIM = POOL_WIDTH // POOL_GROUPS
POOL_OUT_GROUP_DIM = D_MODEL // POOL_GROUPS
POOL_HALO = 16
N_HEADS = 8
QK_NOPE_DIM = D_MODEL // 16
QK_ROPE_DIM = D_MODEL // 32
HALF_ROPE = QK_ROPE_DIM // 2
V_HEAD_DIM = D_MODEL // 16
V_ROWS = V_HEAD_DIM + 16
Q_LORA_RANK = 3 * D_MODEL // 8
KV_LORA_RANK = D_MODEL // 4
ROPE_THETA = 10000.0
N_GROUPS = 4
EXPERTS_PER_GROUP = 8
N_EXPERTS = N_GROUPS * EXPERTS_PER_GROUP
TOP_K = 2
D_EXPERT = D_MODEL // 2
NORM_EPS = 1e-5
DEPTH = 1
DEEPNORM_ALPHA = (2.0 * DEPTH) ** 0.25

LANES = 128
SUBLANES = 8
HEAD_PAD = LANES
ROW_TILES = D_MODEL // LANES
GROUP_ROW0 = 64
NEG_BIG = -1e30
VMEM_LIMIT = 56 * 1024 * 1024

TILE_LATENT = 512
TILE_Q = 512
TILE_K = 256
ATTN_HEADS_PER_STEP = 4
ATTN_PAIRS_PER_TRIP = 2
TILE_MIX = 512
GATHER_AHEAD = 2
GATHER_SLOTS = GATHER_AHEAD + 1
TILE_EXPERT = 256
SC_SCATTER_WINDOW = 128
TILE_COMBINE = 256

F32 = jnp.float32
BF16 = jnp.bfloat16


def _dot(a, b):
    return jnp.dot(a, b, preferred_element_type=F32)


def _dot_nt(a, b):
    return lax.dot_general(a, b, (((1,), (1,)), ((), ())), preferred_element_type=F32)


def _dot_tn(a, b):
    return lax.dot_general(a, b, (((0,), (0,)), ((), ())), preferred_element_type=F32)


def _rms(v, g):
    ms = jnp.mean(jnp.square(v), axis=-1, keepdims=True)
    return v * lax.rsqrt(ms + NORM_EPS) * g


def _layer_norm(v, g, b):
    mu = jnp.mean(v, axis=-1, keepdims=True)
    c = v - mu
    var = jnp.mean(jnp.square(c), axis=-1, keepdims=True)
    return c * lax.rsqrt(var + NORM_EPS) * g + b


def _latent_kernel(x_ref, wlat_ref, gq_ref, gkv_ref, wuqT_ref, wuk_ref, wuvT_ref,
                   cosT_ref, sinT_ref, ra_ref, rm_ref, rp_ref,
                   qT_ref, k_ref, vT_ref, *, q_scale):
    xb = x_ref[...].astype(BF16)
    lat = _dot(xb, wlat_ref[...])
    c_q = lat[:, :Q_LORA_RANK]
    c_kv = lat[:, Q_LORA_RANK:Q_LORA_RANK + KV_LORA_RANK]
    kpe = lat[:, Q_LORA_RANK + KV_LORA_RANK:]
    qn = _rms(c_q, gq_ref[...]).astype(BF16)
    kvn = _rms(c_kv, gkv_ref[...]).astype(BF16)

    qT = _dot_nt(wuqT_ref[...], qn) * q_scale
    cosT = cosT_ref[...]
    sinT = sinT_ref[...]
    for h in range(N_HEADS):
        r0 = h * HEAD_PAD
        x1 = qT[r0 + QK_NOPE_DIM:r0 + QK_NOPE_DIM + HALF_ROPE]
        x2 = qT[r0 + QK_NOPE_DIM + HALF_ROPE:r0 + QK_NOPE_DIM + QK_ROPE_DIM]
        qT_ref[0, r0:r0 + QK_NOPE_DIM, :] = qT[r0:r0 + QK_NOPE_DIM].astype(BF16)
        qT_ref[0, r0 + QK_NOPE_DIM:r0 + QK_NOPE_DIM + HALF_ROPE, :] = (x1 * cosT - x2 * sinT).astype(BF16)
        qT_ref[0, r0 + QK_NOPE_DIM + HALF_ROPE:r0 + QK_NOPE_DIM + QK_ROPE_DIM, :] = (
            x1 * sinT + x2 * cosT).astype(BF16)
        qT_ref[0, r0 + QK_NOPE_DIM + QK_ROPE_DIM:r0 + HEAD_PAD, :] = jnp.zeros(
            (HEAD_PAD - QK_NOPE_DIM - QK_ROPE_DIM, qT.shape[1]), BF16)

    kpe_rot = (kpe * ra_ref[...] + pltpu.roll(kpe, LANES - HALF_ROPE, 1) * rm_ref[...]
               + pltpu.roll(kpe, HALF_ROPE, 1) * rp_ref[...])
    k = _dot(kvn, wuk_ref[...])
    for h in range(N_HEADS):
        k_ref[0, :, h * HEAD_PAD:(h + 1) * HEAD_PAD] = (k[:, h * HEAD_PAD:(h + 1) * HEAD_PAD] + kpe_rot).astype(BF16)
    vT = _dot_nt(wuvT_ref[...], kvn)
    row = lax.broadcasted_iota(jnp.int32, vT.shape, 0) % V_ROWS
    vT_ref[0] = jnp.where(row == V_HEAD_DIM, 1.0, vT).astype(BF16)


def _attn_kernel(qT_ref, k_ref, vT_ref, bias_ref, oT_ref, *bufs):
    qi = pl.program_id(2)
    tq = qT_ref.shape[2]
    heads = range(ATTN_HEADS_PER_STEP)
    s0, s1 = (bufs[i * ATTN_HEADS_PER_STEP:(i + 1) * ATTN_HEADS_PER_STEP] for i in range(2))

    def scores(h, j, c0=0):
        k0 = pl.multiple_of(j * TILE_K, TILE_K)
        return _dot(k_ref[0, pl.ds(k0, TILE_K), h * HEAD_PAD:(h + 1) * HEAD_PAD],
                    qT_ref[0, h * HEAD_PAD:(h + 1) * HEAD_PAD, c0:])

    def col_max(s):
        while s.shape[0] > SUBLANES:
            half = s.shape[0] // 2
            s = jnp.maximum(s[:half], s[half:])
        return jnp.max(s, axis=0, keepdims=True)

    def step(h, j, s_cur, s_nxt, carry, bias=None, c0=0, c0_nxt=0):
        m_all, acc_all, cmax = carry
        cmax_nxt = None
        if s_nxt is not None:
            s_new = scores(h, j + 1, c0_nxt)
            s_nxt[h][:, c0_nxt:] = s_new
            cmax_nxt = col_max(s_new)
        m, acc = m_all[:, c0:], acc_all[:, c0:]
        s = s_cur[h][:, c0:]
        if bias is not None:
            s = s + bias[:, c0:]
            cmax = col_max(s)
        m_new = jnp.maximum(m, cmax)
        p = jnp.exp2(s - m_new).astype(BF16)
        k0 = pl.multiple_of(j * TILE_K, TILE_K)
        pv = _dot(vT_ref[0, h * V_ROWS:(h + 1) * V_ROWS, pl.ds(k0, TILE_K)], p)
        acc = jnp.exp2(m - m_new) * acc + pv
        if c0:
            m_new = jnp.concatenate([m_all[:, :c0], m_new], axis=1)
            acc = jnp.concatenate([acc_all[:, :c0], acc], axis=1)
        return m_new, acc, cmax_nxt

    carry = []
    for h in heads:
        s_first = scores(h, 0)
        s0[h][...] = s_first
        carry.append((jnp.full((1, tq), NEG_BIG, F32), jnp.zeros((V_ROWS, tq), F32), col_max(s_first)))
    carry = tuple(carry)

    def pair(t, c):
        c = tuple(step(h, 2 * t, s0, s1, c[h]) for h in heads)
        return tuple(step(h, 2 * t + 1, s1, s0, c[h]) for h in heads)

    def trip(u, c):
        for i in range(ATTN_PAIRS_PER_TRIP):
            c = pair(ATTN_PAIRS_PER_TRIP * u + i, c)
        return c

    carry = lax.fori_loop(0, qi // ATTN_PAIRS_PER_TRIP, trip, carry)
    done = (qi // ATTN_PAIRS_PER_TRIP) * ATTN_PAIRS_PER_TRIP
    for i in range(ATTN_PAIRS_PER_TRIP - 1):
        carry = lax.cond(done + i < qi, functools.partial(pair, done + i), lambda c: c, carry)
    jd = 2 * qi
    carry = tuple(step(h, jd, s0, s1, carry[h], bias=bias_ref.at[0], c0_nxt=TILE_K) for h in heads)
    carry = tuple(step(h, jd + 1, s1, None, carry[h], bias=bias_ref.at[1], c0=TILE_K) for h in heads)
    for h in heads:
        acc = carry[h][1]
        oT_ref[0, h * V_HEAD_DIM:(h + 1) * V_HEAD_DIM, :] = (
            acc[:V_HEAD_DIM] / acc[V_HEAD_DIM:V_HEAD_DIM + 1]).astype(BF16)


def _mix_kernel(x_ref, oT_ref, wpool_ref, wgate_ref, mixw_ref, pscale_ref, wo_ref, wout_ref,
                g1_ref, b1_ref, wrT_ref, brT_ref, triu_ref,
                h_ref, route_ref, routeT_ref, cnt_ref, ext_ref, base_ref):
    b = pl.program_id(0)
    si = pl.program_id(1)
    tm = x_ref.shape[0]

    @pl.when(jnp.logical_and(b == 0, si == 0))
    def _():
        base_ref[...] = jnp.zeros_like(base_ref)

    @pl.when(si == 0)
    def _():
        ext_ref[0:POOL_HALO, :] = jnp.zeros((POOL_HALO, POOL_WIDTH), F32)

    x = x_ref[...]
    xb = x.astype(BF16)
    u = _dot(xb, wpool_ref[...])
    ext_ref[POOL_HALO:POOL_HALO + tm, :] = u

    pos1 = si * tm + lax.broadcasted_iota(jnp.int32, (tm, POOL_GROUP_DIM), 0) + 1
    y_parts = []
    for g, win in enumerate(POOL_WINDOWS):
        c0 = g * POOL_GROUP_DIM
        ws = u[:, c0:c0 + POOL_GROUP_DIM]
        for kk in range(1, win):
            ws = ws + ext_ref[POOL_HALO - kk:POOL_HALO - kk + tm, c0:c0 + POOL_GROUP_DIM]
        count = jnp.minimum(pos1, win).astype(F32)
        pooled = ws / count - u[:, c0:c0 + POOL_GROUP_DIM]
        y_parts.append(_dot(pooled.astype(BF16), mixw_ref[g]))
    ext_ref[0:POOL_HALO, :] = ext_ref[tm:tm + POOL_HALO, :]
    y_pool = jnp.concatenate(y_parts, axis=-1) * pscale_ref[...]

    glog = _dot(xb, wgate_ref[...])
    y_mla = _dot_tn(oT_ref[0], wo_ref[...])
    merged = (jax.nn.sigmoid(glog[:, :D_MODEL]) * y_pool + jax.nn.sigmoid(glog[:, D_MODEL:]) * y_mla)
    r = DEEPNORM_ALPHA * x + _dot(merged.astype(BF16), wout_ref[...])
    h = _layer_norm(r, g1_ref[...], b1_ref[...])
    for j in range(ROW_TILES):
        h_ref[pl.ds(j, tm, stride=ROW_TILES), :] = h[:, j * LANES:(j + 1) * LANES]

    logits = _dot_nt(wrT_ref[...], h.astype(BF16)) + brT_ref[...]
    big = float(LANES)
    gl = logits[GROUP_ROW0:GROUP_ROW0 + SUBLANES]
    grow = lax.broadcasted_iota(jnp.int32, gl.shape, 0)
    gl = jnp.where(grow < N_GROUPS, gl, NEG_BIG)
    gmax = jnp.max(gl, axis=0, keepdims=True)
    g_w = 1.0 / jnp.sum(jnp.exp(gl - gmax), axis=0, keepdims=True)
    g_idx = jnp.min(jnp.where(gl == gmax, grow.astype(F32), big), axis=0, keepdims=True)
    el = logits[0:N_EXPERTS]
    erow_i = lax.broadcasted_iota(jnp.int32, el.shape, 0)
    erow = erow_i.astype(F32)
    el = jnp.where((erow_i // EXPERTS_PER_GROUP).astype(F32) == g_idx, el, NEG_BIG)
    e1max = jnp.max(el, axis=0, keepdims=True)
    i1 = jnp.min(jnp.where(el == e1max, erow, big), axis=0, keepdims=True)
    el2 = jnp.where(erow == i1, NEG_BIG, el)
    e2max = jnp.max(el2, axis=0, keepdims=True)
    i2 = jnp.min(jnp.where(el2 == e2max, erow, big), axis=0, keepdims=True)
    ratio = jnp.exp(e2max - e1max)
    gate1 = g_w / (1.0 + ratio)
    gate2 = g_w * ratio / (1.0 + ratio)

    hit1 = erow == i1
    hit2 = erow == i2
    onehot = jnp.where(jnp.logical_or(hit1, hit2), 1.0, 0.0)
    base = base_ref[:, 0:1]
    before = _dot(onehot.astype(BF16), triu_ref[...]) + base
    rank1 = jnp.sum(jnp.where(hit1, before, 0.0), axis=0, keepdims=True)
    rank2 = jnp.sum(jnp.where(hit2, before, 0.0), axis=0, keepdims=True)
    new_base = base + jnp.sum(onehot, axis=1, keepdims=True)
    base_ref[...] = jnp.broadcast_to(new_base, base_ref.shape)
    cnt_ref[...] = jnp.broadcast_to(new_base, cnt_ref.shape)

    row = lax.broadcasted_iota(jnp.int32, (LANES, tm), 0)
    routeT = jnp.where(row == 0, i1, 0.0)
    routeT = jnp.where(row == 1, i2, routeT)
    routeT = jnp.where(row == 2, gate1, routeT)
    routeT = jnp.where(row == 3, gate2, routeT)
    routeT = jnp.where(row == 4, rank1, routeT)
    routeT = jnp.where(row == 5, rank2, routeT)
    routeT_ref[...] = routeT[:SUBLANES]
    route_ref[...] = routeT.T


def _gather_row(idx_ref, r, src_hbm, dst_ref, slot, sem, priority=0):
    src0 = pl.multiple_of(idx_ref[0, 0, r] * ROW_TILES, ROW_TILES)
    pltpu.make_async_copy(src_hbm.at[pl.ds(src0, ROW_TILES), :],
                          dst_ref.at[slot, :, r, :],
                          sem.at[slot]).start(priority=priority)


def _gather_rows_loop(idx_ref, src_hbm, dst_ref, slot, sem, n_rows, unroll=8):
    def body(c, _):
        for u in range(unroll):
            _gather_row(idx_ref, c * unroll + u, src_hbm, dst_ref, slot, sem)
        return 0
    lax.fori_loop(0, n_rows // unroll, body, 0)


def _gather_rows_inline(idx_ref, src_hbm, dst_ref, slot, sem, n_rows):
    for r in range(n_rows):
        _gather_row(idx_ref, r, src_hbm, dst_ref, slot, sem, priority=r % 2)


def _wait_rows(dst_ref, slot, sem):
    pltpu.make_async_copy(dst_ref.at[slot], dst_ref.at[slot], sem.at[slot]).wait()


def _expert_kernel(be_ref, nused_ref, tok_ref, tok1_ref, tok2_ref, h_hbm, wg_ref, wu_ref, wd_ref,
                   y_ref, xbuf, xs_ref, wgb, wub, wdb, sem):
    i = pl.program_id(0)
    n_used = nused_ref[0]
    slot = lax.rem(i, GATHER_SLOTS)
    tb = TILE_EXPERT

    @pl.when(i == 0)
    def _():
        _gather_rows_loop(tok_ref, h_hbm, xbuf, 0, sem, tb)
        _gather_rows_loop(tok1_ref, h_hbm, xbuf, 1, sem, tb)

    changed = jnp.logical_or(i == 0, be_ref[i] != be_ref[jnp.maximum(i - 1, 0)])

    @pl.when(jnp.logical_and(changed, i < n_used))
    def _():
        wgb[...] = wg_ref[0].astype(BF16)
        wub[...] = wu_ref[0].astype(BF16)
        wdb[...] = wd_ref[0].astype(BF16)

    @pl.when(jnp.logical_and(i >= n_used, i < n_used + GATHER_AHEAD))
    def _():
        _wait_rows(xbuf, slot, sem)

    @pl.when(i < n_used)
    def _():
        _wait_rows(xbuf, slot, sem)
        for j in range(ROW_TILES):
            xs_ref[:, j * LANES:(j + 1) * LANES] = xbuf[slot, j].astype(BF16)
        _gather_rows_inline(tok2_ref, h_hbm, xbuf, lax.rem(i + GATHER_AHEAD, GATHER_SLOTS), sem, tb)
        xs = xs_ref[...]
        gate = _dot(xs, wgb[...])
        up = _dot(xs, wub[...])
        hid = (jax.nn.silu(gate) * up).astype(BF16)
        y = _dot(hid, wdb[...])
        for j in range(ROW_TILES):
            y_ref[pl.ds(j, tb, stride=ROW_TILES), :] = y[:, j * LANES:(j + 1) * LANES]

    @pl.when(i >= n_used)
    def _():
        y_ref[...] = jnp.zeros_like(y_ref)


def _combine_kernel(dst_ref, dst1_ref, dst2_ref, h_ref, route_ref, y_hbm, g2_ref, b2_ref, o_ref, gbuf, sem):
    i = pl.program_id(0)
    n = pl.num_programs(0)
    slot = lax.rem(i, GATHER_SLOTS)
    tf = TILE_COMBINE

    @pl.when(i == 0)
    def _():
        _gather_rows_loop(dst_ref, y_hbm, gbuf, 0, sem, TOP_K * tf)
        _gather_rows_loop(dst1_ref, y_hbm, gbuf, 1, sem, TOP_K * tf)

    _wait_rows(gbuf, slot, sem)
    route = route_ref[...]
    gate1 = route[:, 2:3]
    gate2 = route[:, 3:4]
    parts = []
    for j in range(ROW_TILES):
        hj = h_ref[pl.ds(j, tf, stride=ROW_TILES), :]
        y1 = gbuf[slot, j, 0:tf, :]
        y2 = gbuf[slot, j, tf:TOP_K * tf, :]
        parts.append(DEEPNORM_ALPHA * hj + (gate1 * y1 + gate2 * y2))
    z = jnp.concatenate(parts, axis=-1)
    o_ref[...] = _layer_norm(z, g2_ref[...], b2_ref[...])
    _gather_rows_inline(dst2_ref, y_hbm, gbuf, lax.rem(i + GATHER_AHEAD, GATHER_SLOTS), sem, TOP_K * tf)

    @pl.when(i == n - 1)
    def _():
        for ahead in range(1, GATHER_AHEAD + 1):
            _wait_rows(gbuf, lax.rem(i + ahead, GATHER_SLOTS), sem)


def _rope_tables(seq):
    f32 = np.float32
    inv = (f32(1.0) / (f32(ROPE_THETA) ** (np.arange(0, QK_ROPE_DIM, 2, dtype=f32) / f32(QK_ROPE_DIM)))).astype(f32)
    ang = (np.arange(seq, dtype=f32)[:, None] * inv[None, :]).astype(f32)
    cos, sin = np.cos(ang.astype(np.float64)).astype(f32), np.sin(ang.astype(np.float64)).astype(f32)
    zeros = np.zeros((seq, QK_NOPE_DIM), f32)
    pad = np.zeros((seq, HEAD_PAD - QK_NOPE_DIM - QK_ROPE_DIM), f32)
    z16 = np.zeros((seq, HALF_ROPE), f32)
    rot_a = np.concatenate([zeros, cos, cos, pad], axis=1)
    rot_m = np.concatenate([zeros, -sin, z16, pad], axis=1)
    rot_p = np.concatenate([zeros, z16, sin, pad], axis=1)
    return tuple(jnp.asarray(t) for t in (np.ascontiguousarray(cos.T), np.ascontiguousarray(sin.T),
                                          rot_a, rot_m, rot_p))


def _scatter_rows_sparsecore(x, indices, n_out):
    n, width = x.shape
    mesh = plsc.VectorSubcoreMesh(core_axis_name="core", subcore_axis_name="subcore")
    per_core = n // SC_SCATTER_WINDOW // mesh.num_cores
    assert per_core * SC_SCATTER_WINDOW * mesh.num_cores == n

    @functools.partial(pl.kernel, out_type=jax.ShapeDtypeStruct((n_out, width), x.dtype), mesh=mesh,
                       scratch_types=[])
    def scatter(x_hbm, i_hbm, o_hbm):
        first = lax.axis_index("core") * per_core

        def body(x_vmem, i_vmem):
            pltpu.sync_copy(x_vmem, o_hbm.at[i_vmem.at[0]])

        pltpu.emit_pipeline(
            body,
            grid=(per_core,),
            in_specs=[pl.BlockSpec((SC_SCATTER_WINDOW, width), index_map=lambda i: (first + i, 0)),
                      pl.BlockSpec((1, SC_SCATTER_WINDOW), index_map=lambda i: (0, first + i))],
            out_specs=[],
            core_axis_name="subcore",
            dimension_semantics=(pltpu.PARALLEL,),
        )(x_hbm, i_hbm)

    return scatter(x, indices.reshape(1, n))


def _full(shape):
    return pl.BlockSpec(shape, lambda *_: (0,) * len(shape))


def _params(sem):
    return pltpu.CompilerParams(dimension_semantics=sem, vmem_limit_bytes=VMEM_LIMIT)


def kernel(x, w_in, pool_mix_w, pool_scale, q_norm_g, w_uq, kv_norm_g, w_ukv, w_mla_o, w_out, ln1_g, ln1_b,
           w_router_group, b_router_group, w_router_expert, b_router_expert, w_gate, w_up, w_down, ln2_g, ln2_b):
    B, S, D = x.shape
    assert D == D_MODEL and w_in.shape[0] == DEPTH == 1
    assert S % TILE_LATENT == 0 and S % TILE_Q == 0 and S % TILE_MIX == 0 and TILE_Q % TILE_K == 0
    N = B * S
    assert N % TILE_COMBINE == 0
    H = N_HEADS

    w = w_in[0]
    o1 = POOL_WIDTH
    o2 = o1 + Q_LORA_RANK
    o3 = o2 + KV_LORA_RANK
    o4 = o3 + QK_ROPE_DIM
    w_pool = w[:, :o1].astype(BF16)
    kpe_cols = jnp.pad(w[:, o3:o4], ((0, 0), (QK_NOPE_DIM, HEAD_PAD - QK_NOPE_DIM - QK_ROPE_DIM)))
    w_lat = jnp.concatenate([w[:, o1:o3], kpe_cols], axis=1).astype(BF16)
    w_gates = w[:, o4:].astype(BF16)
    qd = QK_NOPE_DIM + QK_ROPE_DIM
    wuq = jnp.pad(w_uq[0].reshape(Q_LORA_RANK, H, qd), ((0, 0), (0, 0), (0, HEAD_PAD - qd)))
    wuqT = wuq.reshape(Q_LORA_RANK, H * HEAD_PAD).T.astype(BF16)
    wukv = w_ukv[0].reshape(KV_LORA_RANK, H, QK_NOPE_DIM + V_HEAD_DIM)
    wuk = jnp.pad(wukv[:, :, :QK_NOPE_DIM], ((0, 0), (0, 0), (0, HEAD_PAD - QK_NOPE_DIM)))
    wuk = wuk.reshape(KV_LORA_RANK, H * HEAD_PAD).astype(BF16)
    wuv = jnp.pad(wukv[:, :, QK_NOPE_DIM:], ((0, 0), (0, 0), (0, V_ROWS - V_HEAD_DIM)))
    wuvT = wuv.reshape(KV_LORA_RANK, H * V_ROWS).T.astype(BF16)
    w_rT = jnp.zeros((LANES, D), F32)
    w_rT = w_rT.at[:N_EXPERTS].set(w_router_expert[0].T).at[GROUP_ROW0:GROUP_ROW0 + N_GROUPS].set(
        w_router_group[0].T).astype(BF16)
    b_rT = jnp.zeros((LANES, 1), F32)
    b_rT = b_rT.at[:N_EXPERTS, 0].set(b_router_expert[0]).at[GROUP_ROW0:GROUP_ROW0 + N_GROUPS, 0].set(
        b_router_group[0])
    cosT, sinT, rot_a, rot_m, rot_p = _rope_tables(S)
    q_scale = (QK_NOPE_DIM + QK_ROPE_DIM) ** -0.5 * math.log2(math.e)
    x2 = x.reshape(N, D)

    ta = TILE_LATENT
    nsa = S // ta
    qT, k, vT = pl.pallas_call(
        functools.partial(_latent_kernel, q_scale=q_scale),
        grid=(B, nsa),
        in_specs=[
            pl.BlockSpec((ta, D), lambda b, s: (b * nsa + s, 0)),
            _full(w_lat.shape), _full((1, Q_LORA_RANK)), _full((1, KV_LORA_RANK)),
            _full(wuqT.shape), _full(wuk.shape), _full(wuvT.shape),
            pl.BlockSpec((HALF_ROPE, ta), lambda b, s: (0, s)),
            pl.BlockSpec((HALF_ROPE, ta), lambda b, s: (0, s)),
            pl.BlockSpec((ta, LANES), lambda b, s: (s, 0)),
            pl.BlockSpec((ta, LANES), lambda b, s: (s, 0)),
            pl.BlockSpec((ta, LANES), lambda b, s: (s, 0)),
        ],
        out_specs=[
            pl.BlockSpec((1, H * HEAD_PAD, ta), lambda b, s: (b, 0, s)),
            pl.BlockSpec((1, ta, H * HEAD_PAD), lambda b, s: (b, s, 0)),
            pl.BlockSpec((1, H * V_ROWS, ta), lambda b, s: (b, 0, s)),
        ],
        out_shape=[
            jax.ShapeDtypeStruct((B, H * HEAD_PAD, S), BF16),
            jax.ShapeDtypeStruct((B, S, H * HEAD_PAD), BF16),
            jax.ShapeDtypeStruct((B, H * V_ROWS, S), BF16),
        ],
        compiler_params=_params(("parallel", "parallel")),
    )(x2, w_lat, q_norm_g[0][None], kv_norm_g[0][None], wuqT, wuk, wuvT, cosT, sinT, rot_a, rot_m, rot_p)

    tq = TILE_Q
    assert tq == 2 * TILE_K
    key_chunk = jnp.arange(tq)[:, None] // CHUNK
    q_chunk = jnp.arange(tq)[None, :] // CHUNK
    mask_bias = jnp.where(key_chunk <= q_chunk, 0.0, NEG_BIG).astype(F32).reshape(2, TILE_K, tq)
    hp = ATTN_HEADS_PER_STEP
    assert H % hp == 0
    oT = pl.pallas_call(
        _attn_kernel,
        grid=(B, H // hp, S // tq),
        in_specs=[
            pl.BlockSpec((1, hp * HEAD_PAD, tq), lambda b, h, q: (b, h, q)),
            pl.BlockSpec((1, S, hp * HEAD_PAD), lambda b, h, q: (b, 0, h)),
            pl.BlockSpec((1, hp * V_ROWS, S), lambda b, h, q: (b, h, 0)),
            _full(mask_bias.shape),
        ],
        out_specs=pl.BlockSpec((1, hp * V_HEAD_DIM, tq), lambda b, h, q: (b, h, q)),
        out_shape=jax.ShapeDtypeStruct((B, H * V_HEAD_DIM, S), BF16),
        scratch_shapes=[pltpu.VMEM((TILE_K, tq), F32)] * (2 * hp),
        compiler_params=_params(("parallel", "parallel", "arbitrary")),
    )(qT, k, vT, mask_bias)

    tm = TILE_MIX
    nsm = S // tm
    triu = (jnp.arange(tm)[:, None] < jnp.arange(tm)[None, :]).astype(BF16)
    h_tok, route, routeT, counts = pl.pallas_call(
        _mix_kernel,
        grid=(B, nsm),
        in_specs=[
            pl.BlockSpec((tm, D), lambda b, s: (b * nsm + s, 0)),
            pl.BlockSpec((1, H * V_HEAD_DIM, tm), lambda b, s: (b, 0, s)),
            _full(w_pool.shape), _full(w_gates.shape), _full(pool_mix_w.shape[1:]), _full((1, D)),
            _full(w_mla_o.shape[1:]), _full(w_out.shape[1:]), _full((1, D)), _full((1, D)),
            _full(w_rT.shape), _full(b_rT.shape), _full(triu.shape),
        ],
        out_specs=[
            pl.BlockSpec((tm * ROW_TILES, LANES), lambda b, s: (b * nsm + s, 0)),
            pl.BlockSpec((tm, LANES), lambda b, s: (b * nsm + s, 0)),
            pl.BlockSpec((SUBLANES, tm), lambda b, s: (0, b * nsm + s)),
            _full((N_EXPERTS, LANES)),
        ],
        out_shape=[
            jax.ShapeDtypeStruct((N * ROW_TILES, LANES), F32),
            jax.ShapeDtypeStruct((N, LANES), F32),
            jax.ShapeDtypeStruct((SUBLANES, N), F32),
            jax.ShapeDtypeStruct((N_EXPERTS, LANES), F32),
        ],
        scratch_shapes=[pltpu.VMEM((tm + POOL_HALO, POOL_WIDTH), F32), pltpu.VMEM((N_EXPERTS, LANES), F32)],
        compiler_params=_params(("arbitrary", "arbitrary")),
    )(x2, oT, w_pool, w_gates, pool_mix_w[0].astype(BF16), pool_scale[0][None], w_mla_o[0].astype(BF16),
      w_out[0].astype(BF16), ln1_g[0][None], ln1_b[0][None], w_rT, b_rT, triu)

    tb = TILE_EXPERT
    A = N * TOP_K
    n_blocks = -(-(A + N_EXPERTS * (tb - 1)) // tb) + GATHER_AHEAD
    R = n_blocks * tb
    e_idx = routeT[0:2].astype(jnp.int32)
    rank = routeT[4:6].astype(jnp.int32)
    cnt = counts[:, 0].astype(jnp.int32)
    padded = ((cnt + tb - 1) // tb) * tb
    pad_end = jnp.cumsum(padded)
    pad_start = pad_end - padded
    is_e = e_idx[:, None, :] == jnp.arange(N_EXPERTS, dtype=jnp.int32)[None, :, None]
    dest = jnp.sum(jnp.where(is_e, pad_start[None, :, None], 0), axis=1) + rank
    block_start = jnp.arange(n_blocks, dtype=jnp.int32) * tb
    block_e = jnp.minimum(jnp.sum(pad_end[None, :] <= block_start[:, None], axis=1), N_EXPERTS - 1).astype(jnp.int32)
    n_used = (pad_end[-1] // tb).astype(jnp.int32)[None]
    tok = jnp.broadcast_to(jnp.arange(N, dtype=jnp.int32)[None, :, None], (TOP_K, N, LANES)).reshape(A, LANES)
    tok_rows = _scatter_rows_sparsecore(tok, dest.reshape(A), R)[:, 0].reshape(n_blocks, tb)
    block_is_e = block_e[:, None] == jnp.arange(N_EXPERTS, dtype=jnp.int32)[None, :]
    block_pad_start = jnp.sum(jnp.where(block_is_e, pad_start[None, :], 0), axis=1)
    block_cnt = jnp.sum(jnp.where(block_is_e, cnt[None, :], 0), axis=1)
    row_in_expert = block_start[:, None] + jnp.arange(tb, dtype=jnp.int32)[None, :] - block_pad_start[:, None]
    row_valid = jnp.logical_and(row_in_expert < block_cnt[:, None],
                                jnp.arange(n_blocks, dtype=jnp.int32)[:, None] < n_used[0])
    row_tok3 = jnp.where(row_valid, tok_rows, 0).reshape(n_blocks, 1, tb)

    y_rows = pl.pallas_call(
        _expert_kernel,
        grid_spec=pltpu.PrefetchScalarGridSpec(
            num_scalar_prefetch=2,
            grid=(n_blocks,),
            in_specs=[
                pl.BlockSpec((1, 1, tb), lambda i, be, nu: (i, 0, 0), memory_space=pltpu.SMEM),
                pl.BlockSpec((1, 1, tb), lambda i, be, nu: (jnp.minimum(i + 1, n_blocks - 1), 0, 0),
                             memory_space=pltpu.SMEM),
                pl.BlockSpec((1, 1, tb), lambda i, be, nu: (jnp.minimum(i + GATHER_AHEAD, n_blocks - 1), 0, 0),
                             memory_space=pltpu.SMEM),
                pl.BlockSpec(memory_space=pl.ANY),
                pl.BlockSpec((1, D, D_EXPERT), lambda i, be, nu: (be[i], 0, 0)),
                pl.BlockSpec((1, D, D_EXPERT), lambda i, be, nu: (be[i], 0, 0)),
                pl.BlockSpec((1, D_EXPERT, D), lambda i, be, nu: (be[i], 0, 0)),
            ],
            out_specs=pl.BlockSpec((tb * ROW_TILES, LANES), lambda i, be, nu: (i, 0)),
            scratch_shapes=[
                pltpu.VMEM((GATHER_SLOTS, ROW_TILES, tb, LANES), F32), pltpu.VMEM((tb, D), BF16),
                pltpu.VMEM((D, D_EXPERT), BF16), pltpu.VMEM((D, D_EXPERT), BF16), pltpu.VMEM((D_EXPERT, D), BF16),
                pltpu.SemaphoreType.DMA((GATHER_SLOTS,)),
            ],
        ),
        out_shape=jax.ShapeDtypeStruct((R * ROW_TILES, LANES), F32),
        compiler_params=_params(("arbitrary",)),
    )(block_e, n_used, row_tok3, row_tok3, row_tok3, h_tok, w_gate[0], w_up[0], w_down[0])

    tf = TILE_COMBINE
    nf = N // tf
    dest3 = dest.reshape(TOP_K, nf, tf).transpose(1, 0, 2).reshape(nf, 1, TOP_K * tf)
    out = pl.pallas_call(
        _combine_kernel,
        grid=(nf,),
        in_specs=[
            pl.BlockSpec((1, 1, TOP_K * tf), lambda i: (i, 0, 0), memory_space=pltpu.SMEM),
            pl.BlockSpec((1, 1, TOP_K * tf), lambda i: (jnp.minimum(i + 1, nf - 1), 0, 0), memory_space=pltpu.SMEM),
            pl.BlockSpec((1, 1, TOP_K * tf), lambda i: (jnp.minimum(i + GATHER_AHEAD, nf - 1), 0, 0),
                         memory_space=pltpu.SMEM),
            pl.BlockSpec((tf * ROW_TILES, LANES), lambda i: (i, 0)),
            pl.BlockSpec((tf, LANES), lambda i: (i, 0)),
            pl.BlockSpec(memory_space=pl.ANY),
            _full((1, D)), _full((1, D)),
        ],
        out_specs=pl.BlockSpec((tf, D), lambda i: (i, 0)),
        out_shape=jax.ShapeDtypeStruct((N, D), F32),
        scratch_shapes=[pltpu.VMEM((GATHER_SLOTS, ROW_TILES, TOP_K * tf, LANES), F32),
                        pltpu.SemaphoreType.DMA((GATHER_SLOTS,))],
        compiler_params=_params(("arbitrary",)),
    )(dest3, dest3, dest3, h_tok, route, y_rows, ln2_g[0][None], ln2_b[0][None])
    return out.reshape(B, S, D)
```

```python
import functools
import math

import jax
import jax.numpy as jnp
import numpy as np
from jax import lax
from jax.experimental import pallas as pl
from jax.experimental.pallas import tpu as pltpu
from jax.experimental.pallas import tpu_sc as plsc

D_MODEL = 1024
CHUNK = 64
POOL_WINDOWS = (2, 4, 8, 16)
POOL_GROUPS = len(POOL_WINDOWS)
POOL_WIDTH = D_MODEL // 2
POOL_GROUP_DIM = POOL_WIDTH // POOL_GROUPS
POOL_OUT_GROUP_DIM = D_MODEL // POOL_GROUPS
POOL_HALO = 16
N_HEADS = 8
QK_NOPE_DIM = D_MODEL // 16
QK_ROPE_DIM = D_MODEL // 32
HALF_ROPE = QK_ROPE_DIM // 2
V_HEAD_DIM = D_MODEL // 16
V_ROWS = V_HEAD_DIM + 16
Q_LORA_RANK = 3 * D_MODEL // 8
KV_LORA_RANK = D_MODEL // 4
ROPE_THETA = 10000.0
N_GROUPS = 4
EXPERTS_PER_GROUP = 8
N_EXPERTS = N_GROUPS * EXPERTS_PER_GROUP
TOP_K = 2
D_EXPERT = D_MODEL // 2
NORM_EPS = 1e-5
DEPTH = 1
DEEPNORM_ALPHA = (2.0 * DEPTH) ** 0.25

LANES = 128
SUBLANES = 8
HEAD_PAD = LANES
ROW_TILES = D_MODEL // LANES
GROUP_ROW0 = 64
NEG_BIG = -1e30
VMEM_LIMIT = 56 * 1024 * 1024

TILE_LATENT = 1024
TILE_Q = 512
TILE_K = 256
ATTN_HEADS_PER_STEP = 4
ATTN_PAIRS_PER_TRIP = 2
TILE_MIX = 512
GATHER_AHEAD = 2
GATHER_SLOTS = GATHER_AHEAD + 1
TILE_EXPERT = 256
SC_SCATTER_WINDOW = 128
TILE_COMBINE = 512

F32 = jnp.float32
BF16 = jnp.bfloat16


def _dot(a, b):
    return jnp.dot(a, b, preferred_element_type=F32)


def _dot_nt(a, b):
    return lax.dot_general(a, b, (((1,), (1,)), ((), ())), preferred_element_type=F32)


def _dot_tn(a, b):
    return lax.dot_general(a, b, (((0,), (0,)), ((), ())), preferred_element_type=F32)


def _rms(v, g):
    ms = jnp.mean(jnp.square(v), axis=-1, keepdims=True)
    return v * lax.rsqrt(ms + NORM_EPS) * g


def _layer_norm(v, g, b):
    mu = jnp.mean(v, axis=-1, keepdims=True)
    c = v - mu
    var = jnp.mean(jnp.square(c), axis=-1, keepdims=True)
    return c * lax.rsqrt(var + NORM_EPS) * g + b


def _latent_kernel(x_ref, wlat_ref, gq_ref, gkv_ref, wuqT_ref, wuk_ref, wuvT_ref,
                   cosT_ref, sinT_ref, ra_ref, rm_ref, rp_ref,
                   qT_ref, k_ref, vT_ref, *, q_scale):
    xb = x_ref[...].astype(BF16)
    lat = _dot(xb, wlat_ref[...])
    c_q = lat[:, :Q_LORA_RANK]
    c_kv = lat[:, Q_LORA_RANK:Q_LORA_RANK + KV_LORA_RANK]
    kpe = lat[:, Q_LORA_RANK + KV_LORA_RANK:]
    qn = _rms(c_q, gq_ref[...]).astype(BF16)
    kvn = _rms(c_kv, gkv_ref[...]).astype(BF16)

    qT = _dot_nt(wuqT_ref[...], qn) * q_scale
    cosT = cosT_ref[...]
    sinT = sinT_ref[...]
    for h in range(N_HEADS):
        r0 = h * HEAD_PAD
        x1 = qT[r0 + QK_NOPE_DIM:r0 + QK_NOPE_DIM + HALF_ROPE]
        x2 = qT[r0 + QK_NOPE_DIM + HALF_ROPE:r0 + QK_NOPE_DIM + QK_ROPE_DIM]
        qT_ref[0, r0:r0 + QK_NOPE_DIM, :] = qT[r0:r0 + QK_NOPE_DIM].astype(BF16)
        qT_ref[0, r0 + QK_NOPE_DIM:r0 + QK_NOPE_DIM + HALF_ROPE, :] = (x1 * cosT - x2 * sinT).astype(BF16)
        qT_ref[0, r0 + QK_NOPE_DIM + HALF_ROPE:r0 + QK_NOPE_DIM + QK_ROPE_DIM, :] = (
            x1 * sinT + x2 * cosT).astype(BF16)
        qT_ref[0, r0 + QK_NOPE_DIM + QK_ROPE_DIM:r0 + HEAD_PAD, :] = jnp.zeros(
            (HEAD_PAD - QK_NOPE_DIM - QK_ROPE_DIM, qT.shape[1]), BF16)

    kpe_rot = (kpe * ra_ref[...] + pltpu.roll(kpe, LANES - HALF_ROPE, 1) * rm_ref[...]
               + pltpu.roll(kpe, HALF_ROPE, 1) * rp_ref[...])
    k = _dot(kvn, wuk_ref[...])
    for h in range(N_HEADS):
        k_ref[0, :, h * HEAD_PAD:(h + 1) * HEAD_PAD] = (k[:, h * HEAD_PAD:(h + 1) * HEAD_PAD] + kpe_rot).astype(BF16)
    vT = _dot_nt(wuvT_ref[...], kvn)
    row = lax.broadcasted_iota(jnp.int32, vT.shape, 0) % V_ROWS
    vT_ref[0] = jnp.where(row == V_HEAD_DIM, 1.0, vT).astype(BF16)


def _attn_kernel(qT_ref, k_ref, vT_ref, bias_ref, oT_ref, *bufs):
    qi = pl.program_id(2)
    tq = qT_ref.shape[2]
    heads = range(ATTN_HEADS_PER_STEP)
    s0, s1 = (bufs[i * ATTN_HEADS_PER_STEP:(i + 1) * ATTN_HEADS_PER_STEP] for i in range(2))

    def scores(h, j, c0=0):
        k0 = pl.multiple_of(j * TILE_K, TILE_K)
        return _dot(k_ref[0, pl.ds(k0, TILE_K), h * HEAD_PAD:(h + 1) * HEAD_PAD],
                    qT_ref[0, h * HEAD_PAD:(h + 1) * HEAD_PAD, c0:])

    def col_max(s):
        while s.shape[0] > SUBLANES:
            half = s.shape[0] // 2
            s = jnp.maximum(s[:half], s[half:])
        return jnp.max(s, axis=0, keepdims=True)

    def step(h, j, s_cur, s_nxt, carry, bias=None, c0=0, c0_nxt=0):
        m_all, acc_all, cmax = carry
        cmax_nxt = None
        if s_nxt is not None:
            s_new = scores(h, j + 1, c0_nxt)
            s_nxt[h][:, c0_nxt:] = s_new
            cmax_nxt = col_max(s_new)
        m, acc = m_all[:, c0:], acc_all[:, c0:]
        s = s_cur[h][:, c0:]
        if bias is not None:
            s = s + bias[:, c0:]
            cmax = col_max(s)
        m_new = jnp.maximum(m, cmax)
        p = jnp.exp2(s - m_new).astype(BF16)
        k0 = pl.multiple_of(j * TILE_K, TILE_K)
        pv = _dot(vT_ref[0, h * V_ROWS:(h + 1) * V_ROWS, pl.ds(k0, TILE_K)], p)
        acc = jnp.exp2(m - m_new) * acc + pv
        if c0:
            m_new = jnp.concatenate([m_all[:, :c0], m_new], axis=1)
            acc = jnp.concatenate([acc_all[:, :c0], acc], axis=1)
        return m_new, acc, cmax_nxt

    carry = []
    for h in heads:
        s_first = scores(h, 0)
        s0[h][...] = s_first
        carry.append((jnp.full((1, tq), NEG_BIG, F32), jnp.zeros((V_ROWS, tq), F32), col_max(s_first)))
    carry = tuple(carry)

    def pair(t, c):
        c = tuple(step(h, 2 * t, s0, s1, c[h]) for h in heads)
        return tuple(step(h, 2 * t + 1, s1, s0, c[h]) for h in heads)

    def trip(u, c):
        for i in range(ATTN_PAIRS_PER_TRIP):
            c = pair(ATTN_PAIRS_PER_TRIP * u + i, c)
        return c

    carry = lax.fori_loop(0, qi // ATTN_PAIRS_PER_TRIP, trip, carry)
    done = (qi // ATTN_PAIRS_PER_TRIP) * ATTN_PAIRS_PER_TRIP
    for i in range(ATTN_PAIRS_PER_TRIP - 1):
        carry = lax.cond(done + i < qi, functools.partial(pair, done + i), lambda c: c, carry)
    jd = 2 * qi
    carry = tuple(step(h, jd, s0, s1, carry[h], bias=bias_ref.at[0], c0_nxt=TILE_K) for h in heads)
    carry = tuple(step(h, jd + 1, s1, None, carry[h], bias=bias_ref.at[1], c0=TILE_K) for h in heads)
    for h in heads:
        acc = carry[h][1]
        oT_ref[0, h * V_HEAD_DIM:(h + 1) * V_HEAD_DIM, :] = (
            acc[:V_HEAD_DIM] / acc[V_HEAD_DIM:V_HEAD_DIM + 1]).astype(BF16)


def _mix_kernel(x_ref, oT_ref, wpool_ref, wgate_ref, mixw_ref, pscale_ref, wo_ref, wout_ref,
                g1_ref, b1_ref, wrT_ref, brT_ref, triu_ref,
                h_ref, route_ref, routeT_ref, cnt_ref, ext_ref, base_ref):
    b = pl.program_id(0)
    si = pl.program_id(1)
    tm = x_ref.shape[0]

    @pl.when(jnp.logical_and(b == 0, si == 0))
    def _():
        base_ref[...] = jnp.zeros_like(base_ref)

    @pl.when(si == 0)
    def _():
        ext_ref[0:POOL_HALO, :] = jnp.zeros((POOL_HALO, POOL_WIDTH), F32)

    x = x_ref[...]
    xb = x.astype(BF16)
    u = _dot(xb, wpool_ref[...])
    ext_ref[POOL_HALO:POOL_HALO + tm, :] = u

    pos1 = si * tm + lax.broadcasted_iota(jnp.int32, (tm, POOL_GROUP_DIM), 0) + 1
    y_parts = []
    for g, win in enumerate(POOL_WINDOWS):
        c0 = g * POOL_GROUP_DIM
        ws = u[:, c0:c0 + POOL_GROUP_DIM]
        for kk in range(1, win):
            ws = ws + ext_ref[POOL_HALO - kk:POOL_HALO - kk + tm, c0:c0 + POOL_GROUP_DIM]
        count = jnp.minimum(pos1, win).astype(F32)
        pooled = ws / count - u[:, c0:c0 + POOL_GROUP_DIM]
        y_parts.append(_dot(pooled.astype(BF16), mixw_ref[g]))
    ext_ref[0:POOL_HALO, :] = ext_ref[tm:tm + POOL_HALO, :]
    y_pool = jnp.concatenate(y_parts, axis=-1) * pscale_ref[...]

    glog = _dot(xb, wgate_ref[...])
    y_mla = _dot_tn(oT_ref[0], wo_ref[...])
    merged = (jax.nn.sigmoid(glog[:, :D_MODEL]) * y_pool + jax.nn.sigmoid(glog[:, D_MODEL:]) * y_mla)
    r = DEEPNORM_ALPHA * x + _dot(merged.astype(BF16), wout_ref[...])
    h = _layer_norm(r, g1_ref[...], b1_ref[...])
    for j in range(ROW_TILES):
        h_ref[pl.ds(j, tm, stride=ROW_TILES), :] = h[:, j * LANES:(j + 1) * LANES]

    logits = _dot_nt(wrT_ref[...], h.astype(BF16)) + brT_ref[...]
    big = float(LANES)
    gl = logits[GROUP_ROW0:GROUP_ROW0 + SUBLANES]
    grow = lax.broadcasted_iota(jnp.int32, gl.shape, 0)
    gl = jnp.where(grow < N_GROUPS, gl, NEG_BIG)
    gmax = jnp.max(gl, axis=0, keepdims=True)
    g_w = 1.0 / jnp.sum(jnp.exp(gl - gmax), axis=0, keepdims=True)
    g_idx = jnp.min(jnp.where(gl == gmax, grow.astype(F32), big), axis=0, keepdims=True)
    el = logits[0:N_EXPERTS]
    erow_i = lax.broadcasted_iota(jnp.int32, el.shape, 0)
    erow = erow_i.astype(F32)
    el = jnp.where((erow_i // EXPERTS_PER_GROUP).astype(F32) == g_idx, el, NEG_BIG)
    e1max = jnp.max(el, axis=0, keepdims=True)
    i1 = jnp.min(jnp.where(el == e1max, erow, big), axis=0, keepdims=True)
    el2 = jnp.where(erow == i1, NEG_BIG, el)
    e2max = jnp.max(el2, axis=0, keepdims=True)
    i2 = jnp.min(jnp.where(el2 == e2max, erow, big), axis=0, keepdims=True)
    ratio = jnp.exp(e2max - e1max)
    gate1 = g_w / (1.0 + ratio)
    gate2 = g_w * ratio / (1.0 + ratio)

    hit1 = erow == i1
    hit2 = erow == i2
    onehot = jnp.where(jnp.logical_or(hit1, hit2), 1.0, 0.0)
    base = base_ref[:, 0:1]
    before = _dot(onehot.astype(BF16), triu_ref[...]) + base
    rank1 = jnp.sum(jnp.where(hit1, before, 0.0), axis=0, keepdims=True)
    rank2 = jnp.sum(jnp.where(hit2, before, 0.0), axis=0, keepdims=True)
    new_base = base + jnp.sum(onehot, axis=1, keepdims=True)
    base_ref[...] = jnp.broadcast_to(new_base, base_ref.shape)
    cnt_ref[...] = jnp.broadcast_to(new_base, cnt_ref.shape)

    row = lax.broadcasted_iota(jnp.int32, (LANES, tm), 0)
    routeT = jnp.where(row == 0, i1, 0.0)
    routeT = jnp.where(row == 1, i2, routeT)
    routeT = jnp.where(row == 2, gate1, routeT)
    routeT = jnp.where(row == 3, gate2, routeT)
    routeT = jnp.where(row == 4, rank1, routeT)
    routeT = jnp.where(row == 5, rank2, routeT)
    routeT_ref[...] = routeT[:SUBLANES]
    route_ref[...] = routeT.T


def _gather_row(idx_ref, r, src_hbm, dst_ref, slot, sem, priority=0):
    src0 = pl.multiple_of(idx_ref[0, 0, r] * ROW_TILES, ROW_TILES)
    pltpu.make_async_copy(src_hbm.at[pl.ds(src0, ROW_TILES), :],
                          dst_ref.at[slot, :, r, :],
                          sem.at[slot]).start(priority=priority)


def _gather_rows_loop(idx_ref, src_hbm, dst_ref, slot, sem, n_rows, unroll=8):
    def body(c, _):
        for u in range(unroll):
            _gather_row(idx_ref, c * unroll + u, src_hbm, dst_ref, slot, sem)
        return 0
    lax.fori_loop(0, n_rows // unroll, body, 0)


def _gather_rows_inline(idx_ref, src_hbm, dst_ref, slot, sem, n_rows):
    for r in range(n_rows):
        _gather_row(idx_ref, r, src_hbm, dst_ref, slot, sem, priority=r % 2)


def _wait_rows(dst_ref, slot, sem):
    pltpu.make_async_copy(dst_ref.at[slot], dst_ref.at[slot], sem.at[slot]).wait()


def _expert_kernel(be_ref, nused_ref, tok_ref, tok1_ref, tok2_ref, h_hbm, wg_ref, wu_ref, wd_ref,
                   y_ref, xbuf, xs_ref, wgb, wub, wdb, sem):
    i = pl.program_id(0)
    n_used = nused_ref[0]
    slot = lax.rem(i, GATHER_SLOTS)
    tb = TILE_EXPERT

    @pl.when(i == 0)
    def _():
        _gather_rows_loop(tok_ref, h_hbm, xbuf, 0, sem, tb)
        _gather_rows_loop(tok1_ref, h_hbm, xbuf, 1, sem, tb)

    changed = jnp.logical_or(i == 0, be_ref[i] != be_ref[jnp.maximum(i - 1, 0)])

    @pl.when(jnp.logical_and(changed, i < n_used))
    def _():
        wgb[...] = wg_ref[0].astype(BF16)
        wub[...] = wu_ref[0].astype(BF16)
        wdb[...] = wd_ref[0].astype(BF16)

    @pl.when(jnp.logical_and(i >= n_used, i < n_used + GATHER_AHEAD))
    def _():
        _wait_rows(xbuf, slot, sem)

    @pl.when(i < n_used)
    def _():
        _wait_rows(xbuf, slot, sem)
        for j in range(ROW_TILES):
            xs_ref[:, j * LANES:(j + 1) * LANES] = xbuf[slot, j].astype(BF16)
        _gather_rows_inline(tok2_ref, h_hbm, xbuf, lax.rem(i + GATHER_AHEAD, GATHER_SLOTS), sem, tb)
        xs = xs_ref[...]
        gate = _dot(xs, wgb[...])
        up = _dot(xs, wub[...])
        hid = (jax.nn.silu(gate) * up).astype(BF16)
        y = _dot(hid, wdb[...])
        for j in range(ROW_TILES):
            y_ref[pl.ds(j, tb, stride=ROW_TILES), :] = y[:, j * LANES:(j + 1) * LANES]

    @pl.when(i >= n_used)
    def _():
        y_ref[...] = jnp.zeros_like(y_ref)


def _combine_kernel(dst_ref, dst1_ref, dst2_ref, h_ref, route_ref, y_hbm, g2_ref, b2_ref, o_ref, gbuf, sem):
    i = pl.program_id(0)
    n = pl.num_programs(0)
    slot = lax.rem(i, GATHER_SLOTS)
    tf = TILE_COMBINE

    @pl.when(i == 0)
    def _():
        _gather_rows_loop(dst_ref, y_hbm, gbuf, 0, sem, TOP_K * tf)
        _gather_rows_loop(dst1_ref, y_hbm, gbuf, 1, sem, TOP_K * tf)

    _wait_rows(gbuf, slot, sem)
    route = route_ref[...]
    gate1 = route[:, 2:3]
    gate2 = route[:, 3:4]
    parts = []
    for j in range(ROW_TILES):
        hj = h_ref[pl.ds(j, tf, stride=ROW_TILES), :]
        y1 = gbuf[slot, j, 0:tf, :]
        y2 = gbuf[slot, j, tf:TOP_K * tf, :]
        parts.append(DEEPNORM_ALPHA * hj + (gate1 * y1 + gate2 * y2))
    z = jnp.concatenate(parts, axis=-1)
    o_ref[...] = _layer_norm(z, g2_ref[...], b2_ref[...])
    _gather_rows_inline(dst2_ref, y_hbm, gbuf, lax.rem(i + GATHER_AHEAD, GATHER_SLOTS), sem, TOP_K * tf)

    @pl.when(i == n - 1)
    def _():
        for ahead in range(1, GATHER_AHEAD + 1):
            _wait_rows(gbuf, lax.rem(i + ahead, GATHER_SLOTS), sem)


def _rope_tables(seq):
    f32 = np.float32
    inv = (f32(1.0) / (f32(ROPE_THETA) ** (np.arange(0, QK_ROPE_DIM, 2, dtype=f32) / f32(QK_ROPE_DIM)))).astype(f32)
    ang = (np.arange(seq, dtype=f32)[:, None] * inv[None, :]).astype(f32)
    cos, sin = np.cos(ang.astype(np.float64)).astype(f32), np.sin(ang.astype(np.float64)).astype(f32)
    zeros = np.zeros((seq, QK_NOPE_DIM), f32)
    pad = np.zeros((seq, HEAD_PAD - QK_NOPE_DIM - QK_ROPE_DIM), f32)
    z16 = np.zeros((seq, HALF_ROPE), f32)
    rot_a = np.concatenate([zeros, cos, cos, pad], axis=1)
    rot_m = np.concatenate([zeros, -sin, z16, pad], axis=1)
    rot_p = np.concatenate([zeros, z16, sin, pad], axis=1)
    return tuple(jnp.asarray(t) for t in (np.ascontiguousarray(cos.T), np.ascontiguousarray(sin.T),
                                          rot_a, rot_m, rot_p))


def _scatter_rows_sparsecore(x, indices, n_out):
    n, width = x.shape
    mesh = plsc.VectorSubcoreMesh(core_axis_name="core", subcore_axis_name="subcore")
    per_core = n // SC_SCATTER_WINDOW // mesh.num_cores
    assert per_core * SC_SCATTER_WINDOW * mesh.num_cores == n

    @functools.partial(pl.kernel, out_type=jax.ShapeDtypeStruct((n_out, width), x.dtype), mesh=mesh,
                       scratch_types=[])
    def scatter(x_hbm, i_hbm, o_hbm):
        first = lax.axis_index("core") * per_core

        def body(x_vmem, i_vmem):
            pltpu.sync_copy(x_vmem, o_hbm.at[i_vmem.at[0]])

        pltpu.emit_pipeline(
            body,
            grid=(per_core,),
            in_specs=[pl.BlockSpec((SC_SCATTER_WINDOW, width), index_map=lambda i: (first + i, 0)),
                      pl.BlockSpec((1, SC_SCATTER_WINDOW), index_map=lambda i: (0, first + i))],
            out_specs=[],
            core_axis_name="subcore",
            dimension_semantics=(pltpu.PARALLEL,),
        )(x_hbm, i_hbm)

    return scatter(x, indices.reshape(1, n))


def _full(shape):
    return pl.BlockSpec(shape, lambda *_: (0,) * len(shape))


def _params(sem):
    return pltpu.CompilerParams(dimension_semantics=sem, vmem_limit_bytes=VMEM_LIMIT)


def kernel(x, w_in, pool_mix_w, pool_scale, q_norm_g, w_uq, kv_norm_g, w_ukv, w_mla_o, w_out, ln1_g, ln1_b,
           w_router_group, b_router_group, w_router_expert, b_router_expert, w_gate, w_up, w_down, ln2_g, ln2_b):
    B, S, D = x.shape
    assert D == D_MODEL and w_in.shape[0] == DEPTH == 1
    assert S % TILE_LATENT == 0 and S % TILE_Q == 0 and S % TILE_MIX == 0 and TILE_Q % TILE_K == 0
    N = B * S
    assert N % TILE_COMBINE == 0
    H = N_HEADS

    w = w_in[0]
    o1 = POOL_WIDTH
    o2 = o1 + Q_LORA_RANK
    o3 = o2 + KV_LORA_RANK
    o4 = o3 + QK_ROPE_DIM
    w_pool = w[:, :o1].astype(BF16)
    kpe_cols = jnp.pad(w[:, o3:o4], ((0, 0), (QK_NOPE_DIM, HEAD_PAD - QK_NOPE_DIM - QK_ROPE_DIM)))
    w_lat = jnp.concatenate([w[:, o1:o3], kpe_cols], axis=1).astype(BF16)
    w_gates = w[:, o4:].astype(BF16)
    qd = QK_NOPE_DIM + QK_ROPE_DIM
    wuq = jnp.pad(w_uq[0].reshape(Q_LORA_RANK, H, qd), ((0, 0), (0, 0), (0, HEAD_PAD - qd)))
    wuqT = wuq.reshape(Q_LORA_RANK, H * HEAD_PAD).T.astype(BF16)
    wukv = w_ukv[0].reshape(KV_LORA_RANK, H, QK_NOPE_DIM + V_HEAD_DIM)
    wuk = jnp.pad(wukv[:, :, :QK_NOPE_DIM], ((0, 0), (0, 0), (0, HEAD_PAD - QK_NOPE_DIM)))
    wuk = wuk.reshape(KV_LORA_RANK, H * HEAD_PAD).astype(BF16)
    wuv = jnp.pad(wukv[:, :, QK_NOPE_DIM:], ((0, 0), (0, 0), (0, V_ROWS - V_HEAD_DIM)))
    wuvT = wuv.reshape(KV_LORA_RANK, H * V_ROWS).T.astype(BF16)
    w_rT = jnp.zeros((LANES, D), F32)
    w_rT = w_rT.at[:N_EXPERTS].set(w_router_expert[0].T).at[GROUP_ROW0:GROUP_ROW0 + N_GROUPS].set(
        w_router_group[0].T).astype(BF16)
    b_rT = jnp.zeros((LANES, 1), F32)
    b_rT = b_rT.at[:N_EXPERTS, 0].set(b_router_expert[0]).at[GROUP_ROW0:GROUP_ROW0 + N_GROUPS, 0].set(
        b_router_group[0])
    cosT, sinT, rot_a, rot_m, rot_p = _rope_tables(S)
    q_scale = (QK_NOPE_DIM + QK_ROPE_DIM) ** -0.5 * math.log2(math.e)
    x2 = x.reshape(N, D)

    ta = TILE_LATENT
    nsa = S // ta
    qT, k, vT = pl.pallas_call(
        functools.partial(_latent_kernel, q_scale=q_scale),
        grid=(B, nsa),
        in_specs=[
            pl.BlockSpec((ta, D), lambda b, s: (b * nsa + s, 0)),
            _full(w_lat.shape), _full((1, Q_LORA_RANK)), _full((1, KV_LORA_RANK)),
            _full(wuqT.shape), _full(wuk.shape), _full(wuvT.shape),
            pl.BlockSpec((HALF_ROPE, ta), lambda b, s: (0, s)),
            pl.BlockSpec((HALF_ROPE, ta), lambda b, s: (0, s)),
            pl.BlockSpec((ta, LANES), lambda b, s: (s, 0)),
            pl.BlockSpec((ta, LANES), lambda b, s: (s, 0)),
            pl.BlockSpec((ta, LANES), lambda b, s: (s, 0)),
        ],
        out_specs=[
            pl.BlockSpec((1, H * HEAD_PAD, ta), lambda b, s: (b, 0, s)),
            pl.BlockSpec((1, ta, H * HEAD_PAD), lambda b, s: (b, s, 0)),
            pl.BlockSpec((1, H * V_ROWS, ta), lambda b, s: (b, 0, s)),
        ],
        out_shape=[
            jax.ShapeDtypeStruct((B, H * HEAD_PAD, S), BF16),
            jax.ShapeDtypeStruct((B, S, H * HEAD_PAD), BF16),
            jax.ShapeDtypeStruct((B, H * V_ROWS, S), BF16),
        ],
        compiler_params=_params(("parallel", "parallel")),
    )(x2, w_lat, q_norm_g[0][None], kv_norm_g[0][None], wuqT, wuk, wuvT, cosT, sinT, rot_a, rot_m, rot_p)

    tq = TILE_Q
    assert tq == 2 * TILE_K
    key_chunk = jnp.arange(tq)[:, None] // CHUNK
    q_chunk = jnp.arange(tq)[None, :] // CHUNK
    mask_bias = jnp.where(key_chunk <= q_chunk, 0.0, NEG_BIG).astype(F32).reshape(2, TILE_K, tq)
    hp = ATTN_HEADS_PER_STEP
    assert H % hp == 0
    oT = pl.pallas_call(
        _attn_kernel,
        grid=(B, H // hp, S // tq),
        in_specs=[
            pl.BlockSpec((1, hp * HEAD_PAD, tq), lambda b, h, q: (b, h, q)),
            pl.BlockSpec((1, S, hp * HEAD_PAD), lambda b, h, q: (b, 0, h)),
            pl.BlockSpec((1, hp * V_ROWS, S), lambda b, h, q: (b, h, 0)),
            _full(mask_bias.shape),
        ],
        out_specs=pl.BlockSpec((1, hp * V_HEAD_DIM, tq), lambda b, h, q: (b, h, q)),
        out_shape=jax.ShapeDtypeStruct((B, H * V_HEAD_DIM, S), BF16),
        scratch_shapes=[pltpu.VMEM((TILE_K, tq), F32)] * (2 * hp),
        compiler_params=_params(("parallel", "parallel", "arbitrary")),
    )(qT, k, vT, mask_bias)

    tm = TILE_MIX
    nsm = S // tm
    triu = (jnp.arange(tm)[:, None] < jnp.arange(tm)[None, :]).astype(BF16)
    h_tok, route, routeT, counts = pl.pallas_call(
        _mix_kernel,
        grid=(B, nsm),
        in_specs=[
            pl.BlockSpec((tm, D), lambda b, s: (b * nsm + s, 0)),
            pl.BlockSpec((1, H * V_HEAD_DIM, tm), lambda b, s: (b, 0, s)),
            _full(w_pool.shape), _full(w_gates.shape), _full(pool_mix_w.shape[1:]), _full((1, D)),
            _full(w_mla_o.shape[1:]), _full(w_out.shape[1:]), _full((1, D)), _full((1, D)),
            _full(w_rT.shape), _full(b_rT.shape), _full(triu.shape),
        ],
        out_specs=[
            pl.BlockSpec((tm * ROW_TILES, LANES), lambda b, s: (b * nsm + s, 0)),
            pl.BlockSpec((tm, LANES), lambda b, s: (b * nsm + s, 0)),
            pl.BlockSpec((SUBLANES, tm), lambda b, s: (0, b * nsm + s)),
            _full((N_EXPERTS, LANES)),
        ],
        out_shape=[
            jax.ShapeDtypeStruct((N * ROW_TILES, LANES), F32),
            jax.ShapeDtypeStruct((N, LANES), F32),
            jax.ShapeDtypeStruct((SUBLANES, N), F32),
            jax.ShapeDtypeStruct((N_EXPERTS, LANES), F32),
        ],
        scratch_shapes=[pltpu.VMEM((tm + POOL_HALO, POOL_WIDTH), F32), pltpu.VMEM((N_EXPERTS, LANES), F32)],
        compiler_params=_params(("arbitrary", "arbitrary")),
    )(x2, oT, w_pool, w_gates, pool_mix_w[0].astype(BF16), pool_scale[0][None], w_mla_o[0].astype(BF16),
      w_out[0].astype(BF16), ln1_g[0][None], ln1_b[0][None], w_rT, b_rT, triu)

    tb = TILE_EXPERT
    A = N * TOP_K
    n_blocks = -(-(A + N_EXPERTS * (tb - 1)) // tb) + GATHER_AHEAD
    R = n_blocks * tb
    e_idx = routeT[0:2].astype(jnp.int32)
    rank = routeT[4:6].astype(jnp.int32)
    cnt = counts[:, 0].astype(jnp.int32)
    padded = ((cnt + tb - 1) // tb) * tb
    pad_end = jnp.cumsum(padded)
    pad_start = pad_end - padded
    is_e = e_idx[:, None, :] == jnp.arange(N_EXPERTS, dtype=jnp.int32)[None, :, None]
    dest = jnp.sum(jnp.where(is_e, pad_start[None, :, None], 0), axis=1) + rank
    block_start = jnp.arange(n_blocks, dtype=jnp.int32) * tb
    block_e = jnp.minimum(jnp.sum(pad_end[None, :] <= block_start[:, None], axis=1), N_EXPERTS - 1).astype(jnp.int32)
    n_used = (pad_end[-1] // tb).astype(jnp.int32)[None]
    tok = jnp.broadcast_to(jnp.arange(N, dtype=jnp.int32)[None, :, None], (TOP_K, N, LANES)).reshape(A, LANES)
    tok_rows = _scatter_rows_sparsecore(tok, dest.reshape(A), R)[:, 0].reshape(n_blocks, tb)
    block_is_e = block_e[:, None] == jnp.arange(N_EXPERTS, dtype=jnp.int32)[None, :]
    block_pad_start = jnp.sum(jnp.where(block_is_e, pad_start[None, :], 0), axis=1)
    block_cnt = jnp.sum(jnp.where(block_is_e, cnt[None, :], 0), axis=1)
    row_in_expert = block_start[:, None] + jnp.arange(tb, dtype=jnp.int32)[None, :] - block_pad_start[:, None]
    row_valid = jnp.logical_and(row_in_expert < block_cnt[:, None],
                                jnp.arange(n_blocks, dtype=jnp.int32)[:, None] < n_used[0])
    row_tok3 = jnp.where(row_valid, tok_rows, 0).reshape(n_blocks, 1, tb)

    y_rows = pl.pallas_call(
        _expert_kernel,
        grid_spec=pltpu.PrefetchScalarGridSpec(
            num_scalar_prefetch=2,
            grid=(n_blocks,),
            in_specs=[
                pl.BlockSpec((1, 1, tb), lambda i, be, nu: (i, 0, 0), memory_space=pltpu.SMEM),
                pl.BlockSpec((1, 1, tb), lambda i, be, nu: (jnp.minimum(i + 1, n_blocks - 1), 0, 0),
                             memory_space=pltpu.SMEM),
                pl.BlockSpec((1, 1, tb), lambda i, be, nu: (jnp.minimum(i + GATHER_AHEAD, n_blocks - 1), 0, 0),
                             memory_space=pltpu.SMEM),
                pl.BlockSpec(memory_space=pl.ANY),
                pl.BlockSpec((1, D, D_EXPERT), lambda i, be, nu: (be[i], 0, 0)),
                pl.BlockSpec((1, D, D_EXPERT), lambda i, be, nu: (be[i], 0, 0)),
                pl.BlockSpec((1, D_EXPERT, D), lambda i, be, nu: (be[i], 0, 0)),
            ],
            out_specs=pl.BlockSpec((tb * ROW_TILES, LANES), lambda i, be, nu: (i, 0)),
            scratch_shapes=[
                pltpu.VMEM((GATHER_SLOTS, ROW_TILES, tb, LANES), F32), pltpu.VMEM((tb, D), BF16),
                pltpu.VMEM((D, D_EXPERT), BF16), pltpu.VMEM((D, D_EXPERT), BF16), pltpu.VMEM((D_EXPERT, D), BF16),
                pltpu.SemaphoreType.DMA((GATHER_SLOTS,)),
            ],
        ),
        out_shape=jax.ShapeDtypeStruct((R * ROW_TILES, LANES), F32),
        compiler_params=_params(("arbitrary",)),
    )(block_e, n_used, row_tok3, row_tok3, row_tok3, h_tok, w_gate[0], w_up[0], w_down[0])

    tf = TILE_COMBINE
    nf = N // tf
    dest3 = dest.reshape(TOP_K, nf, tf).transpose(1, 0, 2).reshape(nf, 1, TOP_K * tf)
    out = pl.pallas_call(
        _combine_kernel,
        grid=(nf,),
        in_specs=[
            pl.BlockSpec((1, 1, TOP_K * tf), lambda i: (i, 0, 0), memory_space=pltpu.SMEM),
            pl.BlockSpec((1, 1, TOP_K * tf), lambda i: (jnp.minimum(i + 1, nf - 1), 0, 0), memory_space=pltpu.SMEM),
            pl.BlockSpec((1, 1, TOP_K * tf), lambda i: (jnp.minimum(i + GATHER_AHEAD, nf - 1), 0, 0),
                         memory_space=pltpu.SMEM),
            pl.BlockSpec((tf * ROW_TILES, LANES), lambda i: (i, 0)),
            pl.BlockSpec((tf, LANES), lambda i: (i, 0)),
            pl.BlockSpec(memory_space=pl.ANY),
            _full((1, D)), _full((1, D)),
        ],
        out_specs=pl.BlockSpec((tf, D), lambda i: (i, 0)),
        out_shape=jax.ShapeDtypeStruct((N, D), F32),
        scratch_shapes=[pltpu.VMEM((GATHER_SLOTS, ROW_TILES, TOP_K * tf, LANES), F32),
                        pltpu.SemaphoreType.DMA((GATHER_SLOTS,))],
        compiler_params=_params(("arbitrary",)),
    )(dest3, dest3, dest3, h_tok, route, y_rows, ln2_g[0][None], ln2_b[0][None])
    return out.reshape(B, S, D)
```

```python
import functools
import math

import jax
import jax.numpy as jnp
import numpy as np
from jax import lax
from jax.experimental import pallas as pl
from jax.experimental.pallas import tpu as pltpu
from jax.experimental.pallas import tpu_sc as plsc

D_MODEL = 1024
CHUNK = 64
POOL_WINDOWS = (2, 4, 8, 16)
POOL_GROUPS = len(POOL_WINDOWS)
POOL_WIDTH = D_MODEL // 2
POOL_GROUP_DIM = POOL_WIDTH // POOL_GROUPS
POOL_OUT_GROUP_DIM = D_MODEL // POOL_GROUPS
POOL_HALO = 16
N_HEADS = 8
QK_NOPE_DIM = D_MODEL // 16
QK_ROPE_DIM = D_MODEL // 32
HALF_ROPE = QK_ROPE_DIM // 2
V_HEAD_DIM = D_MODEL // 16
V_ROWS = V_HEAD_DIM + 16
Q_LORA_RANK = 3 * D_MODEL // 8
KV_LORA_RANK = D_MODEL // 4
ROPE_THETA = 10000.0
N_GROUPS = 4
EXPERTS_PER_GROUP = 8
N_EXPERTS = N_GROUPS * EXPERTS_PER_GROUP
TOP_K = 2
D_EXPERT = D_MODEL // 2
NORM_EPS = 1e-5
DEPTH = 1
DEEPNORM_ALPHA = (2.0 * DEPTH) ** 0.25

LANES = 128
SUBLANES = 8
HEAD_PAD = LANES
ROW_TILES = D_MODEL // LANES
GROUP_ROW0 = 64
NEG_BIG = -1e30
VMEM_LIMIT = 56 * 1024 * 1024

TILE_LATENT = 1024
TILE_Q = 512
TILE_K = 256
ATTN_HEADS_PER_STEP = 4
ATTN_PAIRS_PER_TRIP = 2
TILE_MIX = 512
GATHER_AHEAD = 2
GATHER_SLOTS = GATHER_AHEAD + 1
TILE_EXPERT = 512
SC_SCATTER_WINDOW = 128
TILE_COMBINE = 512

F32 = jnp.float32
BF16 = jnp.bfloat16


def _dot(a, b):
    return jnp.dot(a, b, preferred_element_type=F32)


def _dot_nt(a, b):
    return lax.dot_general(a, b, (((1,), (1,)), ((), ())), preferred_element_type=F32)


def _dot_tn(a, b):
    return lax.dot_general(a, b, (((0,), (0,)), ((), ())), preferred_element_type=F32)


def _rms(v, g):
    ms = jnp.mean(jnp.square(v), axis=-1, keepdims=True)
    return v * lax.rsqrt(ms + NORM_EPS) * g


def _layer_norm(v, g, b):
    mu = jnp.mean(v, axis=-1, keepdims=True)
    c = v - mu
    var = jnp.mean(jnp.square(c), axis=-1, keepdims=True)
    return c * lax.rsqrt(var + NORM_EPS) * g + b


def _latent_kernel(x_ref, wlat_ref, gq_ref, gkv_ref, wuqT_ref, wuk_ref, wuvT_ref,
                   cosT_ref, sinT_ref, ra_ref, rm_ref, rp_ref,
                   qT_ref, k_ref, vT_ref, *, q_scale):
    xb = x_ref[...].astype(BF16)
    lat = _dot(xb, wlat_ref[...])
    c_q = lat[:, :Q_LORA_RANK]
    c_kv = lat[:, Q_LORA_RANK:Q_LORA_RANK + KV_LORA_RANK]
    kpe = lat[:, Q_LORA_RANK + KV_LORA_RANK:]
    qn = _rms(c_q, gq_ref[...]).astype(BF16)
    kvn = _rms(c_kv, gkv_ref[...]).astype(BF16)

    qT = _dot_nt(wuqT_ref[...], qn) * q_scale
    cosT = cosT_ref[...]
    sinT = sinT_ref[...]
    for h in range(N_HEADS):
        r0 = h * HEAD_PAD
        x1 = qT[r0 + QK_NOPE_DIM:r0 + QK_NOPE_DIM + HALF_ROPE]
        x2 = qT[r0 + QK_NOPE_DIM + HALF_ROPE:r0 + QK_NOPE_DIM + QK_ROPE_DIM]
        qT_ref[0, r0:r0 + QK_NOPE_DIM, :] = qT[r0:r0 + QK_NOPE_DIM].astype(BF16)
        qT_ref[0, r0 + QK_NOPE_DIM:r0 + QK_NOPE_DIM + HALF_ROPE, :] = (x1 * cosT - x2 * sinT).astype(BF16)
        qT_ref[0, r0 + QK_NOPE_DIM + HALF_ROPE:r0 + QK_NOPE_DIM + QK_ROPE_DIM, :] = (
            x1 * sinT + x2 * cosT).astype(BF16)
        qT_ref[0, r0 + QK_NOPE_DIM + QK_ROPE_DIM:r0 + HEAD_PAD, :] = jnp.zeros(
            (HEAD_PAD - QK_NOPE_DIM - QK_ROPE_DIM, qT.shape[1]), BF16)

    kpe_rot = (kpe * ra_ref[...] + pltpu.roll(kpe, LANES - HALF_ROPE, 1) * rm_ref[...]
               + pltpu.roll(kpe, HALF_ROPE, 1) * rp_ref[...])
    k = _dot(kvn, wuk_ref[...])
    for h in range(N_HEADS):
        k_ref[0, :, h * HEAD_PAD:(h + 1) * HEAD_PAD] = (k[:, h * HEAD_PAD:(h + 1) * HEAD_PAD] + kpe_rot).astype(BF16)
    vT = _dot_nt(wuvT_ref[...], kvn)
    row = lax.broadcasted_iota(jnp.int32, vT.shape, 0) % V_ROWS
    vT_ref[0] = jnp.where(row == V_HEAD_DIM, 1.0, vT).astype(BF16)


def _attn_kernel(qT_ref, k_ref, vT_ref, bias_ref, oT_ref, *bufs):
    qi = pl.program_id(2)
    tq = qT_ref.shape[2]
    heads = range(ATTN_HEADS_PER_STEP)
    s0, s1 = (bufs[i * ATTN_HEADS_PER_STEP:(i + 1) * ATTN_HEADS_PER_STEP] for i in range(2))

    def scores(h, j, c0=0):
        k0 = pl.multiple_of(j * TILE_K, TILE_K)
        return _dot(k_ref[0, pl.ds(k0, TILE_K), h * HEAD_PAD:(h + 1) * HEAD_PAD],
                    qT_ref[0, h * HEAD_PAD:(h + 1) * HEAD_PAD, c0:])

    def col_max(s):
        while s.shape[0] > SUBLANES:
            half = s.shape[0] // 2
            s = jnp.maximum(s[:half], s[half:])
        return jnp.max(s, axis=0, keepdims=True)

    def step(h, j, s_cur, s_nxt, carry, bias=None, c0=0, c0_nxt=0):
        m_all, acc_all, cmax = carry
        cmax_nxt = None
        if s_nxt is not None:
            s_new = scores(h, j + 1, c0_nxt)
            s_nxt[h][:, c0_nxt:] = s_new
            cmax_nxt = col_max(s_new)
        m, acc = m_all[:, c0:], acc_all[:, c0:]
        s = s_cur[h][:, c0:]
        if bias is not None:
            s = s + bias[:, c0:]
            cmax = col_max(s)
        m_new = jnp.maximum(m, cmax)
        p = jnp.exp2(s - m_new).astype(BF16)
        k0 = pl.multiple_of(j * TILE_K, TILE_K)
        pv = _dot(vT_ref[0, h * V_ROWS:(h + 1) * V_ROWS, pl.ds(k0, TILE_K)], p)
        acc = jnp.exp2(m - m_new) * acc + pv
        if c0:
            m_new = jnp.concatenate([m_all[:, :c0], m_new], axis=1)
            acc = jnp.concatenate([acc_all[:, :c0], acc], axis=1)
        return m_new, acc, cmax_nxt

    carry = []
    for h in heads:
        s_first = scores(h, 0)
        s0[h][...] = s_first
        carry.append((jnp.full((1, tq), NEG_BIG, F32), jnp.zeros((V_ROWS, tq), F32), col_max(s_first)))
    carry = tuple(carry)

    def pair(t, c):
        c = tuple(step(h, 2 * t, s0, s1, c[h]) for h in heads)
        return tuple(step(h, 2 * t + 1, s1, s0, c[h]) for h in heads)

    def trip(u, c):
        for i in range(ATTN_PAIRS_PER_TRIP):
            c = pair(ATTN_PAIRS_PER_TRIP * u + i, c)
        return c

    carry = lax.fori_loop(0, qi // ATTN_PAIRS_PER_TRIP, trip, carry)
    done = (qi // ATTN_PAIRS_PER_TRIP) * ATTN_PAIRS_PER_TRIP
    for i in range(ATTN_PAIRS_PER_TRIP - 1):
        carry = lax.cond(done + i < qi, functools.partial(pair, done + i), lambda c: c, carry)
    jd = 2 * qi
    carry = tuple(step(h, jd, s0, s1, carry[h], bias=bias_ref.at[0], c0_nxt=TILE_K) for h in heads)
    carry = tuple(step(h, jd + 1, s1, None, carry[h], bias=bias_ref.at[1], c0=TILE_K) for h in heads)
    for h in heads:
        acc = carry[h][1]
        oT_ref[0, h * V_HEAD_DIM:(h + 1) * V_HEAD_DIM, :] = (
            acc[:V_HEAD_DIM] / acc[V_HEAD_DIM:V_HEAD_DIM + 1]).astype(BF16)


def _mix_kernel(x_ref, oT_ref, wpool_ref, wgate_ref, mixw_ref, pscale_ref, wo_ref, wout_ref,
                g1_ref, b1_ref, wrT_ref, brT_ref, triu_ref,
                h_ref, route_ref, routeT_ref, cnt_ref, ext_ref, base_ref):
    b = pl.program_id(0)
    si = pl.program_id(1)
    tm = x_ref.shape[0]

    @pl.when(jnp.logical_and(b == 0, si == 0))
    def _():
        base_ref[...] = jnp.zeros_like(base_ref)

    @pl.when(si == 0)
    def _():
        ext_ref[0:POOL_HALO, :] = jnp.zeros((POOL_HALO, POOL_WIDTH), F32)

    x = x_ref[...]
    xb = x.astype(BF16)
    u = _dot(xb, wpool_ref[...])
    ext_ref[POOL_HALO:POOL_HALO + tm, :] = u

    pos1 = si * tm + lax.broadcasted_iota(jnp.int32, (tm, POOL_GROUP_DIM), 0) + 1
    y_parts = []
    for g, win in enumerate(POOL_WINDOWS):
        c0 = g * POOL_GROUP_DIM
        ws = u[:, c0:c0 + POOL_GROUP_DIM]
        for kk in range(1, win):
            ws = ws + ext_ref[POOL_HALO - kk:POOL_HALO - kk + tm, c0:c0 + POOL_GROUP_DIM]
        count = jnp.minimum(pos1, win).astype(F32)
        pooled = ws / count - u[:, c0:c0 + POOL_GROUP_DIM]
        y_parts.append(_dot(pooled.astype(BF16), mixw_ref[g]))
    ext_ref[0:POOL_HALO, :] = ext_ref[tm:tm + POOL_HALO, :]
    y_pool = jnp.concatenate(y_parts, axis=-1) * pscale_ref[...]

    glog = _dot(xb, wgate_ref[...])
    y_mla = _dot_tn(oT_ref[0], wo_ref[...])
    merged = (jax.nn.sigmoid(glog[:, :D_MODEL]) * y_pool + jax.nn.sigmoid(glog[:, D_MODEL:]) * y_mla)
    r = DEEPNORM_ALPHA * x + _dot(merged.astype(BF16), wout_ref[...])
    h = _layer_norm(r, g1_ref[...], b1_ref[...])
    for j in range(ROW_TILES):
        h_ref[pl.ds(j, tm, stride=ROW_TILES), :] = h[:, j * LANES:(j + 1) * LANES]

    logits = _dot_nt(wrT_ref[...], h.astype(BF16)) + brT_ref[...]
    big = float(LANES)
    gl = logits[GROUP_ROW0:GROUP_ROW0 + SUBLANES]
    grow = lax.broadcasted_iota(jnp.int32, gl.shape, 0)
    gl = jnp.where(grow < N_GROUPS, gl, NEG_BIG)
    gmax = jnp.max(gl, axis=0, keepdims=True)
    g_w = 1.0 / jnp.sum(jnp.exp(gl - gmax), axis=0, keepdims=True)
    g_idx = jnp.min(jnp.where(gl == gmax, grow.astype(F32), big), axis=0, keepdims=True)
    el = logits[0:N_EXPERTS]
    erow_i = lax.broadcasted_iota(jnp.int32, el.shape, 0)
    erow = erow_i.astype(F32)
    el = jnp.where((erow_i // EXPERTS_PER_GROUP).astype(F32) == g_idx, el, NEG_BIG)
    e1max = jnp.max(el, axis=0, keepdims=True)
    i1 = jnp.min(jnp.where(el == e1max, erow, big), axis=0, keepdims=True)
    el2 = jnp.where(erow == i1, NEG_BIG, el)
    e2max = jnp.max(el2, axis=0, keepdims=True)
    i2 = jnp.min(jnp.where(el2 == e2max, erow, big), axis=0, keepdims=True)
    ratio = jnp.exp(e2max - e1max)
    gate1 = g_w / (1.0 + ratio)
    gate2 = g_w * ratio / (1.0 + ratio)

    hit1 = erow == i1
    hit2 = erow == i2
    onehot = jnp.where(jnp.logical_or(hit1, hit2), 1.0, 0.0)
    base = base_ref[:, 0:1]
    before = _dot(onehot.astype(BF16), triu_ref[...]) + base
    rank1 = jnp.sum(jnp.where(hit1, before, 0.0), axis=0, keepdims=True)
    rank2 = jnp.sum(jnp.where(hit2, before, 0.0), axis=0, keepdims=True)
    new_base = base + jnp.sum(onehot, axis=1, keepdims=True)
    base_ref[...] = jnp.broadcast_to(new_base, base_ref.shape)
    cnt_ref[...] = jnp.broadcast_to(new_base, cnt_ref.shape)

    row = lax.broadcasted_iota(jnp.int32, (LANES, tm), 0)
    routeT = jnp.where(row == 0, i1, 0.0)
    routeT = jnp.where(row == 1, i2, routeT)
    routeT = jnp.where(row == 2, gate1, routeT)
    routeT = jnp.where(row == 3, gate2, routeT)
    routeT = jnp.where(row == 4, rank1, routeT)
    routeT = jnp.where(row == 5, rank2, routeT)
    routeT_ref[...] = routeT[:SUBLANES]
    route_ref[...] = routeT.T


def _gather_row(idx_ref, r, src_hbm, dst_ref, slot, sem, priority=0):
    src0 = pl.multiple_of(idx_ref[0, 0, r] * ROW_TILES, ROW_TILES)
    pltpu.make_async_copy(src_hbm.at[pl.ds(src0, ROW_TILES), :],
                          dst_ref.at[slot, :, r, :],
                          sem.at[slot]).start(priority=priority)


def _gather_rows_loop(idx_ref, src_hbm, dst_ref, slot, sem, n_rows, unroll=8):
    def body(c, _):
        for u in range(unroll):
            _gather_row(idx_ref, c * unroll + u, src_hbm, dst_ref, slot, sem)
        return 0
    lax.fori_loop(0, n_rows // unroll, body, 0)


def _gather_rows_inline(idx_ref, src_hbm, dst_ref, slot, sem, n_rows):
    for r in range(n_rows):
        _gather_row(idx_ref, r, src_hbm, dst_ref, slot, sem, priority=r % 2)


def _wait_rows(dst_ref, slot, sem):
    pltpu.make_async_copy(dst_ref.at[slot], dst_ref.at[slot], sem.at[slot]).wait()


def _expert_kernel(be_ref, nused_ref, tok_ref, tok1_ref, tok2_ref, h_hbm, wg_ref, wu_ref, wd_ref,
                   y_ref, xbuf, xs_ref, wgb, wub, wdb, sem):
    i = pl.program_id(0)
    n_used = nused_ref[0]
    slot = lax.rem(i, GATHER_SLOTS)
    tb = TILE_EXPERT

    @pl.when(i == 0)
    def _():
        _gather_rows_loop(tok_ref, h_hbm, xbuf, 0, sem, tb)
        _gather_rows_loop(tok1_ref, h_hbm, xbuf, 1, sem, tb)

    changed = jnp.logical_or(i == 0, be_ref[i] != be_ref[jnp.maximum(i - 1, 0)])

    @pl.when(jnp.logical_and(changed, i < n_used))
    def _():
        wgb[...] = wg_ref[0].astype(BF16)
        wub[...] = wu_ref[0].astype(BF16)
        wdb[...] = wd_ref[0].astype(BF16)

    @pl.when(jnp.logical_and(i >= n_used, i < n_used + GATHER_AHEAD))
    def _():
        _wait_rows(xbuf, slot, sem)

    @pl.when(i < n_used)
    def _():
        _wait_rows(xbuf, slot, sem)
        for j in range(ROW_TILES):
            xs_ref[:, j * LANES:(j + 1) * LANES] = xbuf[slot, j].astype(BF16)
        _gather_rows_inline(tok2_ref, h_hbm, xbuf, lax.rem(i + GATHER_AHEAD, GATHER_SLOTS), sem, tb)
        xs = xs_ref[...]
        gate = _dot(xs, wgb[...])
        up = _dot(xs, wub[...])
        hid = (jax.nn.silu(gate) * up).astype(BF16)
        y = _dot(hid, wdb[...])
        for j in range(ROW_TILES):
            y_ref[pl.ds(j, tb, stride=ROW_TILES), :] = y[:, j * LANES:(j + 1) * LANES]

    @pl.when(i >= n_used)
    def _():
        y_ref[...] = jnp.zeros_like(y_ref)


def _combine_kernel(dst_ref, dst1_ref, dst2_ref, h_ref, route_ref, y_hbm, g2_ref, b2_ref, o_ref, gbuf, sem):
    i = pl.program_id(0)
    n = pl.num_programs(0)
    slot = lax.rem(i, GATHER_SLOTS)
    tf = TILE_COMBINE

    @pl.when(i == 0)
    def _():
        _gather_rows_loop(dst_ref, y_hbm, gbuf, 0, sem, TOP_K * tf)
        _gather_rows_loop(dst1_ref, y_hbm, gbuf, 1, sem, TOP_K * tf)

    _wait_rows(gbuf, slot, sem)
    route = route_ref[...]
    gate1 = route[:, 2:3]
    gate2 = route[:, 3:4]
    parts = []
    for j in range(ROW_TILES):
        hj = h_ref[pl.ds(j, tf, stride=ROW_TILES), :]
        y1 = gbuf[slot, j, 0:tf, :]
        y2 = gbuf[slot, j, tf:TOP_K * tf, :]
        parts.append(DEEPNORM_ALPHA * hj + (gate1 * y1 + gate2 * y2))
    z = jnp.concatenate(parts, axis=-1)
    o_ref[...] = _layer_norm(z, g2_ref[...], b2_ref[...])
    _gather_rows_inline(dst2_ref, y_hbm, gbuf, lax.rem(i + GATHER_AHEAD, GATHER_SLOTS), sem, TOP_K * tf)

    @pl.when(i == n - 1)
    def _():
        for ahead in range(1, GATHER_AHEAD + 1):
            _wait_rows(gbuf, lax.rem(i + ahead, GATHER_SLOTS), sem)


def _rope_tables(seq):
    f32 = np.float32
    inv = (f32(1.0) / (f32(ROPE_THETA) ** (np.arange(0, QK_ROPE_DIM, 2, dtype=f32) / f32(QK_ROPE_DIM)))).astype(f32)
    ang = (np.arange(seq, dtype=f32)[:, None] * inv[None, :]).astype(f32)
    cos, sin = np.cos(ang.astype(np.float64)).astype(f32), np.sin(ang.astype(np.float64)).astype(f32)
    zeros = np.zeros((seq, QK_NOPE_DIM), f32)
    pad = np.zeros((seq, HEAD_PAD - QK_NOPE_DIM - QK_ROPE_DIM), f32)
    z16 = np.zeros((seq, HALF_ROPE), f32)
    rot_a = np.concatenate([zeros, cos, cos, pad], axis=1)
    rot_m = np.concatenate([zeros, -sin, z16, pad], axis=1)
    rot_p = np.concatenate([zeros, z16, sin, pad], axis=1)
    return tuple(jnp.asarray(t) for t in (np.ascontiguousarray(cos.T), np.ascontiguousarray(sin.T),
                                          rot_a, rot_m, rot_p))


def _scatter_rows_sparsecore(x, indices, n_out):
    n, width = x.shape
    mesh = plsc.VectorSubcoreMesh(core_axis_name="core", subcore_axis_name="subcore")
    per_core = n // SC_SCATTER_WINDOW // mesh.num_cores
    assert per_core * SC_SCATTER_WINDOW * mesh.num_cores == n

    @functools.partial(pl.kernel, out_type=jax.ShapeDtypeStruct((n_out, width), x.dtype), mesh=mesh,
                       scratch_types=[])
    def scatter(x_hbm, i_hbm, o_hbm):
        first = lax.axis_index("core") * per_core

        def body(x_vmem, i_vmem):
            pltpu.sync_copy(x_vmem, o_hbm.at[i_vmem.at[0]])

        pltpu.emit_pipeline(
            body,
            grid=(per_core,),
            in_specs=[pl.BlockSpec((SC_SCATTER_WINDOW, width), index_map=lambda i: (first + i, 0)),
                      pl.BlockSpec((1, SC_SCATTER_WINDOW), index_map=lambda i: (0, first + i))],
            out_specs=[],
            core_axis_name="subcore",
            dimension_semantics=(pltpu.PARALLEL,),
        )(x_hbm, i_hbm)

    return scatter(x, indices.reshape(1, n))


def _full(shape):
    return pl.BlockSpec(shape, lambda *_: (0,) * len(shape))


def _params(sem):
    return pltpu.CompilerParams(dimension_semantics=sem, vmem_limit_bytes=VMEM_LIMIT)


def kernel(x, w_in, pool_mix_w, pool_scale, q_norm_g, w_uq, kv_norm_g, w_ukv, w_mla_o, w_out, ln1_g, ln1_b,
           w_router_group, b_router_group, w_router_expert, b_router_expert, w_gate, w_up, w_down, ln2_g, ln2_b):
    B, S, D = x.shape
    assert D == D_MODEL and w_in.shape[0] == DEPTH == 1
    assert S % TILE_LATENT == 0 and S % TILE_Q == 0 and S % TILE_MIX == 0 and TILE_Q % TILE_K == 0
    N = B * S
    assert N % TILE_COMBINE == 0
    H = N_HEADS

    w = w_in[0]
    o1 = POOL_WIDTH
    o2 = o1 + Q_LORA_RANK
    o3 = o2 + KV_LORA_RANK
    o4 = o3 + QK_ROPE_DIM
    w_pool = w[:, :o1].astype(BF16)
    kpe_cols = jnp.pad(w[:, o3:o4], ((0, 0), (QK_NOPE_DIM, HEAD_PAD - QK_NOPE_DIM - QK_ROPE_DIM)))
    w_lat = jnp.concatenate([w[:, o1:o3], kpe_cols], axis=1).astype(BF16)
    w_gates = w[:, o4:].astype(BF16)
    qd = QK_NOPE_DIM + QK_ROPE_DIM
    wuq = jnp.pad(w_uq[0].reshape(Q_LORA_RANK, H, qd), ((0, 0), (0, 0), (0, HEAD_PAD - qd)))
    wuqT = wuq.reshape(Q_LORA_RANK, H * HEAD_PAD).T.astype(BF16)
    wukv = w_ukv[0].reshape(KV_LORA_RANK, H, QK_NOPE_DIM + V_HEAD_DIM)
    wuk = jnp.pad(wukv[:, :, :QK_NOPE_DIM], ((0, 0), (0, 0), (0, HEAD_PAD - QK_NOPE_DIM)))
    wuk = wuk.reshape(KV_LORA_RANK, H * HEAD_PAD).astype(BF16)
    wuv = jnp.pad(wukv[:, :, QK_NOPE_DIM:], ((0, 0), (0, 0), (0, V_ROWS - V_HEAD_DIM)))
    wuvT = wuv.reshape(KV_LORA_RANK, H * V_ROWS).T.astype(BF16)
    w_rT = jnp.zeros((LANES, D), F32)
    w_rT = w_rT.at[:N_EXPERTS].set(w_router_expert[0].T).at[GROUP_ROW0:GROUP_ROW0 + N_GROUPS].set(
        w_router_group[0].T).astype(BF16)
    b_rT = jnp.zeros((LANES, 1), F32)
    b_rT = b_rT.at[:N_EXPERTS, 0].set(b_router_expert[0]).at[GROUP_ROW0:GROUP_ROW0 + N_GROUPS, 0].set(
        b_router_group[0])
    cosT, sinT, rot_a, rot_m, rot_p = _rope_tables(S)
    q_scale = (QK_NOPE_DIM + QK_ROPE_DIM) ** -0.5 * math.log2(math.e)
    x2 = x.reshape(N, D)

    ta = TILE_LATENT
    nsa = S // ta
    qT, k, vT = pl.pallas_call(
        functools.partial(_latent_kernel, q_scale=q_scale),
        grid=(B, nsa),
        in_specs=[
            pl.BlockSpec((ta, D), lambda b, s: (b * nsa + s, 0)),
            _full(w_lat.shape), _full((1, Q_LORA_RANK)), _full((1, KV_LORA_RANK)),
            _full(wuqT.shape), _full(wuk.shape), _full(wuvT.shape),
            pl.BlockSpec((HALF_ROPE, ta), lambda b, s: (0, s)),
            pl.BlockSpec((HALF_ROPE, ta), lambda b, s: (0, s)),
            pl.BlockSpec((ta, LANES), lambda b, s: (s, 0)),
            pl.BlockSpec((ta, LANES), lambda b, s: (s, 0)),
            pl.BlockSpec((ta, LANES), lambda b, s: (s, 0)),
        ],
        out_specs=[
            pl.BlockSpec((1, H * HEAD_PAD, ta), lambda b, s: (b, 0, s)),
            pl.BlockSpec((1, ta, H * HEAD_PAD), lambda b, s: (b, s, 0)),
            pl.BlockSpec((1, H * V_ROWS, ta), lambda b, s: (b, 0, s)),
        ],
        out_shape=[
            jax.ShapeDtypeStruct((B, H * HEAD_PAD, S), BF16),
            jax.ShapeDtypeStruct((B, S, H * HEAD_PAD), BF16),
            jax.ShapeDtypeStruct((B, H * V_ROWS, S), BF16),
        ],
        compiler_params=_params(("parallel", "parallel")),
    )(x2, w_lat, q_norm_g[0][None], kv_norm_g[0][None], wuqT, wuk, wuvT, cosT, sinT, rot_a, rot_m, rot_p)

    tq = TILE_Q
    assert tq == 2 * TILE_K
    key_chunk = jnp.arange(tq)[:, None] // CHUNK
    q_chunk = jnp.arange(tq)[None, :] // CHUNK
    mask_bias = jnp.where(key_chunk <= q_chunk, 0.0, NEG_BIG).astype(F32).reshape(2, TILE_K, tq)
    hp = ATTN_HEADS_PER_STEP
    assert H % hp == 0
    oT = pl.pallas_call(
        _attn_kernel,
        grid=(B, H // hp, S // tq),
        in_specs=[
            pl.BlockSpec((1, hp * HEAD_PAD, tq), lambda b, h, q: (b, h, q)),
            pl.BlockSpec((1, S, hp * HEAD_PAD), lambda b, h, q: (b, 0, h)),
            pl.BlockSpec((1, hp * V_ROWS, S), lambda b, h, q: (b, h, 0)),
            _full(mask_bias.shape),
        ],
        out_specs=pl.BlockSpec((1, hp * V_HEAD_DIM, tq), lambda b, h, q: (b, h, q)),
        out_shape=jax.ShapeDtypeStruct((B, H * V_HEAD_DIM, S), BF16),
        scratch_shapes=[pltpu.VMEM((TILE_K, tq), F32)] * (2 * hp),
        compiler_params=_params(("parallel", "parallel", "arbitrary")),
    )(qT, k, vT, mask_bias)

    tm = TILE_MIX
    nsm = S // tm
    triu = (jnp.arange(tm)[:, None] < jnp.arange(tm)[None, :]).astype(BF16)
    h_tok, route, routeT, counts = pl.pallas_call(
        _mix_kernel,
        grid=(B, nsm),
        in_specs=[
            pl.BlockSpec((tm, D), lambda b, s: (b * nsm + s, 0)),
            pl.BlockSpec((1, H * V_HEAD_DIM, tm), lambda b, s: (b, 0, s)),
            _full(w_pool.shape), _full(w_gates.shape), _full(pool_mix_w.shape[1:]), _full((1, D)),
            _full(w_mla_o.shape[1:]), _full(w_out.shape[1:]), _full((1, D)), _full((1, D)),
            _full(w_rT.shape), _full(b_rT.shape), _full(triu.shape),
        ],
        out_specs=[
            pl.BlockSpec((tm * ROW_TILES, LANES), lambda b, s: (b * nsm + s, 0)),
            pl.BlockSpec((tm, LANES), lambda b, s: (b * nsm + s, 0)),
            pl.BlockSpec((SUBLANES, tm), lambda b, s: (0, b * nsm + s)),
            _full((N_EXPERTS, LANES)),
        ],
        out_shape=[
            jax.ShapeDtypeStruct((N * ROW_TILES, LANES), F32),
            jax.ShapeDtypeStruct((N, LANES), F32),
            jax.ShapeDtypeStruct((SUBLANES, N), F32),
            jax.ShapeDtypeStruct((N_EXPERTS, LANES), F32),
        ],
        scratch_shapes=[pltpu.VMEM((tm + POOL_HALO, POOL_WIDTH), F32), pltpu.VMEM((N_EXPERTS, LANES), F32)],
        compiler_params=_params(("arbitrary", "arbitrary")),
    )(x2, oT, w_pool, w_gates, pool_mix_w[0].astype(BF16), pool_scale[0][None], w_mla_o[0].astype(BF16),
      w_out[0].astype(BF16), ln1_g[0][None], ln1_b[0][None], w_rT, b_rT, triu)

    tb = TILE_EXPERT
    A = N * TOP_K
    n_blocks = -(-(A + N_EXPERTS * (tb - 1)) // tb) + GATHER_AHEAD
    R = n_blocks * tb
    e_idx = routeT[0:2].astype(jnp.int32)
    rank = routeT[4:6].astype(jnp.int32)
    cnt = counts[:, 0].astype(jnp.int32)
    padded = ((cnt + tb - 1) // tb) * tb
    pad_end = jnp.cumsum(padded)
    pad_start = pad_end - padded
    is_e = e_idx[:, None, :] == jnp.arange(N_EXPERTS, dtype=jnp.int32)[None, :, None]
    dest = jnp.sum(jnp.where(is_e, pad_start[None, :, None], 0), axis=1) + rank
    block_start = jnp.arange(n_blocks, dtype=jnp.int32) * tb
    block_e = jnp.minimum(jnp.sum(pad_end[None, :] <= block_start[:, None], axis=1), N_EXPERTS - 1).astype(jnp.int32)
    n_used = (pad_end[-1] // tb).astype(jnp.int32)[None]
    tok = jnp.broadcast_to(jnp.arange(N, dtype=jnp.int32)[None, :, None], (TOP_K, N, LANES)).reshape(A, LANES)
    tok_rows = _scatter_rows_sparsecore(tok, dest.reshape(A), R)[:, 0].reshape(n_blocks, tb)
    block_is_e = block_e[:, None] == jnp.arange(N_EXPERTS, dtype=jnp.int32)[None, :]
    block_pad_start = jnp.sum(jnp.where(block_is_e, pad_start[None, :], 0), axis=1)
    block_cnt = jnp.sum(jnp.where(block_is_e, cnt[None, :], 0), axis=1)
    row_in_expert = block_start[:, None] + jnp.arange(tb, dtype=jnp.int32)[None, :] - block_pad_start[:, None]
    row_valid = jnp.logical_and(row_in_expert < block_cnt[:, None],
                                jnp.arange(n_blocks, dtype=jnp.int32)[:, None] < n_used[0])
    row_tok3 = jnp.where(row_valid, tok_rows, 0).reshape(n_blocks, 1, tb)

    y_rows = pl.pallas_call(
        _expert_kernel,
        grid_spec=pltpu.PrefetchScalarGridSpec(
            num_scalar_prefetch=2,
            grid=(n_blocks,),
            in_specs=[
                pl.BlockSpec((1, 1, tb), lambda i, be, nu: (i, 0, 0), memory_space=pltpu.SMEM),
                pl.BlockSpec((1, 1, tb), lambda i, be, nu: (jnp.minimum(i + 1, n_blocks - 1), 0, 0),
                             memory_space=pltpu.SMEM),
                pl.BlockSpec((1, 1, tb), lambda i, be, nu: (jnp.minimum(i + GATHER_AHEAD, n_blocks - 1), 0, 0),
                             memory_space=pltpu.SMEM),
                pl.BlockSpec(memory_space=pl.ANY),
                pl.BlockSpec((1, D, D_EXPERT), lambda i, be, nu: (be[i], 0, 0)),
                pl.BlockSpec((1, D, D_EXPERT), lambda i, be, nu: (be[i], 0, 0)),
                pl.BlockSpec((1, D_EXPERT, D), lambda i, be, nu: (be[i], 0, 0)),
            ],
            out_specs=pl.BlockSpec((tb * ROW_TILES, LANES), lambda i, be, nu: (i, 0)),
            scratch_shapes=[
                pltpu.VMEM((GATHER_SLOTS, ROW_TILES, tb, LANES), F32), pltpu.VMEM((tb, D), BF16),
                pltpu.VMEM((D, D_EXPERT), BF16), pltpu.VMEM((D, D_EXPERT), BF16), pltpu.VMEM((D_EXPERT, D), BF16),
                pltpu.SemaphoreType.DMA((GATHER_SLOTS,)),
            ],
        ),
        out_shape=jax.ShapeDtypeStruct((R * ROW_TILES, LANES), F32),
        compiler_params=_params(("arbitrary",)),
    )(block_e, n_used, row_tok3, row_tok3, row_tok3, h_tok, w_gate[0], w_up[0], w_down[0])

    tf = TILE_COMBINE
    nf = N // tf
    dest3 = dest.reshape(TOP_K, nf, tf).transpose(1, 0, 2).reshape(nf, 1, TOP_K * tf)
    out = pl.pallas_call(
        _combine_kernel,
        grid=(nf,),
        in_specs=[
            pl.BlockSpec((1, 1, TOP_K * tf), lambda i: (i, 0, 0), memory_space=pltpu.SMEM),
            pl.BlockSpec((1, 1, TOP_K * tf), lambda i: (jnp.minimum(i + 1, nf - 1), 0, 0), memory_space=pltpu.SMEM),
            pl.BlockSpec((1, 1, TOP_K * tf), lambda i: (jnp.minimum(i + GATHER_AHEAD, nf - 1), 0, 0),
                         memory_space=pltpu.SMEM),
            pl.BlockSpec((tf * ROW_TILES, LANES), lambda i: (i, 0)),
            pl.BlockSpec((tf, LANES), lambda i: (i, 0)),
            pl.BlockSpec(memory_space=pl.ANY),
            _full((1, D)), _full((1, D)),
        ],
        out_specs=pl.BlockSpec((tf, D), lambda i: (i, 0)),
        out_shape=jax.ShapeDtypeStruct((N, D), F32),
        scratch_shapes=[pltpu.VMEM((GATHER_SLOTS, ROW_TILES, TOP_K * tf, LANES), F32),
                        pltpu.SemaphoreType.DMA((GATHER_SLOTS,))],
        compiler_params=_params(("arbitrary",)),
    )(dest3, dest3, dest3, h_tok, route, y_rows, ln2_g[0][None], ln2_b[0][None])
    return out.reshape(B, S, D)
```

```python
import functools
import math

import jax
import jax.numpy as jnp
import numpy as np
from jax import lax
from jax.experimental import pallas as pl
from jax.experimental.pallas import tpu as pltpu
from jax.experimental.pallas import tpu_sc as plsc

D_MODEL = 1024
CHUNK = 64
POOL_WINDOWS = (2, 4, 8, 16)
POOL_GROUPS = len(POOL_WINDOWS)
POOL_WIDTH = D_MODEL // 2
POOL_GROUP_DIM = POOL_WIDTH // POOL_GROUPS
POOL_OUT_GROUP_DIM = D_MODEL // POOL_GROUPS
POOL_HALO = 16
N_HEADS = 8
QK_NOPE_DIM = D_MODEL // 16
QK_ROPE_DIM = D_MODEL // 32
HALF_ROPE = QK_ROPE_DIM // 2
V_HEAD_DIM = D_MODEL // 16
V_ROWS = V_HEAD_DIM + 16
Q_LORA_RANK = 3 * D_MODEL // 8
KV_LORA_RANK = D_MODEL // 4
ROPE_THETA = 10000.0
N_GROUPS = 4
EXPERTS_PER_GROUP = 8
N_EXPERTS = N_GROUPS * EXPERTS_PER_GROUP
TOP_K = 2
D_EXPERT = D_MODEL // 2
NORM_EPS = 1e-5
DEPTH = 1
DEEPNORM_ALPHA = (2.0 * DEPTH) ** 0.25

LANES = 128
SUBLANES = 8
HEAD_PAD = LANES
ROW_TILES = D_MODEL // LANES
GROUP_ROW0 = 64
NEG_BIG = -1e30
VMEM_LIMIT = 56 * 1024 * 1024

TILE_LATENT = 1024
TILE_Q = 512
TILE_K = 256
ATTN_HEADS_PER_STEP = 4
ATTN_PAIRS_PER_TRIP = 2
TILE_MIX = 512
GATHER_AHEAD = 2
GATHER_SLOTS = GATHER_AHEAD + 1
TILE_EXPERT = 256
SC_SCATTER_WINDOW = 128
TILE_COMBINE = 512

F32 = jnp.float32
BF16 = jnp.bfloat16


def _dot(a, b):
    return jnp.dot(a, b, preferred_element_type=F32)


def _dot_nt(a, b):
    return lax.dot_general(a, b, (((1,), (1,)), ((), ())), preferred_element_type=F32)


def _dot_tn(a, b):
    return lax.dot_general(a, b, (((0,), (0,)), ((), ())), preferred_element_type=F32)


def _rms(v, g):
    ms = jnp.mean(jnp.square(v), axis=-1, keepdims=True)
    return v * lax.rsqrt(ms + NORM_EPS) * g


def _layer_norm(v, g, b):
    mu = jnp.mean(v, axis=-1, keepdims=True)
    c = v - mu
    var = jnp.mean(jnp.square(c), axis=-1, keepdims=True)
    return c * lax.rsqrt(var + NORM_EPS) * g + b


def _latent_kernel(x_ref, wlat_ref, gq_ref, gkv_ref, wuqT_ref, wuk_ref, wuvT_ref,
                   cosT_ref, sinT_ref, ra_ref, rm_ref, rp_ref,
                   qT_ref, k_ref, vT_ref, *, q_scale):
    xb = x_ref[...].astype(BF16)
    lat = _dot(xb, wlat_ref[...])
    c_q = lat[:, :Q_LORA_RANK]
    c_kv = lat[:, Q_LORA_RANK:Q_LORA_RANK + KV_LORA_RANK]
    kpe = lat[:, Q_LORA_RANK + KV_LORA_RANK:]
    qn = _rms(c_q, gq_ref[...]).astype(BF16)
    kvn = _rms(c_kv, gkv_ref[...]).astype(BF16)

    qT = _dot_nt(wuqT_ref[...], qn) * q_scale
    cosT = cosT_ref[...]
    sinT = sinT_ref[...]
    for h in range(N_HEADS):
        r0 = h * HEAD_PAD
        x1 = qT[r0 + QK_NOPE_DIM:r0 + QK_NOPE_DIM + HALF_ROPE]
        x2 = qT[r0 + QK_NOPE_DIM + HALF_ROPE:r0 + QK_NOPE_DIM + QK_ROPE_DIM]
        qT_ref[0, r0:r0 + QK_NOPE_DIM, :] = qT[r0:r0 + QK_NOPE_DIM].astype(BF16)
        qT_ref[0, r0 + QK_NOPE_DIM:r0 + QK_NOPE_DIM + HALF_ROPE, :] = (x1 * cosT - x2 * sinT).astype(BF16)
        qT_ref[0, r0 + QK_NOPE_DIM + HALF_ROPE:r0 + QK_NOPE_DIM + QK_ROPE_DIM, :] = (
            x1 * sinT + x2 * cosT).astype(BF16)
        qT_ref[0, r0 + QK_NOPE_DIM + QK_ROPE_DIM:r0 + HEAD_PAD, :] = jnp.zeros(
            (HEAD_PAD - QK_NOPE_DIM - QK_ROPE_DIM, qT.shape[1]), BF16)

    kpe_rot = (kpe * ra_ref[...] + pltpu.roll(kpe, LANES - HALF_ROPE, 1) * rm_ref[...]
               + pltpu.roll(kpe, HALF_ROPE, 1) * rp_ref[...])
    k = _dot(kvn, wuk_ref[...])
    for h in range(N_HEADS):
        k_ref[0, :, h * HEAD_PAD:(h + 1) * HEAD_PAD] = (k[:, h * HEAD_PAD:(h + 1) * HEAD_PAD] + kpe_rot).astype(BF16)
    vT = _dot_nt(wuvT_ref[...], kvn)
    row = lax.broadcasted_iota(jnp.int32, vT.shape, 0) % V_ROWS
    vT_ref[0] = jnp.where(row == V_HEAD_DIM, 1.0, vT).astype(BF16)


def _attn_kernel(qT_ref, k_ref, vT_ref, bias_ref, oT_ref, *bufs):
    qi = pl.program_id(2)
    tq = qT_ref.shape[2]
    heads = range(ATTN_HEADS_PER_STEP)
    s0, s1 = (bufs[i * ATTN_HEADS_PER_STEP:(i + 1) * ATTN_HEADS_PER_STEP] for i in range(2))

    def scores(h, j, c0=0):
        k0 = pl.multiple_of(j * TILE_K, TILE_K)
        return _dot(k_ref[0, pl.ds(k0, TILE_K), h * HEAD_PAD:(h + 1) * HEAD_PAD],
                    qT_ref[0, h * HEAD_PAD:(h + 1) * HEAD_PAD, c0:])

    def col_max(s):
        while s.shape[0] > SUBLANES:
            half = s.shape[0] // 2
            s = jnp.maximum(s[:half], s[half:])
        return jnp.max(s, axis=0, keepdims=True)

    def step(h, j, s_cur, s_nxt, carry, bias=None, c0=0, c0_nxt=0):
        m_all, acc_all, cmax = carry
        cmax_nxt = None
        if s_nxt is not None:
            s_new = scores(h, j + 1, c0_nxt)
            s_nxt[h][:, c0_nxt:] = s_new
            cmax_nxt = col_max(s_new)
        m, acc = m_all[:, c0:], acc_all[:, c0:]
        s = s_cur[h][:, c0:]
        if bias is not None:
            s = s + bias[:, c0:]
            cmax = col_max(s)
        m_new = jnp.maximum(m, cmax)
        p = jnp.exp2(s - m_new).astype(BF16)
        k0 = pl.multiple_of(j * TILE_K, TILE_K)
        pv = _dot(vT_ref[0, h * V_ROWS:(h + 1) * V_ROWS, pl.ds(k0, TILE_K)], p)
        acc = jnp.exp2(m - m_new) * acc + pv
        if c0:
            m_new = jnp.concatenate([m_all[:, :c0], m_new], axis=1)
            acc = jnp.concatenate([acc_all[:, :c0], acc], axis=1)
        return m_new, acc, cmax_nxt

    carry = []
    for h in heads:
        s_first = scores(h, 0)
        s0[h][...] = s_first
        carry.append((jnp.full((1, tq), NEG_BIG, F32), jnp.zeros((V_ROWS, tq), F32), col_max(s_first)))
    carry = tuple(carry)

    def pair(t, c):
        c = tuple(step(h, 2 * t, s0, s1, c[h]) for h in heads)
        return tuple(step(h, 2 * t + 1, s1, s0, c[h]) for h in heads)

    def trip(u, c):
        for i in range(ATTN_PAIRS_PER_TRIP):
            c = pair(ATTN_PAIRS_PER_TRIP * u + i, c)
        return c

    carry = lax.fori_loop(0, qi // ATTN_PAIRS_PER_TRIP, trip, carry)
    done = (qi // ATTN_PAIRS_PER_TRIP) * ATTN_PAIRS_PER_TRIP
    for i in range(ATTN_PAIRS_PER_TRIP - 1):
        carry = lax.cond(done + i < qi, functools.partial(pair, done + i), lambda c: c, carry)
    jd = 2 * qi
    carry = tuple(step(h, jd, s0, s1, carry[h], bias=bias_ref.at[0], c0_nxt=TILE_K) for h in heads)
    carry = tuple(step(h, jd + 1, s1, None, carry[h], bias=bias_ref.at[1], c0=TILE_K) for h in heads)
    for h in heads:
        acc = carry[h][1]
        oT_ref[0, h * V_HEAD_DIM:(h + 1) * V_HEAD_DIM, :] = (
            acc[:V_HEAD_DIM] / acc[V_HEAD_DIM:V_HEAD_DIM + 1]).astype(BF16)


def _mix_kernel(x_ref, oT_ref, wpool_ref, wgate_ref, mixw_ref, pscale_ref, wo_ref, wout_ref,
                g1_ref, b1_ref, wrT_ref, brT_ref, triu_ref,
                h_ref, route_ref, routeT_ref, cnt_ref, ext_ref, base_ref):
    b = pl.program_id(0)
    si = pl.program_id(1)
    tm = x_ref.shape[0]

    @pl.when(jnp.logical_and(b == 0, si == 0))
    def _():
        base_ref[...] = jnp.zeros_like(base_ref)

    @pl.when(si == 0)
    def _():
        ext_ref[0:POOL_HALO, :] = jnp.zeros((POOL_HALO, POOL_WIDTH), F32)

    x = x_ref[...]
    xb = x.astype(BF16)
    u = _dot(xb, wpool_ref[...])
    ext_ref[POOL_HALO:POOL_HALO + tm, :] = u

    pos1 = si * tm + lax.broadcasted_iota(jnp.int32, (tm, POOL_GROUP_DIM), 0) + 1
    y_parts = []
    for g, win in enumerate(POOL_WINDOWS):
        c0 = g * POOL_GROUP_DIM
        ws = u[:, c0:c0 + POOL_GROUP_DIM]
        for kk in range(1, win):
            ws = ws + ext_ref[POOL_HALO - kk:POOL_HALO - kk + tm, c0:c0 + POOL_GROUP_DIM]
        count = jnp.minimum(pos1, win).astype(F32)
        pooled = ws / count - u[:, c0:c0 + POOL_GROUP_DIM]
        y_parts.append(_dot(pooled.astype(BF16), mixw_ref[g]))
    ext_ref[0:POOL_HALO, :] = ext_ref[tm:tm + POOL_HALO, :]
    y_pool = jnp.concatenate(y_parts, axis=-1) * pscale_ref[...]

    glog = _dot(xb, wgate_ref[...])
    y_mla = _dot_tn(oT_ref[0], wo_ref[...])
    merged = (jax.nn.sigmoid(glog[:, :D_MODEL]) * y_pool + jax.nn.sigmoid(glog[:, D_MODEL:]) * y_mla)
    r = DEEPNORM_ALPHA * x + _dot(merged.astype(BF16), wout_ref[...])
    h = _layer_norm(r, g1_ref[...], b1_ref[...])
    for j in range(ROW_TILES):
        h_ref[pl.ds(j, tm, stride=ROW_TILES), :] = h[:, j * LANES:(j + 1) * LANES]

    logits = _dot_nt(wrT_ref[...], h.astype(BF16)) + brT_ref[...]
    big = float(LANES)
    gl = logits[GROUP_ROW0:GROUP_ROW0 + SUBLANES]
    grow = lax.broadcasted_iota(jnp.int32, gl.shape, 0)
    gl = jnp.where(grow < N_GROUPS, gl, NEG_BIG)
    gmax = jnp.max(gl, axis=0, keepdims=True)
    g_w = 1.0 / jnp.sum(jnp.exp(gl - gmax), axis=0, keepdims=True)
    g_idx = jnp.min(jnp.where(gl == gmax, grow.astype(F32), big), axis=0, keepdims=True)
    el = logits[0:N_EXPERTS]
    erow_i = lax.broadcasted_iota(jnp.int32, el.shape, 0)
    erow = erow_i.astype(F32)
    el = jnp.where((erow_i // EXPERTS_PER_GROUP).astype(F32) == g_idx, el, NEG_BIG)
    e1max = jnp.max(el, axis=0, keepdims=True)
    i1 = jnp.min(jnp.where(el == e1max, erow, big), axis=0, keepdims=True)
    el2 = jnp.where(erow == i1, NEG_BIG, el)
    e2max = jnp.max(el2, axis=0, keepdims=True)
    i2 = jnp.min(jnp.where(el2 == e2max, erow, big), axis=0, keepdims=True)
    ratio = jnp.exp(e2max - e1max)
    gate1 = g_w / (1.0 + ratio)
    gate2 = g_w * ratio / (1.0 + ratio)

    hit1 = erow == i1
    hit2 = erow == i2
    onehot = jnp.where(jnp.logical_or(hit1, hit2), 1.0, 0.0)
    base = base_ref[:, 0:1]
    before = _dot(onehot.astype(BF16), triu_ref[...]) + base
    rank1 = jnp.sum(jnp.where(hit1, before, 0.0), axis=0, keepdims=True)
    rank2 = jnp.sum(jnp.where(hit2, before, 0.0), axis=0, keepdims=True)
    new_base = base + jnp.sum(onehot, axis=1, keepdims=True)
    base_ref[...] = jnp.broadcast_to(new_base, base_ref.shape)
    cnt_ref[...] = jnp.broadcast_to(new_base, cnt_ref.shape)

    row = lax.broadcasted_iota(jnp.int32, (LANES, tm), 0)
    routeT = jnp.where(row == 0, i1, 0.0)
    routeT = jnp.where(row == 1, i2, routeT)
    routeT = jnp.where(row == 2, gate1, routeT)
    routeT = jnp.where(row == 3, gate2, routeT)
    routeT = jnp.where(row == 4, rank1, routeT)
    routeT = jnp.where(row == 5, rank2, routeT)
    routeT_ref[...] = routeT[:SUBLANES]
    route_ref[...] = routeT.T


def _gather_row(idx_ref, r, src_hbm, dst_ref, slot, sem, priority=0):
    src0 = pl.multiple_of(idx_ref[0, 0, r] * ROW_TILES, ROW_TILES)
    pltpu.make_async_copy(src_hbm.at[pl.ds(src0, ROW_TILES), :],
                          dst_ref.at[slot, :, r, :],
                          sem.at[slot]).start(priority=priority)


def _gather_rows_loop(idx_ref, src_hbm, dst_ref, slot, sem, n_rows, unroll=8):
    def body(c, _):
        for u in range(unroll):
            _gather_row(idx_ref, c * unroll + u, src_hbm, dst_ref, slot, sem)
        return 0
    lax.fori_loop(0, n_rows // unroll, body, 0)


def _gather_rows_inline(idx_ref, src_hbm, dst_ref, slot, sem, n_rows):
    for r in range(n_rows):
        _gather_row(idx_ref, r, src_hbm, dst_ref, slot, sem, priority=r % 2)


def _wait_rows(dst_ref, slot, sem):
    pltpu.make_async_copy(dst_ref.at[slot], dst_ref.at[slot], sem.at[slot]).wait()


def _expert_kernel(be_ref, nused_ref, tok_ref, tok1_ref, tok2_ref, h_hbm, wg_ref, wu_ref, wd_ref,
                   y_ref, xbuf, xs_ref, wgb, wub, wdb, sem):
    i = pl.program_id(0)
    n_used = nused_ref[0]
    slot = lax.rem(i, GATHER_SLOTS)
    tb = TILE_EXPERT

    @pl.when(i == 0)
    def _():
        _gather_rows_loop(tok_ref, h_hbm, xbuf, 0, sem, tb)
        _gather_rows_loop(tok1_ref, h_hbm, xbuf, 1, sem, tb)

    changed = jnp.logical_or(i == 0, be_ref[i] != be_ref[jnp.maximum(i - 1, 0)])

    @pl.when(jnp.logical_and(changed, i < n_used))
    def _():
        wgb[...] = wg_ref[0].astype(BF16)
        wub[...] = wu_ref[0].astype(BF16)
        wdb[...] = wd_ref[0].astype(BF16)

    @pl.when(jnp.logical_and(i >= n_used, i < n_used + GATHER_AHEAD))
    def _():
        _wait_rows(xbuf, slot, sem)

    @pl.when(i < n_used)
    def _():
        _wait_rows(xbuf, slot, sem)
        for j in range(ROW_TILES):
            xs_ref[:, j * LANES:(j + 1) * LANES] = xbuf[slot, j].astype(BF16)
        _gather_rows_inline(tok2_ref, h_hbm, xbuf, lax.rem(i + GATHER_AHEAD, GATHER_SLOTS), sem, tb)
        xs = xs_ref[...]
        gate = _dot(xs, wgb[...])
        up = _dot(xs, wub[...])
        hid = (jax.nn.silu(gate) * up).astype(BF16)
        y = _dot(hid, wdb[...])
        for j in range(ROW_TILES):
            y_ref[pl.ds(j, tb, stride=ROW_TILES), :] = y[:, j * LANES:(j + 1) * LANES]

    @pl.when(i >= n_used)
    def _():
        y_ref[...] = jnp.zeros_like(y_ref)


def _combine_kernel(dst_ref, dst1_ref, dst2_ref, h_ref, route_ref, y_hbm, g2_ref, b2_ref, o_ref, gbuf, sem):
    i = pl.program_id(0)
    n = pl.num_programs(0)
    slot = lax.rem(i, GATHER_SLOTS)
    tf = TILE_COMBINE

    @pl.when(i == 0)
    def _():
        _gather_rows_loop(dst_ref, y_hbm, gbuf, 0, sem, TOP_K * tf)
        _gather_rows_loop(dst1_ref, y_hbm, gbuf, 1, sem, TOP_K * tf)

    _wait_rows(gbuf, slot, sem)
    route = route_ref[...]
    gate1 = route[:, 2:3]
    gate2 = route[:, 3:4]
    parts = []
    for j in range(ROW_TILES):
        hj = h_ref[pl.ds(j, tf, stride=ROW_TILES), :]
        y1 = gbuf[slot, j, 0:tf, :]
        y2 = gbuf[slot, j, tf:TOP_K * tf, :]
        parts.append(DEEPNORM_ALPHA * hj + (gate1 * y1 + gate2 * y2))
    z = jnp.concatenate(parts, axis=-1)
    o_ref[...] = _layer_norm(z, g2_ref[...], b2_ref[...])
    _gather_rows_inline(dst2_ref, y_hbm, gbuf, lax.rem(i + GATHER_AHEAD, GATHER_SLOTS), sem, TOP_K * tf)

    @pl.when(i == n - 1)
    def _():
        for ahead in range(1, GATHER_AHEAD + 1):
            _wait_rows(gbuf, lax.rem(i + ahead, GATHER_SLOTS), sem)


def _rope_tables(seq):
    f32 = np.float32
    inv = (f32(1.0) / (f32(ROPE_THETA) ** (np.arange(0, QK_ROPE_DIM, 2, dtype=f32) / f32(QK_ROPE_DIM)))).astype(f32)
    ang = (np.arange(seq, dtype=f32)[:, None] * inv[None, :]).astype(f32)
    cos, sin = np.cos(ang.astype(np.float64)).astype(f32), np.sin(ang.astype(np.float64)).astype(f32)
    zeros = np.zeros((seq, QK_NOPE_DIM), f32)
    pad = np.zeros((seq, HEAD_PAD - QK_NOPE_DIM - QK_ROPE_DIM), f32)
    z16 = np.zeros((seq, HALF_ROPE), f32)
    rot_a = np.concatenate([zeros, cos, cos, pad], axis=1)
    rot_m = np.concatenate([zeros, -sin, z16, pad], axis=1)
    rot_p = np.concatenate([zeros, z16, sin, pad], axis=1)
    return tuple(jnp.asarray(t) for t in (np.ascontiguousarray(cos.T), np.ascontiguousarray(sin.T),
                                          rot_a, rot_m, rot_p))


def _scatter_rows_sparsecore(x, indices, n_out):
    n, width = x.shape
    mesh = plsc.VectorSubcoreMesh(core_axis_name="core", subcore_axis_name="subcore")
    per_core = n // SC_SCATTER_WINDOW // mesh.num_cores
    assert per_core * SC_SCATTER_WINDOW * mesh.num_cores == n

    @functools.partial(pl.kernel, out_type=jax.ShapeDtypeStruct((n_out, width), x.dtype), mesh=mesh,
                       scratch_types=[])
    def scatter(x_hbm, i_hbm, o_hbm):
        first = lax.axis_index("core") * per_core

        def body(x_vmem, i_vmem):
            pltpu.sync_copy(x_vmem, o_hbm.at[i_vmem.at[0]])

        pltpu.emit_pipeline(
            body,
            grid=(per_core,),
            in_specs=[pl.BlockSpec((SC_SCATTER_WINDOW, width), index_map=lambda i: (first + i, 0)),
                      pl.BlockSpec((1, SC_SCATTER_WINDOW), index_map=lambda i: (0, first + i))],
            out_specs=[],
            core_axis_name="subcore",
            dimension_semantics=(pltpu.PARALLEL,),
        )(x_hbm, i_hbm)

    return scatter(x, indices.reshape(1, n))


def _full(shape):
    return pl.BlockSpec(shape, lambda *_: (0,) * len(shape))


def _params(sem):
    return pltpu.CompilerParams(dimension_semantics=sem, vmem_limit_bytes=VMEM_LIMIT)


def kernel(x, w_in, pool_mix_w, pool_scale, q_norm_g, w_uq, kv_norm_g, w_ukv, w_mla_o, w_out, ln1_g, ln1_b,
           w_router_group, b_router_group, w_router_expert, b_router_expert, w_gate, w_up, w_down, ln2_g, ln2_b):
    B, S, D = x.shape
    assert D == D_MODEL and w_in.shape[0] == DEPTH == 1
    assert S % TILE_LATENT == 0 and S % TILE_Q == 0 and S % TILE_MIX == 0 and TILE_Q % TILE_K == 0
    N = B * S
    assert N % TILE_COMBINE == 0
    H = N_HEADS

    w = w_in[0]
    o1 = POOL_WIDTH
    o2 = o1 + Q_LORA_RANK
    o3 = o2 + KV_LORA_RANK
    o4 = o3 + QK_ROPE_DIM
    w_pool = w[:, :o1].astype(BF16)
    kpe_cols = jnp.pad(w[:, o3:o4], ((0, 0), (QK_NOPE_DIM, HEAD_PAD - QK_NOPE_DIM - QK_ROPE_DIM)))
    w_lat = jnp.concatenate([w[:, o1:o3], kpe_cols], axis=1).astype(BF16)
    w_gates = w[:, o4:].astype(BF16)
    qd = QK_NOPE_DIM + QK_ROPE_DIM
    wuq = jnp.pad(w_uq[0].reshape(Q_LORA_RANK, H, qd), ((0, 0), (0, 0), (0, HEAD_PAD - qd)))
    wuqT = wuq.reshape(Q_LORA_RANK, H * HEAD_PAD).T.astype(BF16)
    wukv = w_ukv[0].reshape(KV_LORA_RANK, H, QK_NOPE_DIM + V_HEAD_DIM)
    wuk = jnp.pad(wukv[:, :, :QK_NOPE_DIM], ((0, 0), (0, 0), (0, HEAD_PAD - QK_NOPE_DIM)))
    wuk = wuk.reshape(KV_LORA_RANK, H * HEAD_PAD).astype(BF16)
    wuv = jnp.pad(wukv[:, :, QK_NOPE_DIM:], ((0, 0), (0, 0), (0, V_ROWS - V_HEAD_DIM)))
    wuvT = wuv.reshape(KV_LORA_RANK, H * V_ROWS).T.astype(BF16)
    w_rT = jnp.zeros((LANES, D), F32)
    w_rT = w_rT.at[:N_EXPERTS].set(w_router_expert[0].T).at[GROUP_ROW0:GROUP_ROW0 + N_GROUPS].set(
        w_router_group[0].T).astype(BF16)
    b_rT = jnp.zeros((LANES, 1), F32)
    b_rT = b_rT.at[:N_EXPERTS, 0].set(b_router_expert[0]).at[GROUP_ROW0:GROUP_ROW0 + N_GROUPS, 0].set(
        b_router_group[0])
    cosT, sinT, rot_a, rot_m, rot_p = _rope_tables(S)
    q_scale = (QK_NOPE_DIM + QK_ROPE_DIM) ** -0.5 * math.log2(math.e)
    x2 = x.reshape(N, D)

    ta = TILE_LATENT
    nsa = S // ta
    qT, k, vT = pl.pallas_call(
        functools.partial(_latent_kernel, q_scale=q_scale),
        grid=(B, nsa),
        in_specs=[
            pl.BlockSpec((ta, D), lambda b, s: (b * nsa + s, 0)),
            _full(w_lat.shape), _full((1, Q_LORA_RANK)), _full((1, KV_LORA_RANK)),
            _full(wuqT.shape), _full(wuk.shape), _full(wuvT.shape),
            pl.BlockSpec((HALF_ROPE, ta), lambda b, s: (0, s)),
            pl.BlockSpec((HALF_ROPE, ta), lambda b, s: (0, s)),
            pl.BlockSpec((ta, LANES), lambda b, s: (s, 0)),
            pl.BlockSpec((ta, LANES), lambda b, s: (s, 0)),
            pl.BlockSpec((ta, LANES), lambda b, s: (s, 0)),
        ],
        out_specs=[
            pl.BlockSpec((1, H * HEAD_PAD, ta), lambda b, s: (b, 0, s)),
            pl.BlockSpec((1, ta, H * HEAD_PAD), lambda b, s: (b, s, 0)),
            pl.BlockSpec((1, H * V_ROWS, ta), lambda b, s: (b, 0, s)),
        ],
        out_shape=[
            jax.ShapeDtypeStruct((B, H * HEAD_PAD, S), BF16),
            jax.ShapeDtypeStruct((B, S, H * HEAD_PAD), BF16),
            jax.ShapeDtypeStruct((B, H * V_ROWS, S), BF16),
        ],
        compiler_params=_params(("parallel", "parallel")),
    )(x2, w_lat, q_norm_g[0][None], kv_norm_g[0][None], wuqT, wuk, wuvT, cosT, sinT, rot_a, rot_m, rot_p)

    tq = TILE_Q
    assert tq == 2 * TILE_K
    key_chunk = jnp.arange(tq)[:, None] // CHUNK
    q_chunk = jnp.arange(tq)[None, :] // CHUNK
    mask_bias = jnp.where(key_chunk <= q_chunk, 0.0, NEG_BIG).astype(F32).reshape(2, TILE_K, tq)
    hp = ATTN_HEADS_PER_STEP
    assert H % hp == 0
    oT = pl.pallas_call(
        _attn_kernel,
        grid=(B, H // hp, S // tq),
        in_specs=[
            pl.BlockSpec((1, hp * HEAD_PAD, tq), lambda b, h, q: (b, h, q)),
            pl.BlockSpec((1, S, hp * HEAD_PAD), lambda b, h, q: (b, 0, h)),
            pl.BlockSpec((1, hp * V_ROWS, S), lambda b, h, q: (b, h, 0)),
            _full(mask_bias.shape),
        ],
        out_specs=pl.BlockSpec((1, hp * V_HEAD_DIM, tq), lambda b, h, q: (b, h, q)),
        out_shape=jax.ShapeDtypeStruct((B, H * V_HEAD_DIM, S), BF16),
        scratch_shapes=[pltpu.VMEM((TILE_K, tq), F32)] * (2 * hp),
        compiler_params=_params(("parallel", "parallel", "arbitrary")),
    )(qT, k, vT, mask_bias)

    tm = TILE_MIX
    nsm = S // tm
    triu = (jnp.arange(tm)[:, None] < jnp.arange(tm)[None, :]).astype(BF16)
    h_tok, route, routeT, counts = pl.pallas_call(
        _mix_kernel,
        grid=(B, nsm),
        in_specs=[
            pl.BlockSpec((tm, D), lambda b, s: (b * nsm + s, 0)),
            pl.BlockSpec((1, H * V_HEAD_DIM, tm), lambda b, s: (b, 0, s)),
            _full(w_pool.shape), _full(w_gates.shape), _full(pool_mix_w.shape[1:]), _full((1, D)),
            _full(w_mla_o.shape[1:]), _full(w_out.shape[1:]), _full((1, D)), _full((1, D)),
            _full(w_rT.shape), _full(b_rT.shape), _full(triu.shape),
        ],
        out_specs=[
            pl.BlockSpec((tm * ROW_TILES, LANES), lambda b, s: (b * nsm + s, 0)),
            pl.BlockSpec((tm, LANES), lambda b, s: (b * nsm + s, 0)),
            pl.BlockSpec((SUBLANES, tm), lambda b, s: (0, b * nsm + s)),
            _full((N_EXPERTS, LANES)),
        ],
        out_shape=[
            jax.ShapeDtypeStruct((N * ROW_TILES, LANES), F32),
            jax.ShapeDtypeStruct((N, LANES), F32),
            jax.ShapeDtypeStruct((SUBLANES, N), F32),
            jax.ShapeDtypeStruct((N_EXPERTS, LANES), F32),
        ],
        scratch_shapes=[pltpu.VMEM((tm + POOL_HALO, POOL_WIDTH), F32), pltpu.VMEM((N_EXPERTS, LANES), F32)],
        compiler_params=_params(("arbitrary", "arbitrary")),
    )(x2, oT, w_pool, w_gates, pool_mix_w[0].astype(BF16), pool_scale[0][None], w_mla_o[0].astype(BF16),
      w_out[0].astype(BF16), ln1_g[0][None], ln1_b[0][None], w_rT, b_rT, triu)

    tb = TILE_EXPERT
    A = N * TOP_K
    n_blocks = -(-(A + N_EXPERTS * (tb - 1)) // tb) + GATHER_AHEAD
    R = n_blocks * tb
    e_idx = routeT[0:2].astype(jnp.int32)
    rank = routeT[4:6].astype(jnp.int32)
    cnt = counts[:, 0].astype(jnp.int32)
    padded = ((cnt + tb - 1) // tb) * tb
    pad_end = jnp.cumsum(padded)
    pad_start = pad_end - padded
    is_e = e_idx[:, None, :] == jnp.arange(N_EXPERTS, dtype=jnp.int32)[None, :, None]
    dest = jnp.sum(jnp.where(is_e, pad_start[None, :, None], 0), axis=1) + rank
    block_start = jnp.arange(n_blocks, dtype=jnp.int32) * tb
    block_e = jnp.minimum(jnp.sum(pad_end[None, :] <= block_start[:, None], axis=1), N_EXPERTS - 1).astype(jnp.int32)
    n_used = (pad_end[-1] // tb).astype(jnp.int32)[None]
    tok = jnp.broadcast_to(jnp.arange(N, dtype=jnp.int32)[None, :, None], (TOP_K, N, LANES)).reshape(A, LANES)
    tok_rows = _scatter_rows_sparsecore(tok, dest.reshape(A), R)[:, 0].reshape(n_blocks, tb)
    block_is_e = block_e[:, None] == jnp.arange(N_EXPERTS, dtype=jnp.int32)[None, :]
    block_pad_start = jnp.sum(jnp.where(block_is_e, pad_start[None, :], 0), axis=1)
    block_cnt = jnp.sum(jnp.where(block_is_e, cnt[None, :], 0), axis=1)
    row_in_expert = block_start[:, None] + jnp.arange(tb, dtype=jnp.int32)[None, :] - block_pad_start[:, None]
    row_valid = jnp.logical_and(row_in_expert < block_cnt[:, None],
                                jnp.arange(n_blocks, dtype=jnp.int32)[:, None] < n_used[0])
    any_tok = (block_start[:, None] + jnp.arange(tb, dtype=jnp.int32)[None, :]) % N
    row_tok3 = jnp.where(row_valid, tok_rows, any_tok).reshape(n_blocks, 1, tb)

    y_rows = pl.pallas_call(
        _expert_kernel,
        grid_spec=pltpu.PrefetchScalarGridSpec(
            num_scalar_prefetch=2,
            grid=(n_blocks,),
            in_specs=[
                pl.BlockSpec((1, 1, tb), lambda i, be, nu: (i, 0, 0), memory_space=pltpu.SMEM),
                pl.BlockSpec((1, 1, tb), lambda i, be, nu: (jnp.minimum(i + 1, n_blocks - 1), 0, 0),
                             memory_space=pltpu.SMEM),
                pl.BlockSpec((1, 1, tb), lambda i, be, nu: (jnp.minimum(i + GATHER_AHEAD, n_blocks - 1), 0, 0),
                             memory_space=pltpu.SMEM),
                pl.BlockSpec(memory_space=pl.ANY),
                pl.BlockSpec((1, D, D_EXPERT), lambda i, be, nu: (be[i], 0, 0)),
                pl.BlockSpec((1, D, D_EXPERT), lambda i, be, nu: (be[i], 0, 0)),
                pl.BlockSpec((1, D_EXPERT, D), lambda i, be, nu: (be[i], 0, 0)),
            ],
            out_specs=pl.BlockSpec((tb * ROW_TILES, LANES), lambda i, be, nu: (i, 0)),
            scratch_shapes=[
                pltpu.VMEM((GATHER_SLOTS, ROW_TILES, tb, LANES), F32), pltpu.VMEM((tb, D), BF16),
                pltpu.VMEM((D, D_EXPERT), BF16), pltpu.VMEM((D, D_EXPERT), BF16), pltpu.VMEM((D_EXPERT, D), BF16),
                pltpu.SemaphoreType.DMA((GATHER_SLOTS,)),
            ],
        ),
        out_shape=jax.ShapeDtypeStruct((R * ROW_TILES, LANES), F32),
        compiler_params=_params(("arbitrary",)),
    )(block_e, n_used, row_tok3, row_tok3, row_tok3, h_tok, w_gate[0], w_up[0], w_down[0])

    tf = TILE_COMBINE
    nf = N // tf
    dest3 = dest.reshape(TOP_K, nf, tf).transpose(1, 0, 2).reshape(nf, 1, TOP_K * tf)
    out = pl.pallas_call(
        _combine_kernel,
        grid=(nf,),
        in_specs=[
            pl.BlockSpec((1, 1, TOP_K * tf), lambda i: (i, 0, 0), memory_space=pltpu.SMEM),
            pl.BlockSpec((1, 1, TOP_K * tf), lambda i: (jnp.minimum(i + 1, nf - 1), 0, 0), memory_space=pltpu.SMEM),
            pl.BlockSpec((1, 1, TOP_K * tf), lambda i: (jnp.minimum(i + GATHER_AHEAD, nf - 1), 0, 0),
                         memory_space=pltpu.SMEM),
            pl.BlockSpec((tf * ROW_TILES, LANES), lambda i: (i, 0)),
            pl.BlockSpec((tf, LANES), lambda i: (i, 0)),
            pl.BlockSpec(memory_space=pl.ANY),
            _full((1, D)), _full((1, D)),
        ],
        out_specs=pl.BlockSpec((tf, D), lambda i: (i, 0)),
        out_shape=jax.ShapeDtypeStruct((N, D), F32),
        scratch_shapes=[pltpu.VMEM((GATHER_SLOTS, ROW_TILES, TOP_K * tf, LANES), F32),
                        pltpu.SemaphoreType.DMA((GATHER_SLOTS,))],
        compiler_params=_params(("arbitrary",)),
    )(dest3, dest3, dest3, h_tok, route, y_rows, ln2_g[0][None], ln2_b[0][None])
    return out.reshape(B, S, D)
```

```python
import functools
import math

import jax
import jax.numpy as jnp
import numpy as np
from jax import lax
from jax.experimental import pallas as pl
from jax.experimental.pallas import tpu as pltpu
from jax.experimental.pallas import tpu_sc as plsc

D_MODEL = 1024
CHUNK = 64
POOL_WINDOWS = (2, 4, 8, 16)
POOL_GROUPS = len(POOL_WINDOWS)
POOL_WIDTH = D_MODEL // 2
POOL_GROUP_DIM = POOL_WIDTH // POOL_GROUPS
POOL_OUT_GROUP_DIM = D_MODEL // POOL_GROUPS
POOL_HALO = 16
N_HEADS = 8
QK_NOPE_DIM = D_MODEL // 16
QK_ROPE_DIM = D_MODEL // 32
HALF_ROPE = QK_ROPE_DIM // 2
V_HEAD_DIM = D_MODEL // 16
V_ROWS = V_HEAD_DIM + 16
Q_LORA_RANK = 3 * D_MODEL // 8
KV_LORA_RANK = D_MODEL // 4
ROPE_THETA = 10000.0
N_GROUPS = 4
EXPERTS_PER_GROUP = 8
N_EXPERTS = N_GROUPS * EXPERTS_PER_GROUP
TOP_K = 2
D_EXPERT = D_MODEL // 2
NORM_EPS = 1e-5
DEPTH = 1
DEEPNORM_ALPHA = (2.0 * DEPTH) ** 0.25

LANES = 128
SUBLANES = 8
HEAD_PAD = LANES
ROW_TILES = D_MODEL // LANES
GROUP_ROW0 = 64
NEG_BIG = -1e30
VMEM_LIMIT = 56 * 1024 * 1024

TILE_LATENT = 1024
TILE_Q = 512
TILE_K = 256
ATTN_HEADS_PER_STEP = 4
ATTN_PAIRS_PER_TRIP = 2
TILE_MIX = 512
GATHER_AHEAD = 2
GATHER_SLOTS = GATHER_AHEAD + 1
TILE_EXPERT = 512
SC_SCATTER_WINDOW = 128
TILE_COMBINE = 512

F32 = jnp.float32
BF16 = jnp.bfloat16


def _dot(a, b):
    return jnp.dot(a, b, preferred_element_type=F32)


def _dot_nt(a, b):
    return lax.dot_general(a, b, (((1,), (1,)), ((), ())), preferred_element_type=F32)


def _dot_tn(a, b):
    return lax.dot_general(a, b, (((0,), (0,)), ((), ())), preferred_element_type=F32)


def _rms(v, g):
    ms = jnp.mean(jnp.square(v), axis=-1, keepdims=True)
    return v * lax.rsqrt(ms + NORM_EPS) * g


def _layer_norm(v, g, b):
    mu = jnp.mean(v, axis=-1, keepdims=True)
    c = v - mu
    var = jnp.mean(jnp.square(c), axis=-1, keepdims=True)
    return c * lax.rsqrt(var + NORM_EPS) * g + b


def _latent_kernel(x_ref, wlat_ref, gq_ref, gkv_ref, wuqT_ref, wuk_ref, wuvT_ref,
                   cosT_ref, sinT_ref, ra_ref, rm_ref, rp_ref,
                   qT_ref, k_ref, vT_ref, *, q_scale):
    xb = x_ref[...].astype(BF16)
    lat = _dot(xb, wlat_ref[...])
    c_q = lat[:, :Q_LORA_RANK]
    c_kv = lat[:, Q_LORA_RANK:Q_LORA_RANK + KV_LORA_RANK]
    kpe = lat[:, Q_LORA_RANK + KV_LORA_RANK:]
    qn = _rms(c_q, gq_ref[...]).astype(BF16)
    kvn = _rms(c_kv, gkv_ref[...]).astype(BF16)

    qT = _dot_nt(wuqT_ref[...], qn) * q_scale
    cosT = cosT_ref[...]
    sinT = sinT_ref[...]
    for h in range(N_HEADS):
        r0 = h * HEAD_PAD
        x1 = qT[r0 + QK_NOPE_DIM:r0 + QK_NOPE_DIM + HALF_ROPE]
        x2 = qT[r0 + QK_NOPE_DIM + HALF_ROPE:r0 + QK_NOPE_DIM + QK_ROPE_DIM]
        qT_ref[0, r0:r0 + QK_NOPE_DIM, :] = qT[r0:r0 + QK_NOPE_DIM].astype(BF16)
        qT_ref[0, r0 + QK_NOPE_DIM:r0 + QK_NOPE_DIM + HALF_ROPE, :] = (x1 * cosT - x2 * sinT).astype(BF16)
        qT_ref[0, r0 + QK_NOPE_DIM + HALF_ROPE:r0 + QK_NOPE_DIM + QK_ROPE_DIM, :] = (
            x1 * sinT + x2 * cosT).astype(BF16)
        qT_ref[0, r0 + QK_NOPE_DIM + QK_ROPE_DIM:r0 + HEAD_PAD, :] = jnp.zeros(
            (HEAD_PAD - QK_NOPE_DIM - QK_ROPE_DIM, qT.shape[1]), BF16)

    kpe_rot = (kpe * ra_ref[...] + pltpu.roll(kpe, LANES - HALF_ROPE, 1) * rm_ref[...]
               + pltpu.roll(kpe, HALF_ROPE, 1) * rp_ref[...])
    k = _dot(kvn, wuk_ref[...])
    for h in range(N_HEADS):
        k_ref[0, :, h * HEAD_PAD:(h + 1) * HEAD_PAD] = (k[:, h * HEAD_PAD:(h + 1) * HEAD_PAD] + kpe_rot).astype(BF16)
    vT = _dot_nt(wuvT_ref[...], kvn)
    row = lax.broadcasted_iota(jnp.int32, vT.shape, 0) % V_ROWS
    vT_ref[0] = jnp.where(row == V_HEAD_DIM, 1.0, vT).astype(BF16)


def _attn_kernel(qT_ref, k_ref, vT_ref, bias_ref, oT_ref, *bufs):
    qi = pl.program_id(2)
    tq = qT_ref.shape[2]
    heads = range(ATTN_HEADS_PER_STEP)
    s0, s1 = (bufs[i * ATTN_HEADS_PER_STEP:(i + 1) * ATTN_HEADS_PER_STEP] for i in range(2))

    def scores(h, j, c0=0):
        k0 = pl.multiple_of(j * TILE_K, TILE_K)
        return _dot(k_ref[0, pl.ds(k0, TILE_K), h * HEAD_PAD:(h + 1) * HEAD_PAD],
                    qT_ref[0, h * HEAD_PAD:(h + 1) * HEAD_PAD, c0:])

    def col_max(s):
        while s.shape[0] > SUBLANES:
            half = s.shape[0] // 2
            s = jnp.maximum(s[:half], s[half:])
        return jnp.max(s, axis=0, keepdims=True)

    def step(h, j, s_cur, s_nxt, carry, bias=None, c0=0, c0_nxt=0):
        m_all, acc_all, cmax = carry
        cmax_nxt = None
        if s_nxt is not None:
            s_new = scores(h, j + 1, c0_nxt)
            s_nxt[h][:, c0_nxt:] = s_new
            cmax_nxt = col_max(s_new)
        m, acc = m_all[:, c0:], acc_all[:, c0:]
        s = s_cur[h][:, c0:]
        if bias is not None:
            s = s + bias[:, c0:]
            cmax = col_max(s)
        m_new = jnp.maximum(m, cmax)
        p = jnp.exp2(s - m_new).astype(BF16)
        k0 = pl.multiple_of(j * TILE_K, TILE_K)
        pv = _dot(vT_ref[0, h * V_ROWS:(h + 1) * V_ROWS, pl.ds(k0, TILE_K)], p)
        acc = jnp.exp2(m - m_new) * acc + pv
        if c0:
            m_new = jnp.concatenate([m_all[:, :c0], m_new], axis=1)
            acc = jnp.concatenate([acc_all[:, :c0], acc], axis=1)
        return m_new, acc, cmax_nxt

    carry = []
    for h in heads:
        s_first = scores(h, 0)
        s0[h][...] = s_first
        carry.append((jnp.full((1, tq), NEG_BIG, F32), jnp.zeros((V_ROWS, tq), F32), col_max(s_first)))
    carry = tuple(carry)

    def pair(t, c):
        c = tuple(step(h, 2 * t, s0, s1, c[h]) for h in heads)
        return tuple(step(h, 2 * t + 1, s1, s0, c[h]) for h in heads)

    def trip(u, c):
        for i in range(ATTN_PAIRS_PER_TRIP):
            c = pair(ATTN_PAIRS_PER_TRIP * u + i, c)
        return c

    carry = lax.fori_loop(0, qi // ATTN_PAIRS_PER_TRIP, trip, carry)
    done = (qi // ATTN_PAIRS_PER_TRIP) * ATTN_PAIRS_PER_TRIP
    for i in range(ATTN_PAIRS_PER_TRIP - 1):
        carry = lax.cond(done + i < qi, functools.partial(pair, done + i), lambda c: c, carry)
    jd = 2 * qi
    carry = tuple(step(h, jd, s0, s1, carry[h], bias=bias_ref.at[0], c0_nxt=TILE_K) for h in heads)
    carry = tuple(step(h, jd + 1, s1, None, carry[h], bias=bias_ref.at[1], c0=TILE_K) for h in heads)
    for h in heads:
        acc = carry[h][1]
        oT_ref[0, h * V_HEAD_DIM:(h + 1) * V_HEAD_DIM, :] = (
            acc[:V_HEAD_DIM] / acc[V_HEAD_DIM:V_HEAD_DIM + 1]).astype(BF16)


def _mix_kernel(x_ref, oT_ref, wpool_ref, wgate_ref, mixw_ref, pscale_ref, wo_ref, wout_ref,
                g1_ref, b1_ref, wrT_ref, brT_ref, triu_ref,
                h_ref, route_ref, routeT_ref, cnt_ref, ext_ref, base_ref):
    b = pl.program_id(0)
    si = pl.program_id(1)
    tm = x_ref.shape[0]

    @pl.when(jnp.logical_and(b == 0, si == 0))
    def _():
        base_ref[...] = jnp.zeros_like(base_ref)

    @pl.when(si == 0)
    def _():
        ext_ref[0:POOL_HALO, :] = jnp.zeros((POOL_HALO, POOL_WIDTH), F32)

    x = x_ref[...]
    xb = x.astype(BF16)
    u = _dot(xb, wpool_ref[...])
    ext_ref[POOL_HALO:POOL_HALO + tm, :] = u

    pos1 = si * tm + lax.broadcasted_iota(jnp.int32, (tm, POOL_GROUP_DIM), 0) + 1
    y_parts = []
    for g, win in enumerate(POOL_WINDOWS):
        c0 = g * POOL_GROUP_DIM
        ws = u[:, c0:c0 + POOL_GROUP_DIM]
        for kk in range(1, win):
            ws = ws + ext_ref[POOL_HALO - kk:POOL_HALO - kk + tm, c0:c0 + POOL_GROUP_DIM]
        count = jnp.minimum(pos1, win).astype(F32)
        pooled = ws / count - u[:, c0:c0 + POOL_GROUP_DIM]
        y_parts.append(_dot(pooled.astype(BF16), mixw_ref[g]))
    ext_ref[0:POOL_HALO, :] = ext_ref[tm:tm + POOL_HALO, :]
    y_pool = jnp.concatenate(y_parts, axis=-1) * pscale_ref[...]

    glog = _dot(xb, wgate_ref[...])
    y_mla = _dot_tn(oT_ref[0], wo_ref[...])
    merged = (jax.nn.sigmoid(glog[:, :D_MODEL]) * y_pool + jax.nn.sigmoid(glog[:, D_MODEL:]) * y_mla)
    r = DEEPNORM_ALPHA * x + _dot(merged.astype(BF16), wout_ref[...])
    h = _layer_norm(r, g1_ref[...], b1_ref[...])
    for j in range(ROW_TILES):
        h_ref[pl.ds(j, tm, stride=ROW_TILES), :] = h[:, j * LANES:(j + 1) * LANES]

    logits = _dot_nt(wrT_ref[...], h.astype(BF16)) + brT_ref[...]
    big = float(LANES)
    gl = logits[GROUP_ROW0:GROUP_ROW0 + SUBLANES]
    grow = lax.broadcasted_iota(jnp.int32, gl.shape, 0)
    gl = jnp.where(grow < N_GROUPS, gl, NEG_BIG)
    gmax = jnp.max(gl, axis=0, keepdims=True)
    g_w = 1.0 / jnp.sum(jnp.exp(gl - gmax), axis=0, keepdims=True)
    g_idx = jnp.min(jnp.where(gl == gmax, grow.astype(F32), big), axis=0, keepdims=True)
    el = logits[0:N_EXPERTS]
    erow_i = lax.broadcasted_iota(jnp.int32, el.shape, 0)
    erow = erow_i.astype(F32)
    el = jnp.where((erow_i // EXPERTS_PER_GROUP).astype(F32) == g_idx, el, NEG_BIG)
    e1max = jnp.max(el, axis=0, keepdims=True)
    i1 = jnp.min(jnp.where(el == e1max, erow, big), axis=0, keepdims=True)
    el2 = jnp.where(erow == i1, NEG_BIG, el)
    e2max = jnp.max(el2, axis=0, keepdims=True)
    i2 = jnp.min(jnp.where(el2 == e2max, erow, big), axis=0, keepdims=True)
    ratio = jnp.exp(e2max - e1max)
    gate1 = g_w / (1.0 + ratio)
    gate2 = g_w * ratio / (1.0 + ratio)

    hit1 = erow == i1
    hit2 = erow == i2
    onehot = jnp.where(jnp.logical_or(hit1, hit2), 1.0, 0.0)
    base = base_ref[:, 0:1]
    before = _dot(onehot.astype(BF16), triu_ref[...]) + base
    rank1 = jnp.sum(jnp.where(hit1, before, 0.0), axis=0, keepdims=True)
    rank2 = jnp.sum(jnp.where(hit2, before, 0.0), axis=0, keepdims=True)
    new_base = base + jnp.sum(onehot, axis=1, keepdims=True)
    base_ref[...] = jnp.broadcast_to(new_base, base_ref.shape)
    cnt_ref[...] = jnp.broadcast_to(new_base, cnt_ref.shape)

    row = lax.broadcasted_iota(jnp.int32, (LANES, tm), 0)
    routeT = jnp.where(row == 0, i1, 0.0)
    routeT = jnp.where(row == 1, i2, routeT)
    routeT = jnp.where(row == 2, gate1, routeT)
    routeT = jnp.where(row == 3, gate2, routeT)
    routeT = jnp.where(row == 4, rank1, routeT)
    routeT = jnp.where(row == 5, rank2, routeT)
    routeT_ref[...] = routeT[:SUBLANES]
    route_ref[...] = routeT.T


def _gather_row(idx_ref, r, src_hbm, dst_ref, slot, sem, priority=0):
    src0 = pl.multiple_of(idx_ref[0, 0, r] * ROW_TILES, ROW_TILES)
    pltpu.make_async_copy(src_hbm.at[pl.ds(src0, ROW_TILES), :],
                          dst_ref.at[slot, :, r, :],
                          sem.at[slot]).start(priority=priority)


def _gather_rows_loop(idx_ref, src_hbm, dst_ref, slot, sem, n_rows, unroll=8):
    def body(c, _):
        for u in range(unroll):
            _gather_row(idx_ref, c * unroll + u, src_hbm, dst_ref, slot, sem)
        return 0
    lax.fori_loop(0, n_rows // unroll, body, 0)


def _gather_rows_inline(idx_ref, src_hbm, dst_ref, slot, sem, n_rows):
    for r in range(n_rows):
        _gather_row(idx_ref, r, src_hbm, dst_ref, slot, sem, priority=r % 2)


def _wait_rows(dst_ref, slot, sem):
    pltpu.make_async_copy(dst_ref.at[slot], dst_ref.at[slot], sem.at[slot]).wait()


def _expert_kernel(be_ref, nused_ref, tok_ref, tok1_ref, tok2_ref, h_hbm, wg_ref, wu_ref, wd_ref,
                   y_ref, xbuf, xs_ref, wgb, wub, wdb, sem):
    i = pl.program_id(0)
    n_used = nused_ref[0]
    slot = lax.rem(i, GATHER_SLOTS)
    tb = TILE_EXPERT

    @pl.when(i == 0)
    def _():
        _gather_rows_loop(tok_ref, h_hbm, xbuf, 0, sem, tb)
        _gather_rows_loop(tok1_ref, h_hbm, xbuf, 1, sem, tb)

    changed = jnp.logical_or(i == 0, be_ref[i] != be_ref[jnp.maximum(i - 1, 0)])

    @pl.when(jnp.logical_and(changed, i < n_used))
    def _():
        wgb[...] = wg_ref[0].astype(BF16)
        wub[...] = wu_ref[0].astype(BF16)
        wdb[...] = wd_ref[0].astype(BF16)

    @pl.when(jnp.logical_and(i >= n_used, i < n_used + GATHER_AHEAD))
    def _():
        _wait_rows(xbuf, slot, sem)

    @pl.when(i < n_used)
    def _():
        _wait_rows(xbuf, slot, sem)
        for j in range(ROW_TILES):
            xs_ref[:, j * LANES:(j + 1) * LANES] = xbuf[slot, j].astype(BF16)
        _gather_rows_inline(tok2_ref, h_hbm, xbuf, lax.rem(i + GATHER_AHEAD, GATHER_SLOTS), sem, tb)
        xs = xs_ref[...]
        gate = _dot(xs, wgb[...])
        up = _dot(xs, wub[...])
        hid = (jax.nn.silu(gate) * up).astype(BF16)
        y = _dot(hid, wdb[...])
        for j in range(ROW_TILES):
            y_ref[pl.ds(j, tb, stride=ROW_TILES), :] = y[:, j * LANES:(j + 1) * LANES]

    @pl.when(i >= n_used)
    def _():
        y_ref[...] = jnp.zeros_like(y_ref)


def _combine_kernel(dst_ref, dst1_ref, dst2_ref, h_ref, route_ref, y_hbm, g2_ref, b2_ref, o_ref, gbuf, sem):
    i = pl.program_id(0)
    n = pl.num_programs(0)
    slot = lax.rem(i, GATHER_SLOTS)
    tf = TILE_COMBINE

    @pl.when(i == 0)
    def _():
        _gather_rows_loop(dst_ref, y_hbm, gbuf, 0, sem, TOP_K * tf)
        _gather_rows_loop(dst1_ref, y_hbm, gbuf, 1, sem, TOP_K * tf)

    _wait_rows(gbuf, slot, sem)
    route = route_ref[...]
    gate1 = route[:, 2:3]
    gate2 = route[:, 3:4]
    parts = []
    for j in range(ROW_TILES):
        hj = h_ref[pl.ds(j, tf, stride=ROW_TILES), :]
        y1 = gbuf[slot, j, 0:tf, :]
        y2 = gbuf[slot, j, tf:TOP_K * tf, :]
        parts.append(DEEPNORM_ALPHA * hj + (gate1 * y1 + gate2 * y2))
    z = jnp.concatenate(parts, axis=-1)
    o_ref[...] = _layer_norm(z, g2_ref[...], b2_ref[...])
    _gather_rows_inline(dst2_ref, y_hbm, gbuf, lax.rem(i + GATHER_AHEAD, GATHER_SLOTS), sem, TOP_K * tf)

    @pl.when(i == n - 1)
    def _():
        for ahead in range(1, GATHER_AHEAD + 1):
            _wait_rows(gbuf, lax.rem(i + ahead, GATHER_SLOTS), sem)


def _rope_tables(seq):
    f32 = np.float32
    inv = (f32(1.0) / (f32(ROPE_THETA) ** (np.arange(0, QK_ROPE_DIM, 2, dtype=f32) / f32(QK_ROPE_DIM)))).astype(f32)
    ang = (np.arange(seq, dtype=f32)[:, None] * inv[None, :]).astype(f32)
    cos, sin = np.cos(ang.astype(np.float64)).astype(f32), np.sin(ang.astype(np.float64)).astype(f32)
    zeros = np.zeros((seq, QK_NOPE_DIM), f32)
    pad = np.zeros((seq, HEAD_PAD - QK_NOPE_DIM - QK_ROPE_DIM), f32)
    z16 = np.zeros((seq, HALF_ROPE), f32)
    rot_a = np.concatenate([zeros, cos, cos, pad], axis=1)
    rot_m = np.concatenate([zeros, -sin, z16, pad], axis=1)
    rot_p = np.concatenate([zeros, z16, sin, pad], axis=1)
    return tuple(jnp.asarray(t) for t in (np.ascontiguousarray(cos.T), np.ascontiguousarray(sin.T),
                                          rot_a, rot_m, rot_p))


def _scatter_rows_sparsecore(x, indices, n_out):
    n, width = x.shape
    mesh = plsc.VectorSubcoreMesh(core_axis_name="core", subcore_axis_name="subcore")
    per_core = n // SC_SCATTER_WINDOW // mesh.num_cores
    assert per_core * SC_SCATTER_WINDOW * mesh.num_cores == n

    @functools.partial(pl.kernel, out_type=jax.ShapeDtypeStruct((n_out, width), x.dtype), mesh=mesh,
                       scratch_types=[])
    def scatter(x_hbm, i_hbm, o_hbm):
        first = lax.axis_index("core") * per_core

        def body(x_vmem, i_vmem):
            pltpu.sync_copy(x_vmem, o_hbm.at[i_vmem.at[0]])

        pltpu.emit_pipeline(
            body,
            grid=(per_core,),
            in_specs=[pl.BlockSpec((SC_SCATTER_WINDOW, width), index_map=lambda i: (first + i, 0)),
                      pl.BlockSpec((1, SC_SCATTER_WINDOW), index_map=lambda i: (0, first + i))],
            out_specs=[],
            core_axis_name="subcore",
            dimension_semantics=(pltpu.PARALLEL,),
        )(x_hbm, i_hbm)

    return scatter(x, indices.reshape(1, n))


def _full(shape):
    return pl.BlockSpec(shape, lambda *_: (0,) * len(shape))


def _params(sem):
    return pltpu.CompilerParams(dimension_semantics=sem, vmem_limit_bytes=VMEM_LIMIT)


def kernel(x, w_in, pool_mix_w, pool_scale, q_norm_g, w_uq, kv_norm_g, w_ukv, w_mla_o, w_out, ln1_g, ln1_b,
           w_router_group, b_router_group, w_router_expert, b_router_expert, w_gate, w_up, w_down, ln2_g, ln2_b):
    B, S, D = x.shape
    assert D == D_MODEL and w_in.shape[0] == DEPTH == 1
    assert S % TILE_LATENT == 0 and S % TILE_Q == 0 and S % TILE_MIX == 0 and TILE_Q % TILE_K == 0
    N = B * S
    assert N % TILE_COMBINE == 0
    H = N_HEADS

    w = w_in[0]
    o1 = POOL_WIDTH
    o2 = o1 + Q_LORA_RANK
    o3 = o2 + KV_LORA_RANK
    o4 = o3 + QK_ROPE_DIM
    w_pool = w[:, :o1].astype(BF16)
    kpe_cols = jnp.pad(w[:, o3:o4], ((0, 0), (QK_NOPE_DIM, HEAD_PAD - QK_NOPE_DIM - QK_ROPE_DIM)))
    w_lat = jnp.concatenate([w[:, o1:o3], kpe_cols], axis=1).astype(BF16)
    w_gates = w[:, o4:].astype(BF16)
    qd = QK_NOPE_DIM + QK_ROPE_DIM
    wuq = jnp.pad(w_uq[0].reshape(Q_LORA_RANK, H, qd), ((0, 0), (0, 0), (0, HEAD_PAD - qd)))
    wuqT = wuq.reshape(Q_LORA_RANK, H * HEAD_PAD).T.astype(BF16)
    wukv = w_ukv[0].reshape(KV_LORA_RANK, H, QK_NOPE_DIM + V_HEAD_DIM)
    wuk = jnp.pad(wukv[:, :, :QK_NOPE_DIM], ((0, 0), (0, 0), (0, HEAD_PAD - QK_NOPE_DIM)))
    wuk = wuk.reshape(KV_LORA_RANK, H * HEAD_PAD).astype(BF16)
    wuv = jnp.pad(wukv[:, :, QK_NOPE_DIM:], ((0, 0), (0, 0), (0, V_ROWS - V_HEAD_DIM)))
    wuvT = wuv.reshape(KV_LORA_RANK, H * V_ROWS).T.astype(BF16)
    w_rT = jnp.zeros((LANES, D), F32)
    w_rT = w_rT.at[:N_EXPERTS].set(w_router_expert[0].T).at[GROUP_ROW0:GROUP_ROW0 + N_GROUPS].set(
        w_router_group[0].T).astype(BF16)
    b_rT = jnp.zeros((LANES, 1), F32)
    b_rT = b_rT.at[:N_EXPERTS, 0].set(b_router_expert[0]).at[GROUP_ROW0:GROUP_ROW0 + N_GROUPS, 0].set(
        b_router_group[0])
    cosT, sinT, rot_a, rot_m, rot_p = _rope_tables(S)
    q_scale = (QK_NOPE_DIM + QK_ROPE_DIM) ** -0.5 * math.log2(math.e)
    x2 = x.reshape(N, D)

    ta = TILE_LATENT
    nsa = S // ta
    qT, k, vT = pl.pallas_call(
        functools.partial(_latent_kernel, q_scale=q_scale),
        grid=(B, nsa),
        in_specs=[
            pl.BlockSpec((ta, D), lambda b, s: (b * nsa + s, 0)),
            _full(w_lat.shape), _full((1, Q_LORA_RANK)), _full((1, KV_LORA_RANK)),
            _full(wuqT.shape), _full(wuk.shape), _full(wuvT.shape),
            pl.BlockSpec((HALF_ROPE, ta), lambda b, s: (0, s)),
            pl.BlockSpec((HALF_ROPE, ta), lambda b, s: (0, s)),
            pl.BlockSpec((ta, LANES), lambda b, s: (s, 0)),
            pl.BlockSpec((ta, LANES), lambda b, s: (s, 0)),
            pl.BlockSpec((ta, LANES), lambda b, s: (s, 0)),
        ],
        out_specs=[
            pl.BlockSpec((1, H * HEAD_PAD, ta), lambda b, s: (b, 0, s)),
            pl.BlockSpec((1, ta, H * HEAD_PAD), lambda b, s: (b, s, 0)),
            pl.BlockSpec((1, H * V_ROWS, ta), lambda b, s: (b, 0, s)),
        ],
        out_shape=[
            jax.ShapeDtypeStruct((B, H * HEAD_PAD, S), BF16),
            jax.ShapeDtypeStruct((B, S, H * HEAD_PAD), BF16),
            jax.ShapeDtypeStruct((B, H * V_ROWS, S), BF16),
        ],
        compiler_params=_params(("parallel", "parallel")),
    )(x2, w_lat, q_norm_g[0][None], kv_norm_g[0][None], wuqT, wuk, wuvT, cosT, sinT, rot_a, rot_m, rot_p)

    tq = TILE_Q
    assert tq == 2 * TILE_K
    key_chunk = jnp.arange(tq)[:, None] // CHUNK
    q_chunk = jnp.arange(tq)[None, :] // CHUNK
    mask_bias = jnp.where(key_chunk <= q_chunk, 0.0, NEG_BIG).astype(F32).reshape(2, TILE_K, tq)
    hp = ATTN_HEADS_PER_STEP
    assert H % hp == 0
    oT = pl.pallas_call(
        _attn_kernel,
        grid=(B, H // hp, S // tq),
        in_specs=[
            pl.BlockSpec((1, hp * HEAD_PAD, tq), lambda b, h, q: (b, h, q)),
            pl.BlockSpec((1, S, hp * HEAD_PAD), lambda b, h, q: (b, 0, h)),
            pl.BlockSpec((1, hp * V_ROWS, S), lambda b, h, q: (b, h, 0)),
            _full(mask_bias.shape),
        ],
        out_specs=pl.BlockSpec((1, hp * V_HEAD_DIM, tq), lambda b, h, q: (b, h, q)),
        out_shape=jax.ShapeDtypeStruct((B, H * V_HEAD_DIM, S), BF16),
        scratch_shapes=[pltpu.VMEM((TILE_K, tq), F32)] * (2 * hp),
        compiler_params=_params(("parallel", "parallel", "arbitrary")),
    )(qT, k, vT, mask_bias)

    tm = TILE_MIX
    nsm = S // tm
    triu = (jnp.arange(tm)[:, None] < jnp.arange(tm)[None, :]).astype(BF16)
    h_tok, route, routeT, counts = pl.pallas_call(
        _mix_kernel,
        grid=(B, nsm),
        in_specs=[
            pl.BlockSpec((tm, D), lambda b, s: (b * nsm + s, 0)),
            pl.BlockSpec((1, H * V_HEAD_DIM, tm), lambda b, s: (b, 0, s)),
            _full(w_pool.shape), _full(w_gates.shape), _full(pool_mix_w.shape[1:]), _full((1, D)),
            _full(w_mla_o.shape[1:]), _full(w_out.shape[1:]), _full((1, D)), _full((1, D)),
            _full(w_rT.shape), _full(b_rT.shape), _full(triu.shape),
        ],
        out_specs=[
            pl.BlockSpec((tm * ROW_TILES, LANES), lambda b, s: (b * nsm + s, 0)),
            pl.BlockSpec((tm, LANES), lambda b, s: (b * nsm + s, 0)),
            pl.BlockSpec((SUBLANES, tm), lambda b, s: (0, b * nsm + s)),
            _full((N_EXPERTS, LANES)),
        ],
        out_shape=[
            jax.ShapeDtypeStruct((N * ROW_TILES, LANES), F32),
            jax.ShapeDtypeStruct((N, LANES), F32),
            jax.ShapeDtypeStruct((SUBLANES, N), F32),
            jax.ShapeDtypeStruct((N_EXPERTS, LANES), F32),
        ],
        scratch_shapes=[pltpu.VMEM((tm + POOL_HALO, POOL_WIDTH), F32), pltpu.VMEM((N_EXPERTS, LANES), F32)],
        compiler_params=_params(("arbitrary", "arbitrary")),
    )(x2, oT, w_pool, w_gates, pool_mix_w[0].astype(BF16), pool_scale[0][None], w_mla_o[0].astype(BF16),
      w_out[0].astype(BF16), ln1_g[0][None], ln1_b[0][None], w_rT, b_rT, triu)

    tb = TILE_EXPERT
    A = N * TOP_K
    n_blocks = -(-(A + N_EXPERTS * (tb - 1)) // tb) + GATHER_AHEAD
    R = n_blocks * tb
    e_idx = routeT[0:2].astype(jnp.int32)
    rank = routeT[4:6].astype(jnp.int32)
    cnt = counts[:, 0].astype(jnp.int32)
    padded = ((cnt + tb - 1) // tb) * tb
    pad_end = jnp.cumsum(padded)
    pad_start = pad_end - padded
    is_e = e_idx[:, None, :] == jnp.arange(N_EXPERTS, dtype=jnp.int32)[None, :, None]
    dest = jnp.sum(jnp.where(is_e, pad_start[None, :, None], 0), axis=1) + rank
    block_start = jnp.arange(n_blocks, dtype=jnp.int32) * tb
    block_e = jnp.minimum(jnp.sum(pad_end[None, :] <= block_start[:, None], axis=1), N_EXPERTS - 1).astype(jnp.int32)
    n_used = (pad_end[-1] // tb).astype(jnp.int32)[None]
    tok = jnp.broadcast_to(jnp.arange(N, dtype=jnp.int32)[None, :, None], (TOP_K, N, LANES)).reshape(A, LANES)
    tok_rows = _scatter_rows_sparsecore(tok, dest.reshape(A), R)[:, 0].reshape(n_blocks, tb)
    block_is_e = block_e[:, None] == jnp.arange(N_EXPERTS, dtype=jnp.int32)[None, :]
    block_pad_start = jnp.sum(jnp.where(block_is_e, pad_start[None, :], 0), axis=1)
    block_cnt = jnp.sum(jnp.where(block_is_e, cnt[None, :], 0), axis=1)
    row_in_expert = block_start[:, None] + jnp.arange(tb, dtype=jnp.int32)[None, :] - block_pad_start[:, None]
    row_valid = jnp.logical_and(row_in_expert < block_cnt[:, None],
                                jnp.arange(n_blocks, dtype=jnp.int32)[:, None] < n_used[0])
    any_tok = (block_start[:, None] + jnp.arange(tb, dtype=jnp.int32)[None, :]) % N
    row_tok3 = jnp.where(row_valid, tok_rows, any_tok).reshape(n_blocks, 1, tb)

    y_rows = pl.pallas_call(
        _expert_kernel,
        grid_spec=pltpu.PrefetchScalarGridSpec(
            num_scalar_prefetch=2,
            grid=(n_blocks,),
            in_specs=[
                pl.BlockSpec((1, 1, tb), lambda i, be, nu: (i, 0, 0), memory_space=pltpu.SMEM),
                pl.BlockSpec((1, 1, tb), lambda i, be, nu: (jnp.minimum(i + 1, n_blocks - 1), 0, 0),
                             memory_space=pltpu.SMEM),
                pl.BlockSpec((1, 1, tb), lambda i, be, nu: (jnp.minimum(i + GATHER_AHEAD, n_blocks - 1), 0, 0),
                             memory_space=pltpu.SMEM),
                pl.BlockSpec(memory_space=pl.ANY),
                pl.BlockSpec((1, D, D_EXPERT), lambda i, be, nu: (be[i], 0, 0)),
                pl.BlockSpec((1, D, D_EXPERT), lambda i, be, nu: (be[i], 0, 0)),
                pl.BlockSpec((1, D_EXPERT, D), lambda i, be, nu: (be[i], 0, 0)),
            ],
            out_specs=pl.BlockSpec((tb * ROW_TILES, LANES), lambda i, be, nu: (i, 0)),
            scratch_shapes=[
                pltpu.VMEM((GATHER_SLOTS, ROW_TILES, tb, LANES), F32), pltpu.VMEM((tb, D), BF16),
                pltpu.VMEM((D, D_EXPERT), BF16), pltpu.VMEM((D, D_EXPERT), BF16), pltpu.VMEM((D_EXPERT, D), BF16),
                pltpu.SemaphoreType.DMA((GATHER_SLOTS,)),
            ],
        ),
        out_shape=jax.ShapeDtypeStruct((R * ROW_TILES, LANES), F32),
        compiler_params=_params(("arbitrary",)),
    )(block_e, n_used, row_tok3, row_tok3, row_tok3, h_tok, w_gate[0], w_up[0], w_down[0])

    tf = TILE_COMBINE
    nf = N // tf
    dest3 = dest.reshape(TOP_K, nf, tf).transpose(1, 0, 2).reshape(nf, 1, TOP_K * tf)
    out = pl.pallas_call(
        _combine_kernel,
        grid=(nf,),
        in_specs=[
            pl.BlockSpec((1, 1, TOP_K * tf), lambda i: (i, 0, 0), memory_space=pltpu.SMEM),
            pl.BlockSpec((1, 1, TOP_K * tf), lambda i: (jnp.minimum(i + 1, nf - 1), 0, 0), memory_space=pltpu.SMEM),
            pl.BlockSpec((1, 1, TOP_K * tf), lambda i: (jnp.minimum(i + GATHER_AHEAD, nf - 1), 0, 0),
                         memory_space=pltpu.SMEM),
            pl.BlockSpec((tf * ROW_TILES, LANES), lambda i: (i, 0)),
            pl.BlockSpec((tf, LANES), lambda i: (i, 0)),
            pl.BlockSpec(memory_space=pl.ANY),
            _full((1, D)), _full((1, D)),
        ],
        out_specs=pl.BlockSpec((tf, D), lambda i: (i, 0)),
        out_shape=jax.ShapeDtypeStruct((N, D), F32),
        scratch_shapes=[pltpu.VMEM((GATHER_SLOTS, ROW_TILES, TOP_K * tf, LANES), F32),
                        pltpu.SemaphoreType.DMA((GATHER_SLOTS,))],
        compiler_params=_params(("arbitrary",)),
    )(dest3, dest3, dest3, h_tok, route, y_rows, ln2_g[0][None], ln2_b[0][None])
    return out.reshape(B, S, D)
```

```python
import functools
import math

import jax
import jax.numpy as jnp
import numpy as np
from jax import lax
from jax.experimental import pallas as pl
from jax.experimental.pallas import tpu as pltpu
from jax.experimental.pallas import tpu_sc as plsc

D_MODEL = 1024
CHUNK = 64
POOL_WINDOWS = (2, 4, 8, 16)
POOL_GROUPS = len(POOL_WINDOWS)
POOL_WIDTH = D_MODEL // 2
POOL_GROUP_DIM = POOL_WIDTH // POOL_GROUPS
POOL_OUT_GROUP_DIM = D_MODEL // POOL_GROUPS
POOL_HALO = 16
N_HEADS = 8
QK_NOPE_DIM = D_MODEL // 16
QK_ROPE_DIM = D_MODEL // 32
HALF_ROPE = QK_ROPE_DIM // 2
V_HEAD_DIM = D_MODEL // 16
V_ROWS = V_HEAD_DIM + 16
Q_LORA_RANK = 3 * D_MODEL // 8
KV_LORA_RANK = D_MODEL // 4
ROPE_THETA = 10000.0
N_GROUPS = 4
EXPERTS_PER_GROUP = 8
N_EXPERTS = N_GROUPS * EXPERTS_PER_GROUP
TOP_K = 2
D_EXPERT = D_MODEL // 2
NORM_EPS = 1e-5
DEPTH = 1
DEEPNORM_ALPHA = (2.0 * DEPTH) ** 0.25

LANES = 128
SUBLANES = 8
HEAD_PAD = LANES
ROW_TILES = D_MODEL // LANES
GROUP_ROW0 = 64
NEG_BIG = -1e30
VMEM_LIMIT = 56 * 1024 * 1024

TILE_LATENT = 1024
TILE_Q = 512
TILE_K = 256
ATTN_HEADS_PER_STEP = 4
ATTN_PAIRS_PER_TRIP = 2
TILE_MIX = 512
GATHER_AHEAD = 2
GATHER_SLOTS = GATHER_AHEAD + 1
TILE_EXPERT = 512
SC_SCATTER_WINDOW = 128
TILE_COMBINE = 512

F32 = jnp.float32
BF16 = jnp.bfloat16


def _dot(a, b):
    return jnp.dot(a, b, preferred_element_type=F32)


def _dot_nt(a, b):
    return lax.dot_general(a, b, (((1,), (1,)), ((), ())), preferred_element_type=F32)


def _dot_tn(a, b):
    return lax.dot_general(a, b, (((0,), (0,)), ((), ())), preferred_element_type=F32)


def _rms(v, g):
    ms = jnp.mean(jnp.square(v), axis=-1, keepdims=True)
    return v * lax.rsqrt(ms + NORM_EPS) * g


def _layer_norm(v, g, b):
    mu = jnp.mean(v, axis=-1, keepdims=True)
    c = v - mu
    var = jnp.mean(jnp.square(c), axis=-1, keepdims=True)
    return c * lax.rsqrt(var + NORM_EPS) * g + b


def _latent_kernel(x_ref, wlat_ref, gq_ref, gkv_ref, wuqT_ref, wuk_ref, wuvT_ref,
                   cosT_ref, sinT_ref, ra_ref, rm_ref, rp_ref,
                   qT_ref, k_ref, vT_ref, *, q_scale):
    xb = x_ref[...].astype(BF16)
    lat = _dot(xb, wlat_ref[...])
    c_q = lat[:, :Q_LORA_RANK]
    c_kv = lat[:, Q_LORA_RANK:Q_LORA_RANK + KV_LORA_RANK]
    kpe = lat[:, Q_LORA_RANK + KV_LORA_RANK:]
    qn = _rms(c_q, gq_ref[...]).astype(BF16)
    kvn = _rms(c_kv, gkv_ref[...]).astype(BF16)

    qT = _dot_nt(wuqT_ref[...], qn) * q_scale
    cosT = cosT_ref[...]
    sinT = sinT_ref[...]
    for h in range(N_HEADS):
        r0 = h * HEAD_PAD
        x1 = qT[r0 + QK_NOPE_DIM:r0 + QK_NOPE_DIM + HALF_ROPE]
        x2 = qT[r0 + QK_NOPE_DIM + HALF_ROPE:r0 + QK_NOPE_DIM + QK_ROPE_DIM]
        qT_ref[0, r0:r0 + QK_NOPE_DIM, :] = qT[r0:r0 + QK_NOPE_DIM].astype(BF16)
        qT_ref[0, r0 + QK_NOPE_DIM:r0 + QK_NOPE_DIM + HALF_ROPE, :] = (x1 * cosT - x2 * sinT).astype(BF16)
        qT_ref[0, r0 + QK_NOPE_DIM + HALF_ROPE:r0 + QK_NOPE_DIM + QK_ROPE_DIM, :] = (
            x1 * sinT + x2 * cosT).astype(BF16)
        qT_ref[0, r0 + QK_NOPE_DIM + QK_ROPE_DIM:r0 + HEAD_PAD, :] = jnp.zeros(
            (HEAD_PAD - QK_NOPE_DIM - QK_ROPE_DIM, qT.shape[1]), BF16)

    kpe_rot = (kpe * ra_ref[...] + pltpu.roll(kpe, LANES - HALF_ROPE, 1) * rm_ref[...]
               + pltpu.roll(kpe, HALF_ROPE, 1) * rp_ref[...])
    k = _dot(kvn, wuk_ref[...])
    for h in range(N_HEADS):
        k_ref[0, :, h * HEAD_PAD:(h + 1) * HEAD_PAD] = (k[:, h * HEAD_PAD:(h + 1) * HEAD_PAD] + kpe_rot).astype(BF16)
    vT = _dot_nt(wuvT_ref[...], kvn)
    row = lax.broadcasted_iota(jnp.int32, vT.shape, 0) % V_ROWS
    vT_ref[0] = jnp.where(row == V_HEAD_DIM, 1.0, vT).astype(BF16)


def _attn_kernel(qT_ref, qT_next_ref, k_ref, vT_ref, bias_ref, oT_ref, *bufs):
    qi = pl.program_id(2)
    tq = qT_ref.shape[2]
    heads = range(ATTN_HEADS_PER_STEP)
    s0, s1 = (bufs[i * ATTN_HEADS_PER_STEP:(i + 1) * ATTN_HEADS_PER_STEP] for i in range(2))
    first_max = bufs[2 * ATTN_HEADS_PER_STEP]

    def scores(h, j, c0=0):
        k0 = pl.multiple_of(j * TILE_K, TILE_K)
        return _dot(k_ref[0, pl.ds(k0, TILE_K), h * HEAD_PAD:(h + 1) * HEAD_PAD],
                    qT_ref[0, h * HEAD_PAD:(h + 1) * HEAD_PAD, c0:])

    def col_max(s):
        while s.shape[0] > SUBLANES:
            half = s.shape[0] // 2
            s = jnp.maximum(s[:half], s[half:])
        return jnp.max(s, axis=0, keepdims=True)

    def step(h, j, s_cur, s_nxt, carry, bias=None, c0=0, c0_nxt=0):
        m_all, acc_all, cmax = carry
        cmax_nxt = None
        if s_nxt is not None:
            s_new = scores(h, j + 1, c0_nxt)
            s_nxt[h][:, c0_nxt:] = s_new
            cmax_nxt = col_max(s_new)
        m, acc = m_all[:, c0:], acc_all[:, c0:]
        s = s_cur[h][:, c0:]
        if bias is not None:
            s = s + bias[:, c0:]
            cmax = col_max(s)
        m_new = jnp.maximum(m, cmax)
        p = jnp.exp2(s - m_new).astype(BF16)
        k0 = pl.multiple_of(j * TILE_K, TILE_K)
        pv = _dot(vT_ref[0, h * V_ROWS:(h + 1) * V_ROWS, pl.ds(k0, TILE_K)], p)
        acc = jnp.exp2(m - m_new) * acc + pv
        if c0:
            m_new = jnp.concatenate([m_all[:, :c0], m_new], axis=1)
            acc = jnp.concatenate([acc_all[:, :c0], acc], axis=1)
        return m_new, acc, cmax_nxt

    def first_block(q_ref):
        for h in heads:
            s_first = _dot(k_ref[0, 0:TILE_K, h * HEAD_PAD:(h + 1) * HEAD_PAD],
                           q_ref[0, h * HEAD_PAD:(h + 1) * HEAD_PAD, :])
            s0[h][...] = s_first
            first_max[h * SUBLANES:h * SUBLANES + 1, :] = col_max(s_first)

    @pl.when(qi == 0)
    def _():
        first_block(qT_ref)

    carry = tuple((jnp.full((1, tq), NEG_BIG, F32), jnp.zeros((V_ROWS, tq), F32),
                   first_max[h * SUBLANES:h * SUBLANES + 1, :]) for h in heads)

    def pair(t, c):
        c = tuple(step(h, 2 * t, s0, s1, c[h]) for h in heads)
        return tuple(step(h, 2 * t + 1, s1, s0, c[h]) for h in heads)

    def trip(u, c):
        for i in range(ATTN_PAIRS_PER_TRIP):
            c = pair(ATTN_PAIRS_PER_TRIP * u + i, c)
        return c

    carry = lax.fori_loop(0, qi // ATTN_PAIRS_PER_TRIP, trip, carry)
    done = (qi // ATTN_PAIRS_PER_TRIP) * ATTN_PAIRS_PER_TRIP
    for i in range(ATTN_PAIRS_PER_TRIP - 1):
        carry = lax.cond(done + i < qi, functools.partial(pair, done + i), lambda c: c, carry)
    jd = 2 * qi
    carry = tuple(step(h, jd, s0, s1, carry[h], bias=bias_ref.at[0], c0_nxt=TILE_K) for h in heads)
    first_block(qT_next_ref)
    carry = tuple(step(h, jd + 1, s1, None, carry[h], bias=bias_ref.at[1], c0=TILE_K) for h in heads)
    for h in heads:
        acc = carry[h][1]
        oT_ref[0, h * V_HEAD_DIM:(h + 1) * V_HEAD_DIM, :] = (
            acc[:V_HEAD_DIM] / acc[V_HEAD_DIM:V_HEAD_DIM + 1]).astype(BF16)


def _mix_kernel(x_ref, oT_ref, wpool_ref, wgate_ref, mixw_ref, pscale_ref, wo_ref, wout_ref,
                g1_ref, b1_ref, wrT_ref, brT_ref, triu_ref,
                h_ref, route_ref, routeT_ref, cnt_ref, ext_ref, base_ref):
    b = pl.program_id(0)
    si = pl.program_id(1)
    tm = x_ref.shape[0]

    @pl.when(jnp.logical_and(b == 0, si == 0))
    def _():
        base_ref[...] = jnp.zeros_like(base_ref)

    @pl.when(si == 0)
    def _():
        ext_ref[0:POOL_HALO, :] = jnp.zeros((POOL_HALO, POOL_WIDTH), F32)

    x = x_ref[...]
    xb = x.astype(BF16)
    u = _dot(xb, wpool_ref[...])
    ext_ref[POOL_HALO:POOL_HALO + tm, :] = u

    pos1 = si * tm + lax.broadcasted_iota(jnp.int32, (tm, POOL_GROUP_DIM), 0) + 1
    y_parts = []
    for g, win in enumerate(POOL_WINDOWS):
        c0 = g * POOL_GROUP_DIM
        ws = u[:, c0:c0 + POOL_GROUP_DIM]
        for kk in range(1, win):
            ws = ws + ext_ref[POOL_HALO - kk:POOL_HALO - kk + tm, c0:c0 + POOL_GROUP_DIM]
        count = jnp.minimum(pos1, win).astype(F32)
        pooled = ws / count - u[:, c0:c0 + POOL_GROUP_DIM]
        y_parts.append(_dot(pooled.astype(BF16), mixw_ref[g]))
    ext_ref[0:POOL_HALO, :] = ext_ref[tm:tm + POOL_HALO, :]
    y_pool = jnp.concatenate(y_parts, axis=-1) * pscale_ref[...]

    glog = _dot(xb, wgate_ref[...])
    y_mla = _dot_tn(oT_ref[0], wo_ref[...])
    merged = (jax.nn.sigmoid(glog[:, :D_MODEL]) * y_pool + jax.nn.sigmoid(glog[:, D_MODEL:]) * y_mla)
    r = DEEPNORM_ALPHA * x + _dot(merged.astype(BF16), wout_ref[...])
    h = _layer_norm(r, g1_ref[...], b1_ref[...])
    for j in range(ROW_TILES):
        h_ref[pl.ds(j, tm, stride=ROW_TILES), :] = h[:, j * LANES:(j + 1) * LANES]

    logits = _dot_nt(wrT_ref[...], h.astype(BF16)) + brT_ref[...]
    big = float(LANES)
    gl = logits[GROUP_ROW0:GROUP_ROW0 + SUBLANES]
    grow = lax.broadcasted_iota(jnp.int32, gl.shape, 0)
    gl = jnp.where(grow < N_GROUPS, gl, NEG_BIG)
    gmax = jnp.max(gl, axis=0, keepdims=True)
    g_w = 1.0 / jnp.sum(jnp.exp(gl - gmax), axis=0, keepdims=True)
    g_idx = jnp.min(jnp.where(gl == gmax, grow.astype(F32), big), axis=0, keepdims=True)
    el = logits[0:N_EXPERTS]
    erow_i = lax.broadcasted_iota(jnp.int32, el.shape, 0)
    erow = erow_i.astype(F32)
    el = jnp.where((erow_i // EXPERTS_PER_GROUP).astype(F32) == g_idx, el, NEG_BIG)
    e1max = jnp.max(el, axis=0, keepdims=True)
    i1 = jnp.min(jnp.where(el == e1max, erow, big), axis=0, keepdims=True)
    el2 = jnp.where(erow == i1, NEG_BIG, el)
    e2max = jnp.max(el2, axis=0, keepdims=True)
    i2 = jnp.min(jnp.where(el2 == e2max, erow, big), axis=0, keepdims=True)
    ratio = jnp.exp(e2max - e1max)
    gate1 = g_w / (1.0 + ratio)
    gate2 = g_w * ratio / (1.0 + ratio)

    hit1 = erow == i1
    hit2 = erow == i2
    onehot = jnp.where(jnp.logical_or(hit1, hit2), 1.0, 0.0)
    base = base_ref[:, 0:1]
    before = _dot(onehot.astype(BF16), triu_ref[...]) + base
    rank1 = jnp.sum(jnp.where(hit1, before, 0.0), axis=0, keepdims=True)
    rank2 = jnp.sum(jnp.where(hit2, before, 0.0), axis=0, keepdims=True)
    new_base = base + jnp.sum(onehot, axis=1, keepdims=True)
    base_ref[...] = jnp.broadcast_to(new_base, base_ref.shape)
    cnt_ref[...] = jnp.broadcast_to(new_base, cnt_ref.shape)

    row = lax.broadcasted_iota(jnp.int32, (LANES, tm), 0)
    routeT = jnp.where(row == 0, i1, 0.0)
    routeT = jnp.where(row == 1, i2, routeT)
    routeT = jnp.where(row == 2, gate1, routeT)
    routeT = jnp.where(row == 3, gate2, routeT)
    routeT = jnp.where(row == 4, rank1, routeT)
    routeT = jnp.where(row == 5, rank2, routeT)
    routeT_ref[...] = routeT[:SUBLANES]
    route_ref[...] = routeT.T


def _gather_row(idx_ref, r, src_hbm, dst_ref, slot, sem, priority=0):
    src0 = pl.multiple_of(idx_ref[0, 0, r] * ROW_TILES, ROW_TILES)
    pltpu.make_async_copy(src_hbm.at[pl.ds(src0, ROW_TILES), :],
                          dst_ref.at[slot, :, r, :],
                          sem.at[slot]).start(priority=priority)


def _gather_rows_loop(idx_ref, src_hbm, dst_ref, slot, sem, n_rows, unroll=8):
    def body(c, _):
        for u in range(unroll):
            _gather_row(idx_ref, c * unroll + u, src_hbm, dst_ref, slot, sem)
        return 0
    lax.fori_loop(0, n_rows // unroll, body, 0)


def _gather_rows_inline(idx_ref, src_hbm, dst_ref, slot, sem, n_rows):
    for r in range(n_rows):
        _gather_row(idx_ref, r, src_hbm, dst_ref, slot, sem, priority=r % 2)


def _wait_rows(dst_ref, slot, sem):
    pltpu.make_async_copy(dst_ref.at[slot], dst_ref.at[slot], sem.at[slot]).wait()


def _expert_kernel(be_ref, nused_ref, tok_ref, tok1_ref, tok2_ref, h_hbm, wg_ref, wu_ref, wd_ref,
                   y_ref, xbuf, xs_ref, wgb, wub, wdb, sem):
    i = pl.program_id(0)
    n_used = nused_ref[0]
    slot = lax.rem(i, GATHER_SLOTS)
    tb = TILE_EXPERT

    @pl.when(i == 0)
    def _():
        _gather_rows_loop(tok_ref, h_hbm, xbuf, 0, sem, tb)
        _gather_rows_loop(tok1_ref, h_hbm, xbuf, 1, sem, tb)

    changed = jnp.logical_or(i == 0, be_ref[i] != be_ref[jnp.maximum(i - 1, 0)])

    @pl.when(jnp.logical_and(changed, i < n_used))
    def _():
        wgb[...] = wg_ref[0].astype(BF16)
        wub[...] = wu_ref[0].astype(BF16)
        wdb[...] = wd_ref[0].astype(BF16)

    @pl.when(jnp.logical_and(i >= n_used, i < n_used + GATHER_AHEAD))
    def _():
        _wait_rows(xbuf, slot, sem)

    @pl.when(i < n_used)
    def _():
        _wait_rows(xbuf, slot, sem)
        for j in range(ROW_TILES):
            xs_ref[:, j * LANES:(j + 1) * LANES] = xbuf[slot, j].astype(BF16)
        _gather_rows_inline(tok2_ref, h_hbm, xbuf, lax.rem(i + GATHER_AHEAD, GATHER_SLOTS), sem, tb)
        xs = xs_ref[...]
        gate = _dot(xs, wgb[...])
        up = _dot(xs, wub[...])
        hid = (jax.nn.silu(gate) * up).astype(BF16)
        y = _dot(hid, wdb[...])
        for j in range(ROW_TILES):
            y_ref[pl.ds(j, tb, stride=ROW_TILES), :] = y[:, j * LANES:(j + 1) * LANES]

    @pl.when(i >= n_used)
    def _():
        y_ref[...] = jnp.zeros_like(y_ref)


def _combine_kernel(dst_ref, dst1_ref, dst2_ref, h_ref, route_ref, y_hbm, g2_ref, b2_ref, o_ref, gbuf, sem):
    i = pl.program_id(0)
    n = pl.num_programs(0)
    slot = lax.rem(i, GATHER_SLOTS)
    tf = TILE_COMBINE

    @pl.when(i == 0)
    def _():
        _gather_rows_loop(dst_ref, y_hbm, gbuf, 0, sem, TOP_K * tf)
        _gather_rows_loop(dst1_ref, y_hbm, gbuf, 1, sem, TOP_K * tf)

    _wait_rows(gbuf, slot, sem)
    route = route_ref[...]
    gate1 = route[:, 2:3]
    gate2 = route[:, 3:4]
    parts = []
    for j in range(ROW_TILES):
        hj = h_ref[pl.ds(j, tf, stride=ROW_TILES), :]
        y1 = gbuf[slot, j, 0:tf, :]
        y2 = gbuf[slot, j, tf:TOP_K * tf, :]
        parts.append(DEEPNORM_ALPHA * hj + (gate1 * y1 + gate2 * y2))
    z = jnp.concatenate(parts, axis=-1)
    o_ref[...] = _layer_norm(z, g2_ref[...], b2_ref[...])
    _gather_rows_inline(dst2_ref, y_hbm, gbuf, lax.rem(i + GATHER_AHEAD, GATHER_SLOTS), sem, TOP_K * tf)

    @pl.when(i == n - 1)
    def _():
        for ahead in range(1, GATHER_AHEAD + 1):
            _wait_rows(gbuf, lax.rem(i + ahead, GATHER_SLOTS), sem)


def _rope_tables(seq):
    f32 = np.float32
    inv = (f32(1.0) / (f32(ROPE_THETA) ** (np.arange(0, QK_ROPE_DIM, 2, dtype=f32) / f32(QK_ROPE_DIM)))).astype(f32)
    ang = (np.arange(seq, dtype=f32)[:, None] * inv[None, :]).astype(f32)
    cos, sin = np.cos(ang.astype(np.float64)).astype(f32), np.sin(ang.astype(np.float64)).astype(f32)
    zeros = np.zeros((seq, QK_NOPE_DIM), f32)
    pad = np.zeros((seq, HEAD_PAD - QK_NOPE_DIM - QK_ROPE_DIM), f32)
    z16 = np.zeros((seq, HALF_ROPE), f32)
    rot_a = np.concatenate([zeros, cos, cos, pad], axis=1)
    rot_m = np.concatenate([zeros, -sin, z16, pad], axis=1)
    rot_p = np.concatenate([zeros, z16, sin, pad], axis=1)
    return tuple(jnp.asarray(t) for t in (np.ascontiguousarray(cos.T), np.ascontiguousarray(sin.T),
                                          rot_a, rot_m, rot_p))


def _scatter_rows_sparsecore(x, indices, n_out):
    n, width = x.shape
    mesh = plsc.VectorSubcoreMesh(core_axis_name="core", subcore_axis_name="subcore")
    per_core = n // SC_SCATTER_WINDOW // mesh.num_cores
    assert per_core * SC_SCATTER_WINDOW * mesh.num_cores == n

    @functools.partial(pl.kernel, out_type=jax.ShapeDtypeStruct((n_out, width), x.dtype), mesh=mesh,
                       scratch_types=[])
    def scatter(x_hbm, i_hbm, o_hbm):
        first = lax.axis_index("core") * per_core

        def body(x_vmem, i_vmem):
            pltpu.sync_copy(x_vmem, o_hbm.at[i_vmem.at[0]])

        pltpu.emit_pipeline(
            body,
            grid=(per_core,),
            in_specs=[pl.BlockSpec((SC_SCATTER_WINDOW, width), index_map=lambda i: (first + i, 0)),
                      pl.BlockSpec((1, SC_SCATTER_WINDOW), index_map=lambda i: (0, first + i))],
            out_specs=[],
            core_axis_name="subcore",
            dimension_semantics=(pltpu.PARALLEL,),
        )(x_hbm, i_hbm)

    return scatter(x, indices.reshape(1, n))


def _full(shape):
    return pl.BlockSpec(shape, lambda *_: (0,) * len(shape))


def _params(sem):
    return pltpu.CompilerParams(dimension_semantics=sem, vmem_limit_bytes=VMEM_LIMIT)


def kernel(x, w_in, pool_mix_w, pool_scale, q_norm_g, w_uq, kv_norm_g, w_ukv, w_mla_o, w_out, ln1_g, ln1_b,
           w_router_group, b_router_group, w_router_expert, b_router_expert, w_gate, w_up, w_down, ln2_g, ln2_b):
    B, S, D = x.shape
    assert D == D_MODEL and w_in.shape[0] == DEPTH == 1
    assert S % TILE_LATENT == 0 and S % TILE_Q == 0 and S % TILE_MIX == 0 and TILE_Q % TILE_K == 0
    N = B * S
    assert N % TILE_COMBINE == 0
    H = N_HEADS

    w = w_in[0]
    o1 = POOL_WIDTH
    o2 = o1 + Q_LORA_RANK
    o3 = o2 + KV_LORA_RANK
    o4 = o3 + QK_ROPE_DIM
    w_pool = w[:, :o1].astype(BF16)
    kpe_cols = jnp.pad(w[:, o3:o4], ((0, 0), (QK_NOPE_DIM, HEAD_PAD - QK_NOPE_DIM - QK_ROPE_DIM)))
    w_lat = jnp.concatenate([w[:, o1:o3], kpe_cols], axis=1).astype(BF16)
    w_gates = w[:, o4:].astype(BF16)
    qd = QK_NOPE_DIM + QK_ROPE_DIM
    wuq = jnp.pad(w_uq[0].reshape(Q_LORA_RANK, H, qd), ((0, 0), (0, 0), (0, HEAD_PAD - qd)))
    wuqT = wuq.reshape(Q_LORA_RANK, H * HEAD_PAD).T.astype(BF16)
    wukv = w_ukv[0].reshape(KV_LORA_RANK, H, QK_NOPE_DIM + V_HEAD_DIM)
    wuk = jnp.pad(wukv[:, :, :QK_NOPE_DIM], ((0, 0), (0, 0), (0, HEAD_PAD - QK_NOPE_DIM)))
    wuk = wuk.reshape(KV_LORA_RANK, H * HEAD_PAD).astype(BF16)
    wuv = jnp.pad(wukv[:, :, QK_NOPE_DIM:], ((0, 0), (0, 0), (0, V_ROWS - V_HEAD_DIM)))
    wuvT = wuv.reshape(KV_LORA_RANK, H * V_ROWS).T.astype(BF16)
    w_rT = jnp.zeros((LANES, D), F32)
    w_rT = w_rT.at[:N_EXPERTS].set(w_router_expert[0].T).at[GROUP_ROW0:GROUP_ROW0 + N_GROUPS].set(
        w_router_group[0].T).astype(BF16)
    b_rT = jnp.zeros((LANES, 1), F32)
    b_rT = b_rT.at[:N_EXPERTS, 0].set(b_router_expert[0]).at[GROUP_ROW0:GROUP_ROW0 + N_GROUPS, 0].set(
        b_router_group[0])
    cosT, sinT, rot_a, rot_m, rot_p = _rope_tables(S)
    q_scale = (QK_NOPE_DIM + QK_ROPE_DIM) ** -0.5 * math.log2(math.e)
    x2 = x.reshape(N, D)

    ta = TILE_LATENT
    nsa = S // ta
    qT, k, vT = pl.pallas_call(
        functools.partial(_latent_kernel, q_scale=q_scale),
        grid=(B, nsa),
        in_specs=[
            pl.BlockSpec((ta, D), lambda b, s: (b * nsa + s, 0)),
            _full(w_lat.shape), _full((1, Q_LORA_RANK)), _full((1, KV_LORA_RANK)),
            _full(wuqT.shape), _full(wuk.shape), _full(wuvT.shape),
            pl.BlockSpec((HALF_ROPE, ta), lambda b, s: (0, s)),
            pl.BlockSpec((HALF_ROPE, ta), lambda b, s: (0, s)),
            pl.BlockSpec((ta, LANES), lambda b, s: (s, 0)),
            pl.BlockSpec((ta, LANES), lambda b, s: (s, 0)),
            pl.BlockSpec((ta, LANES), lambda b, s: (s, 0)),
        ],
        out_specs=[
            pl.BlockSpec((1, H * HEAD_PAD, ta), lambda b, s: (b, 0, s)),
            pl.BlockSpec((1, ta, H * HEAD_PAD), lambda b, s: (b, s, 0)),
            pl.BlockSpec((1, H * V_ROWS, ta), lambda b, s: (b, 0, s)),
        ],
        out_shape=[
            jax.ShapeDtypeStruct((B, H * HEAD_PAD, S), BF16),
            jax.ShapeDtypeStruct((B, S, H * HEAD_PAD), BF16),
            jax.ShapeDtypeStruct((B, H * V_ROWS, S), BF16),
        ],
        compiler_params=_params(("parallel", "parallel")),
    )(x2, w_lat, q_norm_g[0][None], kv_norm_g[0][None], wuqT, wuk, wuvT, cosT, sinT, rot_a, rot_m, rot_p)

    tq = TILE_Q
    assert tq == 2 * TILE_K
    key_chunk = jnp.arange(tq)[:, None] // CHUNK
    q_chunk = jnp.arange(tq)[None, :] // CHUNK
    mask_bias = jnp.where(key_chunk <= q_chunk, 0.0, NEG_BIG).astype(F32).reshape(2, TILE_K, tq)
    hp = ATTN_HEADS_PER_STEP
    assert H % hp == 0
    oT = pl.pallas_call(
        _attn_kernel,
        grid=(B, H // hp, S // tq),
        in_specs=[
            pl.BlockSpec((1, hp * HEAD_PAD, tq), lambda b, h, q: (b, h, q)),
            pl.BlockSpec((1, hp * HEAD_PAD, tq), lambda b, h, q: (b, h, jnp.minimum(q + 1, S // tq - 1))),
            pl.BlockSpec((1, S, hp * HEAD_PAD), lambda b, h, q: (b, 0, h)),
            pl.BlockSpec((1, hp * V_ROWS, S), lambda b, h, q: (b, h, 0)),
            _full(mask_bias.shape),
        ],
        out_specs=pl.BlockSpec((1, hp * V_HEAD_DIM, tq), lambda b, h, q: (b, h, q)),
        out_shape=jax.ShapeDtypeStruct((B, H * V_HEAD_DIM, S), BF16),
        scratch_shapes=[pltpu.VMEM((TILE_K, tq), F32)] * (2 * hp) + [pltpu.VMEM((hp * SUBLANES, tq), F32)],
        compiler_params=_params(("parallel", "parallel", "arbitrary")),
    )(qT, qT, k, vT, mask_bias)

    tm = TILE_MIX
    nsm = S // tm
    triu = (jnp.arange(tm)[:, None] < jnp.arange(tm)[None, :]).astype(BF16)
    h_tok, route, routeT, counts = pl.pallas_call(
        _mix_kernel,
        grid=(B, nsm),
        in_specs=[
            pl.BlockSpec((tm, D), lambda b, s: (b * nsm + s, 0)),
            pl.BlockSpec((1, H * V_HEAD_DIM, tm), lambda b, s: (b, 0, s)),
            _full(w_pool.shape), _full(w_gates.shape), _full(pool_mix_w.shape[1:]), _full((1, D)),
            _full(w_mla_o.shape[1:]), _full(w_out.shape[1:]), _full((1, D)), _full((1, D)),
            _full(w_rT.shape), _full(b_rT.shape), _full(triu.shape),
        ],
        out_specs=[
            pl.BlockSpec((tm * ROW_TILES, LANES), lambda b, s: (b * nsm + s, 0)),
            pl.BlockSpec((tm, LANES), lambda b, s: (b * nsm + s, 0)),
            pl.BlockSpec((SUBLANES, tm), lambda b, s: (0, b * nsm + s)),
            _full((N_EXPERTS, LANES)),
        ],
        out_shape=[
            jax.ShapeDtypeStruct((N * ROW_TILES, LANES), F32),
            jax.ShapeDtypeStruct((N, LANES), F32),
            jax.ShapeDtypeStruct((SUBLANES, N), F32),
            jax.ShapeDtypeStruct((N_EXPERTS, LANES), F32),
        ],
        scratch_shapes=[pltpu.VMEM((tm + POOL_HALO, POOL_WIDTH), F32), pltpu.VMEM((N_EXPERTS, LANES), F32)],
        compiler_params=_params(("arbitrary", "arbitrary")),
    )(x2, oT, w_pool, w_gates, pool_mix_w[0].astype(BF16), pool_scale[0][None], w_mla_o[0].astype(BF16),
      w_out[0].astype(BF16), ln1_g[0][None], ln1_b[0][None], w_rT, b_rT, triu)

    tb = TILE_EXPERT
    A = N * TOP_K
    n_blocks = -(-(A + N_EXPERTS * (tb - 1)) // tb) + GATHER_AHEAD
    R = n_blocks * tb
    e_idx = routeT[0:2].astype(jnp.int32)
    rank = routeT[4:6].astype(jnp.int32)
    cnt = counts[:, 0].astype(jnp.int32)
    padded = ((cnt + tb - 1) // tb) * tb
    pad_end = jnp.cumsum(padded)
    pad_start = pad_end - padded
    is_e = e_idx[:, None, :] == jnp.arange(N_EXPERTS, dtype=jnp.int32)[None, :, None]
    dest = jnp.sum(jnp.where(is_e, pad_start[None, :, None], 0), axis=1) + rank
    block_start = jnp.arange(n_blocks, dtype=jnp.int32) * tb
    block_e = jnp.minimum(jnp.sum(pad_end[None, :] <= block_start[:, None], axis=1), N_EXPERTS - 1).astype(jnp.int32)
    n_used = (pad_end[-1] // tb).astype(jnp.int32)[None]
    tok = jnp.broadcast_to(jnp.arange(N, dtype=jnp.int32)[None, :, None], (TOP_K, N, LANES)).reshape(A, LANES)
    tok_rows = _scatter_rows_sparsecore(tok, dest.reshape(A), R)[:, 0].reshape(n_blocks, tb)
    block_is_e = block_e[:, None] == jnp.arange(N_EXPERTS, dtype=jnp.int32)[None, :]
    block_pad_start = jnp.sum(jnp.where(block_is_e, pad_start[None, :], 0), axis=1)
    block_cnt = jnp.sum(jnp.where(block_is_e, cnt[None, :], 0), axis=1)
    row_in_expert = block_start[:, None] + jnp.arange(tb, dtype=jnp.int32)[None, :] - block_pad_start[:, None]
    row_valid = jnp.logical_and(row_in_expert < block_cnt[:, None],
                                jnp.arange(n_blocks, dtype=jnp.int32)[:, None] < n_used[0])
    any_tok = (block_start[:, None] + jnp.arange(tb, dtype=jnp.int32)[None, :]) % N
    row_tok3 = jnp.where(row_valid, tok_rows, any_tok).reshape(n_blocks, 1, tb)

    y_rows = pl.pallas_call(
        _expert_kernel,
        grid_spec=pltpu.PrefetchScalarGridSpec(
            num_scalar_prefetch=2,
            grid=(n_blocks,),
            in_specs=[
                pl.BlockSpec((1, 1, tb), lambda i, be, nu: (i, 0, 0), memory_space=pltpu.SMEM),
                pl.BlockSpec((1, 1, tb), lambda i, be, nu: (jnp.minimum(i + 1, n_blocks - 1), 0, 0),
                             memory_space=pltpu.SMEM),
                pl.BlockSpec((1, 1, tb), lambda i, be, nu: (jnp.minimum(i + GATHER_AHEAD, n_blocks - 1), 0, 0),
                             memory_space=pltpu.SMEM),
                pl.BlockSpec(memory_space=pl.ANY),
                pl.BlockSpec((1, D, D_EXPERT), lambda i, be, nu: (be[i], 0, 0)),
                pl.BlockSpec((1, D, D_EXPERT), lambda i, be, nu: (be[i], 0, 0)),
                pl.BlockSpec((1, D_EXPERT, D), lambda i, be, nu: (be[i], 0, 0)),
            ],
            out_specs=pl.BlockSpec((tb * ROW_TILES, LANES), lambda i, be, nu: (i, 0)),
            scratch_shapes=[
                pltpu.VMEM((GATHER_SLOTS, ROW_TILES, tb, LANES), F32), pltpu.VMEM((tb, D), BF16),
                pltpu.VMEM((D, D_EXPERT), BF16), pltpu.VMEM((D, D_EXPERT), BF16), pltpu.VMEM((D_EXPERT, D), BF16),
                pltpu.SemaphoreType.DMA((GATHER_SLOTS,)),
            ],
        ),
        out_shape=jax.ShapeDtypeStruct((R * ROW_TILES, LANES), F32),
        compiler_params=_params(("arbitrary",)),
    )(block_e, n_used, row_tok3, row_tok3, row_tok3, h_tok, w_gate[0], w_up[0], w_down[0])

    tf = TILE_COMBINE
    nf = N // tf
    dest3 = dest.reshape(TOP_K, nf, tf).transpose(1, 0, 2).reshape(nf, 1, TOP_K * tf)
    out = pl.pallas_call(
        _combine_kernel,
        grid=(nf,),
        in_specs=[
            pl.BlockSpec((1, 1, TOP_K * tf), lambda i: (i, 0, 0), memory_space=pltpu.SMEM),
            pl.BlockSpec((1, 1, TOP_K * tf), lambda i: (jnp.minimum(i + 1, nf - 1), 0, 0), memory_space=pltpu.SMEM),
            pl.BlockSpec((1, 1, TOP_K * tf), lambda i: (jnp.minimum(i + GATHER_AHEAD, nf - 1), 0, 0),
                         memory_space=pltpu.SMEM),
            pl.BlockSpec((tf * ROW_TILES, LANES), lambda i: (i, 0)),
            pl.BlockSpec((tf, LANES), lambda i: (i, 0)),
            pl.BlockSpec(memory_space=pl.ANY),
            _full((1, D)), _full((1, D)),
        ],
        out_specs=pl.BlockSpec((tf, D), lambda i: (i, 0)),
        out_shape=jax.ShapeDtypeStruct((N, D), F32),
        scratch_shapes=[pltpu.VMEM((GATHER_SLOTS, ROW_TILES, TOP_K * tf, LANES), F32),
                        pltpu.SemaphoreType.DMA((GATHER_SLOTS,))],
        compiler_params=_params(("arbitrary",)),
    )(dest3, dest3, dest3, h_tok, route, y_rows, ln2_g[0][None], ln2_b[0][None])
    return out.reshape(B, S, D)
```

```python
import functools
import math

import jax
import jax.numpy as jnp
import numpy as np
from jax import lax
from jax.experimental import pallas as pl
from jax.experimental.pallas import tpu as pltpu
from jax.experimental.pallas import tpu_sc as plsc

D_MODEL = 1024
CHUNK = 64
POOL_WINDOWS = (2, 4, 8, 16)
POOL_GROUPS = len(POOL_WINDOWS)
POOL_WIDTH = D_MODEL // 2
POOL_GROUP_DIM = POOL_WIDTH // POOL_GROUPS
POOL_OUT_GROUP_DIM = D_MODEL // POOL_GROUPS
POOL_HALO = 16
POOL_PAD = 8
N_HEADS = 8
QK_NOPE_DIM = D_MODEL // 16
QK_ROPE_DIM = D_MODEL // 32
HALF_ROPE = QK_ROPE_DIM // 2
V_HEAD_DIM = D_MODEL // 16
V_ROWS = V_HEAD_DIM + 16
Q_LORA_RANK = 3 * D_MODEL // 8
KV_LORA_RANK = D_MODEL // 4
ROPE_THETA = 10000.0
N_GROUPS = 4
EXPERTS_PER_GROUP = 8
N_EXPERTS = N_GROUPS * EXPERTS_PER_GROUP
TOP_K = 2
D_EXPERT = D_MODEL // 2
NORM_EPS = 1e-5
DEPTH = 1
DEEPNORM_ALPHA = (2.0 * DEPTH) ** 0.25

LANES = 128
SUBLANES = 8
HEAD_PAD = LANES
ROW_TILES = D_MODEL // LANES
GROUP_ROW0 = 64
NEG_BIG = -1e30
VMEM_LIMIT = 56 * 1024 * 1024

TILE_LATENT = 1024
TILE_Q = 512
TILE_K = 256
ATTN_HEADS_PER_STEP = 4
ATTN_PAIRS_PER_TRIP = 2
TILE_MIX = 512
MIX_STRIP = 256
GATHER_AHEAD = 2
GATHER_SLOTS = GATHER_AHEAD + 1
TILE_EXPERT = 512
SC_SCATTER_WINDOW = 128
TILE_COMBINE = 512

F32 = jnp.float32
BF16 = jnp.bfloat16


def _dot(a, b):
    return jnp.dot(a, b, preferred_element_type=F32)


def _dot_nt(a, b):
    return lax.dot_general(a, b, (((1,), (1,)), ((), ())), preferred_element_type=F32)


def _dot_tn(a, b):
    return lax.dot_general(a, b, (((0,), (0,)), ((), ())), preferred_element_type=F32)


def _rms(v, g):
    ms = jnp.mean(jnp.square(v), axis=-1, keepdims=True)
    return v * lax.rsqrt(ms + NORM_EPS) * g


def _layer_norm(v, g, b):
    mu = jnp.mean(v, axis=-1, keepdims=True)
    c = v - mu
    var = jnp.mean(jnp.square(c), axis=-1, keepdims=True)
    return c * lax.rsqrt(var + NORM_EPS) * g + b


def _latent_kernel(x_ref, wlat_ref, gq_ref, gkv_ref, wuqT_ref, wuk_ref, wuvT_ref,
                   cosT_ref, sinT_ref, ra_ref, rm_ref, rp_ref,
                   qT_ref, k_ref, vT_ref, *, q_scale):
    xb = x_ref[...].astype(BF16)
    lat = _dot(xb, wlat_ref[...])
    c_q = lat[:, :Q_LORA_RANK]
    c_kv = lat[:, Q_LORA_RANK:Q_LORA_RANK + KV_LORA_RANK]
    kpe = lat[:, Q_LORA_RANK + KV_LORA_RANK:]
    qn = _rms(c_q, gq_ref[...]).astype(BF16)
    kvn = _rms(c_kv, gkv_ref[...]).astype(BF16)

    qT = _dot_nt(wuqT_ref[...], qn) * q_scale
    cosT = cosT_ref[...]
    sinT = sinT_ref[...]
    for h in range(N_HEADS):
        r0 = h * HEAD_PAD
        x1 = qT[r0 + QK_NOPE_DIM:r0 + QK_NOPE_DIM + HALF_ROPE]
        x2 = qT[r0 + QK_NOPE_DIM + HALF_ROPE:r0 + QK_NOPE_DIM + QK_ROPE_DIM]
        qT_ref[0, r0:r0 + QK_NOPE_DIM, :] = qT[r0:r0 + QK_NOPE_DIM].astype(BF16)
        qT_ref[0, r0 + QK_NOPE_DIM:r0 + QK_NOPE_DIM + HALF_ROPE, :] = (x1 * cosT - x2 * sinT).astype(BF16)
        qT_ref[0, r0 + QK_NOPE_DIM + HALF_ROPE:r0 + QK_NOPE_DIM + QK_ROPE_DIM, :] = (
            x1 * sinT + x2 * cosT).astype(BF16)
        qT_ref[0, r0 + QK_NOPE_DIM + QK_ROPE_DIM:r0 + HEAD_PAD, :] = jnp.zeros(
            (HEAD_PAD - QK_NOPE_DIM - QK_ROPE_DIM, qT.shape[1]), BF16)

    kpe_rot = (kpe * ra_ref[...] + pltpu.roll(kpe, LANES - HALF_ROPE, 1) * rm_ref[...]
               + pltpu.roll(kpe, HALF_ROPE, 1) * rp_ref[...])
    k = _dot(kvn, wuk_ref[...])
    for h in range(N_HEADS):
        k_ref[0, :, h * HEAD_PAD:(h + 1) * HEAD_PAD] = (k[:, h * HEAD_PAD:(h + 1) * HEAD_PAD] + kpe_rot).astype(BF16)
    vT = _dot_nt(wuvT_ref[...], kvn)
    row = lax.broadcasted_iota(jnp.int32, vT.shape, 0) % V_ROWS
    vT_ref[0] = jnp.where(row == V_HEAD_DIM, 1.0, vT).astype(BF16)


def _attn_kernel(qT_ref, qT_next_ref, k_ref, vT_ref, bias_ref, oT_ref, *bufs):
    qi = pl.program_id(2)
    tq = qT_ref.shape[2]
    heads = range(ATTN_HEADS_PER_STEP)
    s0, s1 = (bufs[i * ATTN_HEADS_PER_STEP:(i + 1) * ATTN_HEADS_PER_STEP] for i in range(2))
    first_max = bufs[2 * ATTN_HEADS_PER_STEP]

    def scores(h, j, c0=0):
        k0 = pl.multiple_of(j * TILE_K, TILE_K)
        return _dot(k_ref[0, pl.ds(k0, TILE_K), h * HEAD_PAD:(h + 1) * HEAD_PAD],
                    qT_ref[0, h * HEAD_PAD:(h + 1) * HEAD_PAD, c0:])

    def col_max(s):
        while s.shape[0] > SUBLANES:
            half = s.shape[0] // 2
            s = jnp.maximum(s[:half], s[half:])
        return jnp.max(s, axis=0, keepdims=True)

    def step(h, j, s_cur, s_nxt, carry, bias=None, c0=0, c0_nxt=0):
        m_all, acc_all, cmax = carry
        cmax_nxt = None
        if s_nxt is not None:
            s_new = scores(h, j + 1, c0_nxt)
            s_nxt[h][:, c0_nxt:] = s_new
            cmax_nxt = col_max(s_new)
        m, acc = m_all[:, c0:], acc_all[:, c0:]
        s = s_cur[h][:, c0:]
        if bias is not None:
            s = s + bias[:, c0:]
            cmax = col_max(s)
        m_new = jnp.maximum(m, cmax)
        p = jnp.exp2(s - m_new).astype(BF16)
        k0 = pl.multiple_of(j * TILE_K, TILE_K)
        pv = _dot(vT_ref[0, h * V_ROWS:(h + 1) * V_ROWS, pl.ds(k0, TILE_K)], p)
        acc = jnp.exp2(m - m_new) * acc + pv
        if c0:
            m_new = jnp.concatenate([m_all[:, :c0], m_new], axis=1)
            acc = jnp.concatenate([acc_all[:, :c0], acc], axis=1)
        return m_new, acc, cmax_nxt

    def first_block(q_ref):
        for h in heads:
            s_first = _dot(k_ref[0, 0:TILE_K, h * HEAD_PAD:(h + 1) * HEAD_PAD],
                           q_ref[0, h * HEAD_PAD:(h + 1) * HEAD_PAD, :])
            s0[h][...] = s_first
            first_max[h * SUBLANES:h * SUBLANES + 1, :] = col_max(s_first)

    @pl.when(qi == 0)
    def _():
        first_block(qT_ref)

    carry = tuple((jnp.full((1, tq), NEG_BIG, F32), jnp.zeros((V_ROWS, tq), F32),
                   first_max[h * SUBLANES:h * SUBLANES + 1, :]) for h in heads)

    def pair(t, c):
        c = tuple(step(h, 2 * t, s0, s1, c[h]) for h in heads)
        return tuple(step(h, 2 * t + 1, s1, s0, c[h]) for h in heads)

    def trip(u, c):
        for i in range(ATTN_PAIRS_PER_TRIP):
            c = pair(ATTN_PAIRS_PER_TRIP * u + i, c)
        return c

    carry = lax.fori_loop(0, qi // ATTN_PAIRS_PER_TRIP, trip, carry)
    done = (qi // ATTN_PAIRS_PER_TRIP) * ATTN_PAIRS_PER_TRIP
    for i in range(ATTN_PAIRS_PER_TRIP - 1):
        carry = lax.cond(done + i < qi, functools.partial(pair, done + i), lambda c: c, carry)
    jd = 2 * qi
    carry = tuple(step(h, jd, s0, s1, carry[h], bias=bias_ref.at[0], c0_nxt=TILE_K) for h in heads)
    first_block(qT_next_ref)
    carry = tuple(step(h, jd + 1, s1, None, carry[h], bias=bias_ref.at[1], c0=TILE_K) for h in heads)
    for h in heads:
        acc = carry[h][1]
        oT_ref[0, h * V_HEAD_DIM:(h + 1) * V_HEAD_DIM, :] = (
            acc[:V_HEAD_DIM] / acc[V_HEAD_DIM:V_HEAD_DIM + 1]).astype(BF16)


def _mix_kernel(x_ref, oT_ref, wpool_ref, wgate_ref, mixw_ref, pscale_ref, wo_ref, wout_ref,
                g1_ref, b1_ref, wrT_ref, brT_ref, triu_ref,
                h_ref, route_ref, routeT_ref, cnt_ref, ext_ref, base_ref, *dbl_refs):
    b = pl.program_id(0)
    si = pl.program_id(1)
    tm = x_ref.shape[0]

    @pl.when(jnp.logical_and(b == 0, si == 0))
    def _():
        base_ref[...] = jnp.zeros_like(base_ref)

    first = POOL_PAD + POOL_HALO

    @pl.when(si == 0)
    def _():
        ext_ref[0:first, :] = jnp.zeros((first, POOL_WIDTH), F32)
        for d in dbl_refs:
            d[0:POOL_PAD, :] = jnp.zeros((POOL_PAD, POOL_GROUP_DIM), F32)

    x = x_ref[...]
    xb = x.astype(BF16)
    u = _dot(xb, wpool_ref[...])
    ext_ref[first:first + tm, :] = u

    pos1 = si * tm + lax.broadcasted_iota(jnp.int32, (tm, POOL_GROUP_DIM), 0) + 1
    y_parts = []
    for g, win in enumerate(POOL_WINDOWS):
        cols = slice(g * POOL_GROUP_DIM, (g + 1) * POOL_GROUP_DIM)

        def rows(level, lo, n):
            return ext_ref[lo:lo + n, cols] if level == 0 else dbl_refs[(level - 1) % 2][lo:lo + n, :]

        level, m = 0, 1
        while 2 * m < win:
            n = tm + POOL_HALO
            dbl_refs[level % 2][POOL_PAD:POOL_PAD + n, :] = rows(level, POOL_PAD, n) + rows(level, POOL_PAD - m, n)
            level, m = level + 1, 2 * m
        ws = rows(level, first, tm) + rows(level, first - m, tm)
        count = jnp.minimum(pos1, win).astype(F32)
        pooled = ws / count - u[:, cols]
        y_parts.append(_dot(pooled.astype(BF16), mixw_ref[g]))
    ext_ref[POOL_PAD:first, :] = ext_ref[tm + POOL_PAD:tm + first, :]
    y_pool = jnp.concatenate(y_parts, axis=-1) * pscale_ref[...]

    oT = oT_ref[0]
    strips = []
    for c in range(0, D_MODEL, MIX_STRIP):
        cols = slice(c, c + MIX_STRIP)
        gate_pool = jax.nn.sigmoid(_dot(xb, wgate_ref[:, c:c + MIX_STRIP]))
        gate_mla = jax.nn.sigmoid(_dot(xb, wgate_ref[:, D_MODEL + c:D_MODEL + c + MIX_STRIP]))
        y_mla = _dot_tn(oT, wo_ref[:, cols])
        strips.append((gate_pool * y_pool[:, cols] + gate_mla * y_mla).astype(BF16))
    merged = jnp.concatenate(strips, axis=-1)
    r = DEEPNORM_ALPHA * x + _dot(merged, wout_ref[...])
    h = _layer_norm(r, g1_ref[...], b1_ref[...])
    for j in range(ROW_TILES):
        h_ref[pl.ds(j, tm, stride=ROW_TILES), :] = h[:, j * LANES:(j + 1) * LANES]

    logits = _dot_nt(wrT_ref[...], h.astype(BF16)) + brT_ref[...]
    big = float(LANES)
    gl = logits[GROUP_ROW0:GROUP_ROW0 + SUBLANES]
    grow = lax.broadcasted_iota(jnp.int32, gl.shape, 0)
    gl = jnp.where(grow < N_GROUPS, gl, NEG_BIG)
    gmax = jnp.max(gl, axis=0, keepdims=True)
    g_w = 1.0 / jnp.sum(jnp.exp(gl - gmax), axis=0, keepdims=True)
    g_idx = jnp.min(jnp.where(gl == gmax, grow.astype(F32), big), axis=0, keepdims=True)
    el = logits[0:N_EXPERTS]
    erow_i = lax.broadcasted_iota(jnp.int32, el.shape, 0)
    erow = erow_i.astype(F32)
    el = jnp.where((erow_i // EXPERTS_PER_GROUP).astype(F32) == g_idx, el, NEG_BIG)
    e1max = jnp.max(el, axis=0, keepdims=True)
    i1 = jnp.min(jnp.where(el == e1max, erow, big), axis=0, keepdims=True)
    el2 = jnp.where(erow == i1, NEG_BIG, el)
    e2max = jnp.max(el2, axis=0, keepdims=True)
    i2 = jnp.min(jnp.where(el2 == e2max, erow, big), axis=0, keepdims=True)
    ratio = jnp.exp(e2max - e1max)
    gate1 = g_w / (1.0 + ratio)
    gate2 = g_w * ratio / (1.0 + ratio)

    hit1 = erow == i1
    hit2 = erow == i2
    onehot = jnp.where(jnp.logical_or(hit1, hit2), 1.0, 0.0)
    base = base_ref[:, 0:1]
    before = _dot(onehot.astype(BF16), triu_ref[...]) + base
    rank1 = jnp.sum(jnp.where(hit1, before, 0.0), axis=0, keepdims=True)
    rank2 = jnp.sum(jnp.where(hit2, before, 0.0), axis=0, keepdims=True)
    new_base = base + jnp.sum(onehot, axis=1, keepdims=True)
    base_ref[...] = jnp.broadcast_to(new_base, base_ref.shape)
    cnt_ref[...] = jnp.broadcast_to(new_base, cnt_ref.shape)

    row = lax.broadcasted_iota(jnp.int32, (LANES, tm), 0)
    routeT = jnp.where(row == 0, i1, 0.0)
    routeT = jnp.where(row == 1, i2, routeT)
    routeT = jnp.where(row == 2, gate1, routeT)
    routeT = jnp.where(row == 3, gate2, routeT)
    routeT = jnp.where(row == 4, rank1, routeT)
    routeT = jnp.where(row == 5, rank2, routeT)
    routeT_ref[...] = routeT[:SUBLANES]
    route_ref[...] = routeT.T


def _gather_row(idx_ref, r, src_hbm, dst_ref, slot, sem, priority=0):
    src0 = pl.multiple_of(idx_ref[0, 0, r] * ROW_TILES, ROW_TILES)
    pltpu.make_async_copy(src_hbm.at[pl.ds(src0, ROW_TILES), :],
                          dst_ref.at[slot, :, r, :],
                          sem.at[slot]).start(priority=priority)


def _gather_rows_loop(idx_ref, src_hbm, dst_ref, slot, sem, n_rows, unroll=8):
    def body(c, _):
        for u in range(unroll):
            _gather_row(idx_ref, c * unroll + u, src_hbm, dst_ref, slot, sem)
        return 0
    lax.fori_loop(0, n_rows // unroll, body, 0)


def _gather_rows_inline(idx_ref, src_hbm, dst_ref, slot, sem, n_rows):
    for r in range(n_rows):
        _gather_row(idx_ref, r, src_hbm, dst_ref, slot, sem, priority=r % 2)


def _wait_rows(dst_ref, slot, sem):
    pltpu.make_async_copy(dst_ref.at[slot], dst_ref.at[slot], sem.at[slot]).wait()


def _expert_kernel(be_ref, nused_ref, tok_ref, tok1_ref, tok2_ref, h_hbm, wg_ref, wu_ref, wd_ref,
                   y_ref, xbuf, xs_ref, wgb, wub, wdb, sem):
    i = pl.program_id(0)
    n_used = nused_ref[0]
    slot = lax.rem(i, GATHER_SLOTS)
    tb = TILE_EXPERT

    @pl.when(i == 0)
    def _():
        _gather_rows_loop(tok_ref, h_hbm, xbuf, 0, sem, tb)
        _gather_rows_loop(tok1_ref, h_hbm, xbuf, 1, sem, tb)

    changed = jnp.logical_or(i == 0, be_ref[i] != be_ref[jnp.maximum(i - 1, 0)])

    @pl.when(jnp.logical_and(changed, i < n_used))
    def _():
        wgb[...] = wg_ref[0].astype(BF16)
        wub[...] = wu_ref[0].astype(BF16)
        wdb[...] = wd_ref[0].astype(BF16)

    @pl.when(jnp.logical_and(i >= n_used, i < n_used + GATHER_AHEAD))
    def _():
        _wait_rows(xbuf, slot, sem)

    @pl.when(i < n_used)
    def _():
        _wait_rows(xbuf, slot, sem)
        for j in range(ROW_TILES):
            xs_ref[:, j * LANES:(j + 1) * LANES] = xbuf[slot, j].astype(BF16)
        _gather_rows_inline(tok2_ref, h_hbm, xbuf, lax.rem(i + GATHER_AHEAD, GATHER_SLOTS), sem, tb)
        xs = xs_ref[...]
        gate = _dot(xs, wgb[...])
        up = _dot(xs, wub[...])
        hid = (jax.nn.silu(gate) * up).astype(BF16)
        y = _dot(hid, wdb[...])
        for j in range(ROW_TILES):
            y_ref[pl.ds(j, tb, stride=ROW_TILES), :] = y[:, j * LANES:(j + 1) * LANES]

    @pl.when(i >= n_used)
    def _():
        y_ref[...] = jnp.zeros_like(y_ref)


def _combine_kernel(dst_ref, dst1_ref, dst2_ref, h_ref, route_ref, y_hbm, g2_ref, b2_ref, o_ref, gbuf, sem):
    i = pl.program_id(0)
    n = pl.num_programs(0)
    slot = lax.rem(i, GATHER_SLOTS)
    tf = TILE_COMBINE

    @pl.when(i == 0)
    def _():
        _gather_rows_loop(dst_ref, y_hbm, gbuf, 0, sem, TOP_K * tf)
        _gather_rows_loop(dst1_ref, y_hbm, gbuf, 1, sem, TOP_K * tf)

    _wait_rows(gbuf, slot, sem)
    route = route_ref[...]
    gate1 = route[:, 2:3]
    gate2 = route[:, 3:4]
    parts = []
    for j in range(ROW_TILES):
        hj = h_ref[pl.ds(j, tf, stride=ROW_TILES), :]
        y1 = gbuf[slot, j, 0:tf, :]
        y2 = gbuf[slot, j, tf:TOP_K * tf, :]
        parts.append(DEEPNORM_ALPHA * hj + (gate1 * y1 + gate2 * y2))
    z = jnp.concatenate(parts, axis=-1)
    o_ref[...] = _layer_norm(z, g2_ref[...], b2_ref[...])
    _gather_rows_inline(dst2_ref, y_hbm, gbuf, lax.rem(i + GATHER_AHEAD, GATHER_SLOTS), sem, TOP_K * tf)

    @pl.when(i == n - 1)
    def _():
        for ahead in range(1, GATHER_AHEAD + 1):
            _wait_rows(gbuf, lax.rem(i + ahead, GATHER_SLOTS), sem)


def _rope_tables(seq):
    f32 = np.float32
    inv = (f32(1.0) / (f32(ROPE_THETA) ** (np.arange(0, QK_ROPE_DIM, 2, dtype=f32) / f32(QK_ROPE_DIM)))).astype(f32)
    ang = (np.arange(seq, dtype=f32)[:, None] * inv[None, :]).astype(f32)
    cos, sin = np.cos(ang.astype(np.float64)).astype(f32), np.sin(ang.astype(np.float64)).astype(f32)
    zeros = np.zeros((seq, QK_NOPE_DIM), f32)
    pad = np.zeros((seq, HEAD_PAD - QK_NOPE_DIM - QK_ROPE_DIM), f32)
    z16 = np.zeros((seq, HALF_ROPE), f32)
    rot_a = np.concatenate([zeros, cos, cos, pad], axis=1)
    rot_m = np.concatenate([zeros, -sin, z16, pad], axis=1)
    rot_p = np.concatenate([zeros, z16, sin, pad], axis=1)
    return tuple(jnp.asarray(t) for t in (np.ascontiguousarray(cos.T), np.ascontiguousarray(sin.T),
                                          rot_a, rot_m, rot_p))


def _scatter_rows_sparsecore(x, indices, n_out):
    n, width = x.shape
    mesh = plsc.VectorSubcoreMesh(core_axis_name="core", subcore_axis_name="subcore")
    per_core = n // SC_SCATTER_WINDOW // mesh.num_cores
    assert per_core * SC_SCATTER_WINDOW * mesh.num_cores == n

    @functools.partial(pl.kernel, out_type=jax.ShapeDtypeStruct((n_out, width), x.dtype), mesh=mesh,
                       scratch_types=[])
    def scatter(x_hbm, i_hbm, o_hbm):
        first = lax.axis_index("core") * per_core

        def body(x_vmem, i_vmem):
            pltpu.sync_copy(x_vmem, o_hbm.at[i_vmem.at[0]])

        pltpu.emit_pipeline(
            body,
            grid=(per_core,),
            in_specs=[pl.BlockSpec((SC_SCATTER_WINDOW, width), index_map=lambda i: (first + i, 0)),
                      pl.BlockSpec((1, SC_SCATTER_WINDOW), index_map=lambda i: (0, first + i))],
            out_specs=[],
            core_axis_name="subcore",
            dimension_semantics=(pltpu.PARALLEL,),
        )(x_hbm, i_hbm)

    return scatter(x, indices.reshape(1, n))


def _full(shape):
    return pl.BlockSpec(shape, lambda *_: (0,) * len(shape))


def _params(sem):
    return pltpu.CompilerParams(dimension_semantics=sem, vmem_limit_bytes=VMEM_LIMIT)


def kernel(x, w_in, pool_mix_w, pool_scale, q_norm_g, w_uq, kv_norm_g, w_ukv, w_mla_o, w_out, ln1_g, ln1_b,
           w_router_group, b_router_group, w_router_expert, b_router_expert, w_gate, w_up, w_down, ln2_g, ln2_b):
    B, S, D = x.shape
    assert D == D_MODEL and w_in.shape[0] == DEPTH == 1
    assert S % TILE_LATENT == 0 and S % TILE_Q == 0 and S % TILE_MIX == 0 and TILE_Q % TILE_K == 0
    N = B * S
    assert N % TILE_COMBINE == 0
    H = N_HEADS

    w = w_in[0]
    o1 = POOL_WIDTH
    o2 = o1 + Q_LORA_RANK
    o3 = o2 + KV_LORA_RANK
    o4 = o3 + QK_ROPE_DIM
    w_pool = w[:, :o1].astype(BF16)
    kpe_cols = jnp.pad(w[:, o3:o4], ((0, 0), (QK_NOPE_DIM, HEAD_PAD - QK_NOPE_DIM - QK_ROPE_DIM)))
    w_lat = jnp.concatenate([w[:, o1:o3], kpe_cols], axis=1).astype(BF16)
    w_gates = w[:, o4:].astype(BF16)
    qd = QK_NOPE_DIM + QK_ROPE_DIM
    wuq = jnp.pad(w_uq[0].reshape(Q_LORA_RANK, H, qd), ((0, 0), (0, 0), (0, HEAD_PAD - qd)))
    wuqT = wuq.reshape(Q_LORA_RANK, H * HEAD_PAD).T.astype(BF16)
    wukv = w_ukv[0].reshape(KV_LORA_RANK, H, QK_NOPE_DIM + V_HEAD_DIM)
    wuk = jnp.pad(wukv[:, :, :QK_NOPE_DIM], ((0, 0), (0, 0), (0, HEAD_PAD - QK_NOPE_DIM)))
    wuk = wuk.reshape(KV_LORA_RANK, H * HEAD_PAD).astype(BF16)
    wuv = jnp.pad(wukv[:, :, QK_NOPE_DIM:], ((0, 0), (0, 0), (0, V_ROWS - V_HEAD_DIM)))
    wuvT = wuv.reshape(KV_LORA_RANK, H * V_ROWS).T.astype(BF16)
    w_rT = jnp.zeros((LANES, D), F32)
    w_rT = w_rT.at[:N_EXPERTS].set(w_router_expert[0].T).at[GROUP_ROW0:GROUP_ROW0 + N_GROUPS].set(
        w_router_group[0].T).astype(BF16)
    b_rT = jnp.zeros((LANES, 1), F32)
    b_rT = b_rT.at[:N_EXPERTS, 0].set(b_router_expert[0]).at[GROUP_ROW0:GROUP_ROW0 + N_GROUPS, 0].set(
        b_router_group[0])
    cosT, sinT, rot_a, rot_m, rot_p = _rope_tables(S)
    q_scale = (QK_NOPE_DIM + QK_ROPE_DIM) ** -0.5 * math.log2(math.e)
    x2 = x.reshape(N, D)

    ta = TILE_LATENT
    nsa = S // ta
    qT, k, vT = pl.pallas_call(
        functools.partial(_latent_kernel, q_scale=q_scale),
        grid=(B, nsa),
        in_specs=[
            pl.BlockSpec((ta, D), lambda b, s: (b * nsa + s, 0)),
            _full(w_lat.shape), _full((1, Q_LORA_RANK)), _full((1, KV_LORA_RANK)),
            _full(wuqT.shape), _full(wuk.shape), _full(wuvT.shape),
            pl.BlockSpec((HALF_ROPE, ta), lambda b, s: (0, s)),
            pl.BlockSpec((HALF_ROPE, ta), lambda b, s: (0, s)),
            pl.BlockSpec((ta, LANES), lambda b, s: (s, 0)),
            pl.BlockSpec((ta, LANES), lambda b, s: (s, 0)),
            pl.BlockSpec((ta, LANES), lambda b, s: (s, 0)),
        ],
        out_specs=[
            pl.BlockSpec((1, H * HEAD_PAD, ta), lambda b, s: (b, 0, s)),
            pl.BlockSpec((1, ta, H * HEAD_PAD), lambda b, s: (b, s, 0)),
            pl.BlockSpec((1, H * V_ROWS, ta), lambda b, s: (b, 0, s)),
        ],
        out_shape=[
            jax.ShapeDtypeStruct((B, H * HEAD_PAD, S), BF16),
            jax.ShapeDtypeStruct((B, S, H * HEAD_PAD), BF16),
            jax.ShapeDtypeStruct((B, H * V_ROWS, S), BF16),
        ],
        compiler_params=_params(("parallel", "parallel")),
    )(x2, w_lat, q_norm_g[0][None], kv_norm_g[0][None], wuqT, wuk, wuvT, cosT, sinT, rot_a, rot_m, rot_p)

    tq = TILE_Q
    assert tq == 2 * TILE_K
    key_chunk = jnp.arange(tq)[:, None] // CHUNK
    q_chunk = jnp.arange(tq)[None, :] // CHUNK
    mask_bias = jnp.where(key_chunk <= q_chunk, 0.0, NEG_BIG).astype(F32).reshape(2, TILE_K, tq)
    hp = ATTN_HEADS_PER_STEP
    assert H % hp == 0
    oT = pl.pallas_call(
        _attn_kernel,
        grid=(B, H // hp, S // tq),
        in_specs=[
            pl.BlockSpec((1, hp * HEAD_PAD, tq), lambda b, h, q: (b, h, q)),
            pl.BlockSpec((1, hp * HEAD_PAD, tq), lambda b, h, q: (b, h, jnp.minimum(q + 1, S // tq - 1))),
            pl.BlockSpec((1, S, hp * HEAD_PAD), lambda b, h, q: (b, 0, h)),
            pl.BlockSpec((1, hp * V_ROWS, S), lambda b, h, q: (b, h, 0)),
            _full(mask_bias.shape),
        ],
        out_specs=pl.BlockSpec((1, hp * V_HEAD_DIM, tq), lambda b, h, q: (b, h, q)),
        out_shape=jax.ShapeDtypeStruct((B, H * V_HEAD_DIM, S), BF16),
        scratch_shapes=[pltpu.VMEM((TILE_K, tq), F32)] * (2 * hp) + [pltpu.VMEM((hp * SUBLANES, tq), F32)],
        compiler_params=_params(("parallel", "parallel", "arbitrary")),
    )(qT, qT, k, vT, mask_bias)

    tm = TILE_MIX
    nsm = S // tm
    triu = (jnp.arange(tm)[:, None] < jnp.arange(tm)[None, :]).astype(BF16)
    h_tok, route, routeT, counts = pl.pallas_call(
        _mix_kernel,
        grid=(B, nsm),
        in_specs=[
            pl.BlockSpec((tm, D), lambda b, s: (b * nsm + s, 0)),
            pl.BlockSpec((1, H * V_HEAD_DIM, tm), lambda b, s: (b, 0, s)),
            _full(w_pool.shape), _full(w_gates.shape), _full(pool_mix_w.shape[1:]), _full((1, D)),
            _full(w_mla_o.shape[1:]), _full(w_out.shape[1:]), _full((1, D)), _full((1, D)),
            _full(w_rT.shape), _full(b_rT.shape), _full(triu.shape),
        ],
        out_specs=[
            pl.BlockSpec((tm * ROW_TILES, LANES), lambda b, s: (b * nsm + s, 0)),
            pl.BlockSpec((tm, LANES), lambda b, s: (b * nsm + s, 0)),
            pl.BlockSpec((SUBLANES, tm), lambda b, s: (0, b * nsm + s)),
            _full((N_EXPERTS, LANES)),
        ],
        out_shape=[
            jax.ShapeDtypeStruct((N * ROW_TILES, LANES), F32),
            jax.ShapeDtypeStruct((N, LANES), F32),
            jax.ShapeDtypeStruct((SUBLANES, N), F32),
            jax.ShapeDtypeStruct((N_EXPERTS, LANES), F32),
        ],
        scratch_shapes=[pltpu.VMEM((tm + POOL_PAD + POOL_HALO, POOL_WIDTH), F32), pltpu.VMEM((N_EXPERTS, LANES), F32),
                        pltpu.VMEM((tm + POOL_PAD + POOL_HALO, POOL_GROUP_DIM), F32),
                        pltpu.VMEM((tm + POOL_PAD + POOL_HALO, POOL_GROUP_DIM), F32)],
        compiler_params=_params(("arbitrary", "arbitrary")),
    )(x2, oT, w_pool, w_gates, pool_mix_w[0].astype(BF16), pool_scale[0][None], w_mla_o[0].astype(BF16),
      w_out[0].astype(BF16), ln1_g[0][None], ln1_b[0][None], w_rT, b_rT, triu)

    tb = TILE_EXPERT
    A = N * TOP_K
    n_blocks = -(-(A + N_EXPERTS * (tb - 1)) // tb) + GATHER_AHEAD
    R = n_blocks * tb
    e_idx = routeT[0:2].astype(jnp.int32)
    rank = routeT[4:6].astype(jnp.int32)
    cnt = counts[:, 0].astype(jnp.int32)
    padded = ((cnt + tb - 1) // tb) * tb
    pad_end = jnp.cumsum(padded)
    pad_start = pad_end - padded
    is_e = e_idx[:, None, :] == jnp.arange(N_EXPERTS, dtype=jnp.int32)[None, :, None]
    dest = jnp.sum(jnp.where(is_e, pad_start[None, :, None], 0), axis=1) + rank
    block_start = jnp.arange(n_blocks, dtype=jnp.int32) * tb
    block_e = jnp.minimum(jnp.sum(pad_end[None, :] <= block_start[:, None], axis=1), N_EXPERTS - 1).astype(jnp.int32)
    n_used = (pad_end[-1] // tb).astype(jnp.int32)[None]
    tok = jnp.broadcast_to(jnp.arange(N, dtype=jnp.int32)[None, :, None], (TOP_K, N, LANES)).reshape(A, LANES)
    tok_rows = _scatter_rows_sparsecore(tok, dest.reshape(A), R)[:, 0].reshape(n_blocks, tb)
    block_is_e = block_e[:, None] == jnp.arange(N_EXPERTS, dtype=jnp.int32)[None, :]
    block_pad_start = jnp.sum(jnp.where(block_is_e, pad_start[None, :], 0), axis=1)
    block_cnt = jnp.sum(jnp.where(block_is_e, cnt[None, :], 0), axis=1)
    row_in_expert = block_start[:, None] + jnp.arange(tb, dtype=jnp.int32)[None, :] - block_pad_start[:, None]
    row_valid = jnp.logical_and(row_in_expert < block_cnt[:, None],
                                jnp.arange(n_blocks, dtype=jnp.int32)[:, None] < n_used[0])
    any_tok = (block_start[:, None] + jnp.arange(tb, dtype=jnp.int32)[None, :]) % N
    row_tok3 = jnp.where(row_valid, tok_rows, any_tok).reshape(n_blocks, 1, tb)

    y_rows = pl.pallas_call(
        _expert_kernel,
        grid_spec=pltpu.PrefetchScalarGridSpec(
            num_scalar_prefetch=2,
            grid=(n_blocks,),
            in_specs=[
                pl.BlockSpec((1, 1, tb), lambda i, be, nu: (i, 0, 0), memory_space=pltpu.SMEM),
                pl.BlockSpec((1, 1, tb), lambda i, be, nu: (jnp.minimum(i + 1, n_blocks - 1), 0, 0),
                             memory_space=pltpu.SMEM),
                pl.BlockSpec((1, 1, tb), lambda i, be, nu: (jnp.minimum(i + GATHER_AHEAD, n_blocks - 1), 0, 0),
                             memory_space=pltpu.SMEM),
                pl.BlockSpec(memory_space=pl.ANY),
                pl.BlockSpec((1, D, D_EXPERT), lambda i, be, nu: (be[i], 0, 0)),
                pl.BlockSpec((1, D, D_EXPERT), lambda i, be, nu: (be[i], 0, 0)),
                pl.BlockSpec((1, D_EXPERT, D), lambda i, be, nu: (be[i], 0, 0)),
            ],
            out_specs=pl.BlockSpec((tb * ROW_TILES, LANES), lambda i, be, nu: (i, 0)),
            scratch_shapes=[
                pltpu.VMEM((GATHER_SLOTS, ROW_TILES, tb, LANES), F32), pltpu.VMEM((tb, D), BF16),
                pltpu.VMEM((D, D_EXPERT), BF16), pltpu.VMEM((D, D_EXPERT), BF16), pltpu.VMEM((D_EXPERT, D), BF16),
                pltpu.SemaphoreType.DMA((GATHER_SLOTS,)),
            ],
        ),
        out_shape=jax.ShapeDtypeStruct((R * ROW_TILES, LANES), F32),
        compiler_params=_params(("arbitrary",)),
    )(block_e, n_used, row_tok3, row_tok3, row_tok3, h_tok, w_gate[0], w_up[0], w_down[0])

    tf = TILE_COMBINE
    nf = N // tf
    dest3 = dest.reshape(TOP_K, nf, tf).transpose(1, 0, 2).reshape(nf, 1, TOP_K * tf)
    out = pl.pallas_call(
        _combine_kernel,
        grid=(nf,),
        in_specs=[
            pl.BlockSpec((1, 1, TOP_K * tf), lambda i: (i, 0, 0), memory_space=pltpu.SMEM),
            pl.BlockSpec((1, 1, TOP_K * tf), lambda i: (jnp.minimum(i + 1, nf - 1), 0, 0), memory_space=pltpu.SMEM),
            pl.BlockSpec((1, 1, TOP_K * tf), lambda i: (jnp.minimum(i + GATHER_AHEAD, nf - 1), 0, 0),
                         memory_space=pltpu.SMEM),
            pl.BlockSpec((tf * ROW_TILES, LANES), lambda i: (i, 0)),
            pl.BlockSpec((tf, LANES), lambda i: (i, 0)),
            pl.BlockSpec(memory_space=pl.ANY),
            _full((1, D)), _full((1, D)),
        ],
        out_specs=pl.BlockSpec((tf, D), lambda i: (i, 0)),
        out_shape=jax.ShapeDtypeStruct((N, D), F32),
        scratch_shapes=[pltpu.VMEM((GATHER_SLOTS, ROW_TILES, TOP_K * tf, LANES), F32),
                        pltpu.SemaphoreType.DMA((GATHER_SLOTS,))],
        compiler_params=_params(("arbitrary",)),
    )(dest3, dest3, dest3, h_tok, route, y_rows, ln2_g[0][None], ln2_b[0][None])
    return out.reshape(B, S, D)
```

```python
import functools
import math

import jax
import jax.numpy as jnp
import numpy as np
from jax import lax
from jax.experimental import pallas as pl
from jax.experimental.pallas import tpu as pltpu
from jax.experimental.pallas import tpu_sc as plsc

D_MODEL = 1024
CHUNK = 64
POOL_WINDOWS = (2, 4, 8, 16)
POOL_GROUPS = len(POOL_WINDOWS)
POOL_WIDTH = D_MODEL // 2
POOL_GROUP_DIM = POOL_WIDTH // POOL_GROUPS
POOL_OUT_GROUP_DIM = D_MODEL // POOL_GROUPS
POOL_HALO = 16
POOL_PAD = 8
N_HEADS = 8
QK_NOPE_DIM = D_MODEL // 16
QK_ROPE_DIM = D_MODEL // 32
HALF_ROPE = QK_ROPE_DIM // 2
V_HEAD_DIM = D_MODEL // 16
V_ROWS = V_HEAD_DIM + 16
Q_LORA_RANK = 3 * D_MODEL // 8
KV_LORA_RANK = D_MODEL // 4
ROPE_THETA = 10000.0
N_GROUPS = 4
EXPERTS_PER_GROUP = 8
N_EXPERTS = N_GROUPS * EXPERTS_PER_GROUP
TOP_K = 2
D_EXPERT = D_MODEL // 2
NORM_EPS = 1e-5
DEPTH = 1
DEEPNORM_ALPHA = (2.0 * DEPTH) ** 0.25

LANES = 128
SUBLANES = 8
HEAD_PAD = LANES
ROW_TILES = D_MODEL // LANES
GROUP_ROW0 = 64
NEG_BIG = -1e30
VMEM_LIMIT = 56 * 1024 * 1024

TILE_LATENT = 1024
TILE_Q = 512
TILE_K = 256
ATTN_HEADS_PER_STEP = 4
ATTN_PAIRS_PER_TRIP = 2
TILE_MIX = 512
MIX_STRIP = 256
GATHER_AHEAD = 2
GATHER_SLOTS = GATHER_AHEAD + 1
TILE_EXPERT = 512
SC_SCATTER_WINDOW = 128
TILE_COMBINE = 512

F32 = jnp.float32
BF16 = jnp.bfloat16


def _dot(a, b):
    return jnp.dot(a, b, preferred_element_type=F32)


def _dot_nt(a, b):
    return lax.dot_general(a, b, (((1,), (1,)), ((), ())), preferred_element_type=F32)


def _dot_tn(a, b):
    return lax.dot_general(a, b, (((0,), (0,)), ((), ())), preferred_element_type=F32)


def _rms(v, g):
    ms = jnp.mean(jnp.square(v), axis=-1, keepdims=True)
    return v * lax.rsqrt(ms + NORM_EPS) * g


def _layer_norm(v, g, b):
    mu = jnp.mean(v, axis=-1, keepdims=True)
    c = v - mu
    var = jnp.mean(jnp.square(c), axis=-1, keepdims=True)
    return c * lax.rsqrt(var + NORM_EPS) * g + b


def _latent_kernel(x_ref, wlat_ref, gq_ref, gkv_ref, wuqT_ref, wuk_ref, wuvT_ref,
                   cosT_ref, sinT_ref, ra_ref, rm_ref, rp_ref,
                   qT_ref, k_ref, vT_ref, *, q_scale):
    xb = x_ref[...].astype(BF16)
    lat = _dot(xb, wlat_ref[...])
    c_q = lat[:, :Q_LORA_RANK]
    c_kv = lat[:, Q_LORA_RANK:Q_LORA_RANK + KV_LORA_RANK]
    kpe = lat[:, Q_LORA_RANK + KV_LORA_RANK:]
    qn = _rms(c_q, gq_ref[...]).astype(BF16)
    kvn = _rms(c_kv, gkv_ref[...]).astype(BF16)

    qT = _dot_nt(wuqT_ref[...], qn) * q_scale
    cosT = cosT_ref[...]
    sinT = sinT_ref[...]
    for h in range(N_HEADS):
        r0 = h * HEAD_PAD
        x1 = qT[r0 + QK_NOPE_DIM:r0 + QK_NOPE_DIM + HALF_ROPE]
        x2 = qT[r0 + QK_NOPE_DIM + HALF_ROPE:r0 + QK_NOPE_DIM + QK_ROPE_DIM]
        qT_ref[0, r0:r0 + QK_NOPE_DIM, :] = qT[r0:r0 + QK_NOPE_DIM].astype(BF16)
        qT_ref[0, r0 + QK_NOPE_DIM:r0 + QK_NOPE_DIM + HALF_ROPE, :] = (x1 * cosT - x2 * sinT).astype(BF16)
        qT_ref[0, r0 + QK_NOPE_DIM + HALF_ROPE:r0 + QK_NOPE_DIM + QK_ROPE_DIM, :] = (
            x1 * sinT + x2 * cosT).astype(BF16)
        qT_ref[0, r0 + QK_NOPE_DIM + QK_ROPE_DIM:r0 + HEAD_PAD, :] = jnp.zeros(
            (HEAD_PAD - QK_NOPE_DIM - QK_ROPE_DIM, qT.shape[1]), BF16)

    kpe_rot = (kpe * ra_ref[...] + pltpu.roll(kpe, LANES - HALF_ROPE, 1) * rm_ref[...]
               + pltpu.roll(kpe, HALF_ROPE, 1) * rp_ref[...])
    k = _dot(kvn, wuk_ref[...])
    for h in range(N_HEADS):
        k_ref[0, :, h * HEAD_PAD:(h + 1) * HEAD_PAD] = (k[:, h * HEAD_PAD:(h + 1) * HEAD_PAD] + kpe_rot).astype(BF16)
    vT = _dot_nt(wuvT_ref[...], kvn)
    row = lax.broadcasted_iota(jnp.int32, vT.shape, 0) % V_ROWS
    vT_ref[0] = jnp.where(row == V_HEAD_DIM, 1.0, vT).astype(BF16)


def _attn_kernel(qT_ref, qT_next_ref, k_ref, vT_ref, bias_ref, oT_ref, *bufs):
    qi = pl.program_id(2)
    tq = qT_ref.shape[2]
    heads = range(ATTN_HEADS_PER_STEP)
    s0, s1 = (bufs[i * ATTN_HEADS_PER_STEP:(i + 1) * ATTN_HEADS_PER_STEP] for i in range(2))
    first_max = bufs[2 * ATTN_HEADS_PER_STEP]

    def scores(h, j, c0=0):
        k0 = pl.multiple_of(j * TILE_K, TILE_K)
        return _dot(k_ref[0, pl.ds(k0, TILE_K), h * HEAD_PAD:(h + 1) * HEAD_PAD],
                    qT_ref[0, h * HEAD_PAD:(h + 1) * HEAD_PAD, c0:])

    def col_max(s):
        while s.shape[0] > SUBLANES:
            half = s.shape[0] // 2
            s = jnp.maximum(s[:half], s[half:])
        return jnp.max(s, axis=0, keepdims=True)

    def step(h, j, s_cur, s_nxt, carry, bias=None, c0=0, c0_nxt=0):
        m_all, acc_all, cmax = carry
        cmax_nxt = None
        if s_nxt is not None:
            s_new = scores(h, j + 1, c0_nxt)
            s_nxt[h][:, c0_nxt:] = s_new
            cmax_nxt = col_max(s_new)
        m, acc = m_all[:, c0:], acc_all[:, c0:]
        s = s_cur[h][:, c0:]
        if bias is not None:
            s = s + bias[:, c0:]
            cmax = col_max(s)
        m_new = jnp.maximum(m, cmax)
        p = jnp.exp2(s - m_new).astype(BF16)
        k0 = pl.multiple_of(j * TILE_K, TILE_K)
        pv = _dot(vT_ref[0, h * V_ROWS:(h + 1) * V_ROWS, pl.ds(k0, TILE_K)], p)
        acc = jnp.exp2(m - m_new) * acc + pv
        if c0:
            m_new = jnp.concatenate([m_all[:, :c0], m_new], axis=1)
            acc = jnp.concatenate([acc_all[:, :c0], acc], axis=1)
        return m_new, acc, cmax_nxt

    def first_block(q_ref):
        for h in heads:
            s_first = _dot(k_ref[0, 0:TILE_K, h * HEAD_PAD:(h + 1) * HEAD_PAD],
                           q_ref[0, h * HEAD_PAD:(h + 1) * HEAD_PAD, :])
            s0[h][...] = s_first
            first_max[h * SUBLANES:h * SUBLANES + 1, :] = col_max(s_first)

    @pl.when(qi == 0)
    def _():
        first_block(qT_ref)

    carry = tuple((jnp.full((1, tq), NEG_BIG, F32), jnp.zeros((V_ROWS, tq), F32),
                   first_max[h * SUBLANES:h * SUBLANES + 1, :]) for h in heads)

    def pair(t, c):
        c = tuple(step(h, 2 * t, s0, s1, c[h]) for h in heads)
        return tuple(step(h, 2 * t + 1, s1, s0, c[h]) for h in heads)

    def trip(u, c):
        for i in range(ATTN_PAIRS_PER_TRIP):
            c = pair(ATTN_PAIRS_PER_TRIP * u + i, c)
        return c

    carry = lax.fori_loop(0, qi // ATTN_PAIRS_PER_TRIP, trip, carry)
    done = (qi // ATTN_PAIRS_PER_TRIP) * ATTN_PAIRS_PER_TRIP
    for i in range(ATTN_PAIRS_PER_TRIP - 1):
        carry = lax.cond(done + i < qi, functools.partial(pair, done + i), lambda c: c, carry)
    jd = 2 * qi
    carry = tuple(step(h, jd, s0, s1, carry[h], bias=bias_ref.at[0], c0_nxt=TILE_K) for h in heads)
    first_block(qT_next_ref)
    carry = tuple(step(h, jd + 1, s1, None, carry[h], bias=bias_ref.at[1], c0=TILE_K) for h in heads)
    for h in heads:
        acc = carry[h][1]
        oT_ref[0, h * V_HEAD_DIM:(h + 1) * V_HEAD_DIM, :] = (
            acc[:V_HEAD_DIM] / acc[V_HEAD_DIM:V_HEAD_DIM + 1]).astype(BF16)


def _mix_kernel(x_ref, oT_ref, wpool_ref, wgate_ref, mixw_ref, pscale_ref, wo_ref, wout_ref,
                g1_ref, b1_ref, wrT_ref, brT_ref, triu_ref,
                h_ref, route_ref, routeT_ref, cnt_ref, ext_ref, base_ref, *dbl_refs):
    b = pl.program_id(0)
    si = pl.program_id(1)
    tm = x_ref.shape[0]

    @pl.when(jnp.logical_and(b == 0, si == 0))
    def _():
        base_ref[...] = jnp.zeros_like(base_ref)

    first = POOL_PAD + POOL_HALO

    @pl.when(si == 0)
    def _():
        ext_ref[0:first, :] = jnp.zeros((first, POOL_WIDTH), F32)
        for d in dbl_refs:
            d[0:POOL_PAD, :] = jnp.zeros((POOL_PAD, POOL_GROUP_DIM), F32)

    x = x_ref[...]
    xb = x.astype(BF16)
    u = _dot(xb, wpool_ref[...])
    ext_ref[first:first + tm, :] = u

    pos1 = si * tm + lax.broadcasted_iota(jnp.int32, (tm, POOL_GROUP_DIM), 0) + 1
    y_parts = []
    for g, win in enumerate(POOL_WINDOWS):
        cols = slice(g * POOL_GROUP_DIM, (g + 1) * POOL_GROUP_DIM)

        def rows(level, lo, n):
            return ext_ref[lo:lo + n, cols] if level == 0 else dbl_refs[(level - 1) % 2][lo:lo + n, :]

        level, m = 0, 1
        while 2 * m < win:
            n = tm + POOL_HALO
            dbl_refs[level % 2][POOL_PAD:POOL_PAD + n, :] = rows(level, POOL_PAD, n) + rows(level, POOL_PAD - m, n)
            level, m = level + 1, 2 * m
        ws = rows(level, first, tm) + rows(level, first - m, tm)
        count = jnp.minimum(pos1, win).astype(F32)
        pooled = ws / count - u[:, cols]
        y_parts.append(_dot(pooled.astype(BF16), mixw_ref[g]))
    ext_ref[POOL_PAD:first, :] = ext_ref[tm + POOL_PAD:tm + first, :]
    y_pool = jnp.concatenate(y_parts, axis=-1) * pscale_ref[...]

    oT = oT_ref[0]
    strips = []
    for c in range(0, D_MODEL, MIX_STRIP):
        cols = slice(c, c + MIX_STRIP)
        gate_pool = jax.nn.sigmoid(_dot(xb, wgate_ref[:, c:c + MIX_STRIP]))
        gate_mla = jax.nn.sigmoid(_dot(xb, wgate_ref[:, D_MODEL + c:D_MODEL + c + MIX_STRIP]))
        y_mla = _dot_tn(oT, wo_ref[:, cols])
        strips.append((gate_pool * y_pool[:, cols] + gate_mla * y_mla).astype(BF16))
    merged = jnp.concatenate(strips, axis=-1)
    r = DEEPNORM_ALPHA * x + _dot(merged, wout_ref[...])
    h = _layer_norm(r, g1_ref[...], b1_ref[...])
    for j in range(ROW_TILES):
        h_ref[pl.ds(j, tm, stride=ROW_TILES), :] = h[:, j * LANES:(j + 1) * LANES]

    logits = _dot_nt(wrT_ref[...], h.astype(BF16)) + brT_ref[...]
    big = float(LANES)
    gl = logits[GROUP_ROW0:GROUP_ROW0 + SUBLANES]
    grow = lax.broadcasted_iota(jnp.int32, gl.shape, 0)
    gl = jnp.where(grow < N_GROUPS, gl, NEG_BIG)
    gmax = jnp.max(gl, axis=0, keepdims=True)
    g_w = 1.0 / jnp.sum(jnp.exp(gl - gmax), axis=0, keepdims=True)
    g_idx = jnp.min(jnp.where(gl == gmax, grow.astype(F32), big), axis=0, keepdims=True)
    el = logits[0:N_EXPERTS]
    erow_i = lax.broadcasted_iota(jnp.int32, el.shape, 0)
    erow = erow_i.astype(F32)
    el = jnp.where((erow_i // EXPERTS_PER_GROUP).astype(F32) == g_idx, el, NEG_BIG)
    e1max = jnp.max(el, axis=0, keepdims=True)
    i1 = jnp.min(jnp.where(el == e1max, erow, big), axis=0, keepdims=True)
    el2 = jnp.where(erow == i1, NEG_BIG, el)
    e2max = jnp.max(el2, axis=0, keepdims=True)
    i2 = jnp.min(jnp.where(el2 == e2max, erow, big), axis=0, keepdims=True)
    ratio = jnp.exp(e2max - e1max)
    gate1 = g_w / (1.0 + ratio)
    gate2 = g_w * ratio / (1.0 + ratio)

    hit1 = erow == i1
    hit2 = erow == i2
    onehot = jnp.where(jnp.logical_or(hit1, hit2), 1.0, 0.0)
    base = base_ref[:, 0:1]
    before = _dot(onehot.astype(BF16), triu_ref[...]) + base
    rank1 = jnp.sum(jnp.where(hit1, before, 0.0), axis=0, keepdims=True)
    rank2 = jnp.sum(jnp.where(hit2, before, 0.0), axis=0, keepdims=True)
    new_base = base + jnp.sum(onehot, axis=1, keepdims=True)
    base_ref[...] = jnp.broadcast_to(new_base, base_ref.shape)
    cnt_ref[...] = jnp.broadcast_to(new_base, cnt_ref.shape)

    row = lax.broadcasted_iota(jnp.int32, (LANES, tm), 0)
    routeT = jnp.where(row == 0, i1, 0.0)
    routeT = jnp.where(row == 1, i2, routeT)
    routeT = jnp.where(row == 2, gate1, routeT)
    routeT = jnp.where(row == 3, gate2, routeT)
    routeT = jnp.where(row == 4, rank1, routeT)
    routeT = jnp.where(row == 5, rank2, routeT)
    routeT_ref[...] = routeT[:SUBLANES]
    route_ref[...] = routeT.T


def _gather_row(idx_ref, r, src_hbm, dst_ref, slot, sem, priority=0):
    src0 = pl.multiple_of(idx_ref[0, 0, r] * ROW_TILES, ROW_TILES)
    pltpu.make_async_copy(src_hbm.at[pl.ds(src0, ROW_TILES), :],
                          dst_ref.at[slot, :, r, :],
                          sem.at[slot]).start(priority=priority)


def _gather_rows_loop(idx_ref, src_hbm, dst_ref, slot, sem, n_rows, unroll=8):
    def body(c, _):
        for u in range(unroll):
            _gather_row(idx_ref, c * unroll + u, src_hbm, dst_ref, slot, sem)
        return 0
    lax.fori_loop(0, n_rows // unroll, body, 0)


def _gather_rows_inline(idx_ref, src_hbm, dst_ref, slot, sem, n_rows):
    for r in range(n_rows):
        _gather_row(idx_ref, r, src_hbm, dst_ref, slot, sem, priority=r % 2)


def _wait_rows(dst_ref, slot, sem):
    pltpu.make_async_copy(dst_ref.at[slot], dst_ref.at[slot], sem.at[slot]).wait()


def _expert_kernel(be_ref, nused_ref, tok_ref, tok1_ref, tok2_ref, h_hbm, wg_ref, wu_ref, wd_ref,
                   y_hbm, xbuf, xs_ref, wgb, wub, wdb, sem, ybuf, ysem):
    i = pl.program_id(0)
    n_used = nused_ref[0]
    slot = lax.rem(i, GATHER_SLOTS)
    tb = TILE_EXPERT
    yslot = i % 2

    def y_copy(j, slot_, step_):
        return pltpu.make_async_copy(ybuf.at[slot_, :, pl.ds(j * LANES, LANES)],
                                     y_hbm.at[pl.ds(step_ * tb, tb), j, :], ysem.at[slot_])

    @pl.when(i >= 2)
    def _():
        for j in range(ROW_TILES):
            y_copy(j, yslot, i - 2).wait()

    @pl.when(i == 0)
    def _():
        _gather_rows_loop(tok_ref, h_hbm, xbuf, 0, sem, tb)
        _gather_rows_loop(tok1_ref, h_hbm, xbuf, 1, sem, tb)

    changed = jnp.logical_or(i == 0, be_ref[i] != be_ref[jnp.maximum(i - 1, 0)])

    @pl.when(jnp.logical_and(changed, i < n_used))
    def _():
        wgb[...] = wg_ref[0].astype(BF16)
        wub[...] = wu_ref[0].astype(BF16)
        wdb[...] = wd_ref[0].astype(BF16)

    @pl.when(jnp.logical_and(i >= n_used, i < n_used + GATHER_AHEAD))
    def _():
        _wait_rows(xbuf, slot, sem)

    @pl.when(i < n_used)
    def _():
        _wait_rows(xbuf, slot, sem)
        for j in range(ROW_TILES):
            xs_ref[:, j * LANES:(j + 1) * LANES] = xbuf[slot, j].astype(BF16)
        _gather_rows_inline(tok2_ref, h_hbm, xbuf, lax.rem(i + GATHER_AHEAD, GATHER_SLOTS), sem, tb)
        xs = xs_ref[...]
        gate = _dot(xs, wgb[...])
        up = _dot(xs, wub[...])
        hid = (jax.nn.silu(gate) * up).astype(BF16)
        ybuf[yslot] = _dot(hid, wdb[...])

    @pl.when(i >= n_used)
    def _():
        ybuf[yslot] = jnp.zeros((tb, D_MODEL), F32)

    for j in range(ROW_TILES):
        y_copy(j, yslot, i).start()

    @pl.when(i == pl.num_programs(0) - 1)
    def _():
        for j in range(ROW_TILES):
            y_copy(j, 1 - yslot, i - 1).wait()
            y_copy(j, yslot, i).wait()


def _combine_kernel(dst_ref, dst1_ref, dst2_ref, h_ref, route_ref, y_hbm, g2_ref, b2_ref, o_ref, gbuf, sem):
    i = pl.program_id(0)
    n = pl.num_programs(0)
    slot = lax.rem(i, GATHER_SLOTS)
    tf = TILE_COMBINE

    @pl.when(i == 0)
    def _():
        _gather_rows_loop(dst_ref, y_hbm, gbuf, 0, sem, TOP_K * tf)
        _gather_rows_loop(dst1_ref, y_hbm, gbuf, 1, sem, TOP_K * tf)

    _wait_rows(gbuf, slot, sem)
    route = route_ref[...]
    gate1 = route[:, 2:3]
    gate2 = route[:, 3:4]
    parts = []
    for j in range(ROW_TILES):
        hj = h_ref[pl.ds(j, tf, stride=ROW_TILES), :]
        y1 = gbuf[slot, j, 0:tf, :]
        y2 = gbuf[slot, j, tf:TOP_K * tf, :]
        parts.append(DEEPNORM_ALPHA * hj + (gate1 * y1 + gate2 * y2))
    z = jnp.concatenate(parts, axis=-1)
    o_ref[...] = _layer_norm(z, g2_ref[...], b2_ref[...])
    _gather_rows_inline(dst2_ref, y_hbm, gbuf, lax.rem(i + GATHER_AHEAD, GATHER_SLOTS), sem, TOP_K * tf)

    @pl.when(i == n - 1)
    def _():
        for ahead in range(1, GATHER_AHEAD + 1):
            _wait_rows(gbuf, lax.rem(i + ahead, GATHER_SLOTS), sem)


def _rope_tables(seq):
    f32 = np.float32
    inv = (f32(1.0) / (f32(ROPE_THETA) ** (np.arange(0, QK_ROPE_DIM, 2, dtype=f32) / f32(QK_ROPE_DIM)))).astype(f32)
    ang = (np.arange(seq, dtype=f32)[:, None] * inv[None, :]).astype(f32)
    cos, sin = np.cos(ang.astype(np.float64)).astype(f32), np.sin(ang.astype(np.float64)).astype(f32)
    zeros = np.zeros((seq, QK_NOPE_DIM), f32)
    pad = np.zeros((seq, HEAD_PAD - QK_NOPE_DIM - QK_ROPE_DIM), f32)
    z16 = np.zeros((seq, HALF_ROPE), f32)
    rot_a = np.concatenate([zeros, cos, cos, pad], axis=1)
    rot_m = np.concatenate([zeros, -sin, z16, pad], axis=1)
    rot_p = np.concatenate([zeros, z16, sin, pad], axis=1)
    return tuple(jnp.asarray(t) for t in (np.ascontiguousarray(cos.T), np.ascontiguousarray(sin.T),
                                          rot_a, rot_m, rot_p))


def _scatter_rows_sparsecore(x, indices, n_out):
    n, width = x.shape
    mesh = plsc.VectorSubcoreMesh(core_axis_name="core", subcore_axis_name="subcore")
    per_core = n // SC_SCATTER_WINDOW // mesh.num_cores
    assert per_core * SC_SCATTER_WINDOW * mesh.num_cores == n

    @functools.partial(pl.kernel, out_type=jax.ShapeDtypeStruct((n_out, width), x.dtype), mesh=mesh,
                       scratch_types=[])
    def scatter(x_hbm, i_hbm, o_hbm):
        first = lax.axis_index("core") * per_core

        def body(x_vmem, i_vmem):
            pltpu.sync_copy(x_vmem, o_hbm.at[i_vmem.at[0]])

        pltpu.emit_pipeline(
            body,
            grid=(per_core,),
            in_specs=[pl.BlockSpec((SC_SCATTER_WINDOW, width), index_map=lambda i: (first + i, 0)),
                      pl.BlockSpec((1, SC_SCATTER_WINDOW), index_map=lambda i: (0, first + i))],
            out_specs=[],
            core_axis_name="subcore",
            dimension_semantics=(pltpu.PARALLEL,),
        )(x_hbm, i_hbm)

    return scatter(x, indices.reshape(1, n))


def _full(shape):
    return pl.BlockSpec(shape, lambda *_: (0,) * len(shape))


def _params(sem):
    return pltpu.CompilerParams(dimension_semantics=sem, vmem_limit_bytes=VMEM_LIMIT)


def kernel(x, w_in, pool_mix_w, pool_scale, q_norm_g, w_uq, kv_norm_g, w_ukv, w_mla_o, w_out, ln1_g, ln1_b,
           w_router_group, b_router_group, w_router_expert, b_router_expert, w_gate, w_up, w_down, ln2_g, ln2_b):
    B, S, D = x.shape
    assert D == D_MODEL and w_in.shape[0] == DEPTH == 1
    assert S % TILE_LATENT == 0 and S % TILE_Q == 0 and S % TILE_MIX == 0 and TILE_Q % TILE_K == 0
    N = B * S
    assert N % TILE_COMBINE == 0
    H = N_HEADS

    w = w_in[0]
    o1 = POOL_WIDTH
    o2 = o1 + Q_LORA_RANK
    o3 = o2 + KV_LORA_RANK
    o4 = o3 + QK_ROPE_DIM
    w_pool = w[:, :o1].astype(BF16)
    kpe_cols = jnp.pad(w[:, o3:o4], ((0, 0), (QK_NOPE_DIM, HEAD_PAD - QK_NOPE_DIM - QK_ROPE_DIM)))
    w_lat = jnp.concatenate([w[:, o1:o3], kpe_cols], axis=1).astype(BF16)
    w_gates = w[:, o4:].astype(BF16)
    qd = QK_NOPE_DIM + QK_ROPE_DIM
    wuq = jnp.pad(w_uq[0].reshape(Q_LORA_RANK, H, qd), ((0, 0), (0, 0), (0, HEAD_PAD - qd)))
    wuqT = wuq.reshape(Q_LORA_RANK, H * HEAD_PAD).T.astype(BF16)
    wukv = w_ukv[0].reshape(KV_LORA_RANK, H, QK_NOPE_DIM + V_HEAD_DIM)
    wuk = jnp.pad(wukv[:, :, :QK_NOPE_DIM], ((0, 0), (0, 0), (0, HEAD_PAD - QK_NOPE_DIM)))
    wuk = wuk.reshape(KV_LORA_RANK, H * HEAD_PAD).astype(BF16)
    wuv = jnp.pad(wukv[:, :, QK_NOPE_DIM:], ((0, 0), (0, 0), (0, V_ROWS - V_HEAD_DIM)))
    wuvT = wuv.reshape(KV_LORA_RANK, H * V_ROWS).T.astype(BF16)
    w_rT = jnp.zeros((LANES, D), F32)
    w_rT = w_rT.at[:N_EXPERTS].set(w_router_expert[0].T).at[GROUP_ROW0:GROUP_ROW0 + N_GROUPS].set(
        w_router_group[0].T).astype(BF16)
    b_rT = jnp.zeros((LANES, 1), F32)
    b_rT = b_rT.at[:N_EXPERTS, 0].set(b_router_expert[0]).at[GROUP_ROW0:GROUP_ROW0 + N_GROUPS, 0].set(
        b_router_group[0])
    cosT, sinT, rot_a, rot_m, rot_p = _rope_tables(S)
    q_scale = (QK_NOPE_DIM + QK_ROPE_DIM) ** -0.5 * math.log2(math.e)
    x2 = x.reshape(N, D)

    ta = TILE_LATENT
    nsa = S // ta
    qT, k, vT = pl.pallas_call(
        functools.partial(_latent_kernel, q_scale=q_scale),
        grid=(B, nsa),
        in_specs=[
            pl.BlockSpec((ta, D), lambda b, s: (b * nsa + s, 0)),
            _full(w_lat.shape), _full((1, Q_LORA_RANK)), _full((1, KV_LORA_RANK)),
            _full(wuqT.shape), _full(wuk.shape), _full(wuvT.shape),
            pl.BlockSpec((HALF_ROPE, ta), lambda b, s: (0, s)),
            pl.BlockSpec((HALF_ROPE, ta), lambda b, s: (0, s)),
            pl.BlockSpec((ta, LANES), lambda b, s: (s, 0)),
            pl.BlockSpec((ta, LANES), lambda b, s: (s, 0)),
            pl.BlockSpec((ta, LANES), lambda b, s: (s, 0)),
        ],
        out_specs=[
            pl.BlockSpec((1, H * HEAD_PAD, ta), lambda b, s: (b, 0, s)),
            pl.BlockSpec((1, ta, H * HEAD_PAD), lambda b, s: (b, s, 0)),
            pl.BlockSpec((1, H * V_ROWS, ta), lambda b, s: (b, 0, s)),
        ],
        out_shape=[
            jax.ShapeDtypeStruct((B, H * HEAD_PAD, S), BF16),
            jax.ShapeDtypeStruct((B, S, H * HEAD_PAD), BF16),
            jax.ShapeDtypeStruct((B, H * V_ROWS, S), BF16),
        ],
        compiler_params=_params(("parallel", "parallel")),
    )(x2, w_lat, q_norm_g[0][None], kv_norm_g[0][None], wuqT, wuk, wuvT, cosT, sinT, rot_a, rot_m, rot_p)

    tq = TILE_Q
    assert tq == 2 * TILE_K
    key_chunk = jnp.arange(tq)[:, None] // CHUNK
    q_chunk = jnp.arange(tq)[None, :] // CHUNK
    mask_bias = jnp.where(key_chunk <= q_chunk, 0.0, NEG_BIG).astype(F32).reshape(2, TILE_K, tq)
    hp = ATTN_HEADS_PER_STEP
    assert H % hp == 0
    oT = pl.pallas_call(
        _attn_kernel,
        grid=(B, H // hp, S // tq),
        in_specs=[
            pl.BlockSpec((1, hp * HEAD_PAD, tq), lambda b, h, q: (b, h, q)),
            pl.BlockSpec((1, hp * HEAD_PAD, tq), lambda b, h, q: (b, h, jnp.minimum(q + 1, S // tq - 1))),
            pl.BlockSpec((1, S, hp * HEAD_PAD), lambda b, h, q: (b, 0, h)),
            pl.BlockSpec((1, hp * V_ROWS, S), lambda b, h, q: (b, h, 0)),
            _full(mask_bias.shape),
        ],
        out_specs=pl.BlockSpec((1, hp * V_HEAD_DIM, tq), lambda b, h, q: (b, h, q)),
        out_shape=jax.ShapeDtypeStruct((B, H * V_HEAD_DIM, S), BF16),
        scratch_shapes=[pltpu.VMEM((TILE_K, tq), F32)] * (2 * hp) + [pltpu.VMEM((hp * SUBLANES, tq), F32)],
        compiler_params=_params(("parallel", "parallel", "arbitrary")),
    )(qT, qT, k, vT, mask_bias)

    tm = TILE_MIX
    nsm = S // tm
    triu = (jnp.arange(tm)[:, None] < jnp.arange(tm)[None, :]).astype(BF16)
    h_tok, route, routeT, counts = pl.pallas_call(
        _mix_kernel,
        grid=(B, nsm),
        in_specs=[
            pl.BlockSpec((tm, D), lambda b, s: (b * nsm + s, 0)),
            pl.BlockSpec((1, H * V_HEAD_DIM, tm), lambda b, s: (b, 0, s)),
            _full(w_pool.shape), _full(w_gates.shape), _full(pool_mix_w.shape[1:]), _full((1, D)),
            _full(w_mla_o.shape[1:]), _full(w_out.shape[1:]), _full((1, D)), _full((1, D)),
            _full(w_rT.shape), _full(b_rT.shape), _full(triu.shape),
        ],
        out_specs=[
            pl.BlockSpec((tm * ROW_TILES, LANES), lambda b, s: (b * nsm + s, 0)),
            pl.BlockSpec((tm, LANES), lambda b, s: (b * nsm + s, 0)),
            pl.BlockSpec((SUBLANES, tm), lambda b, s: (0, b * nsm + s)),
            _full((N_EXPERTS, LANES)),
        ],
        out_shape=[
            jax.ShapeDtypeStruct((N * ROW_TILES, LANES), F32),
            jax.ShapeDtypeStruct((N, LANES), F32),
            jax.ShapeDtypeStruct((SUBLANES, N), F32),
            jax.ShapeDtypeStruct((N_EXPERTS, LANES), F32),
        ],
        scratch_shapes=[pltpu.VMEM((tm + POOL_PAD + POOL_HALO, POOL_WIDTH), F32), pltpu.VMEM((N_EXPERTS, LANES), F32),
                        pltpu.VMEM((tm + POOL_PAD + POOL_HALO, POOL_GROUP_DIM), F32),
                        pltpu.VMEM((tm + POOL_PAD + POOL_HALO, POOL_GROUP_DIM), F32)],
        compiler_params=_params(("arbitrary", "arbitrary")),
    )(x2, oT, w_pool, w_gates, pool_mix_w[0].astype(BF16), pool_scale[0][None], w_mla_o[0].astype(BF16),
      w_out[0].astype(BF16), ln1_g[0][None], ln1_b[0][None], w_rT, b_rT, triu)

    tb = TILE_EXPERT
    A = N * TOP_K
    n_blocks = -(-(A + N_EXPERTS * (tb - 1)) // tb) + GATHER_AHEAD
    R = n_blocks * tb
    e_idx = routeT[0:2].astype(jnp.int32)
    rank = routeT[4:6].astype(jnp.int32)
    cnt = counts[:, 0].astype(jnp.int32)
    padded = ((cnt + tb - 1) // tb) * tb
    pad_end = jnp.cumsum(padded)
    pad_start = pad_end - padded
    is_e = e_idx[:, None, :] == jnp.arange(N_EXPERTS, dtype=jnp.int32)[None, :, None]
    dest = jnp.sum(jnp.where(is_e, pad_start[None, :, None], 0), axis=1) + rank
    block_start = jnp.arange(n_blocks, dtype=jnp.int32) * tb
    block_e = jnp.minimum(jnp.sum(pad_end[None, :] <= block_start[:, None], axis=1), N_EXPERTS - 1).astype(jnp.int32)
    n_used = (pad_end[-1] // tb).astype(jnp.int32)[None]
    tok = jnp.broadcast_to(jnp.arange(N, dtype=jnp.int32)[None, :, None], (TOP_K, N, LANES)).reshape(A, LANES)
    tok_rows = _scatter_rows_sparsecore(tok, dest.reshape(A), R)[:, 0].reshape(n_blocks, tb)
    block_is_e = block_e[:, None] == jnp.arange(N_EXPERTS, dtype=jnp.int32)[None, :]
    block_pad_start = jnp.sum(jnp.where(block_is_e, pad_start[None, :], 0), axis=1)
    block_cnt = jnp.sum(jnp.where(block_is_e, cnt[None, :], 0), axis=1)
    row_in_expert = block_start[:, None] + jnp.arange(tb, dtype=jnp.int32)[None, :] - block_pad_start[:, None]
    row_valid = jnp.logical_and(row_in_expert < block_cnt[:, None],
                                jnp.arange(n_blocks, dtype=jnp.int32)[:, None] < n_used[0])
    any_tok = (block_start[:, None] + jnp.arange(tb, dtype=jnp.int32)[None, :]) % N
    row_tok3 = jnp.where(row_valid, tok_rows, any_tok).reshape(n_blocks, 1, tb)

    y_rows = pl.pallas_call(
        _expert_kernel,
        grid_spec=pltpu.PrefetchScalarGridSpec(
            num_scalar_prefetch=2,
            grid=(n_blocks,),
            in_specs=[
                pl.BlockSpec((1, 1, tb), lambda i, be, nu: (i, 0, 0), memory_space=pltpu.SMEM),
                pl.BlockSpec((1, 1, tb), lambda i, be, nu: (jnp.minimum(i + 1, n_blocks - 1), 0, 0),
                             memory_space=pltpu.SMEM),
                pl.BlockSpec((1, 1, tb), lambda i, be, nu: (jnp.minimum(i + GATHER_AHEAD, n_blocks - 1), 0, 0),
                             memory_space=pltpu.SMEM),
                pl.BlockSpec(memory_space=pl.ANY),
                pl.BlockSpec((1, D, D_EXPERT), lambda i, be, nu: (be[i], 0, 0)),
                pl.BlockSpec((1, D, D_EXPERT), lambda i, be, nu: (be[i], 0, 0)),
                pl.BlockSpec((1, D_EXPERT, D), lambda i, be, nu: (be[i], 0, 0)),
            ],
            out_specs=pl.BlockSpec(memory_space=pl.ANY),
            scratch_shapes=[
                pltpu.VMEM((GATHER_SLOTS, ROW_TILES, tb, LANES), F32), pltpu.VMEM((tb, D), BF16),
                pltpu.VMEM((D, D_EXPERT), BF16), pltpu.VMEM((D, D_EXPERT), BF16), pltpu.VMEM((D_EXPERT, D), BF16),
                pltpu.SemaphoreType.DMA((GATHER_SLOTS,)),
                pltpu.VMEM((2, tb, D), F32), pltpu.SemaphoreType.DMA((2,)),
            ],
        ),
        out_shape=jax.ShapeDtypeStruct((R, ROW_TILES, LANES), F32),
        compiler_params=_params(("arbitrary",)),
    )(block_e, n_used, row_tok3, row_tok3, row_tok3, h_tok, w_gate[0], w_up[0], w_down[0])
    y_rows = y_rows.reshape(R * ROW_TILES, LANES)

    tf = TILE_COMBINE
    nf = N // tf
    dest3 = dest.reshape(TOP_K, nf, tf).transpose(1, 0, 2).reshape(nf, 1, TOP_K * tf)
    out = pl.pallas_call(
        _combine_kernel,
        grid=(nf,),
        in_specs=[
            pl.BlockSpec((1, 1, TOP_K * tf), lambda i: (i, 0, 0), memory_space=pltpu.SMEM),
            pl.BlockSpec((1, 1, TOP_K * tf), lambda i: (jnp.minimum(i + 1, nf - 1), 0, 0), memory_space=pltpu.SMEM),
            pl.BlockSpec((1, 1, TOP_K * tf), lambda i: (jnp.minimum(i + GATHER_AHEAD, nf - 1), 0, 0),
                         memory_space=pltpu.SMEM),
            pl.BlockSpec((tf * ROW_TILES, LANES), lambda i: (i, 0)),
            pl.BlockSpec((tf, LANES), lambda i: (i, 0)),
            pl.BlockSpec(memory_space=pl.ANY),
            _full((1, D)), _full((1, D)),
        ],
        out_specs=pl.BlockSpec((tf, D), lambda i: (i, 0)),
        out_shape=jax.ShapeDtypeStruct((N, D), F32),
        scratch_shapes=[pltpu.VMEM((GATHER_SLOTS, ROW_TILES, TOP_K * tf, LANES), F32),
                        pltpu.SemaphoreType.DMA((GATHER_SLOTS,))],
        compiler_params=_params(("arbitrary",)),
    )(dest3, dest3, dest3, h_tok, route, y_rows, ln2_g[0][None], ln2_b[0][None])
    return out.reshape(B, S, D)
```

```python
import functools
import math

import jax
import jax.numpy as jnp
import numpy as np
from jax import lax
from jax.experimental import pallas as pl
from jax.experimental.pallas import tpu as pltpu
from jax.experimental.pallas import tpu_sc as plsc

D_MODEL = 1024
CHUNK = 64
POOL_WINDOWS = (2, 4, 8, 16)
POOL_GROUPS = len(POOL_WINDOWS)
POOL_WIDTH = D_MODEL // 2
POOL_GROUP_DIM = POOL_WIDTH // POOL_GROUPS
POOL_OUT_GROUP_DIM = D_MODEL // POOL_GROUPS
POOL_HALO = 16
POOL_PAD = 8
N_HEADS = 8
QK_NOPE_DIM = D_MODEL // 16
QK_ROPE_DIM = D_MODEL // 32
HALF_ROPE = QK_ROPE_DIM // 2
V_HEAD_DIM = D_MODEL // 16
V_ROWS = V_HEAD_DIM + 16
Q_LORA_RANK = 3 * D_MODEL // 8
KV_LORA_RANK = D_MODEL // 4
ROPE_THETA = 10000.0
N_GROUPS = 4
EXPERTS_PER_GROUP = 8
N_EXPERTS = N_GROUPS * EXPERTS_PER_GROUP
TOP_K = 2
D_EXPERT = D_MODEL // 2
NORM_EPS = 1e-5
DEPTH = 1
DEEPNORM_ALPHA = (2.0 * DEPTH) ** 0.25

LANES = 128
SUBLANES = 8
HEAD_PAD = LANES
ROW_TILES = D_MODEL // LANES
GROUP_ROW0 = 64
NEG_BIG = -1e30
VMEM_LIMIT = 56 * 1024 * 1024

TILE_LATENT = 1024
TILE_Q = 512
TILE_K = 256
ATTN_HEADS_PER_STEP = 4
ATTN_PAIRS_PER_TRIP = 2
TILE_MIX = 512
MIX_STRIP = 256
GATHER_AHEAD = 2
GATHER_SLOTS = GATHER_AHEAD + 1
TILE_EXPERT = 512
SC_SCATTER_WINDOW = 128
TILE_COMBINE = 512

F32 = jnp.float32
BF16 = jnp.bfloat16


def _dot(a, b):
    return jnp.dot(a, b, preferred_element_type=F32)


def _dot_nt(a, b):
    return lax.dot_general(a, b, (((1,), (1,)), ((), ())), preferred_element_type=F32)


def _dot_tn(a, b):
    return lax.dot_general(a, b, (((0,), (0,)), ((), ())), preferred_element_type=F32)


def _rms(v, g):
    ms = jnp.mean(jnp.square(v), axis=-1, keepdims=True)
    return v * lax.rsqrt(ms + NORM_EPS) * g


def _layer_norm(v, g, b):
    mu = jnp.mean(v, axis=-1, keepdims=True)
    c = v - mu
    var = jnp.mean(jnp.square(c), axis=-1, keepdims=True)
    return c * lax.rsqrt(var + NORM_EPS) * g + b


def _latent_kernel(x_ref, wlat_ref, gq_ref, gkv_ref, wuqT_ref, wuk_ref, wuvT_ref,
                   cosT_ref, sinT_ref, ra_ref, rm_ref, rp_ref,
                   qT_ref, k_ref, vT_ref, *, q_scale):
    xb = x_ref[...].astype(BF16)
    lat = _dot(xb, wlat_ref[...])
    c_q = lat[:, :Q_LORA_RANK]
    c_kv = lat[:, Q_LORA_RANK:Q_LORA_RANK + KV_LORA_RANK]
    kpe = lat[:, Q_LORA_RANK + KV_LORA_RANK:]
    qn = _rms(c_q, gq_ref[...]).astype(BF16)
    kvn = _rms(c_kv, gkv_ref[...]).astype(BF16)

    qT = _dot_nt(wuqT_ref[...], qn) * q_scale
    cosT = cosT_ref[...]
    sinT = sinT_ref[...]
    for h in range(N_HEADS):
        r0 = h * HEAD_PAD
        x1 = qT[r0 + QK_NOPE_DIM:r0 + QK_NOPE_DIM + HALF_ROPE]
        x2 = qT[r0 + QK_NOPE_DIM + HALF_ROPE:r0 + QK_NOPE_DIM + QK_ROPE_DIM]
        qT_ref[0, r0:r0 + QK_NOPE_DIM, :] = qT[r0:r0 + QK_NOPE_DIM].astype(BF16)
        qT_ref[0, r0 + QK_NOPE_DIM:r0 + QK_NOPE_DIM + HALF_ROPE, :] = (x1 * cosT - x2 * sinT).astype(BF16)
        qT_ref[0, r0 + QK_NOPE_DIM + HALF_ROPE:r0 + QK_NOPE_DIM + QK_ROPE_DIM, :] = (
            x1 * sinT + x2 * cosT).astype(BF16)
        qT_ref[0, r0 + QK_NOPE_DIM + QK_ROPE_DIM:r0 + HEAD_PAD, :] = jnp.zeros(
            (HEAD_PAD - QK_NOPE_DIM - QK_ROPE_DIM, qT.shape[1]), BF16)

    kpe_rot = (kpe * ra_ref[...] + pltpu.roll(kpe, LANES - HALF_ROPE, 1) * rm_ref[...]
               + pltpu.roll(kpe, HALF_ROPE, 1) * rp_ref[...])
    k = _dot(kvn, wuk_ref[...])
    for h in range(N_HEADS):
        k_ref[0, :, h * HEAD_PAD:(h + 1) * HEAD_PAD] = (k[:, h * HEAD_PAD:(h + 1) * HEAD_PAD] + kpe_rot).astype(BF16)
    vT = _dot_nt(wuvT_ref[...], kvn)
    row = lax.broadcasted_iota(jnp.int32, vT.shape, 0) % V_ROWS
    vT_ref[0] = jnp.where(row == V_HEAD_DIM, 1.0, vT).astype(BF16)


def _attn_kernel(qT_ref, qT_next_ref, k_ref, vT_ref, bias_ref, oT_ref, *bufs):
    qi = pl.program_id(2)
    tq = qT_ref.shape[2]
    heads = range(ATTN_HEADS_PER_STEP)
    s0, s1 = (bufs[i * ATTN_HEADS_PER_STEP:(i + 1) * ATTN_HEADS_PER_STEP] for i in range(2))
    first_max = bufs[2 * ATTN_HEADS_PER_STEP]

    def scores(h, j, c0=0):
        k0 = pl.multiple_of(j * TILE_K, TILE_K)
        return _dot(k_ref[0, pl.ds(k0, TILE_K), h * HEAD_PAD:(h + 1) * HEAD_PAD],
                    qT_ref[0, h * HEAD_PAD:(h + 1) * HEAD_PAD, c0:])

    def col_max(s):
        while s.shape[0] > SUBLANES:
            half = s.shape[0] // 2
            s = jnp.maximum(s[:half], s[half:])
        return jnp.max(s, axis=0, keepdims=True)

    def step(h, j, s_cur, s_nxt, carry, bias=None, c0=0, c0_nxt=0, bias_nxt=None):
        m_all, acc_all, cmax = carry
        cmax_nxt = None
        if s_nxt is not None:
            s_new = scores(h, j + 1, c0_nxt)
            if bias_nxt is not None:
                s_new = s_new + bias_nxt[:, c0_nxt:]
            s_nxt[h][:, c0_nxt:] = s_new
            cmax_nxt = col_max(s_new)
        m, acc = m_all[:, c0:], acc_all[:, c0:]
        s = s_cur[h][:, c0:]
        if bias is not None:
            s = s + bias[:, c0:]
            cmax = col_max(s)
        m_new = jnp.maximum(m, cmax)
        p = jnp.exp2(s - m_new).astype(BF16)
        k0 = pl.multiple_of(j * TILE_K, TILE_K)
        pv = _dot(vT_ref[0, h * V_ROWS:(h + 1) * V_ROWS, pl.ds(k0, TILE_K)], p)
        acc = jnp.exp2(m - m_new) * acc + pv
        if c0:
            m_new = jnp.concatenate([m_all[:, :c0], m_new], axis=1)
            acc = jnp.concatenate([acc_all[:, :c0], acc], axis=1)
        return m_new, acc, cmax_nxt

    def first_block(q_ref):
        for h in heads:
            s_first = _dot(k_ref[0, 0:TILE_K, h * HEAD_PAD:(h + 1) * HEAD_PAD],
                           q_ref[0, h * HEAD_PAD:(h + 1) * HEAD_PAD, :])
            s0[h][...] = s_first
            first_max[h * SUBLANES:h * SUBLANES + 1, :] = col_max(s_first)

    @pl.when(qi == 0)
    def _():
        first_block(qT_ref)

    carry = tuple((jnp.full((1, tq), NEG_BIG, F32), jnp.zeros((V_ROWS, tq), F32),
                   first_max[h * SUBLANES:h * SUBLANES + 1, :]) for h in heads)

    def pair(t, c):
        c = tuple(step(h, 2 * t, s0, s1, c[h]) for h in heads)
        return tuple(step(h, 2 * t + 1, s1, s0, c[h]) for h in heads)

    def trip(u, c):
        for i in range(ATTN_PAIRS_PER_TRIP):
            c = pair(ATTN_PAIRS_PER_TRIP * u + i, c)
        return c

    carry = lax.fori_loop(0, qi // ATTN_PAIRS_PER_TRIP, trip, carry)
    done = (qi // ATTN_PAIRS_PER_TRIP) * ATTN_PAIRS_PER_TRIP
    for i in range(ATTN_PAIRS_PER_TRIP - 1):
        carry = lax.cond(done + i < qi, functools.partial(pair, done + i), lambda c: c, carry)
    jd = 2 * qi
    carry = tuple(step(h, jd, s0, s1, carry[h], bias=bias_ref.at[0], c0_nxt=TILE_K, bias_nxt=bias_ref.at[1])
                  for h in heads)
    first_block(qT_next_ref)
    carry = tuple(step(h, jd + 1, s1, None, carry[h], c0=TILE_K) for h in heads)
    for h in heads:
        acc = carry[h][1]
        oT_ref[0, h * V_HEAD_DIM:(h + 1) * V_HEAD_DIM, :] = (
            acc[:V_HEAD_DIM] / acc[V_HEAD_DIM:V_HEAD_DIM + 1]).astype(BF16)


def _mix_kernel(x_ref, oT_ref, wpool_ref, wgate_ref, mixw_ref, pscale_ref, wo_ref, wout_ref,
                g1_ref, b1_ref, wrT_ref, brT_ref, triu_ref,
                h_ref, route_ref, routeT_ref, cnt_ref, ext_ref, base_ref, *dbl_refs):
    b = pl.program_id(0)
    si = pl.program_id(1)
    tm = x_ref.shape[0]

    @pl.when(jnp.logical_and(b == 0, si == 0))
    def _():
        base_ref[...] = jnp.zeros_like(base_ref)

    first = POOL_PAD + POOL_HALO

    @pl.when(si == 0)
    def _():
        ext_ref[0:first, :] = jnp.zeros((first, POOL_WIDTH), F32)
        for d in dbl_refs:
            d[0:POOL_PAD, :] = jnp.zeros((POOL_PAD, POOL_GROUP_DIM), F32)

    x = x_ref[...]
    xb = x.astype(BF16)
    u = _dot(xb, wpool_ref[...])
    ext_ref[first:first + tm, :] = u

    pos1 = si * tm + lax.broadcasted_iota(jnp.int32, (tm, POOL_GROUP_DIM), 0) + 1
    y_parts = []
    for g, win in enumerate(POOL_WINDOWS):
        cols = slice(g * POOL_GROUP_DIM, (g + 1) * POOL_GROUP_DIM)

        def rows(level, lo, n):
            return ext_ref[lo:lo + n, cols] if level == 0 else dbl_refs[(level - 1) % 2][lo:lo + n, :]

        level, m = 0, 1
        while 2 * m < win:
            n = tm + POOL_HALO
            dbl_refs[level % 2][POOL_PAD:POOL_PAD + n, :] = rows(level, POOL_PAD, n) + rows(level, POOL_PAD - m, n)
            level, m = level + 1, 2 * m
        ws = rows(level, first, tm) + rows(level, first - m, tm)
        count = jnp.minimum(pos1, win).astype(F32)
        pooled = ws / count - u[:, cols]
        y_parts.append(_dot(pooled.astype(BF16), mixw_ref[g]))
    ext_ref[POOL_PAD:first, :] = ext_ref[tm + POOL_PAD:tm + first, :]
    y_pool = jnp.concatenate(y_parts, axis=-1) * pscale_ref[...]

    oT = oT_ref[0]
    strips = []
    for c in range(0, D_MODEL, MIX_STRIP):
        cols = slice(c, c + MIX_STRIP)
        gate_pool = jax.nn.sigmoid(_dot(xb, wgate_ref[:, c:c + MIX_STRIP]))
        gate_mla = jax.nn.sigmoid(_dot(xb, wgate_ref[:, D_MODEL + c:D_MODEL + c + MIX_STRIP]))
        y_mla = _dot_tn(oT, wo_ref[:, cols])
        strips.append((gate_pool * y_pool[:, cols] + gate_mla * y_mla).astype(BF16))
    merged = jnp.concatenate(strips, axis=-1)
    r = DEEPNORM_ALPHA * x + _dot(merged, wout_ref[...])
    h = _layer_norm(r, g1_ref[...], b1_ref[...])
    for j in range(ROW_TILES):
        h_ref[pl.ds(j, tm, stride=ROW_TILES), :] = h[:, j * LANES:(j + 1) * LANES]

    logits = _dot_nt(wrT_ref[...], h.astype(BF16)) + brT_ref[...]
    big = float(LANES)
    gl = logits[GROUP_ROW0:GROUP_ROW0 + SUBLANES]
    grow = lax.broadcasted_iota(jnp.int32, gl.shape, 0)
    gl = jnp.where(grow < N_GROUPS, gl, NEG_BIG)
    gmax = jnp.max(gl, axis=0, keepdims=True)
    g_w = 1.0 / jnp.sum(jnp.exp(gl - gmax), axis=0, keepdims=True)
    g_idx = jnp.min(jnp.where(gl == gmax, grow.astype(F32), big), axis=0, keepdims=True)
    el = logits[0:N_EXPERTS]
    erow_i = lax.broadcasted_iota(jnp.int32, el.shape, 0)
    erow = erow_i.astype(F32)
    el = jnp.where((erow_i // EXPERTS_PER_GROUP).astype(F32) == g_idx, el, NEG_BIG)
    e1max = jnp.max(el, axis=0, keepdims=True)
    i1 = jnp.min(jnp.where(el == e1max, erow, big), axis=0, keepdims=True)
    el2 = jnp.where(erow == i1, NEG_BIG, el)
    e2max = jnp.max(el2, axis=0, keepdims=True)
    i2 = jnp.min(jnp.where(el2 == e2max, erow, big), axis=0, keepdims=True)
    ratio = jnp.exp(e2max - e1max)
    gate1 = g_w / (1.0 + ratio)
    gate2 = g_w * ratio / (1.0 + ratio)

    hit1 = erow == i1
    hit2 = erow == i2
    onehot = jnp.where(jnp.logical_or(hit1, hit2), 1.0, 0.0)
    base = base_ref[:, 0:1]
    before = _dot(onehot.astype(BF16), triu_ref[...]) + base
    rank1 = jnp.sum(jnp.where(hit1, before, 0.0), axis=0, keepdims=True)
    rank2 = jnp.sum(jnp.where(hit2, before, 0.0), axis=0, keepdims=True)
    new_base = base + jnp.sum(onehot, axis=1, keepdims=True)
    base_ref[...] = jnp.broadcast_to(new_base, base_ref.shape)
    cnt_ref[...] = jnp.broadcast_to(new_base, cnt_ref.shape)

    row = lax.broadcasted_iota(jnp.int32, (LANES, tm), 0)
    routeT = jnp.where(row == 0, i1, 0.0)
    routeT = jnp.where(row == 1, i2, routeT)
    routeT = jnp.where(row == 2, gate1, routeT)
    routeT = jnp.where(row == 3, gate2, routeT)
    routeT = jnp.where(row == 4, rank1, routeT)
    routeT = jnp.where(row == 5, rank2, routeT)
    routeT_ref[...] = routeT[:SUBLANES]
    route_ref[...] = routeT.T


def _gather_row(idx_ref, r, src_hbm, dst_ref, slot, sem, priority=0):
    src0 = pl.multiple_of(idx_ref[0, 0, r] * ROW_TILES, ROW_TILES)
    pltpu.make_async_copy(src_hbm.at[pl.ds(src0, ROW_TILES), :],
                          dst_ref.at[slot, :, r, :],
                          sem.at[slot]).start(priority=priority)


def _gather_rows_loop(idx_ref, src_hbm, dst_ref, slot, sem, n_rows, unroll=8):
    def body(c, _):
        for u in range(unroll):
            _gather_row(idx_ref, c * unroll + u, src_hbm, dst_ref, slot, sem)
        return 0
    lax.fori_loop(0, n_rows // unroll, body, 0)


def _gather_rows_inline(idx_ref, src_hbm, dst_ref, slot, sem, n_rows):
    for r in range(n_rows):
        _gather_row(idx_ref, r, src_hbm, dst_ref, slot, sem, priority=r % 2)


def _wait_rows(dst_ref, slot, sem):
    pltpu.make_async_copy(dst_ref.at[slot], dst_ref.at[slot], sem.at[slot]).wait()


def _expert_kernel(be_ref, nused_ref, tok_ref, tok1_ref, tok2_ref, h_hbm, wg_ref, wu_ref, wd_ref,
                   y_ref, xbuf, xs_ref, wgb, wub, wdb, sem):
    i = pl.program_id(0)
    n_used = nused_ref[0]
    slot = lax.rem(i, GATHER_SLOTS)
    tb = TILE_EXPERT

    @pl.when(i == 0)
    def _():
        _gather_rows_loop(tok_ref, h_hbm, xbuf, 0, sem, tb)
        _gather_rows_loop(tok1_ref, h_hbm, xbuf, 1, sem, tb)

    changed = jnp.logical_or(i == 0, be_ref[i] != be_ref[jnp.maximum(i - 1, 0)])

    @pl.when(jnp.logical_and(changed, i < n_used))
    def _():
        wgb[...] = wg_ref[0].astype(BF16)
        wub[...] = wu_ref[0].astype(BF16)
        wdb[...] = wd_ref[0].astype(BF16)

    @pl.when(jnp.logical_and(i >= n_used, i < n_used + GATHER_AHEAD))
    def _():
        _wait_rows(xbuf, slot, sem)

    @pl.when(i < n_used)
    def _():
        _wait_rows(xbuf, slot, sem)
        for j in range(ROW_TILES):
            xs_ref[:, j * LANES:(j + 1) * LANES] = xbuf[slot, j].astype(BF16)
        _gather_rows_inline(tok2_ref, h_hbm, xbuf, lax.rem(i + GATHER_AHEAD, GATHER_SLOTS), sem, tb)
        xs = xs_ref[...]
        gate = _dot(xs, wgb[...])
        up = _dot(xs, wub[...])
        hid = (jax.nn.silu(gate) * up).astype(BF16)
        y = _dot(hid, wdb[...])
        for j in range(ROW_TILES):
            y_ref[pl.ds(j, tb, stride=ROW_TILES), :] = y[:, j * LANES:(j + 1) * LANES]

    @pl.when(i >= n_used)
    def _():
        y_ref[...] = jnp.zeros_like(y_ref)


def _combine_kernel(dst_ref, dst1_ref, dst2_ref, h_ref, route_ref, y_hbm, g2_ref, b2_ref, o_ref, gbuf, sem):
    i = pl.program_id(0)
    n = pl.num_programs(0)
    slot = lax.rem(i, GATHER_SLOTS)
    tf = TILE_COMBINE

    @pl.when(i == 0)
    def _():
        _gather_rows_loop(dst_ref, y_hbm, gbuf, 0, sem, TOP_K * tf)
        _gather_rows_loop(dst1_ref, y_hbm, gbuf, 1, sem, TOP_K * tf)

    _wait_rows(gbuf, slot, sem)
    route = route_ref[...]
    gate1 = route[:, 2:3]
    gate2 = route[:, 3:4]
    parts = []
    for j in range(ROW_TILES):
        hj = h_ref[pl.ds(j, tf, stride=ROW_TILES), :]
        y1 = gbuf[slot, j, 0:tf, :]
        y2 = gbuf[slot, j, tf:TOP_K * tf, :]
        parts.append(DEEPNORM_ALPHA * hj + (gate1 * y1 + gate2 * y2))
    z = jnp.concatenate(parts, axis=-1)
    o_ref[...] = _layer_norm(z, g2_ref[...], b2_ref[...])
    _gather_rows_inline(dst2_ref, y_hbm, gbuf, lax.rem(i + GATHER_AHEAD, GATHER_SLOTS), sem, TOP_K * tf)

    @pl.when(i == n - 1)
    def _():
        for ahead in range(1, GATHER_AHEAD + 1):
            _wait_rows(gbuf, lax.rem(i + ahead, GATHER_SLOTS), sem)


def _rope_tables(seq):
    f32 = np.float32
    inv = (f32(1.0) / (f32(ROPE_THETA) ** (np.arange(0, QK_ROPE_DIM, 2, dtype=f32) / f32(QK_ROPE_DIM)))).astype(f32)
    ang = (np.arange(seq, dtype=f32)[:, None] * inv[None, :]).astype(f32)
    cos, sin = np.cos(ang.astype(np.float64)).astype(f32), np.sin(ang.astype(np.float64)).astype(f32)
    zeros = np.zeros((seq, QK_NOPE_DIM), f32)
    pad = np.zeros((seq, HEAD_PAD - QK_NOPE_DIM - QK_ROPE_DIM), f32)
    z16 = np.zeros((seq, HALF_ROPE), f32)
    rot_a = np.concatenate([zeros, cos, cos, pad], axis=1)
    rot_m = np.concatenate([zeros, -sin, z16, pad], axis=1)
    rot_p = np.concatenate([zeros, z16, sin, pad], axis=1)
    return tuple(jnp.asarray(t) for t in (np.ascontiguousarray(cos.T), np.ascontiguousarray(sin.T),
                                          rot_a, rot_m, rot_p))


def _scatter_rows_sparsecore(x, indices, n_out):
    n, width = x.shape
    mesh = plsc.VectorSubcoreMesh(core_axis_name="core", subcore_axis_name="subcore")
    per_core = n // SC_SCATTER_WINDOW // mesh.num_cores
    assert per_core * SC_SCATTER_WINDOW * mesh.num_cores == n

    @functools.partial(pl.kernel, out_type=jax.ShapeDtypeStruct((n_out, width), x.dtype), mesh=mesh,
                       scratch_types=[])
    def scatter(x_hbm, i_hbm, o_hbm):
        first = lax.axis_index("core") * per_core

        def body(x_vmem, i_vmem):
            pltpu.sync_copy(x_vmem, o_hbm.at[i_vmem.at[0]])

        pltpu.emit_pipeline(
            body,
            grid=(per_core,),
            in_specs=[pl.BlockSpec((SC_SCATTER_WINDOW, width), index_map=lambda i: (first + i, 0)),
                      pl.BlockSpec((1, SC_SCATTER_WINDOW), index_map=lambda i: (0, first + i))],
            out_specs=[],
            core_axis_name="subcore",
            dimension_semantics=(pltpu.PARALLEL,),
        )(x_hbm, i_hbm)

    return scatter(x, indices.reshape(1, n))


def _full(shape):
    return pl.BlockSpec(shape, lambda *_: (0,) * len(shape))


def _params(sem):
    return pltpu.CompilerParams(dimension_semantics=sem, vmem_limit_bytes=VMEM_LIMIT)


def kernel(x, w_in, pool_mix_w, pool_scale, q_norm_g, w_uq, kv_norm_g, w_ukv, w_mla_o, w_out, ln1_g, ln1_b,
           w_router_group, b_router_group, w_router_expert, b_router_expert, w_gate, w_up, w_down, ln2_g, ln2_b):
    B, S, D = x.shape
    assert D == D_MODEL and w_in.shape[0] == DEPTH == 1
    assert S % TILE_LATENT == 0 and S % TILE_Q == 0 and S % TILE_MIX == 0 and TILE_Q % TILE_K == 0
    N = B * S
    assert N % TILE_COMBINE == 0
    H = N_HEADS

    w = w_in[0]
    o1 = POOL_WIDTH
    o2 = o1 + Q_LORA_RANK
    o3 = o2 + KV_LORA_RANK
    o4 = o3 + QK_ROPE_DIM
    w_pool = w[:, :o1].astype(BF16)
    kpe_cols = jnp.pad(w[:, o3:o4], ((0, 0), (QK_NOPE_DIM, HEAD_PAD - QK_NOPE_DIM - QK_ROPE_DIM)))
    w_lat = jnp.concatenate([w[:, o1:o3], kpe_cols], axis=1).astype(BF16)
    w_gates = w[:, o4:].astype(BF16)
    qd = QK_NOPE_DIM + QK_ROPE_DIM
    wuq = jnp.pad(w_uq[0].reshape(Q_LORA_RANK, H, qd), ((0, 0), (0, 0), (0, HEAD_PAD - qd)))
    wuqT = wuq.reshape(Q_LORA_RANK, H * HEAD_PAD).T.astype(BF16)
    wukv = w_ukv[0].reshape(KV_LORA_RANK, H, QK_NOPE_DIM + V_HEAD_DIM)
    wuk = jnp.pad(wukv[:, :, :QK_NOPE_DIM], ((0, 0), (0, 0), (0, HEAD_PAD - QK_NOPE_DIM)))
    wuk = wuk.reshape(KV_LORA_RANK, H * HEAD_PAD).astype(BF16)
    wuv = jnp.pad(wukv[:, :, QK_NOPE_DIM:], ((0, 0), (0, 0), (0, V_ROWS - V_HEAD_DIM)))
    wuvT = wuv.reshape(KV_LORA_RANK, H * V_ROWS).T.astype(BF16)
    w_rT = jnp.zeros((LANES, D), F32)
    w_rT = w_rT.at[:N_EXPERTS].set(w_router_expert[0].T).at[GROUP_ROW0:GROUP_ROW0 + N_GROUPS].set(
        w_router_group[0].T).astype(BF16)
    b_rT = jnp.zeros((LANES, 1), F32)
    b_rT = b_rT.at[:N_EXPERTS, 0].set(b_router_expert[0]).at[GROUP_ROW0:GROUP_ROW0 + N_GROUPS, 0].set(
        b_router_group[0])
    cosT, sinT, rot_a, rot_m, rot_p = _rope_tables(S)
    q_scale = (QK_NOPE_DIM + QK_ROPE_DIM) ** -0.5 * math.log2(math.e)
    x2 = x.reshape(N, D)

    ta = TILE_LATENT
    nsa = S // ta
    qT, k, vT = pl.pallas_call(
        functools.partial(_latent_kernel, q_scale=q_scale),
        grid=(B, nsa),
        in_specs=[
            pl.BlockSpec((ta, D), lambda b, s: (b * nsa + s, 0)),
            _full(w_lat.shape), _full((1, Q_LORA_RANK)), _full((1, KV_LORA_RANK)),
            _full(wuqT.shape), _full(wuk.shape), _full(wuvT.shape),
            pl.BlockSpec((HALF_ROPE, ta), lambda b, s: (0, s)),
            pl.BlockSpec((HALF_ROPE, ta), lambda b, s: (0, s)),
            pl.BlockSpec((ta, LANES), lambda b, s: (s, 0)),
            pl.BlockSpec((ta, LANES), lambda b, s: (s, 0)),
            pl.BlockSpec((ta, LANES), lambda b, s: (s, 0)),
        ],
        out_specs=[
            pl.BlockSpec((1, H * HEAD_PAD, ta), lambda b, s: (b, 0, s)),
            pl.BlockSpec((1, ta, H * HEAD_PAD), lambda b, s: (b, s, 0)),
            pl.BlockSpec((1, H * V_ROWS, ta), lambda b, s: (b, 0, s)),
        ],
        out_shape=[
            jax.ShapeDtypeStruct((B, H * HEAD_PAD, S), BF16),
            jax.ShapeDtypeStruct((B, S, H * HEAD_PAD), BF16),
            jax.ShapeDtypeStruct((B, H * V_ROWS, S), BF16),
        ],
        compiler_params=_params(("parallel", "parallel")),
    )(x2, w_lat, q_norm_g[0][None], kv_norm_g[0][None], wuqT, wuk, wuvT, cosT, sinT, rot_a, rot_m, rot_p)

    tq = TILE_Q
    assert tq == 2 * TILE_K
    key_chunk = jnp.arange(tq)[:, None] // CHUNK
    q_chunk = jnp.arange(tq)[None, :] // CHUNK
    mask_bias = jnp.where(key_chunk <= q_chunk, 0.0, NEG_BIG).astype(F32).reshape(2, TILE_K, tq)
    hp = ATTN_HEADS_PER_STEP
    assert H % hp == 0
    oT = pl.pallas_call(
        _attn_kernel,
        grid=(B, H // hp, S // tq),
        in_specs=[
            pl.BlockSpec((1, hp * HEAD_PAD, tq), lambda b, h, q: (b, h, q)),
            pl.BlockSpec((1, hp * HEAD_PAD, tq), lambda b, h, q: (b, h, jnp.minimum(q + 1, S // tq - 1))),
            pl.BlockSpec((1, S, hp * HEAD_PAD), lambda b, h, q: (b, 0, h)),
            pl.BlockSpec((1, hp * V_ROWS, S), lambda b, h, q: (b, h, 0)),
            _full(mask_bias.shape),
        ],
        out_specs=pl.BlockSpec((1, hp * V_HEAD_DIM, tq), lambda b, h, q: (b, h, q)),
        out_shape=jax.ShapeDtypeStruct((B, H * V_HEAD_DIM, S), BF16),
        scratch_shapes=[pltpu.VMEM((TILE_K, tq), F32)] * (2 * hp) + [pltpu.VMEM((hp * SUBLANES, tq), F32)],
        compiler_params=_params(("parallel", "parallel", "arbitrary")),
    )(qT, qT, k, vT, mask_bias)

    tm = TILE_MIX
    nsm = S // tm
    triu = (jnp.arange(tm)[:, None] < jnp.arange(tm)[None, :]).astype(BF16)
    h_tok, route, routeT, counts = pl.pallas_call(
        _mix_kernel,
        grid=(B, nsm),
        in_specs=[
            pl.BlockSpec((tm, D), lambda b, s: (b * nsm + s, 0)),
            pl.BlockSpec((1, H * V_HEAD_DIM, tm), lambda b, s: (b, 0, s)),
            _full(w_pool.shape), _full(w_gates.shape), _full(pool_mix_w.shape[1:]), _full((1, D)),
            _full(w_mla_o.shape[1:]), _full(w_out.shape[1:]), _full((1, D)), _full((1, D)),
            _full(w_rT.shape), _full(b_rT.shape), _full(triu.shape),
        ],
        out_specs=[
            pl.BlockSpec((tm * ROW_TILES, LANES), lambda b, s: (b * nsm + s, 0)),
            pl.BlockSpec((tm, LANES), lambda b, s: (b * nsm + s, 0)),
            pl.BlockSpec((SUBLANES, tm), lambda b, s: (0, b * nsm + s)),
            _full((N_EXPERTS, LANES)),
        ],
        out_shape=[
            jax.ShapeDtypeStruct((N * ROW_TILES, LANES), F32),
            jax.ShapeDtypeStruct((N, LANES), F32),
            jax.ShapeDtypeStruct((SUBLANES, N), F32),
            jax.ShapeDtypeStruct((N_EXPERTS, LANES), F32),
        ],
        scratch_shapes=[pltpu.VMEM((tm + POOL_PAD + POOL_HALO, POOL_WIDTH), F32), pltpu.VMEM((N_EXPERTS, LANES), F32),
                        pltpu.VMEM((tm + POOL_PAD + POOL_HALO, POOL_GROUP_DIM), F32),
                        pltpu.VMEM((tm + POOL_PAD + POOL_HALO, POOL_GROUP_DIM), F32)],
        compiler_params=_params(("arbitrary", "arbitrary")),
    )(x2, oT, w_pool, w_gates, pool_mix_w[0].astype(BF16), pool_scale[0][None], w_mla_o[0].astype(BF16),
      w_out[0].astype(BF16), ln1_g[0][None], ln1_b[0][None], w_rT, b_rT, triu)

    tb = TILE_EXPERT
    A = N * TOP_K
    n_blocks = -(-(A + N_EXPERTS * (tb - 1)) // tb) + GATHER_AHEAD
    R = n_blocks * tb
    e_idx = routeT[0:2].astype(jnp.int32)
    rank = routeT[4:6].astype(jnp.int32)
    cnt = counts[:, 0].astype(jnp.int32)
    padded = ((cnt + tb - 1) // tb) * tb
    pad_end = jnp.cumsum(padded)
    pad_start = pad_end - padded
    is_e = e_idx[:, None, :] == jnp.arange(N_EXPERTS, dtype=jnp.int32)[None, :, None]
    dest = jnp.sum(jnp.where(is_e, pad_start[None, :, None], 0), axis=1) + rank
    block_start = jnp.arange(n_blocks, dtype=jnp.int32) * tb
    block_e = jnp.minimum(jnp.sum(pad_end[None, :] <= block_start[:, None], axis=1), N_EXPERTS - 1).astype(jnp.int32)
    n_used = (pad_end[-1] // tb).astype(jnp.int32)[None]
    tok = jnp.broadcast_to(jnp.arange(N, dtype=jnp.int32)[None, :, None], (TOP_K, N, LANES)).reshape(A, LANES)
    tok_rows = _scatter_rows_sparsecore(tok, dest.reshape(A), R)[:, 0].reshape(n_blocks, tb)
    block_is_e = block_e[:, None] == jnp.arange(N_EXPERTS, dtype=jnp.int32)[None, :]
    block_pad_start = jnp.sum(jnp.where(block_is_e, pad_start[None, :], 0), axis=1)
    block_cnt = jnp.sum(jnp.where(block_is_e, cnt[None, :], 0), axis=1)
    row_in_expert = block_start[:, None] + jnp.arange(tb, dtype=jnp.int32)[None, :] - block_pad_start[:, None]
    row_valid = jnp.logical_and(row_in_expert < block_cnt[:, None],
                                jnp.arange(n_blocks, dtype=jnp.int32)[:, None] < n_used[0])
    any_tok = (block_start[:, None] + jnp.arange(tb, dtype=jnp.int32)[None, :]) % N
    row_tok3 = jnp.where(row_valid, tok_rows, any_tok).reshape(n_blocks, 1, tb)

    y_rows = pl.pallas_call(
        _expert_kernel,
        grid_spec=pltpu.PrefetchScalarGridSpec(
            num_scalar_prefetch=2,
            grid=(n_blocks,),
            in_specs=[
                pl.BlockSpec((1, 1, tb), lambda i, be, nu: (i, 0, 0), memory_space=pltpu.SMEM),
                pl.BlockSpec((1, 1, tb), lambda i, be, nu: (jnp.minimum(i + 1, n_blocks - 1), 0, 0),
                             memory_space=pltpu.SMEM),
                pl.BlockSpec((1, 1, tb), lambda i, be, nu: (jnp.minimum(i + GATHER_AHEAD, n_blocks - 1), 0, 0),
                             memory_space=pltpu.SMEM),
                pl.BlockSpec(memory_space=pl.ANY),
                pl.BlockSpec((1, D, D_EXPERT), lambda i, be, nu: (be[i], 0, 0)),
                pl.BlockSpec((1, D, D_EXPERT), lambda i, be, nu: (be[i], 0, 0)),
                pl.BlockSpec((1, D_EXPERT, D), lambda i, be, nu: (be[i], 0, 0)),
            ],
            out_specs=pl.BlockSpec((tb * ROW_TILES, LANES), lambda i, be, nu: (i, 0)),
            scratch_shapes=[
                pltpu.VMEM((GATHER_SLOTS, ROW_TILES, tb, LANES), F32), pltpu.VMEM((tb, D), BF16),
                pltpu.VMEM((D, D_EXPERT), BF16), pltpu.VMEM((D, D_EXPERT), BF16), pltpu.VMEM((D_EXPERT, D), BF16),
                pltpu.SemaphoreType.DMA((GATHER_SLOTS,)),
            ],
        ),
        out_shape=jax.ShapeDtypeStruct((R * ROW_TILES, LANES), F32),
        compiler_params=_params(("arbitrary",)),
    )(block_e, n_used, row_tok3, row_tok3, row_tok3, h_tok, w_gate[0], w_up[0], w_down[0])

    tf = TILE_COMBINE
    nf = N // tf
    dest3 = dest.reshape(TOP_K, nf, tf).transpose(1, 0, 2).reshape(nf, 1, TOP_K * tf)
    out = pl.pallas_call(
        _combine_kernel,
        grid=(nf,),
        in_specs=[
            pl.BlockSpec((1, 1, TOP_K * tf), lambda i: (i, 0, 0), memory_space=pltpu.SMEM),
            pl.BlockSpec((1, 1, TOP_K * tf), lambda i: (jnp.minimum(i + 1, nf - 1), 0, 0), memory_space=pltpu.SMEM),
            pl.BlockSpec((1, 1, TOP_K * tf), lambda i: (jnp.minimum(i + GATHER_AHEAD, nf - 1), 0, 0),
                         memory_space=pltpu.SMEM),
            pl.BlockSpec((tf * ROW_TILES, LANES), lambda i: (i, 0)),
            pl.BlockSpec((tf, LANES), lambda i: (i, 0)),
            pl.BlockSpec(memory_space=pl.ANY),
            _full((1, D)), _full((1, D)),
        ],
        out_specs=pl.BlockSpec((tf, D), lambda i: (i, 0)),
        out_shape=jax.ShapeDtypeStruct((N, D), F32),
        scratch_shapes=[pltpu.VMEM((GATHER_SLOTS, ROW_TILES, TOP_K * tf, LANES), F32),
                        pltpu.SemaphoreType.DMA((GATHER_SLOTS,))],
        compiler_params=_params(("arbitrary",)),
    )(dest3, dest3, dest3, h_tok, route, y_rows, ln2_g[0][None], ln2_b[0][None])
    return out.reshape(B, S, D)
```

```python
import functools
import math

import jax
import jax.numpy as jnp
import numpy as np
from jax import lax
from jax.experimental import pallas as pl
from jax.experimental.pallas import tpu as pltpu
from jax.experimental.pallas import tpu_sc as plsc

D_MODEL = 1024
CHUNK = 64
POOL_WINDOWS = (2, 4, 8, 16)
POOL_GROUPS = len(POOL_WINDOWS)
POOL_WIDTH = D_MODEL // 2
POOL_GROUP_DIM = POOL_WIDTH // POOL_GROUPS
POOL_OUT_GROUP_DIM = D_MODEL // POOL_GROUPS
POOL_HALO = 16
POOL_PAD = 8
N_HEADS = 8
QK_NOPE_DIM = D_MODEL // 16
QK_ROPE_DIM = D_MODEL // 32
HALF_ROPE = QK_ROPE_DIM // 2
V_HEAD_DIM = D_MODEL // 16
V_ROWS = V_HEAD_DIM + 16
Q_LORA_RANK = 3 * D_MODEL // 8
KV_LORA_RANK = D_MODEL // 4
ROPE_THETA = 10000.0
N_GROUPS = 4
EXPERTS_PER_GROUP = 8
N_EXPERTS = N_GROUPS * EXPERTS_PER_GROUP
TOP_K = 2
D_EXPERT = D_MODEL // 2
NORM_EPS = 1e-5
DEPTH = 1
DEEPNORM_ALPHA = (2.0 * DEPTH) ** 0.25

LANES = 128
SUBLANES = 8
HEAD_PAD = LANES
ROW_TILES = D_MODEL // LANES
GROUP_ROW0 = 64
NEG_BIG = -1e30
VMEM_LIMIT = 56 * 1024 * 1024

TILE_LATENT = 1024
TILE_Q = 512
TILE_K = 256
ATTN_HEADS_PER_STEP = 4
ATTN_PAIRS_PER_TRIP = 2
TILE_MIX = 512
MIX_STRIP = 256
GATHER_AHEAD = 2
GATHER_SLOTS = GATHER_AHEAD + 1
TILE_EXPERT = 512
SC_SCATTER_WINDOW = 128
TILE_COMBINE = 512

F32 = jnp.float32
BF16 = jnp.bfloat16


def _dot(a, b):
    return jnp.dot(a, b, preferred_element_type=F32)


def _dot_nt(a, b):
    return lax.dot_general(a, b, (((1,), (1,)), ((), ())), preferred_element_type=F32)


def _dot_tn(a, b):
    return lax.dot_general(a, b, (((0,), (0,)), ((), ())), preferred_element_type=F32)


def _rms(v, g):
    ms = jnp.mean(jnp.square(v), axis=-1, keepdims=True)
    return v * lax.rsqrt(ms + NORM_EPS) * g


def _layer_norm(v, g, b):
    mu = jnp.mean(v, axis=-1, keepdims=True)
    c = v - mu
    var = jnp.mean(jnp.square(c), axis=-1, keepdims=True)
    return c * lax.rsqrt(var + NORM_EPS) * g + b


def _latent_kernel(x_ref, wlat_ref, gq_ref, gkv_ref, wuqT_ref, wuk_ref, wuvT_ref,
                   cosT_ref, sinT_ref, ra_ref, rm_ref, rp_ref,
                   qT_ref, k_ref, vT_ref, *, q_scale):
    xb = x_ref[...].astype(BF16)
    lat = _dot(xb, wlat_ref[...])
    c_q = lat[:, :Q_LORA_RANK]
    c_kv = lat[:, Q_LORA_RANK:Q_LORA_RANK + KV_LORA_RANK]
    kpe = lat[:, Q_LORA_RANK + KV_LORA_RANK:]
    qn = _rms(c_q, gq_ref[...]).astype(BF16)
    kvn = _rms(c_kv, gkv_ref[...]).astype(BF16)

    qT = _dot_nt(wuqT_ref[...], qn) * q_scale
    cosT = cosT_ref[...]
    sinT = sinT_ref[...]
    for h in range(N_HEADS):
        r0 = h * HEAD_PAD
        x1 = qT[r0 + QK_NOPE_DIM:r0 + QK_NOPE_DIM + HALF_ROPE]
        x2 = qT[r0 + QK_NOPE_DIM + HALF_ROPE:r0 + QK_NOPE_DIM + QK_ROPE_DIM]
        qT_ref[0, r0:r0 + QK_NOPE_DIM, :] = qT[r0:r0 + QK_NOPE_DIM].astype(BF16)
        qT_ref[0, r0 + QK_NOPE_DIM:r0 + QK_NOPE_DIM + HALF_ROPE, :] = (x1 * cosT - x2 * sinT).astype(BF16)
        qT_ref[0, r0 + QK_NOPE_DIM + HALF_ROPE:r0 + QK_NOPE_DIM + QK_ROPE_DIM, :] = (
            x1 * sinT + x2 * cosT).astype(BF16)
        qT_ref[0, r0 + QK_NOPE_DIM + QK_ROPE_DIM:r0 + HEAD_PAD, :] = jnp.zeros(
            (HEAD_PAD - QK_NOPE_DIM - QK_ROPE_DIM, qT.shape[1]), BF16)

    kpe_rot = (kpe * ra_ref[...] + pltpu.roll(kpe, LANES - HALF_ROPE, 1) * rm_ref[...]
               + pltpu.roll(kpe, HALF_ROPE, 1) * rp_ref[...])
    k = _dot(kvn, wuk_ref[...])
    for h in range(N_HEADS):
        k_ref[0, :, h * HEAD_PAD:(h + 1) * HEAD_PAD] = (k[:, h * HEAD_PAD:(h + 1) * HEAD_PAD] + kpe_rot).astype(BF16)
    vT = _dot_nt(wuvT_ref[...], kvn)
    row = lax.broadcasted_iota(jnp.int32, vT.shape, 0) % V_ROWS
    vT_ref[0] = jnp.where(row == V_HEAD_DIM, 1.0, vT).astype(BF16)


def _attn_kernel(qT_ref, qT_next_ref, k_ref, vT_ref, bias_ref, oT_ref, *bufs):
    qi = pl.program_id(2)
    tq = qT_ref.shape[2]
    heads = range(ATTN_HEADS_PER_STEP)
    s0, s1 = (bufs[i * ATTN_HEADS_PER_STEP:(i + 1) * ATTN_HEADS_PER_STEP] for i in range(2))
    first_max = bufs[2 * ATTN_HEADS_PER_STEP]

    def scores(h, j, c0=0):
        k0 = pl.multiple_of(j * TILE_K, TILE_K)
        return _dot(k_ref[0, pl.ds(k0, TILE_K), h * HEAD_PAD:(h + 1) * HEAD_PAD],
                    qT_ref[0, h * HEAD_PAD:(h + 1) * HEAD_PAD, c0:])

    def col_max(s):
        while s.shape[0] > SUBLANES:
            half = s.shape[0] // 2
            s = jnp.maximum(s[:half], s[half:])
        return jnp.max(s, axis=0, keepdims=True)

    def step(h, j, s_cur, s_nxt, carry, bias=None, c0=0, c0_nxt=0, bias_nxt=None):
        m_all, acc_all, cmax = carry
        cmax_nxt = None
        if s_nxt is not None:
            s_new = scores(h, j + 1, c0_nxt)
            if bias_nxt is not None:
                s_new = s_new + bias_nxt[:, c0_nxt:]
            s_nxt[h][:, c0_nxt:] = s_new
            cmax_nxt = col_max(s_new)
        m, acc = m_all[:, c0:], acc_all[:, c0:]
        s = s_cur[h][:, c0:]
        if bias is not None:
            s = s + bias[:, c0:]
            cmax = col_max(s)
        m_new = jnp.maximum(m, cmax)
        p = jnp.exp2(s - m_new).astype(BF16)
        k0 = pl.multiple_of(j * TILE_K, TILE_K)
        pv = _dot(vT_ref[0, h * V_ROWS:(h + 1) * V_ROWS, pl.ds(k0, TILE_K)], p)
        acc = jnp.exp2(m - m_new) * acc + pv
        if c0:
            m_new = jnp.concatenate([m_all[:, :c0], m_new], axis=1)
            acc = jnp.concatenate([acc_all[:, :c0], acc], axis=1)
        return m_new, acc, cmax_nxt

    def first_block(q_ref):
        for h in heads:
            s_first = _dot(k_ref[0, 0:TILE_K, h * HEAD_PAD:(h + 1) * HEAD_PAD],
                           q_ref[0, h * HEAD_PAD:(h + 1) * HEAD_PAD, :])
            s0[h][...] = s_first
            first_max[h * SUBLANES:h * SUBLANES + 1, :] = col_max(s_first)

    @pl.when(qi == 0)
    def _():
        first_block(qT_ref)

    carry = tuple((jnp.full((1, tq), NEG_BIG, F32), jnp.zeros((V_ROWS, tq), F32),
                   first_max[h * SUBLANES:h * SUBLANES + 1, :]) for h in heads)

    def pair(t, c):
        c = tuple(step(h, 2 * t, s0, s1, c[h]) for h in heads)
        return tuple(step(h, 2 * t + 1, s1, s0, c[h]) for h in heads)

    def trip(u, c):
        for i in range(ATTN_PAIRS_PER_TRIP):
            c = pair(ATTN_PAIRS_PER_TRIP * u + i, c)
        return c

    carry = lax.fori_loop(0, qi // ATTN_PAIRS_PER_TRIP, trip, carry)
    done = (qi // ATTN_PAIRS_PER_TRIP) * ATTN_PAIRS_PER_TRIP
    for i in range(ATTN_PAIRS_PER_TRIP - 1):
        carry = lax.cond(done + i < qi, functools.partial(pair, done + i), lambda c: c, carry)
    jd = 2 * qi
    carry = tuple(step(h, jd, s0, s1, carry[h], bias=bias_ref.at[0], c0_nxt=TILE_K, bias_nxt=bias_ref.at[1])
                  for h in heads)
    first_block(qT_next_ref)
    carry = tuple(step(h, jd + 1, s1, None, carry[h], c0=TILE_K) for h in heads)
    for h in heads:
        acc = carry[h][1]
        oT_ref[0, h * V_HEAD_DIM:(h + 1) * V_HEAD_DIM, :] = (
            acc[:V_HEAD_DIM] / acc[V_HEAD_DIM:V_HEAD_DIM + 1]).astype(BF16)


def _mix_kernel(x_ref, oT_ref, wpool_ref, wgate_ref, mixw_ref, pscale_ref, wo_ref, wout_ref,
                g1_ref, b1_ref, wrT_ref, brT_ref, triu_ref,
                h_ref, route_ref, routeT_ref, cnt_ref, ext_ref, base_ref, *dbl_refs):
    b = pl.program_id(0)
    si = pl.program_id(1)
    tm = x_ref.shape[0]

    @pl.when(jnp.logical_and(b == 0, si == 0))
    def _():
        base_ref[...] = jnp.zeros_like(base_ref)

    first = POOL_PAD + POOL_HALO

    @pl.when(si == 0)
    def _():
        ext_ref[0:first, :] = jnp.zeros((first, POOL_WIDTH), F32)
        for d in dbl_refs:
            d[0:POOL_PAD, :] = jnp.zeros((POOL_PAD, POOL_GROUP_DIM), F32)

    x = x_ref[...]
    xb = x.astype(BF16)
    u = _dot(xb, wpool_ref[...])
    ext_ref[first:first + tm, :] = u

    pos1 = si * tm + lax.broadcasted_iota(jnp.int32, (tm, POOL_GROUP_DIM), 0) + 1
    y_parts = []
    for g, win in enumerate(POOL_WINDOWS):
        cols = slice(g * POOL_GROUP_DIM, (g + 1) * POOL_GROUP_DIM)

        def rows(level, lo, n):
            return ext_ref[lo:lo + n, cols] if level == 0 else dbl_refs[(level - 1) % 2][lo:lo + n, :]

        level, m = 0, 1
        while 2 * m < win:
            n = tm + POOL_HALO
            dbl_refs[level % 2][POOL_PAD:POOL_PAD + n, :] = rows(level, POOL_PAD, n) + rows(level, POOL_PAD - m, n)
            level, m = level + 1, 2 * m
        ws = rows(level, first, tm) + rows(level, first - m, tm)
        count = jnp.minimum(pos1, win).astype(F32)
        pooled = ws / count - u[:, cols]
        y_parts.append(_dot(pooled.astype(BF16), mixw_ref[g]))
    ext_ref[POOL_PAD:first, :] = ext_ref[tm + POOL_PAD:tm + first, :]
    y_pool = jnp.concatenate(y_parts, axis=-1) * pscale_ref[...]

    oT = oT_ref[0]
    strips = []
    for c in range(0, D_MODEL, MIX_STRIP):
        cols = slice(c, c + MIX_STRIP)
        gate_pool = jax.nn.sigmoid(_dot(xb, wgate_ref[:, c:c + MIX_STRIP]))
        gate_mla = jax.nn.sigmoid(_dot(xb, wgate_ref[:, D_MODEL + c:D_MODEL + c + MIX_STRIP]))
        y_mla = _dot_tn(oT, wo_ref[:, cols])
        strips.append((gate_pool * y_pool[:, cols] + gate_mla * y_mla).astype(BF16))
    merged = jnp.concatenate(strips, axis=-1)
    r = DEEPNORM_ALPHA * x + _dot(merged, wout_ref[...])
    h = _layer_norm(r, g1_ref[...], b1_ref[...])
    for j in range(ROW_TILES):
        h_ref[pl.ds(j, tm, stride=ROW_TILES), :] = h[:, j * LANES:(j + 1) * LANES]

    logits = _dot_nt(wrT_ref[...], h.astype(BF16)) + brT_ref[...]
    big = float(LANES)
    gl = logits[GROUP_ROW0:GROUP_ROW0 + SUBLANES]
    grow = lax.broadcasted_iota(jnp.int32, gl.shape, 0)
    gl = jnp.where(grow < N_GROUPS, gl, NEG_BIG)
    gmax = jnp.max(gl, axis=0, keepdims=True)
    g_w = 1.0 / jnp.sum(jnp.exp(gl - gmax), axis=0, keepdims=True)
    g_idx = jnp.min(jnp.where(gl == gmax, grow.astype(F32), big), axis=0, keepdims=True)
    el = logits[0:N_EXPERTS]
    erow_i = lax.broadcasted_iota(jnp.int32, el.shape, 0)
    erow = erow_i.astype(F32)
    el = jnp.where((erow_i // EXPERTS_PER_GROUP).astype(F32) == g_idx, el, NEG_BIG)
    e1max = jnp.max(el, axis=0, keepdims=True)
    i1 = jnp.min(jnp.where(el == e1max, erow, big), axis=0, keepdims=True)
    el2 = jnp.where(erow == i1, NEG_BIG, el)
    e2max = jnp.max(el2, axis=0, keepdims=True)
    i2 = jnp.min(jnp.where(el2 == e2max, erow, big), axis=0, keepdims=True)
    ratio = jnp.exp(e2max - e1max)
    gate1 = g_w / (1.0 + ratio)
    gate2 = g_w * ratio / (1.0 + ratio)

    hit1 = erow == i1
    hit2 = erow == i2
    onehot = jnp.where(jnp.logical_or(hit1, hit2), 1.0, 0.0)
    base = base_ref[:, 0:1]
    before = _dot(onehot.astype(BF16), triu_ref[...]) + base
    rank1 = jnp.sum(jnp.where(hit1, before, 0.0), axis=0, keepdims=True)
    rank2 = jnp.sum(jnp.where(hit2, before, 0.0), axis=0, keepdims=True)
    new_base = base + jnp.sum(onehot, axis=1, keepdims=True)
    base_ref[...] = jnp.broadcast_to(new_base, base_ref.shape)
    cnt_ref[...] = jnp.broadcast_to(new_base, cnt_ref.shape)

    row = lax.broadcasted_iota(jnp.int32, (LANES, tm), 0)
    routeT = jnp.where(row == 0, i1, 0.0)
    routeT = jnp.where(row == 1, i2, routeT)
    routeT = jnp.where(row == 2, gate1, routeT)
    routeT = jnp.where(row == 3, gate2, routeT)
    routeT = jnp.where(row == 4, rank1, routeT)
    routeT = jnp.where(row == 5, rank2, routeT)
    routeT_ref[...] = routeT[:SUBLANES]
    route_ref[...] = routeT.T


def _gather_row(idx_ref, r, src_hbm, dst_ref, slot, sem, priority=0):
    src0 = pl.multiple_of(idx_ref[0, 0, r] * ROW_TILES, ROW_TILES)
    pltpu.make_async_copy(src_hbm.at[pl.ds(src0, ROW_TILES), :],
                          dst_ref.at[slot, :, r, :],
                          sem.at[slot]).start(priority=priority)


def _gather_rows_loop(idx_ref, src_hbm, dst_ref, slot, sem, n_rows, unroll=8):
    def body(c, _):
        for u in range(unroll):
            _gather_row(idx_ref, c * unroll + u, src_hbm, dst_ref, slot, sem)
        return 0
    lax.fori_loop(0, n_rows // unroll, body, 0)


def _gather_rows_inline(idx_ref, src_hbm, dst_ref, slot, sem, n_rows):
    for r in range(n_rows):
        _gather_row(idx_ref, r, src_hbm, dst_ref, slot, sem, priority=r % 2)


def _wait_rows(dst_ref, slot, sem):
    pltpu.make_async_copy(dst_ref.at[slot], dst_ref.at[slot], sem.at[slot]).wait()


def _expert_kernel(be_ref, nused_ref, wplan_ref, tok_ref, tok1_ref, tok2_ref, h_hbm, wg_hbm, wu_hbm, wd_hbm,
                   y_ref, xbuf, xs_ref, wgb, wub, wdb, sem, wg_st, wu_st, wd_st, wsem):
    i = pl.program_id(0)
    n_used = nused_ref[0]
    slot = lax.rem(i, GATHER_SLOTS)
    tb = TILE_EXPERT

    @pl.when(i == 0)
    def _():
        _gather_rows_loop(tok_ref, h_hbm, xbuf, 0, sem, tb)
        _gather_rows_loop(tok1_ref, h_hbm, xbuf, 1, sem, tb)

    changed = jnp.logical_or(i == 0, be_ref[i] != be_ref[jnp.maximum(i - 1, 0)])

    def weight_copies(expert, wslot):
        return (pltpu.make_async_copy(wg_hbm.at[expert], wg_st.at[wslot], wsem.at[wslot]),
                pltpu.make_async_copy(wu_hbm.at[expert], wu_st.at[wslot], wsem.at[wslot]),
                pltpu.make_async_copy(wd_hbm.at[expert], wd_st.at[wslot], wsem.at[wslot]))

    @pl.when(jnp.logical_and(i == 0, n_used > 0))
    def _():
        for c in weight_copies(be_ref[0], 0):
            c.start()

    @pl.when(jnp.logical_and(changed, i < n_used))
    def _():
        wslot = wplan_ref[0, i]
        for c in weight_copies(be_ref[i], wslot):
            c.wait()

        @pl.when(wplan_ref[2, i] == 1)
        def _():
            for c in weight_copies(wplan_ref[1, i], 1 - wslot):
                c.start()

        wgb[...] = wg_st[wslot].astype(BF16)
        wub[...] = wu_st[wslot].astype(BF16)
        wdb[...] = wd_st[wslot].astype(BF16)

    @pl.when(jnp.logical_and(i >= n_used, i < n_used + GATHER_AHEAD))
    def _():
        _wait_rows(xbuf, slot, sem)

    @pl.when(i < n_used)
    def _():
        _wait_rows(xbuf, slot, sem)
        for j in range(ROW_TILES):
            xs_ref[:, j * LANES:(j + 1) * LANES] = xbuf[slot, j].astype(BF16)
        _gather_rows_inline(tok2_ref, h_hbm, xbuf, lax.rem(i + GATHER_AHEAD, GATHER_SLOTS), sem, tb)
        xs = xs_ref[...]
        gate = _dot(xs, wgb[...])
        up = _dot(xs, wub[...])
        hid = (jax.nn.silu(gate) * up).astype(BF16)
        y = _dot(hid, wdb[...])
        for j in range(ROW_TILES):
            y_ref[pl.ds(j, tb, stride=ROW_TILES), :] = y[:, j * LANES:(j + 1) * LANES]

    @pl.when(i >= n_used)
    def _():
        y_ref[...] = jnp.zeros_like(y_ref)


def _combine_kernel(dst_ref, dst1_ref, dst2_ref, h_ref, route_ref, y_hbm, g2_ref, b2_ref, o_ref, gbuf, sem):
    i = pl.program_id(0)
    n = pl.num_programs(0)
    slot = lax.rem(i, GATHER_SLOTS)
    tf = TILE_COMBINE

    @pl.when(i == 0)
    def _():
        _gather_rows_loop(dst_ref, y_hbm, gbuf, 0, sem, TOP_K * tf)
        _gather_rows_loop(dst1_ref, y_hbm, gbuf, 1, sem, TOP_K * tf)

    _wait_rows(gbuf, slot, sem)
    route = route_ref[...]
    gate1 = route[:, 2:3]
    gate2 = route[:, 3:4]
    parts = []
    for j in range(ROW_TILES):
        hj = h_ref[pl.ds(j, tf, stride=ROW_TILES), :]
        y1 = gbuf[slot, j, 0:tf, :]
        y2 = gbuf[slot, j, tf:TOP_K * tf, :]
        parts.append(DEEPNORM_ALPHA * hj + (gate1 * y1 + gate2 * y2))
    z = jnp.concatenate(parts, axis=-1)
    o_ref[...] = _layer_norm(z, g2_ref[...], b2_ref[...])
    _gather_rows_inline(dst2_ref, y_hbm, gbuf, lax.rem(i + GATHER_AHEAD, GATHER_SLOTS), sem, TOP_K * tf)

    @pl.when(i == n - 1)
    def _():
        for ahead in range(1, GATHER_AHEAD + 1):
            _wait_rows(gbuf, lax.rem(i + ahead, GATHER_SLOTS), sem)


def _rope_tables(seq):
    f32 = np.float32
    inv = (f32(1.0) / (f32(ROPE_THETA) ** (np.arange(0, QK_ROPE_DIM, 2, dtype=f32) / f32(QK_ROPE_DIM)))).astype(f32)
    ang = (np.arange(seq, dtype=f32)[:, None] * inv[None, :]).astype(f32)
    cos, sin = np.cos(ang.astype(np.float64)).astype(f32), np.sin(ang.astype(np.float64)).astype(f32)
    zeros = np.zeros((seq, QK_NOPE_DIM), f32)
    pad = np.zeros((seq, HEAD_PAD - QK_NOPE_DIM - QK_ROPE_DIM), f32)
    z16 = np.zeros((seq, HALF_ROPE), f32)
    rot_a = np.concatenate([zeros, cos, cos, pad], axis=1)
    rot_m = np.concatenate([zeros, -sin, z16, pad], axis=1)
    rot_p = np.concatenate([zeros, z16, sin, pad], axis=1)
    return tuple(jnp.asarray(t) for t in (np.ascontiguousarray(cos.T), np.ascontiguousarray(sin.T),
                                          rot_a, rot_m, rot_p))


def _scatter_rows_sparsecore(x, indices, n_out):
    n, width = x.shape
    mesh = plsc.VectorSubcoreMesh(core_axis_name="core", subcore_axis_name="subcore")
    per_core = n // SC_SCATTER_WINDOW // mesh.num_cores
    assert per_core * SC_SCATTER_WINDOW * mesh.num_cores == n

    @functools.partial(pl.kernel, out_type=jax.ShapeDtypeStruct((n_out, width), x.dtype), mesh=mesh,
                       scratch_types=[])
    def scatter(x_hbm, i_hbm, o_hbm):
        first = lax.axis_index("core") * per_core

        def body(x_vmem, i_vmem):
            pltpu.sync_copy(x_vmem, o_hbm.at[i_vmem.at[0]])

        pltpu.emit_pipeline(
            body,
            grid=(per_core,),
            in_specs=[pl.BlockSpec((SC_SCATTER_WINDOW, width), index_map=lambda i: (first + i, 0)),
                      pl.BlockSpec((1, SC_SCATTER_WINDOW), index_map=lambda i: (0, first + i))],
            out_specs=[],
            core_axis_name="subcore",
            dimension_semantics=(pltpu.PARALLEL,),
        )(x_hbm, i_hbm)

    return scatter(x, indices.reshape(1, n))


def _full(shape):
    return pl.BlockSpec(shape, lambda *_: (0,) * len(shape))


def _params(sem):
    return pltpu.CompilerParams(dimension_semantics=sem, vmem_limit_bytes=VMEM_LIMIT)


def kernel(x, w_in, pool_mix_w, pool_scale, q_norm_g, w_uq, kv_norm_g, w_ukv, w_mla_o, w_out, ln1_g, ln1_b,
           w_router_group, b_router_group, w_router_expert, b_router_expert, w_gate, w_up, w_down, ln2_g, ln2_b):
    B, S, D = x.shape
    assert D == D_MODEL and w_in.shape[0] == DEPTH == 1
    assert S % TILE_LATENT == 0 and S % TILE_Q == 0 and S % TILE_MIX == 0 and TILE_Q % TILE_K == 0
    N = B * S
    assert N % TILE_COMBINE == 0
    H = N_HEADS

    w = w_in[0]
    o1 = POOL_WIDTH
    o2 = o1 + Q_LORA_RANK
    o3 = o2 + KV_LORA_RANK
    o4 = o3 + QK_ROPE_DIM
    w_pool = w[:, :o1].astype(BF16)
    kpe_cols = jnp.pad(w[:, o3:o4], ((0, 0), (QK_NOPE_DIM, HEAD_PAD - QK_NOPE_DIM - QK_ROPE_DIM)))
    w_lat = jnp.concatenate([w[:, o1:o3], kpe_cols], axis=1).astype(BF16)
    w_gates = w[:, o4:].astype(BF16)
    qd = QK_NOPE_DIM + QK_ROPE_DIM
    wuq = jnp.pad(w_uq[0].reshape(Q_LORA_RANK, H, qd), ((0, 0), (0, 0), (0, HEAD_PAD - qd)))
    wuqT = wuq.reshape(Q_LORA_RANK, H * HEAD_PAD).T.astype(BF16)
    wukv = w_ukv[0].reshape(KV_LORA_RANK, H, QK_NOPE_DIM + V_HEAD_DIM)
    wuk = jnp.pad(wukv[:, :, :QK_NOPE_DIM], ((0, 0), (0, 0), (0, HEAD_PAD - QK_NOPE_DIM)))
    wuk = wuk.reshape(KV_LORA_RANK, H * HEAD_PAD).astype(BF16)
    wuv = jnp.pad(wukv[:, :, QK_NOPE_DIM:], ((0, 0), (0, 0), (0, V_ROWS - V_HEAD_DIM)))
    wuvT = wuv.reshape(KV_LORA_RANK, H * V_ROWS).T.astype(BF16)
    w_rT = jnp.zeros((LANES, D), F32)
    w_rT = w_rT.at[:N_EXPERTS].set(w_router_expert[0].T).at[GROUP_ROW0:GROUP_ROW0 + N_GROUPS].set(
        w_router_group[0].T).astype(BF16)
    b_rT = jnp.zeros((LANES, 1), F32)
    b_rT = b_rT.at[:N_EXPERTS, 0].set(b_router_expert[0]).at[GROUP_ROW0:GROUP_ROW0 + N_GROUPS, 0].set(
        b_router_group[0])
    cosT, sinT, rot_a, rot_m, rot_p = _rope_tables(S)
    q_scale = (QK_NOPE_DIM + QK_ROPE_DIM) ** -0.5 * math.log2(math.e)
    x2 = x.reshape(N, D)

    ta = TILE_LATENT
    nsa = S // ta
    qT, k, vT = pl.pallas_call(
        functools.partial(_latent_kernel, q_scale=q_scale),
        grid=(B, nsa),
        in_specs=[
            pl.BlockSpec((ta, D), lambda b, s: (b * nsa + s, 0)),
            _full(w_lat.shape), _full((1, Q_LORA_RANK)), _full((1, KV_LORA_RANK)),
            _full(wuqT.shape), _full(wuk.shape), _full(wuvT.shape),
            pl.BlockSpec((HALF_ROPE, ta), lambda b, s: (0, s)),
            pl.BlockSpec((HALF_ROPE, ta), lambda b, s: (0, s)),
            pl.BlockSpec((ta, LANES), lambda b, s: (s, 0)),
            pl.BlockSpec((ta, LANES), lambda b, s: (s, 0)),
            pl.BlockSpec((ta, LANES), lambda b, s: (s, 0)),
        ],
        out_specs=[
            pl.BlockSpec((1, H * HEAD_PAD, ta), lambda b, s: (b, 0, s)),
            pl.BlockSpec((1, ta, H * HEAD_PAD), lambda b, s: (b, s, 0)),
            pl.BlockSpec((1, H * V_ROWS, ta), lambda b, s: (b, 0, s)),
        ],
        out_shape=[
            jax.ShapeDtypeStruct((B, H * HEAD_PAD, S), BF16),
            jax.ShapeDtypeStruct((B, S, H * HEAD_PAD), BF16),
            jax.ShapeDtypeStruct((B, H * V_ROWS, S), BF16),
        ],
        compiler_params=_params(("parallel", "parallel")),
    )(x2, w_lat, q_norm_g[0][None], kv_norm_g[0][None], wuqT, wuk, wuvT, cosT, sinT, rot_a, rot_m, rot_p)

    tq = TILE_Q
    assert tq == 2 * TILE_K
    key_chunk = jnp.arange(tq)[:, None] // CHUNK
    q_chunk = jnp.arange(tq)[None, :] // CHUNK
    mask_bias = jnp.where(key_chunk <= q_chunk, 0.0, NEG_BIG).astype(F32).reshape(2, TILE_K, tq)
    hp = ATTN_HEADS_PER_STEP
    assert H % hp == 0
    oT = pl.pallas_call(
        _attn_kernel,
        grid=(B, H // hp, S // tq),
        in_specs=[
            pl.BlockSpec((1, hp * HEAD_PAD, tq), lambda b, h, q: (b, h, q)),
            pl.BlockSpec((1, hp * HEAD_PAD, tq), lambda b, h, q: (b, h, jnp.minimum(q + 1, S // tq - 1))),
            pl.BlockSpec((1, S, hp * HEAD_PAD), lambda b, h, q: (b, 0, h)),
            pl.BlockSpec((1, hp * V_ROWS, S), lambda b, h, q: (b, h, 0)),
            _full(mask_bias.shape),
        ],
        out_specs=pl.BlockSpec((1, hp * V_HEAD_DIM, tq), lambda b, h, q: (b, h, q)),
        out_shape=jax.ShapeDtypeStruct((B, H * V_HEAD_DIM, S), BF16),
        scratch_shapes=[pltpu.VMEM((TILE_K, tq), F32)] * (2 * hp) + [pltpu.VMEM((hp * SUBLANES, tq), F32)],
        compiler_params=_params(("parallel", "parallel", "arbitrary")),
    )(qT, qT, k, vT, mask_bias)

    tm = TILE_MIX
    nsm = S // tm
    triu = (jnp.arange(tm)[:, None] < jnp.arange(tm)[None, :]).astype(BF16)
    h_tok, route, routeT, counts = pl.pallas_call(
        _mix_kernel,
        grid=(B, nsm),
        in_specs=[
            pl.BlockSpec((tm, D), lambda b, s: (b * nsm + s, 0)),
            pl.BlockSpec((1, H * V_HEAD_DIM, tm), lambda b, s: (b, 0, s)),
            _full(w_pool.shape), _full(w_gates.shape), _full(pool_mix_w.shape[1:]), _full((1, D)),
            _full(w_mla_o.shape[1:]), _full(w_out.shape[1:]), _full((1, D)), _full((1, D)),
            _full(w_rT.shape), _full(b_rT.shape), _full(triu.shape),
        ],
        out_specs=[
            pl.BlockSpec((tm * ROW_TILES, LANES), lambda b, s: (b * nsm + s, 0)),
            pl.BlockSpec((tm, LANES), lambda b, s: (b * nsm + s, 0)),
            pl.BlockSpec((SUBLANES, tm), lambda b, s: (0, b * nsm + s)),
            _full((N_EXPERTS, LANES)),
        ],
        out_shape=[
            jax.ShapeDtypeStruct((N * ROW_TILES, LANES), F32),
            jax.ShapeDtypeStruct((N, LANES), F32),
            jax.ShapeDtypeStruct((SUBLANES, N), F32),
            jax.ShapeDtypeStruct((N_EXPERTS, LANES), F32),
        ],
        scratch_shapes=[pltpu.VMEM((tm + POOL_PAD + POOL_HALO, POOL_WIDTH), F32), pltpu.VMEM((N_EXPERTS, LANES), F32),
                        pltpu.VMEM((tm + POOL_PAD + POOL_HALO, POOL_GROUP_DIM), F32),
                        pltpu.VMEM((tm + POOL_PAD + POOL_HALO, POOL_GROUP_DIM), F32)],
        compiler_params=_params(("arbitrary", "arbitrary")),
    )(x2, oT, w_pool, w_gates, pool_mix_w[0].astype(BF16), pool_scale[0][None], w_mla_o[0].astype(BF16),
      w_out[0].astype(BF16), ln1_g[0][None], ln1_b[0][None], w_rT, b_rT, triu)

    tb = TILE_EXPERT
    A = N * TOP_K
    n_blocks = -(-(A + N_EXPERTS * (tb - 1)) // tb) + GATHER_AHEAD
    R = n_blocks * tb
    e_idx = routeT[0:2].astype(jnp.int32)
    rank = routeT[4:6].astype(jnp.int32)
    cnt = counts[:, 0].astype(jnp.int32)
    padded = ((cnt + tb - 1) // tb) * tb
    pad_end = jnp.cumsum(padded)
    pad_start = pad_end - padded
    is_e = e_idx[:, None, :] == jnp.arange(N_EXPERTS, dtype=jnp.int32)[None, :, None]
    dest = jnp.sum(jnp.where(is_e, pad_start[None, :, None], 0), axis=1) + rank
    block_start = jnp.arange(n_blocks, dtype=jnp.int32) * tb
    block_e = jnp.minimum(jnp.sum(pad_end[None, :] <= block_start[:, None], axis=1), N_EXPERTS - 1).astype(jnp.int32)
    n_used = (pad_end[-1] // tb).astype(jnp.int32)[None]
    tok = jnp.broadcast_to(jnp.arange(N, dtype=jnp.int32)[None, :, None], (TOP_K, N, LANES)).reshape(A, LANES)
    tok_rows = _scatter_rows_sparsecore(tok, dest.reshape(A), R)[:, 0].reshape(n_blocks, tb)
    block_is_e = block_e[:, None] == jnp.arange(N_EXPERTS, dtype=jnp.int32)[None, :]
    block_pad_start = jnp.sum(jnp.where(block_is_e, pad_start[None, :], 0), axis=1)
    block_cnt = jnp.sum(jnp.where(block_is_e, cnt[None, :], 0), axis=1)
    row_in_expert = block_start[:, None] + jnp.arange(tb, dtype=jnp.int32)[None, :] - block_pad_start[:, None]
    row_valid = jnp.logical_and(row_in_expert < block_cnt[:, None],
                                jnp.arange(n_blocks, dtype=jnp.int32)[:, None] < n_used[0])
    any_tok = (block_start[:, None] + jnp.arange(tb, dtype=jnp.int32)[None, :]) % N
    row_tok3 = jnp.where(row_valid, tok_rows, any_tok).reshape(n_blocks, 1, tb)

    blk = jnp.arange(n_blocks, dtype=jnp.int32)
    first_of_expert = jnp.logical_or(blk == 0, block_e != jnp.roll(block_e, 1))
    expert_ordinal = jnp.cumsum(first_of_expert.astype(jnp.int32)) - 1
    later_first = jnp.where(jnp.logical_and(first_of_expert[None, :], blk[None, :] > blk[:, None]), blk[None, :],
                            n_blocks)
    next_first = jnp.min(later_first, axis=1)
    has_next = (next_first < n_used[0]).astype(jnp.int32)
    next_expert = block_e[jnp.minimum(next_first, n_blocks - 1)]
    weight_plan = jnp.stack([expert_ordinal % 2, next_expert, has_next]).astype(jnp.int32)

    y_rows = pl.pallas_call(
        _expert_kernel,
        grid_spec=pltpu.PrefetchScalarGridSpec(
            num_scalar_prefetch=3,
            grid=(n_blocks,),
            in_specs=[
                pl.BlockSpec((1, 1, tb), lambda i, be, nu, wp: (i, 0, 0), memory_space=pltpu.SMEM),
                pl.BlockSpec((1, 1, tb), lambda i, be, nu, wp: (jnp.minimum(i + 1, n_blocks - 1), 0, 0),
                             memory_space=pltpu.SMEM),
                pl.BlockSpec((1, 1, tb), lambda i, be, nu, wp: (jnp.minimum(i + GATHER_AHEAD, n_blocks - 1), 0, 0),
                             memory_space=pltpu.SMEM),
                pl.BlockSpec(memory_space=pl.ANY),
                pl.BlockSpec(memory_space=pl.ANY), pl.BlockSpec(memory_space=pl.ANY), pl.BlockSpec(memory_space=pl.ANY),
            ],
            out_specs=pl.BlockSpec((tb * ROW_TILES, LANES), lambda i, be, nu, wp: (i, 0)),
            scratch_shapes=[
                pltpu.VMEM((GATHER_SLOTS, ROW_TILES, tb, LANES), F32), pltpu.VMEM((tb, D), BF16),
                pltpu.VMEM((D, D_EXPERT), BF16), pltpu.VMEM((D, D_EXPERT), BF16), pltpu.VMEM((D_EXPERT, D), BF16),
                pltpu.SemaphoreType.DMA((GATHER_SLOTS,)),
                pltpu.VMEM((2, D, D_EXPERT), F32), pltpu.VMEM((2, D, D_EXPERT), F32), pltpu.VMEM((2, D_EXPERT, D), F32),
                pltpu.SemaphoreType.DMA((2,)),
            ],
        ),
        out_shape=jax.ShapeDtypeStruct((R * ROW_TILES, LANES), F32),
        compiler_params=_params(("arbitrary",)),
    )(block_e, n_used, weight_plan, row_tok3, row_tok3, row_tok3, h_tok, w_gate[0], w_up[0], w_down[0])

    tf = TILE_COMBINE
    nf = N // tf
    dest3 = dest.reshape(TOP_K, nf, tf).transpose(1, 0, 2).reshape(nf, 1, TOP_K * tf)
    out = pl.pallas_call(
        _combine_kernel,
        grid=(nf,),
        in_specs=[
            pl.BlockSpec((1, 1, TOP_K * tf), lambda i: (i, 0, 0), memory_space=pltpu.SMEM),
            pl.BlockSpec((1, 1, TOP_K * tf), lambda i: (jnp.minimum(i + 1, nf - 1), 0, 0), memory_space=pltpu.SMEM),
            pl.BlockSpec((1, 1, TOP_K * tf), lambda i: (jnp.minimum(i + GATHER_AHEAD, nf - 1), 0, 0),
                         memory_space=pltpu.SMEM),
            pl.BlockSpec((tf * ROW_TILES, LANES), lambda i: (i, 0)),
            pl.BlockSpec((tf, LANES), lambda i: (i, 0)),
            pl.BlockSpec(memory_space=pl.ANY),
            _full((1, D)), _full((1, D)),
        ],
        out_specs=pl.BlockSpec((tf, D), lambda i: (i, 0)),
        out_shape=jax.ShapeDtypeStruct((N, D), F32),
        scratch_shapes=[pltpu.VMEM((GATHER_SLOTS, ROW_TILES, TOP_K * tf, LANES), F32),
                        pltpu.SemaphoreType.DMA((GATHER_SLOTS,))],
        compiler_params=_params(("arbitrary",)),
    )(dest3, dest3, dest3, h_tok, route, y_rows, ln2_g[0][None], ln2_b[0][None])
    return out.reshape(B, S, D)
```

```python
import functools
import math

import jax
import jax.numpy as jnp
import numpy as np
from jax import lax
from jax.experimental import pallas as pl
from jax.experimental.pallas import tpu as pltpu
from jax.experimental.pallas import tpu_sc as plsc

D_MODEL = 1024
CHUNK = 64
POOL_WINDOWS = (2, 4, 8, 16)
POOL_GROUPS = len(POOL_WINDOWS)
POOL_WIDTH = D_MODEL // 2
POOL_GROUP_DIM = POOL_WIDTH // POOL_GROUPS
POOL_OUT_GROUP_DIM = D_MODEL // POOL_GROUPS
POOL_HALO = 16
POOL_PAD = 8
N_HEADS = 8
QK_NOPE_DIM = D_MODEL // 16
QK_ROPE_DIM = D_MODEL // 32
HALF_ROPE = QK_ROPE_DIM // 2
V_HEAD_DIM = D_MODEL // 16
V_ROWS = V_HEAD_DIM + 16
Q_LORA_RANK = 3 * D_MODEL // 8
KV_LORA_RANK = D_MODEL // 4
ROPE_THETA = 10000.0
N_GROUPS = 4
EXPERTS_PER_GROUP = 8
N_EXPERTS = N_GROUPS * EXPERTS_PER_GROUP
TOP_K = 2
D_EXPERT = D_MODEL // 2
NORM_EPS = 1e-5
DEPTH = 1
DEEPNORM_ALPHA = (2.0 * DEPTH) ** 0.25

LANES = 128
SUBLANES = 8
HEAD_PAD = LANES
ROW_TILES = D_MODEL // LANES
GROUP_ROW0 = 64
NEG_BIG = -1e30
VMEM_LIMIT = 56 * 1024 * 1024

TILE_LATENT = 1024
TILE_Q = 512
TILE_K = 256
ATTN_HEADS_PER_STEP = 4
ATTN_PAIRS_PER_TRIP = 2
TILE_MIX = 512
MIX_STRIP = 256
GATHER_AHEAD = 2
GATHER_SLOTS = GATHER_AHEAD + 1
TILE_EXPERT = 512
SC_SCATTER_WINDOW = 128
TILE_COMBINE = 512

F32 = jnp.float32
BF16 = jnp.bfloat16


def _dot(a, b):
    return jnp.dot(a, b, preferred_element_type=F32)


def _dot_nt(a, b):
    return lax.dot_general(a, b, (((1,), (1,)), ((), ())), preferred_element_type=F32)


def _dot_tn(a, b):
    return lax.dot_general(a, b, (((0,), (0,)), ((), ())), preferred_element_type=F32)


def _rms(v, g):
    ms = jnp.mean(jnp.square(v), axis=-1, keepdims=True)
    return v * lax.rsqrt(ms + NORM_EPS) * g


def _layer_norm(v, g, b):
    mu = jnp.mean(v, axis=-1, keepdims=True)
    c = v - mu
    var = jnp.mean(jnp.square(c), axis=-1, keepdims=True)
    return c * lax.rsqrt(var + NORM_EPS) * g + b


def _latent_kernel(x_ref, wlat_ref, gq_ref, gkv_ref, wuqT_ref, wuk_ref, wuvT_ref,
                   cosT_ref, sinT_ref, ra_ref, rm_ref, rp_ref,
                   qT_ref, k_ref, vT_ref, *, q_scale):
    xb = x_ref[...].astype(BF16)
    lat = _dot(xb, wlat_ref[...])
    c_q = lat[:, :Q_LORA_RANK]
    c_kv = lat[:, Q_LORA_RANK:Q_LORA_RANK + KV_LORA_RANK]
    kpe = lat[:, Q_LORA_RANK + KV_LORA_RANK:]
    qn = _rms(c_q, gq_ref[...]).astype(BF16)
    kvn = _rms(c_kv, gkv_ref[...]).astype(BF16)

    qT = _dot_nt(wuqT_ref[...], qn) * q_scale
    cosT = cosT_ref[...]
    sinT = sinT_ref[...]
    for h in range(N_HEADS):
        r0 = h * HEAD_PAD
        x1 = qT[r0 + QK_NOPE_DIM:r0 + QK_NOPE_DIM + HALF_ROPE]
        x2 = qT[r0 + QK_NOPE_DIM + HALF_ROPE:r0 + QK_NOPE_DIM + QK_ROPE_DIM]
        qT_ref[0, r0:r0 + QK_NOPE_DIM, :] = qT[r0:r0 + QK_NOPE_DIM].astype(BF16)
        qT_ref[0, r0 + QK_NOPE_DIM:r0 + QK_NOPE_DIM + HALF_ROPE, :] = (x1 * cosT - x2 * sinT).astype(BF16)
        qT_ref[0, r0 + QK_NOPE_DIM + HALF_ROPE:r0 + QK_NOPE_DIM + QK_ROPE_DIM, :] = (
            x1 * sinT + x2 * cosT).astype(BF16)
        qT_ref[0, r0 + QK_NOPE_DIM + QK_ROPE_DIM:r0 + HEAD_PAD, :] = jnp.zeros(
            (HEAD_PAD - QK_NOPE_DIM - QK_ROPE_DIM, qT.shape[1]), BF16)

    kpe_rot = (kpe * ra_ref[...] + pltpu.roll(kpe, LANES - HALF_ROPE, 1) * rm_ref[...]
               + pltpu.roll(kpe, HALF_ROPE, 1) * rp_ref[...])
    k = _dot(kvn, wuk_ref[...])
    for h in range(N_HEADS):
        k_ref[0, :, h * HEAD_PAD:(h + 1) * HEAD_PAD] = (k[:, h * HEAD_PAD:(h + 1) * HEAD_PAD] + kpe_rot).astype(BF16)
    vT = _dot_nt(wuvT_ref[...], kvn)
    row = lax.broadcasted_iota(jnp.int32, vT.shape, 0) % V_ROWS
    vT_ref[0] = jnp.where(row == V_HEAD_DIM, 1.0, vT).astype(BF16)


def _attn_kernel(qT_ref, qT_next_ref, k_ref, vT_ref, bias_ref, oT_ref, *bufs):
    qi = pl.program_id(2)
    tq = qT_ref.shape[2]
    heads = range(ATTN_HEADS_PER_STEP)
    s0, s1 = (bufs[i * ATTN_HEADS_PER_STEP:(i + 1) * ATTN_HEADS_PER_STEP] for i in range(2))
    first_max = bufs[2 * ATTN_HEADS_PER_STEP]

    def scores(h, j, c0=0):
        k0 = pl.multiple_of(j * TILE_K, TILE_K)
        return _dot(k_ref[0, pl.ds(k0, TILE_K), h * HEAD_PAD:(h + 1) * HEAD_PAD],
                    qT_ref[0, h * HEAD_PAD:(h + 1) * HEAD_PAD, c0:])

    def col_max(s):
        while s.shape[0] > SUBLANES:
            half = s.shape[0] // 2
            s = jnp.maximum(s[:half], s[half:])
        return jnp.max(s, axis=0, keepdims=True)

    def step(h, j, s_cur, s_nxt, carry, bias=None, c0=0, c0_nxt=0, bias_nxt=None):
        m_all, acc_all, cmax = carry
        cmax_nxt = None
        if s_nxt is not None:
            s_new = scores(h, j + 1, c0_nxt)
            if bias_nxt is not None:
                s_new = s_new + bias_nxt[:, c0_nxt:]
            s_nxt[h][:, c0_nxt:] = s_new
            cmax_nxt = col_max(s_new)
        m, acc = m_all[:, c0:], acc_all[:, c0:]
        s = s_cur[h][:, c0:]
        if bias is not None:
            s = s + bias[:, c0:]
            cmax = col_max(s)
        m_new = jnp.maximum(m, cmax)
        p = jnp.exp2(s - m_new).astype(BF16)
        k0 = pl.multiple_of(j * TILE_K, TILE_K)
        pv = _dot(vT_ref[0, h * V_ROWS:(h + 1) * V_ROWS, pl.ds(k0, TILE_K)], p)
        acc = jnp.exp2(m - m_new) * acc + pv
        if c0:
            m_new = jnp.concatenate([m_all[:, :c0], m_new], axis=1)
            acc = jnp.concatenate([acc_all[:, :c0], acc], axis=1)
        return m_new, acc, cmax_nxt

    def first_block(q_ref):
        for h in heads:
            s_first = _dot(k_ref[0, 0:TILE_K, h * HEAD_PAD:(h + 1) * HEAD_PAD],
                           q_ref[0, h * HEAD_PAD:(h + 1) * HEAD_PAD, :])
            s0[h][...] = s_first
            first_max[h * SUBLANES:h * SUBLANES + 1, :] = col_max(s_first)

    @pl.when(qi == 0)
    def _():
        first_block(qT_ref)

    carry = tuple((jnp.full((1, tq), NEG_BIG, F32), jnp.zeros((V_ROWS, tq), F32),
                   first_max[h * SUBLANES:h * SUBLANES + 1, :]) for h in heads)

    def pair(t, c):
        c = tuple(step(h, 2 * t, s0, s1, c[h]) for h in heads)
        return tuple(step(h, 2 * t + 1, s1, s0, c[h]) for h in heads)

    def trip(u, c):
        for i in range(ATTN_PAIRS_PER_TRIP):
            c = pair(ATTN_PAIRS_PER_TRIP * u + i, c)
        return c

    carry = lax.fori_loop(0, qi // ATTN_PAIRS_PER_TRIP, trip, carry)
    done = (qi // ATTN_PAIRS_PER_TRIP) * ATTN_PAIRS_PER_TRIP
    for i in range(ATTN_PAIRS_PER_TRIP - 1):
        carry = lax.cond(done + i < qi, functools.partial(pair, done + i), lambda c: c, carry)
    jd = 2 * qi
    carry = tuple(step(h, jd, s0, s1, carry[h], bias=bias_ref.at[0], c0_nxt=TILE_K, bias_nxt=bias_ref.at[1])
                  for h in heads)
    first_block(qT_next_ref)
    carry = tuple(step(h, jd + 1, s1, None, carry[h], c0=TILE_K) for h in heads)
    for h in heads:
        acc = carry[h][1]
        oT_ref[0, h * V_HEAD_DIM:(h + 1) * V_HEAD_DIM, :] = (
            acc[:V_HEAD_DIM] / acc[V_HEAD_DIM:V_HEAD_DIM + 1]).astype(BF16)


def _mix_kernel(x_ref, oT_ref, wpool_ref, wgate_ref, mixw_ref, pscale_ref, wo_ref, wout_ref,
                g1_ref, b1_ref, wrT_ref, brT_ref, triu_ref,
                h_ref, route_ref, routeT_ref, cnt_ref, ext_ref, base_ref, *dbl_refs):
    b = pl.program_id(0)
    si = pl.program_id(1)
    tm = x_ref.shape[0]

    @pl.when(jnp.logical_and(b == 0, si == 0))
    def _():
        base_ref[...] = jnp.zeros_like(base_ref)

    first = POOL_PAD + POOL_HALO

    @pl.when(si == 0)
    def _():
        ext_ref[0:first, :] = jnp.zeros((first, POOL_WIDTH), F32)
        for d in dbl_refs:
            d[0:POOL_PAD, :] = jnp.zeros((POOL_PAD, POOL_GROUP_DIM), F32)

    x = x_ref[...]
    xb = x.astype(BF16)
    u = _dot(xb, wpool_ref[...])
    ext_ref[first:first + tm, :] = u

    pos1 = si * tm + lax.broadcasted_iota(jnp.int32, (tm, POOL_GROUP_DIM), 0) + 1
    y_parts = []
    for g, win in enumerate(POOL_WINDOWS):
        cols = slice(g * POOL_GROUP_DIM, (g + 1) * POOL_GROUP_DIM)

        def rows(level, lo, n):
            return ext_ref[lo:lo + n, cols] if level == 0 else dbl_refs[(level - 1) % 2][lo:lo + n, :]

        level, m = 0, 1
        while 2 * m < win:
            n = tm + POOL_HALO
            dbl_refs[level % 2][POOL_PAD:POOL_PAD + n, :] = rows(level, POOL_PAD, n) + rows(level, POOL_PAD - m, n)
            level, m = level + 1, 2 * m
        ws = rows(level, first, tm) + rows(level, first - m, tm)
        count = jnp.minimum(pos1, win).astype(F32)
        pooled = ws / count - u[:, cols]
        y_parts.append(_dot(pooled.astype(BF16), mixw_ref[g]))
    ext_ref[POOL_PAD:first, :] = ext_ref[tm + POOL_PAD:tm + first, :]
    y_pool = jnp.concatenate(y_parts, axis=-1) * pscale_ref[...]

    oT = oT_ref[0]
    strips = []
    for c in range(0, D_MODEL, MIX_STRIP):
        cols = slice(c, c + MIX_STRIP)
        gate_pool = jax.nn.sigmoid(_dot(xb, wgate_ref[:, c:c + MIX_STRIP]))
        gate_mla = jax.nn.sigmoid(_dot(xb, wgate_ref[:, D_MODEL + c:D_MODEL + c + MIX_STRIP]))
        y_mla = _dot_tn(oT, wo_ref[:, cols])
        strips.append((gate_pool * y_pool[:, cols] + gate_mla * y_mla).astype(BF16))
    merged = jnp.concatenate(strips, axis=-1)
    r = DEEPNORM_ALPHA * x + _dot(merged, wout_ref[...])
    h = _layer_norm(r, g1_ref[...], b1_ref[...])
    for j in range(ROW_TILES):
        h_ref[pl.ds(j, tm, stride=ROW_TILES), :] = h[:, j * LANES:(j + 1) * LANES]

    logits = _dot_nt(wrT_ref[...], h.astype(BF16)) + brT_ref[...]
    big = float(LANES)
    gl = logits[GROUP_ROW0:GROUP_ROW0 + SUBLANES]
    grow = lax.broadcasted_iota(jnp.int32, gl.shape, 0)
    gl = jnp.where(grow < N_GROUPS, gl, NEG_BIG)
    gmax = jnp.max(gl, axis=0, keepdims=True)
    g_w = 1.0 / jnp.sum(jnp.exp(gl - gmax), axis=0, keepdims=True)
    g_idx = jnp.min(jnp.where(gl == gmax, grow.astype(F32), big), axis=0, keepdims=True)
    el = logits[0:N_EXPERTS]
    erow_i = lax.broadcasted_iota(jnp.int32, el.shape, 0)
    erow = erow_i.astype(F32)
    el = jnp.where((erow_i // EXPERTS_PER_GROUP).astype(F32) == g_idx, el, NEG_BIG)
    e1max = jnp.max(el, axis=0, keepdims=True)
    i1 = jnp.min(jnp.where(el == e1max, erow, big), axis=0, keepdims=True)
    el2 = jnp.where(erow == i1, NEG_BIG, el)
    e2max = jnp.max(el2, axis=0, keepdims=True)
    i2 = jnp.min(jnp.where(el2 == e2max, erow, big), axis=0, keepdims=True)
    ratio = jnp.exp(e2max - e1max)
    gate1 = g_w / (1.0 + ratio)
    gate2 = g_w * ratio / (1.0 + ratio)

    hit1 = erow == i1
    hit2 = erow == i2
    onehot = jnp.where(jnp.logical_or(hit1, hit2), 1.0, 0.0)
    base = base_ref[:, 0:1]
    before = _dot(onehot.astype(BF16), triu_ref[...]) + base
    rank1 = jnp.sum(jnp.where(hit1, before, 0.0), axis=0, keepdims=True)
    rank2 = jnp.sum(jnp.where(hit2, before, 0.0), axis=0, keepdims=True)
    new_base = base + jnp.sum(onehot, axis=1, keepdims=True)
    base_ref[...] = jnp.broadcast_to(new_base, base_ref.shape)
    cnt_ref[...] = jnp.broadcast_to(new_base, cnt_ref.shape)

    row = lax.broadcasted_iota(jnp.int32, (LANES, tm), 0)
    routeT = jnp.where(row == 0, i1, 0.0)
    routeT = jnp.where(row == 1, i2, routeT)
    routeT = jnp.where(row == 2, gate1, routeT)
    routeT = jnp.where(row == 3, gate2, routeT)
    routeT = jnp.where(row == 4, rank1, routeT)
    routeT = jnp.where(row == 5, rank2, routeT)
    routeT_ref[...] = routeT[:SUBLANES]
    route_ref[...] = routeT.T


def _gather_row(idx_ref, r, src_hbm, dst_ref, slot, sem, priority=0):
    src0 = pl.multiple_of(idx_ref[0, 0, r] * ROW_TILES, ROW_TILES)
    pltpu.make_async_copy(src_hbm.at[pl.ds(src0, ROW_TILES), :],
                          dst_ref.at[slot, :, r, :],
                          sem.at[slot]).start(priority=priority)


def _gather_rows_loop(idx_ref, src_hbm, dst_ref, slot, sem, n_rows, unroll=8):
    def body(c, _):
        for u in range(unroll):
            _gather_row(idx_ref, c * unroll + u, src_hbm, dst_ref, slot, sem)
        return 0
    lax.fori_loop(0, n_rows // unroll, body, 0)


def _gather_rows_inline(idx_ref, src_hbm, dst_ref, slot, sem, n_rows):
    for r in range(n_rows):
        _gather_row(idx_ref, r, src_hbm, dst_ref, slot, sem, priority=r % 2)


def _wait_rows(dst_ref, slot, sem):
    pltpu.make_async_copy(dst_ref.at[slot], dst_ref.at[slot], sem.at[slot]).wait()


def _expert_kernel(be_ref, nused_ref, wplan_ref, tok_ref, tok1_ref, tok2_ref, h_hbm, wg_hbm, wu_hbm, wd_hbm,
                   y_ref, xbuf, xs_ref, wgb, wub, wdb, sem, wg_st, wu_st, wd_st, wsem):
    i = pl.program_id(0)
    n_used = nused_ref[0]
    slot = lax.rem(i, GATHER_SLOTS)
    tb = TILE_EXPERT

    @pl.when(i == 0)
    def _():
        _gather_rows_loop(tok_ref, h_hbm, xbuf, 0, sem, tb)
        _gather_rows_loop(tok1_ref, h_hbm, xbuf, 1, sem, tb)

    changed = jnp.logical_or(i == 0, be_ref[i] != be_ref[jnp.maximum(i - 1, 0)])

    def weight_copies(expert, wslot):
        return (pltpu.make_async_copy(wg_hbm.at[expert], wg_st.at[wslot], wsem.at[wslot]),
                pltpu.make_async_copy(wu_hbm.at[expert], wu_st.at[wslot], wsem.at[wslot]),
                pltpu.make_async_copy(wd_hbm.at[expert], wd_st.at[wslot], wsem.at[wslot]))

    @pl.when(jnp.logical_and(i == 0, n_used > 0))
    def _():
        for c in weight_copies(be_ref[0], 0):
            c.start()

    @pl.when(jnp.logical_and(changed, i < n_used))
    def _():
        wslot = wplan_ref[0, i]
        for c in weight_copies(be_ref[i], wslot):
            c.wait()

        @pl.when(wplan_ref[2, i] == 1)
        def _():
            for c in weight_copies(wplan_ref[1, i], 1 - wslot):
                c.start(priority=1)

        wgb[...] = wg_st[wslot].astype(BF16)
        wub[...] = wu_st[wslot].astype(BF16)
        wdb[...] = wd_st[wslot].astype(BF16)

    @pl.when(jnp.logical_and(i >= n_used, i < n_used + GATHER_AHEAD))
    def _():
        _wait_rows(xbuf, slot, sem)

    @pl.when(i < n_used)
    def _():
        _wait_rows(xbuf, slot, sem)
        for j in range(ROW_TILES):
            xs_ref[:, j * LANES:(j + 1) * LANES] = xbuf[slot, j].astype(BF16)
        _gather_rows_inline(tok2_ref, h_hbm, xbuf, lax.rem(i + GATHER_AHEAD, GATHER_SLOTS), sem, tb)
        xs = xs_ref[...]
        gate = _dot(xs, wgb[...])
        up = _dot(xs, wub[...])
        hid = (jax.nn.silu(gate) * up).astype(BF16)
        y = _dot(hid, wdb[...])
        for j in range(ROW_TILES):
            y_ref[pl.ds(j, tb, stride=ROW_TILES), :] = y[:, j * LANES:(j + 1) * LANES]

    @pl.when(i >= n_used)
    def _():
        y_ref[...] = jnp.zeros_like(y_ref)


def _combine_kernel(dst_ref, dst1_ref, dst2_ref, h_ref, route_ref, y_hbm, g2_ref, b2_ref, o_ref, gbuf, sem):
    i = pl.program_id(0)
    n = pl.num_programs(0)
    slot = lax.rem(i, GATHER_SLOTS)
    tf = TILE_COMBINE

    @pl.when(i == 0)
    def _():
        _gather_rows_loop(dst_ref, y_hbm, gbuf, 0, sem, TOP_K * tf)
        _gather_rows_loop(dst1_ref, y_hbm, gbuf, 1, sem, TOP_K * tf)

    _wait_rows(gbuf, slot, sem)
    route = route_ref[...]
    gate1 = route[:, 2:3]
    gate2 = route[:, 3:4]
    parts = []
    for j in range(ROW_TILES):
        hj = h_ref[pl.ds(j, tf, stride=ROW_TILES), :]
        y1 = gbuf[slot, j, 0:tf, :]
        y2 = gbuf[slot, j, tf:TOP_K * tf, :]
        parts.append(DEEPNORM_ALPHA * hj + (gate1 * y1 + gate2 * y2))
    z = jnp.concatenate(parts, axis=-1)
    o_ref[...] = _layer_norm(z, g2_ref[...], b2_ref[...])
    _gather_rows_inline(dst2_ref, y_hbm, gbuf, lax.rem(i + GATHER_AHEAD, GATHER_SLOTS), sem, TOP_K * tf)

    @pl.when(i == n - 1)
    def _():
        for ahead in range(1, GATHER_AHEAD + 1):
            _wait_rows(gbuf, lax.rem(i + ahead, GATHER_SLOTS), sem)


def _rope_tables(seq):
    f32 = np.float32
    inv = (f32(1.0) / (f32(ROPE_THETA) ** (np.arange(0, QK_ROPE_DIM, 2, dtype=f32) / f32(QK_ROPE_DIM)))).astype(f32)
    ang = (np.arange(seq, dtype=f32)[:, None] * inv[None, :]).astype(f32)
    cos, sin = np.cos(ang.astype(np.float64)).astype(f32), np.sin(ang.astype(np.float64)).astype(f32)
    zeros = np.zeros((seq, QK_NOPE_DIM), f32)
    pad = np.zeros((seq, HEAD_PAD - QK_NOPE_DIM - QK_ROPE_DIM), f32)
    z16 = np.zeros((seq, HALF_ROPE), f32)
    rot_a = np.concatenate([zeros, cos, cos, pad], axis=1)
    rot_m = np.concatenate([zeros, -sin, z16, pad], axis=1)
    rot_p = np.concatenate([zeros, z16, sin, pad], axis=1)
    return tuple(jnp.asarray(t) for t in (np.ascontiguousarray(cos.T), np.ascontiguousarray(sin.T),
                                          rot_a, rot_m, rot_p))


def _scatter_rows_sparsecore(x, indices, n_out):
    n, width = x.shape
    mesh = plsc.VectorSubcoreMesh(core_axis_name="core", subcore_axis_name="subcore")
    per_core = n // SC_SCATTER_WINDOW // mesh.num_cores
    assert per_core * SC_SCATTER_WINDOW * mesh.num_cores == n

    @functools.partial(pl.kernel, out_type=jax.ShapeDtypeStruct((n_out, width), x.dtype), mesh=mesh,
                       scratch_types=[])
    def scatter(x_hbm, i_hbm, o_hbm):
        first = lax.axis_index("core") * per_core

        def body(x_vmem, i_vmem):
            pltpu.sync_copy(x_vmem, o_hbm.at[i_vmem.at[0]])

        pltpu.emit_pipeline(
            body,
            grid=(per_core,),
            in_specs=[pl.BlockSpec((SC_SCATTER_WINDOW, width), index_map=lambda i: (first + i, 0)),
                      pl.BlockSpec((1, SC_SCATTER_WINDOW), index_map=lambda i: (0, first + i))],
            out_specs=[],
            core_axis_name="subcore",
            dimension_semantics=(pltpu.PARALLEL,),
        )(x_hbm, i_hbm)

    return scatter(x, indices.reshape(1, n))


def _full(shape):
    return pl.BlockSpec(shape, lambda *_: (0,) * len(shape))


def _params(sem):
    return pltpu.CompilerParams(dimension_semantics=sem, vmem_limit_bytes=VMEM_LIMIT)


def kernel(x, w_in, pool_mix_w, pool_scale, q_norm_g, w_uq, kv_norm_g, w_ukv, w_mla_o, w_out, ln1_g, ln1_b,
           w_router_group, b_router_group, w_router_expert, b_router_expert, w_gate, w_up, w_down, ln2_g, ln2_b):
    B, S, D = x.shape
    assert D == D_MODEL and w_in.shape[0] == DEPTH == 1
    assert S % TILE_LATENT == 0 and S % TILE_Q == 0 and S % TILE_MIX == 0 and TILE_Q % TILE_K == 0
    N = B * S
    assert N % TILE_COMBINE == 0
    H = N_HEADS

    w = w_in[0]
    o1 = POOL_WIDTH
    o2 = o1 + Q_LORA_RANK
    o3 = o2 + KV_LORA_RANK
    o4 = o3 + QK_ROPE_DIM
    w_pool = w[:, :o1].astype(BF16)
    kpe_cols = jnp.pad(w[:, o3:o4], ((0, 0), (QK_NOPE_DIM, HEAD_PAD - QK_NOPE_DIM - QK_ROPE_DIM)))
    w_lat = jnp.concatenate([w[:, o1:o3], kpe_cols], axis=1).astype(BF16)
    w_gates = w[:, o4:].astype(BF16)
    qd = QK_NOPE_DIM + QK_ROPE_DIM
    wuq = jnp.pad(w_uq[0].reshape(Q_LORA_RANK, H, qd), ((0, 0), (0, 0), (0, HEAD_PAD - qd)))
    wuqT = wuq.reshape(Q_LORA_RANK, H * HEAD_PAD).T.astype(BF16)
    wukv = w_ukv[0].reshape(KV_LORA_RANK, H, QK_NOPE_DIM + V_HEAD_DIM)
    wuk = jnp.pad(wukv[:, :, :QK_NOPE_DIM], ((0, 0), (0, 0), (0, HEAD_PAD - QK_NOPE_DIM)))
    wuk = wuk.reshape(KV_LORA_RANK, H * HEAD_PAD).astype(BF16)
    wuv = jnp.pad(wukv[:, :, QK_NOPE_DIM:], ((0, 0), (0, 0), (0, V_ROWS - V_HEAD_DIM)))
    wuvT = wuv.reshape(KV_LORA_RANK, H * V_ROWS).T.astype(BF16)
    w_rT = jnp.zeros((LANES, D), F32)
    w_rT = w_rT.at[:N_EXPERTS].set(w_router_expert[0].T).at[GROUP_ROW0:GROUP_ROW0 + N_GROUPS].set(
        w_router_group[0].T).astype(BF16)
    b_rT = jnp.zeros((LANES, 1), F32)
    b_rT = b_rT.at[:N_EXPERTS, 0].set(b_router_expert[0]).at[GROUP_ROW0:GROUP_ROW0 + N_GROUPS, 0].set(
        b_router_group[0])
    cosT, sinT, rot_a, rot_m, rot_p = _rope_tables(S)
    q_scale = (QK_NOPE_DIM + QK_ROPE_DIM) ** -0.5 * math.log2(math.e)
    x2 = x.reshape(N, D)

    ta = TILE_LATENT
    nsa = S // ta
    qT, k, vT = pl.pallas_call(
        functools.partial(_latent_kernel, q_scale=q_scale),
        grid=(B, nsa),
        in_specs=[
            pl.BlockSpec((ta, D), lambda b, s: (b * nsa + s, 0)),
            _full(w_lat.shape), _full((1, Q_LORA_RANK)), _full((1, KV_LORA_RANK)),
            _full(wuqT.shape), _full(wuk.shape), _full(wuvT.shape),
            pl.BlockSpec((HALF_ROPE, ta), lambda b, s: (0, s)),
            pl.BlockSpec((HALF_ROPE, ta), lambda b, s: (0, s)),
            pl.BlockSpec((ta, LANES), lambda b, s: (s, 0)),
            pl.BlockSpec((ta, LANES), lambda b, s: (s, 0)),
            pl.BlockSpec((ta, LANES), lambda b, s: (s, 0)),
        ],
        out_specs=[
            pl.BlockSpec((1, H * HEAD_PAD, ta), lambda b, s: (b, 0, s)),
            pl.BlockSpec((1, ta, H * HEAD_PAD), lambda b, s: (b, s, 0)),
            pl.BlockSpec((1, H * V_ROWS, ta), lambda b, s: (b, 0, s)),
        ],
        out_shape=[
            jax.ShapeDtypeStruct((B, H * HEAD_PAD, S), BF16),
            jax.ShapeDtypeStruct((B, S, H * HEAD_PAD), BF16),
            jax.ShapeDtypeStruct((B, H * V_ROWS, S), BF16),
        ],
        compiler_params=_params(("parallel", "parallel")),
    )(x2, w_lat, q_norm_g[0][None], kv_norm_g[0][None], wuqT, wuk, wuvT, cosT, sinT, rot_a, rot_m, rot_p)

    tq = TILE_Q
    assert tq == 2 * TILE_K
    key_chunk = jnp.arange(tq)[:, None] // CHUNK
    q_chunk = jnp.arange(tq)[None, :] // CHUNK
    mask_bias = jnp.where(key_chunk <= q_chunk, 0.0, NEG_BIG).astype(F32).reshape(2, TILE_K, tq)
    hp = ATTN_HEADS_PER_STEP
    assert H % hp == 0
    oT = pl.pallas_call(
        _attn_kernel,
        grid=(B, H // hp, S // tq),
        in_specs=[
            pl.BlockSpec((1, hp * HEAD_PAD, tq), lambda b, h, q: (b, h, q)),
            pl.BlockSpec((1, hp * HEAD_PAD, tq), lambda b, h, q: (b, h, jnp.minimum(q + 1, S // tq - 1))),
            pl.BlockSpec((1, S, hp * HEAD_PAD), lambda b, h, q: (b, 0, h)),
            pl.BlockSpec((1, hp * V_ROWS, S), lambda b, h, q: (b, h, 0)),
            _full(mask_bias.shape),
        ],
        out_specs=pl.BlockSpec((1, hp * V_HEAD_DIM, tq), lambda b, h, q: (b, h, q)),
        out_shape=jax.ShapeDtypeStruct((B, H * V_HEAD_DIM, S), BF16),
        scratch_shapes=[pltpu.VMEM((TILE_K, tq), F32)] * (2 * hp) + [pltpu.VMEM((hp * SUBLANES, tq), F32)],
        compiler_params=_params(("parallel", "parallel", "arbitrary")),
    )(qT, qT, k, vT, mask_bias)

    tm = TILE_MIX
    nsm = S // tm
    triu = (jnp.arange(tm)[:, None] < jnp.arange(tm)[None, :]).astype(BF16)
    h_tok, route, routeT, counts = pl.pallas_call(
        _mix_kernel,
        grid=(B, nsm),
        in_specs=[
            pl.BlockSpec((tm, D), lambda b, s: (b * nsm + s, 0)),
            pl.BlockSpec((1, H * V_HEAD_DIM, tm), lambda b, s: (b, 0, s)),
            _full(w_pool.shape), _full(w_gates.shape), _full(pool_mix_w.shape[1:]), _full((1, D)),
            _full(w_mla_o.shape[1:]), _full(w_out.shape[1:]), _full((1, D)), _full((1, D)),
            _full(w_rT.shape), _full(b_rT.shape), _full(triu.shape),
        ],
        out_specs=[
            pl.BlockSpec((tm * ROW_TILES, LANES), lambda b, s: (b * nsm + s, 0)),
            pl.BlockSpec((tm, LANES), lambda b, s: (b * nsm + s, 0)),
            pl.BlockSpec((SUBLANES, tm), lambda b, s: (0, b * nsm + s)),
            _full((N_EXPERTS, LANES)),
        ],
        out_shape=[
            jax.ShapeDtypeStruct((N * ROW_TILES, LANES), F32),
            jax.ShapeDtypeStruct((N, LANES), F32),
            jax.ShapeDtypeStruct((SUBLANES, N), F32),
            jax.ShapeDtypeStruct((N_EXPERTS, LANES), F32),
        ],
        scratch_shapes=[pltpu.VMEM((tm + POOL_PAD + POOL_HALO, POOL_WIDTH), F32), pltpu.VMEM((N_EXPERTS, LANES), F32),
                        pltpu.VMEM((tm + POOL_PAD + POOL_HALO, POOL_GROUP_DIM), F32),
                        pltpu.VMEM((tm + POOL_PAD + POOL_HALO, POOL_GROUP_DIM), F32)],
        compiler_params=_params(("arbitrary", "arbitrary")),
    )(x2, oT, w_pool, w_gates, pool_mix_w[0].astype(BF16), pool_scale[0][None], w_mla_o[0].astype(BF16),
      w_out[0].astype(BF16), ln1_g[0][None], ln1_b[0][None], w_rT, b_rT, triu)

    tb = TILE_EXPERT
    A = N * TOP_K
    n_blocks = -(-(A + N_EXPERTS * (tb - 1)) // tb) + GATHER_AHEAD
    R = n_blocks * tb
    e_idx = routeT[0:2].astype(jnp.int32)
    rank = routeT[4:6].astype(jnp.int32)
    cnt = counts[:, 0].astype(jnp.int32)
    padded = ((cnt + tb - 1) // tb) * tb
    pad_end = jnp.cumsum(padded)
    pad_start = pad_end - padded
    is_e = e_idx[:, None, :] == jnp.arange(N_EXPERTS, dtype=jnp.int32)[None, :, None]
    dest = jnp.sum(jnp.where(is_e, pad_start[None, :, None], 0), axis=1) + rank
    block_start = jnp.arange(n_blocks, dtype=jnp.int32) * tb
    block_e = jnp.minimum(jnp.sum(pad_end[None, :] <= block_start[:, None], axis=1), N_EXPERTS - 1).astype(jnp.int32)
    n_used = (pad_end[-1] // tb).astype(jnp.int32)[None]
    tok = jnp.broadcast_to(jnp.arange(N, dtype=jnp.int32)[None, :, None], (TOP_K, N, LANES)).reshape(A, LANES)
    tok_rows = _scatter_rows_sparsecore(tok, dest.reshape(A), R)[:, 0].reshape(n_blocks, tb)
    block_is_e = block_e[:, None] == jnp.arange(N_EXPERTS, dtype=jnp.int32)[None, :]
    block_pad_start = jnp.sum(jnp.where(block_is_e, pad_start[None, :], 0), axis=1)
    block_cnt = jnp.sum(jnp.where(block_is_e, cnt[None, :], 0), axis=1)
    row_in_expert = block_start[:, None] + jnp.arange(tb, dtype=jnp.int32)[None, :] - block_pad_start[:, None]
    row_valid = jnp.logical_and(row_in_expert < block_cnt[:, None],
                                jnp.arange(n_blocks, dtype=jnp.int32)[:, None] < n_used[0])
    any_tok = (block_start[:, None] + jnp.arange(tb, dtype=jnp.int32)[None, :]) % N
    row_tok3 = jnp.where(row_valid, tok_rows, any_tok).reshape(n_blocks, 1, tb)

    blk = jnp.arange(n_blocks, dtype=jnp.int32)
    first_of_expert = jnp.logical_or(blk == 0, block_e != jnp.roll(block_e, 1))
    expert_ordinal = jnp.cumsum(first_of_expert.astype(jnp.int32)) - 1
    later_first = jnp.where(jnp.logical_and(first_of_expert[None, :], blk[None, :] > blk[:, None]), blk[None, :],
                            n_blocks)
    next_first = jnp.min(later_first, axis=1)
    has_next = (next_first < n_used[0]).astype(jnp.int32)
    next_expert = block_e[jnp.minimum(next_first, n_blocks - 1)]
    weight_plan = jnp.stack([expert_ordinal % 2, next_expert, has_next]).astype(jnp.int32)

    y_rows = pl.pallas_call(
        _expert_kernel,
        grid_spec=pltpu.PrefetchScalarGridSpec(
            num_scalar_prefetch=3,
            grid=(n_blocks,),
            in_specs=[
                pl.BlockSpec((1, 1, tb), lambda i, be, nu, wp: (i, 0, 0), memory_space=pltpu.SMEM),
                pl.BlockSpec((1, 1, tb), lambda i, be, nu, wp: (jnp.minimum(i + 1, n_blocks - 1), 0, 0),
                             memory_space=pltpu.SMEM),
                pl.BlockSpec((1, 1, tb), lambda i, be, nu, wp: (jnp.minimum(i + GATHER_AHEAD, n_blocks - 1), 0, 0),
                             memory_space=pltpu.SMEM),
                pl.BlockSpec(memory_space=pl.ANY),
                pl.BlockSpec(memory_space=pl.ANY), pl.BlockSpec(memory_space=pl.ANY), pl.BlockSpec(memory_space=pl.ANY),
            ],
            out_specs=pl.BlockSpec((tb * ROW_TILES, LANES), lambda i, be, nu, wp: (i, 0)),
            scratch_shapes=[
                pltpu.VMEM((GATHER_SLOTS, ROW_TILES, tb, LANES), F32), pltpu.VMEM((tb, D), BF16),
                pltpu.VMEM((D, D_EXPERT), BF16), pltpu.VMEM((D, D_EXPERT), BF16), pltpu.VMEM((D_EXPERT, D), BF16),
                pltpu.SemaphoreType.DMA((GATHER_SLOTS,)),
                pltpu.VMEM((2, D, D_EXPERT), F32), pltpu.VMEM((2, D, D_EXPERT), F32), pltpu.VMEM((2, D_EXPERT, D), F32),
                pltpu.SemaphoreType.DMA((2,)),
            ],
        ),
        out_shape=jax.ShapeDtypeStruct((R * ROW_TILES, LANES), F32),
        compiler_params=_params(("arbitrary",)),
    )(block_e, n_used, weight_plan, row_tok3, row_tok3, row_tok3, h_tok, w_gate[0], w_up[0], w_down[0])

    tf = TILE_COMBINE
    nf = N // tf
    dest3 = dest.reshape(TOP_K, nf, tf).transpose(1, 0, 2).reshape(nf, 1, TOP_K * tf)
    out = pl.pallas_call(
        _combine_kernel,
        grid=(nf,),
        in_specs=[
            pl.BlockSpec((1, 1, TOP_K * tf), lambda i: (i, 0, 0), memory_space=pltpu.SMEM),
            pl.BlockSpec((1, 1, TOP_K * tf), lambda i: (jnp.minimum(i + 1, nf - 1), 0, 0), memory_space=pltpu.SMEM),
            pl.BlockSpec((1, 1, TOP_K * tf), lambda i: (jnp.minimum(i + GATHER_AHEAD, nf - 1), 0, 0),
                         memory_space=pltpu.SMEM),
            pl.BlockSpec((tf * ROW_TILES, LANES), lambda i: (i, 0)),
            pl.BlockSpec((tf, LANES), lambda i: (i, 0)),
            pl.BlockSpec(memory_space=pl.ANY),
            _full((1, D)), _full((1, D)),
        ],
        out_specs=pl.BlockSpec((tf, D), lambda i: (i, 0)),
        out_shape=jax.ShapeDtypeStruct((N, D), F32),
        scratch_shapes=[pltpu.VMEM((GATHER_SLOTS, ROW_TILES, TOP_K * tf, LANES), F32),
                        pltpu.SemaphoreType.DMA((GATHER_SLOTS,))],
        compiler_params=_params(("arbitrary",)),
    )(dest3, dest3, dest3, h_tok, route, y_rows, ln2_g[0][None], ln2_b[0][None])
    return out.reshape(B, S, D)
```
